```python
import math
import jax, jax.numpy as jnp
from jax import lax
import numpy as np

D_MODEL = 1024
BATCH = 4
SEQ = 8192
DEPTH = 2

GLA_HEADS = 4
GLA_DV = 96
GLA_DK = 48
GLA_QK = GLA_HEADS * GLA_DK
GLA_WIDTH = GLA_HEADS * GLA_DV
GLA_RANK = 16
GLA_TAU = 16.0
GLA_CHUNK = 64
DSA_HEADS = 4
DSA_HEAD_DIM = 64
DSA_WIDTH = DSA_HEADS * DSA_HEAD_DIM
DSA_IDX_HEADS = 8
DSA_IDX_DIM = 32
DSA_TOPK = 256
DSA_QBLOCK = 128
S5_GROUPS = 24
S5_GROUP_CH = 16
S5_STATE = 64
S5_WIDTH = S5_GROUPS * S5_GROUP_CH
D_MIX = GLA_WIDTH + DSA_WIDTH + S5_WIDTH
IN_SIZES = (GLA_QK, GLA_QK, GLA_WIDTH, GLA_RANK, GLA_WIDTH,
            DSA_WIDTH, DSA_WIDTH, DSA_WIDTH, DSA_IDX_HEADS * DSA_IDX_DIM, DSA_IDX_DIM, DSA_IDX_HEADS,
            S5_WIDTH)
D_IN = 2616
MOE_GROUPS = 4
MOE_EXPERTS_PER_GROUP = 8
MOE_EXPERTS = MOE_GROUPS * MOE_EXPERTS_PER_GROUP
MOE_TOPK = 2
MOE_HIDDEN = 512
MOE_BLOCK = 128
RMS_EPS = 1e-6

kernel_name = 'hybrid_gla_dsa_s5_hier_moe_block'


def rms_norm(x, g):
    xf = x.astype(jnp.float32)
    y = xf * lax.rsqrt(jnp.mean(xf * xf, axis=-1, keepdims=True) + RMS_EPS)
    return (y * g.astype(jnp.float32)).astype(x.dtype)


def gla_mixer(q, k, v, lr, r, gate_w, gate_b, norm_g):
    B, L, _ = q.shape
    n = L // GLA_CHUNK
    f32 = jnp.float32
    g = jax.nn.log_sigmoid((lr @ gate_w + gate_b).astype(f32)) / GLA_TAU

    def heads(t, d):
        return t.astype(f32).reshape(B, n, GLA_CHUNK, GLA_HEADS, d).transpose(0, 3, 1, 2, 4)

    qh = heads(q, GLA_DK) * GLA_DK ** -0.5
    kh = heads(k, GLA_DK)
    vh = heads(v, GLA_DV)
    bcum = jnp.cumsum(heads(g, GLA_DK), axis=3)
    blast = bcum[:, :, :, -1:, :]
    q_dec = qh * jnp.exp(bcum)
    causal = jnp.tril(jnp.ones((GLA_CHUNK, GLA_CHUNK), dtype=bool))
    att = jnp.einsum('bhncd,bhnsd->bhncs', q_dec, kh * jnp.exp(-bcum))
    att = jnp.where(causal, att, 0.0)
    o_intra = jnp.einsum('bhncs,bhnse->bhnce', att, vh)
    upd = jnp.einsum('bhncd,bhnce->bhnde', kh * jnp.exp(blast - bcum), vh)
    decay = jnp.exp(blast[:, :, :, 0, :])

    def step(S, inp):
        dec, u = inp
        return dec[..., None] * S + u, S

    S0 = jnp.zeros((B, GLA_HEADS, GLA_DK, GLA_DV), f32)
    _, S_in = lax.scan(step, S0, (jnp.moveaxis(decay, 2, 0), jnp.moveaxis(upd, 2, 0)))
    S_in = jnp.moveaxis(S_in, 0, 2)
    o = o_intra + jnp.einsum('bhncd,bhnde->bhnce', q_dec, S_in)
    o = o.transpose(0, 2, 3, 1, 4).reshape(B, L, GLA_HEADS, GLA_DV)
    o = rms_norm(o, norm_g).reshape(B, L, GLA_WIDTH) * jax.nn.silu(r.astype(f32))
    return o.astype(q.dtype)


def dsa_mixer(q, k, v, iq, ik, iw, norm_g):
    B, L, _ = q.shape
    f32 = jnp.float32
    n_sel = min(DSA_TOPK, L // 4)
    nb = L // DSA_QBLOCK
    qh = q.reshape(B, L, DSA_HEADS, DSA_HEAD_DIM)
    kh = k.reshape(B, L, DSA_HEADS, DSA_HEAD_DIM)
    vh = v.reshape(B, L, DSA_HEADS, DSA_HEAD_DIM)
    iqh = iq.reshape(B, L, DSA_IDX_HEADS, DSA_IDX_DIM)
    iwh = iw.astype(f32) * DSA_IDX_HEADS ** -0.5
    ikf = ik.astype(f32)
    key_pos = jnp.arange(L)

    def blocks(t):
        return jnp.moveaxis(t.reshape((B, nb, DSA_QBLOCK) + t.shape[2:]), 1, 0)

    def one_block(inp):
        qb, iqb, iwb, qpos = inp
        logits = jnp.einsum('bqhd,bsd->bqhs', iqb.astype(f32), ikf) * DSA_IDX_DIM ** -0.5
        score = jnp.einsum('bqh,bqhs->bqs', iwb, jax.nn.relu(logits))
        admissible = key_pos[None, :] <= qpos[:, None]
        score = jnp.where(admissible[None], score, -jnp.inf)
        _, sel = lax.top_k(score, n_sel)
        valid = sel <= qpos[None, :, None]
        kg = jax.vmap(lambda kb, ib: kb[ib])(kh, sel)
        vg = jax.vmap(lambda vb, ib: vb[ib])(vh, sel)
        s = jnp.einsum('bqhd,bqkhd->bhqk', qb.astype(f32), kg.astype(f32)) * DSA_HEAD_DIM ** -0.5
        s = jnp.where(valid[:, None], s, -jnp.inf)
        p = jax.nn.softmax(s, axis=-1)
        return jnp.einsum('bhqk,bqkhd->bqhd', p, vg.astype(f32))

    qpos = jnp.arange(L).reshape(nb, DSA_QBLOCK)
    out = lax.map(one_block, (blocks(qh), blocks(iqh), blocks(iwh), qpos))
    out = jnp.moveaxis(out, 0, 1).reshape(B, L, DSA_WIDTH)
    return rms_norm(out, norm_g).astype(q.dtype)


def s5_mixer(u, a_re, a_im, log_step, b_re, b_im, c_re, c_im, d, glu_w, glu_b, norm_g):
    B, L, _ = u.shape
    f32 = jnp.float32
    ug = u.astype(f32).reshape(B, L, S5_GROUPS, S5_GROUP_CH)
    dt = jnp.exp(log_step.astype(f32))[:, None]
    ar, ai = a_re.astype(f32), a_im.astype(f32)
    mag = jnp.exp(ar * dt)
    lam_re, lam_im = mag * jnp.cos(ai * dt), mag * jnp.sin(ai * dt)
    den = ar * ar + ai * ai
    nr, ni = lam_re - 1.0, lam_im
    coef_re = (nr * ar + ni * ai) / den
    coef_im = (ni * ar - nr * ai) / den
    br, bi = b_re.astype(f32), b_im.astype(f32)
    bb_re = coef_re[..., None] * br - coef_im[..., None] * bi
    bb_im = coef_re[..., None] * bi + coef_im[..., None] * br
    x_re = jnp.einsum('blgc,gpc->blgp', ug, bb_re)
    x_im = jnp.einsum('blgc,gpc->blgp', ug, bb_im)
    a_re_t = jnp.broadcast_to(lam_re, (1, L) + lam_re.shape)
    a_im_t = jnp.broadcast_to(lam_im, (1, L) + lam_im.shape)

    def combine(e1, e2):
        a1r, a1i, x1r, x1i = e1
        a2r, a2i, x2r, x2i = e2
        return (a2r * a1r - a2i * a1i, a2r * a1i + a2i * a1r,
                a2r * x1r - a2i * x1i + x2r, a2r * x1i + a2i * x1r + x2i)

    _, _, h_re, h_im = lax.associative_scan(combine, (a_re_t, a_im_t, x_re, x_im), axis=1)
    y = (jnp.einsum('blgp,gcp->blgc', h_re, c_re.astype(f32))
         - jnp.einsum('blgp,gcp->blgc', h_im, c_im.astype(f32)))
    y = (y + d.astype(f32) * ug).reshape(B, L, S5_WIDTH)
    y = jax.nn.gelu(y)
    y = y * jax.nn.sigmoid(y @ glu_w.astype(f32) + glu_b.astype(f32))
    return rms_norm(y, norm_g).astype(u.dtype)


def token_mix(h, w_in, gla_gate_w, gla_gate_b, gla_norm_g, dsa_norm_g, s5_a_re, s5_a_im,
              s5_log_step, s5_b_re, s5_b_im, s5_c_re, s5_c_im, s5_d, s5_glu_w, s5_glu_b,
              s5_norm_g, w_out):
    proj = h @ w_in
    offsets = np.cumsum(IN_SIZES)[:-1].tolist()
    (g_q, g_k, g_v, g_lr, g_r, d_q, d_k, d_v, i_q, i_k, i_w, s_u) = jnp.split(proj, offsets, axis=-1)
    o_gla = gla_mixer(g_q, g_k, g_v, g_lr, g_r, gla_gate_w, gla_gate_b, gla_norm_g)
    o_dsa = dsa_mixer(d_q, d_k, d_v, i_q, i_k, i_w, dsa_norm_g)
    o_s5 = s5_mixer(s_u, s5_a_re, s5_a_im, s5_log_step, s5_b_re, s5_b_im, s5_c_re, s5_c_im,
                    s5_d, s5_glu_w, s5_glu_b, s5_norm_g)
    return jnp.concatenate([o_gla, o_dsa, o_s5], axis=-1) @ w_out


def hier_moe(h, wg, bg, we, be, w_gate, w_up, w_down):
    B, L, D = h.shape
    T = B * L
    f32 = jnp.float32
    xt = h.reshape(T, D)
    grp_prob = jax.nn.softmax((xt @ wg + bg).astype(f32), axis=-1)
    grp_p, grp_i = lax.top_k(grp_prob, 1)
    exp_logits = (xt @ we + be).astype(f32).reshape(T, MOE_GROUPS, MOE_EXPERTS_PER_GROUP)
    in_grp = jnp.take_along_axis(exp_logits, grp_i[:, :, None], axis=1)[:, 0]
    top_l, top_j = lax.top_k(in_grp, MOE_TOPK)
    weights = grp_p * jax.nn.softmax(top_l, axis=-1)
    expert = grp_i * MOE_EXPERTS_PER_GROUP + top_j
    N = T * MOE_TOPK
    flat_e = expert.reshape(N)
    flat_tok = jnp.repeat(jnp.arange(T), MOE_TOPK)
    flat_w = weights.reshape(N)
    order = jnp.argsort(flat_e)
    se = flat_e[order]
    tok_sorted = flat_tok[order]
    counts = jnp.bincount(flat_e, length=MOE_EXPERTS)
    padded = (counts + MOE_BLOCK - 1) // MOE_BLOCK * MOE_BLOCK
    start = jnp.cumsum(counts) - counts
    pend = jnp.cumsum(padded)
    pstart = pend - padded
    dest = pstart[se] + jnp.arange(N) - start[se]
    n_blocks = -(-N // MOE_BLOCK) + MOE_EXPERTS
    rows = jnp.zeros((n_blocks * MOE_BLOCK, D), h.dtype).at[dest].set(xt[tok_sorted])
    block_e = jnp.minimum(jnp.searchsorted(pend, jnp.arange(n_blocks) * MOE_BLOCK, side='right'),
                          MOE_EXPERTS - 1)

    def run(inp):
        xb, e = inp
        hid = jax.nn.silu(xb @ w_gate[e]) * (xb @ w_up[e])
        return hid @ w_down[e]

    out = lax.map(run, (rows.reshape(n_blocks, MOE_BLOCK, D), block_e)).reshape(-1, D)
    contrib = out[dest] * flat_w[order][:, None].astype(out.dtype)
    y = jnp.zeros((T, D), out.dtype).at[tok_sorted].add(contrib)
    return y.reshape(B, L, D)


def setup_inputs(seed: int = 0) -> dict:
    key = jax.random.key(seed)
    ks = iter(jax.random.split(key, 40))

    def nrm(shape, s):
        return jax.random.normal(next(ks), shape, jnp.float32) * s

    D = D_MODEL
    n = jnp.arange(S5_STATE, dtype=jnp.float32)
    return {
        'x': nrm((BATCH, SEQ, D), 1.0),
        'c': nrm((BATCH, D), 1.0),
        'ada_w': nrm((DEPTH, D, 6 * D), 0.5 * D ** -0.5),
        'ada_b': nrm((DEPTH, 6 * D), 0.02),
        'norm1_g': 1.0 + nrm((DEPTH, D), 0.02),
        'w_in': nrm((DEPTH, D, D_IN), D ** -0.5),
        'gla_gate_w': nrm((DEPTH, GLA_RANK, GLA_QK), GLA_RANK ** -0.5),
        'gla_gate_b': nrm((DEPTH, GLA_QK), 0.1),
        'gla_norm_g': 1.0 + nrm((DEPTH, GLA_DV), 0.02),
        'dsa_norm_g': 1.0 + nrm((DEPTH, DSA_WIDTH), 0.02),
        's5_a_re': -0.5 * (1.0 + nrm((DEPTH, S5_GROUPS, S5_STATE), 0.01)),
        's5_a_im': jnp.broadcast_to(math.pi * n, (DEPTH, S5_GROUPS, S5_STATE)),
        's5_log_step': jax.random.uniform(next(ks), (DEPTH, S5_GROUPS), jnp.float32,
                                          minval=math.log(1e-3), maxval=math.log(1e-1)),
        's5_b_re': nrm((DEPTH, S5_GROUPS, S5_STATE, S5_GROUP_CH), (2 * S5_GROUP_CH) ** -0.5),
        's5_b_im': nrm((DEPTH, S5_GROUPS, S5_STATE, S5_GROUP_CH), (2 * S5_GROUP_CH) ** -0.5),
        's5_c_re': nrm((DEPTH, S5_GROUPS, S5_GROUP_CH, S5_STATE), (2 * S5_STATE) ** -0.5),
        's5_c_im': nrm((DEPTH, S5_GROUPS, S5_GROUP_CH, S5_STATE), (2 * S5_STATE) ** -0.5),
        's5_d': nrm((DEPTH, S5_GROUPS, S5_GROUP_CH), 0.5),
        's5_glu_w': nrm((DEPTH, S5_WIDTH, S5_WIDTH), S5_WIDTH ** -0.5),
        's5_glu_b': nrm((DEPTH, S5_WIDTH), 0.02),
        's5_norm_g': 1.0 + nrm((DEPTH, S5_WIDTH), 0.02),
        'w_out': nrm((DEPTH, D_MIX, D), D_MIX ** -0.5),
        'norm2_g': 1.0 + nrm((DEPTH, D), 0.02),
        'router_grp_w': nrm((DEPTH, D, MOE_GROUPS), D ** -0.5),
        'router_grp_b': nrm((DEPTH, MOE_GROUPS), 0.01),
        'router_exp_w': nrm((DEPTH, D, MOE_EXPERTS), D ** -0.5),
        'router_exp_b': nrm((DEPTH, MOE_EXPERTS), 0.01),
        'exp_w_gate': nrm((DEPTH, MOE_EXPERTS, D, MOE_HIDDEN), D ** -0.5),
        'exp_w_up': nrm((DEPTH, MOE_EXPERTS, D, MOE_HIDDEN), D ** -0.5),
        'exp_w_down': nrm((DEPTH, MOE_EXPERTS, MOE_HIDDEN, D), MOE_HIDDEN ** -0.5),
        'final_norm_g': 1.0 + nrm((D,), 0.02),
    }


def reference(x, c, ada_w, ada_b, norm1_g, w_in, gla_gate_w, gla_gate_b, gla_norm_g, dsa_norm_g,
              s5_a_re, s5_a_im, s5_log_step, s5_b_re, s5_b_im, s5_c_re, s5_c_im, s5_d, s5_glu_w,
              s5_glu_b, s5_norm_g, w_out, norm2_g, router_grp_w, router_grp_b, router_exp_w,
              router_exp_b, exp_w_gate, exp_w_up, exp_w_down, final_norm_g):
    cond = jax.nn.silu(c)
    for l in range(DEPTH):
        mod = (cond @ ada_w[l] + ada_b[l])[:, None, :]
        sh1, sc1, gt1, sh2, sc2, gt2 = jnp.split(mod, 6, axis=-1)
        h = rms_norm(x, norm1_g[l]) * (1.0 + sc1) + sh1
        x = x + gt1 * token_mix(h, w_in[l], gla_gate_w[l], gla_gate_b[l], gla_norm_g[l],
                                dsa_norm_g[l], s5_a_re[l], s5_a_im[l], s5_log_step[l],
                                s5_b_re[l], s5_b_im[l], s5_c_re[l], s5_c_im[l], s5_d[l],
                                s5_glu_w[l], s5_glu_b[l], s5_norm_g[l], w_out[l])
        h = rms_norm(x, norm2_g[l]) * (1.0 + sc2) + sh2
        x = x + gt2 * hier_moe(h, router_grp_w[l], router_grp_b[l], router_exp_w[l],
                               router_exp_b[l], exp_w_gate[l], exp_w_up[l], exp_w_down[l])
    return rms_norm(x, final_norm_g)
```

```python
import functools
import math

import jax
import jax.numpy as jnp
import numpy as np
from jax import lax
from jax.experimental import pallas as pl
from jax.experimental.pallas import tpu as pltpu

F32 = jnp.float32
BF16 = jnp.bfloat16
I32 = jnp.int32

D_MODEL = 1024
GLA_HEADS = 4
GLA_DV = 96
GLA_DK = 48
GLA_RANK = 16
GLA_TAU = 16.0
GLA_CHUNK = 64
DSA_HEADS = 4
DSA_HEAD_DIM = 64
DSA_WIDTH = DSA_HEADS * DSA_HEAD_DIM
DSA_IDX_HEADS = 8
DSA_IDX_DIM = 32
DSA_TOPK = 256
S5_GROUPS = 24
S5_GROUP_CH = 16
S5_STATE = 64
S5_WIDTH = S5_GROUPS * S5_GROUP_CH
MOE_GROUPS = 4
MOE_EXPERTS_PER_GROUP = 8
MOE_EXPERTS = MOE_GROUPS * MOE_EXPERTS_PER_GROUP
MOE_HIDDEN = 512
RMS_EPS = 1e-6
IN_SIZES = (192, 192, 384, 16, 384, 256, 256, 256, 256, 32, 8, 384)

LANES = 128
SUBLANES = 8
VMEM_LIMIT = 56 * 1024 * 1024

HEAD_PAD = LANES
GLA_W = GLA_HEADS * HEAD_PAD
S5_CHUNK = 16
S5_PAIRS = S5_GROUPS // 2
S5_PAIR_W = 2 * S5_CHUNK * S5_GROUP_CH
MOE_BM = 256
INT_MIN = -2 ** 31


def _cparams(*sem):
    return pltpu.CompilerParams(dimension_semantics=tuple(sem), vmem_limit_bytes=VMEM_LIMIT)


def _sigmoid(x):
    return 1.0 / (1.0 + jnp.exp(-x))


def _rms(x, g, n=None):
    n = x.shape[-1] if n is None else n
    ms = jnp.sum(x * x, axis=-1, keepdims=True) * (1.0 / n)
    return x * lax.rsqrt(ms + RMS_EPS) * g


def _ada_kernel(c_ref, w_ref, b_ref, o_ref):
    c = c_ref[...]
    cond = (c * _sigmoid(c)).astype(BF16)
    o_ref[...] = jnp.dot(cond, w_ref[...].astype(BF16), preferred_element_type=F32) + b_ref[...]


def _ada_mod(c, ada_w, ada_b):
    depth, d, d6 = ada_w.shape
    nblk = d6 // d
    bp = -(-c.shape[0] // SUBLANES) * SUBLANES
    cp = jnp.pad(c, ((0, bp - c.shape[0]), (0, 0)))
    out = pl.pallas_call(
        _ada_kernel,
        grid=(depth, nblk),
        in_specs=[
            pl.BlockSpec((bp, d), lambda l, j: (0, 0)),
            pl.BlockSpec((None, d, d), lambda l, j: (l, 0, j)),
            pl.BlockSpec((None, None, 1, d), lambda l, j: (l, j, 0, 0)),
        ],
        out_specs=pl.BlockSpec((None, None, bp, d), lambda l, j: (l, j, 0, 0)),
        out_shape=jax.ShapeDtypeStruct((depth, nblk, bp, d), F32),
        compiler_params=_cparams("arbitrary", "arbitrary"),
        name="ada_mod",
    )(cp, ada_w, ada_b.reshape(depth, nblk, 1, d))
    return out[:, :, : c.shape[0], :]


PROJ_OUTS = (
    ("gq", GLA_W, BF16), ("gk", GLA_W, BF16), ("gv", GLA_W, BF16), ("gr", GLA_W, BF16),
    ("glr", LANES, BF16),
    ("dq", DSA_WIDTH, BF16), ("dk", DSA_WIDTH, BF16), ("dv", DSA_WIDTH, BF16),
    ("iq", DSA_IDX_HEADS * DSA_IDX_DIM, BF16), ("ik", DSA_IDX_HEADS * DSA_IDX_DIM, BF16),
    ("iw", LANES, F32), ("su", S5_WIDTH, F32),
)
PROJ_COLS = sum(w for _, w, _ in PROJ_OUTS)


def _pad_heads(w, heads, dim):
    d = w.shape[0]
    w = w.reshape(d, heads, dim)
    return jnp.pad(w, ((0, 0), (0, 0), (0, HEAD_PAD - dim))).reshape(d, heads * HEAD_PAD)


def _pad_cols(w, width):
    return jnp.pad(w, ((0, 0), (0, width - w.shape[1])))


def _layout_w_in(w_in):
    offs = np.cumsum((0,) + IN_SIZES)
    p = [w_in[:, offs[i]:offs[i + 1]] for i in range(len(IN_SIZES))]
    cols = [
        _pad_heads(p[0], GLA_HEADS, GLA_DK), _pad_heads(p[1], GLA_HEADS, GLA_DK),
        _pad_heads(p[2], GLA_HEADS, GLA_DV), _pad_heads(p[4], GLA_HEADS, GLA_DV),
        _pad_cols(p[3], LANES),
        p[5], p[6], p[7], p[8], jnp.tile(p[9], (1, DSA_IDX_HEADS)),
        _pad_cols(p[10], LANES), p[11],
    ]
    return jnp.concatenate(cols, axis=1).astype(BF16)


def _proj_kernel(x_ref, g_ref, sc_ref, sh_ref, w_ref, *o_refs):
    x = x_ref[...]
    h = _rms(x, g_ref[...]) * (1.0 + sc_ref[...]) + sh_ref[...]
    hb = h.astype(BF16)
    c0 = 0
    for o_ref, (_, width, _) in zip(o_refs, PROJ_OUTS):
        o_ref[...] = jnp.dot(hb, w_ref[:, c0:c0 + width], preferred_element_type=F32).astype(o_ref.dtype)
        c0 += width


def _in_proj(x2, norm_g, sc, sh, w_lay, seq, tm=512):
    t, d = x2.shape
    per_b = seq // tm
    outs = pl.pallas_call(
        _proj_kernel,
        grid=(t // tm,),
        in_specs=[
            pl.BlockSpec((tm, d), lambda i: (i, 0)),
            pl.BlockSpec((1, d), lambda i: (0, 0)),
            pl.BlockSpec((None, 1, d), lambda i: (i // per_b, 0, 0)),
            pl.BlockSpec((None, 1, d), lambda i: (i // per_b, 0, 0)),
            pl.BlockSpec((d, PROJ_COLS), lambda i: (0, 0)),
        ],
        out_specs=[pl.BlockSpec((tm, w), lambda i: (i, 0)) for _, w, _ in PROJ_OUTS],
        out_shape=[jax.ShapeDtypeStruct((t, w), dt) for _, w, dt in PROJ_OUTS],
        compiler_params=_cparams("arbitrary"),
        name="in_proj",
    )(x2, norm_g.reshape(1, d), sc[:, None, :], sh[:, None, :], w_lay)
    return {name: o for (name, _, _), o in zip(PROJ_OUTS, outs)}


def _gla_kernel(q_ref, k_ref, v_ref, r_ref, lr_ref, gw_ref, gb_ref, ng_ref, o_ref, s_ref, *, rows):
    @pl.when(pl.program_id(1) == 0)
    def _():
        s_ref[...] = jnp.zeros_like(s_ref)

    c = GLA_CHUNK
    gl = jnp.dot(lr_ref[...], gw_ref[...], preferred_element_type=F32) + gb_ref[...]
    g = (jnp.minimum(gl, 0.0) - jnp.log1p(jnp.exp(-jnp.abs(gl)))) * (1.0 / GLA_TAU)
    g1 = g.astype(BF16)
    rem = g - g1.astype(F32)
    g2 = rem.astype(BF16)
    g3 = (rem - g2.astype(F32)).astype(BF16)
    ri = lax.broadcasted_iota(I32, (c, c), 0)
    ci = lax.broadcasted_iota(I32, (c, c), 1)
    causal = ri >= ci
    tri = causal.astype(BF16)
    ng = ng_ref[...]
    for ch in range(rows // c):
        rs = slice(ch * c, (ch + 1) * c)
        bcum = (jnp.dot(tri, g1[rs], preferred_element_type=F32)
                + jnp.dot(tri, g2[rs], preferred_element_type=F32)
                + jnp.dot(tri, g3[rs], preferred_element_type=F32))
        blast = bcum[c - 1:c, :]
        e_pos = jnp.exp(bcum)
        kf = k_ref[rs, :].astype(F32)
        qd = (q_ref[rs, :].astype(F32) * (GLA_DK ** -0.5) * e_pos).astype(BF16)
        kd = (kf * jnp.exp(-bcum)).astype(BF16)
        ku = (kf * jnp.exp(blast - bcum)).astype(BF16)
        dec = jnp.exp(blast)
        for h in range(GLA_HEADS):
            hs = slice(h * HEAD_PAD, (h + 1) * HEAD_PAD)
            vh = v_ref[rs, hs]
            att = lax.dot_general(qd[:, hs], kd[:, hs], (((1,), (1,)), ((), ())),
                                  preferred_element_type=F32)
            att = jnp.where(causal, att, 0.0).astype(BF16)
            st = s_ref[h]
            o = jnp.dot(att, vh, preferred_element_type=F32)
            o = o + lax.dot_general(qd[:, hs], st.astype(BF16), (((1,), (1,)), ((), ())),
                                    preferred_element_type=F32)
            upd = lax.dot_general(vh, ku[:, hs], (((0,), (0,)), ((), ())),
                                  preferred_element_type=F32)
            s_ref[h] = st * dec[:, hs] + upd
            y = _rms(o, ng[:, hs], GLA_DV)
            rr = r_ref[rs, hs].astype(F32)
            o_ref[rs, hs] = (y * (rr * _sigmoid(rr))).astype(o_ref.dtype)


def _gla(p, gate_w, gate_b, norm_g, batch, seq, rows=512):
    gw = _pad_cols(jnp.pad(_pad_heads(gate_w, GLA_HEADS, GLA_DK), ((0, LANES - GLA_RANK), (0, 0))), GLA_W)
    gb = _pad_heads(gate_b.reshape(1, -1), GLA_HEADS, GLA_DK)
    ng = _pad_heads(jnp.tile(norm_g.reshape(1, GLA_DV), (1, GLA_HEADS)), GLA_HEADS, GLA_DV)
    rows = min(rows, seq)
    blk = lambda w: pl.BlockSpec((None, rows, w), lambda b, j: (b, j, 0))
    full = lambda a: pl.BlockSpec(a.shape, lambda b, j: (0, 0))
    r3 = lambda a: a.reshape(batch, seq, a.shape[-1])
    out = pl.pallas_call(
        functools.partial(_gla_kernel, rows=rows),
        grid=(batch, seq // rows),
        in_specs=[blk(GLA_W), blk(GLA_W), blk(GLA_W), blk(GLA_W), blk(LANES),
                  full(gw), full(gb), full(ng)],
        out_specs=blk(GLA_W),
        out_shape=jax.ShapeDtypeStruct((batch, seq, GLA_W), BF16),
        scratch_shapes=[pltpu.VMEM((GLA_HEADS, HEAD_PAD, HEAD_PAD), F32)],
        compiler_params=_cparams("arbitrary", "arbitrary"),
        name="gla",
    )(r3(p["gq"]), r3(p["gk"]), r3(p["gv"]), r3(p["gr"]), r3(p["glr"]), gw.astype(BF16), gb, ng)
    return out.reshape(batch * seq, GLA_W)


def _dsa_kernel(iq_ref, ik_ref, iw_ref, q_ref, k_ref, v_ref, ng_ref, o_ref,
                qs_ref, keys_ref, acc_ref, *, tq, kc, n_sel):
    qi = pl.program_id(1)
    nkc = ((qi + 1) * tq + kc - 1) // kc
    lane_w = lax.broadcasted_iota(I32, (tq, DSA_WIDTH), 1)
    iq = iq_ref[...]
    for h in range(DSA_IDX_HEADS):
        m = (lane_w >= h * DSA_IDX_DIM) & (lane_w < (h + 1) * DSA_IDX_DIM)
        qs_ref[h * tq:(h + 1) * tq, :] = jnp.where(m, iq, jnp.zeros_like(iq))
    iw = iw_ref[...] * (DSA_IDX_HEADS ** -0.5)
    lane = lax.broadcasted_iota(I32, (tq, LANES), 1)
    w_cols = [jnp.sum(jnp.where(lane == h, iw, 0.0), axis=1, keepdims=True) for h in range(DSA_IDX_HEADS)]
    qpos = qi * tq + lax.broadcasted_iota(I32, (tq, kc), 0)
    kcol = lax.broadcasted_iota(I32, (tq, kc), 1)

    def score_body(c, carry):
        k0 = pl.multiple_of(c * kc, kc)
        lg = lax.dot_general(qs_ref[...], ik_ref[pl.ds(k0, kc), :], (((1,), (1,)), ((), ())),
                             preferred_element_type=F32) * (DSA_IDX_DIM ** -0.5)
        sc = jnp.zeros((tq, kc), F32)
        for h in range(DSA_IDX_HEADS):
            sc = sc + w_cols[h] * jnp.maximum(lg[h * tq:(h + 1) * tq, :], 0.0)
        sc = sc + 0.0
        bits = lax.bitcast_convert_type(sc, I32)
        key = jnp.where(bits < 0, bits ^ jnp.int32(0x7FFFFFFF), bits)
        key = jnp.where(kcol + k0 <= qpos, key, jnp.int32(INT_MIN))
        keys_ref[c] = key
        return carry

    lax.fori_loop(0, nkc, score_body, 0)

    def count_ge(cand):
        candb = jnp.broadcast_to(cand, (tq, LANES))

        def body(c, acc):
            blk = keys_ref[c]
            for j in range(kc // LANES):
                acc = acc + jnp.where(blk[:, j * LANES:(j + 1) * LANES] >= candb, 1.0, 0.0)
            return acc

        acc = lax.fori_loop(0, nkc, body, jnp.zeros((tq, LANES), F32))
        return jnp.sum(acc, axis=1, keepdims=True)

    need = jnp.float32(n_sel)
    t0 = jnp.full((tq, 1), INT_MIN, I32)
    zero = jnp.zeros((tq, 1), I32)
    t0 = jnp.where(count_ge(zero) >= need, zero, t0)

    def bit_body(i, t):
        cand = t | (jnp.int32(1) << (30 - i))
        return jnp.where(count_ge(cand) >= need, cand, t)

    thr = lax.fori_loop(0, 31, bit_body, t0)
    thr = jnp.maximum(thr, jnp.int32(INT_MIN + 1))
    n_gt = count_ge(thr + 1)
    n_tie = need - n_gt

    acc_ref[...] = jnp.zeros_like(acc_ref)
    ui = lax.broadcasted_iota(I32, (kc, kc), 0)
    uj = lax.broadcasted_iota(I32, (kc, kc), 1)
    before = (ui < uj).astype(BF16)
    q = q_ref[...]
    qh = []
    for h in range(DSA_HEADS):
        m = (lane_w >= h * DSA_HEAD_DIM) & (lane_w < (h + 1) * DSA_HEAD_DIM)
        qh.append(jnp.where(m, q, jnp.zeros_like(q)))
    neg = jnp.float32(-1e30)

    def att_body(c, carry):
        seen, ms, ls = carry
        k0 = pl.multiple_of(c * kc, kc)
        key = keys_ref[c]
        eq = key == thr
        eqf = jnp.where(eq, 1.0, 0.0)
        rank = seen + jnp.dot(eqf.astype(BF16), before, preferred_element_type=F32)
        sel = (key > thr) | (eq & (rank < n_tie))
        seen = seen + jnp.sum(eqf, axis=1, keepdims=True)
        kblk = k_ref[pl.ds(k0, kc), :]
        vblk = v_ref[pl.ds(k0, kc), :]
        new_ms, new_ls = [], []
        for h in range(DSA_HEADS):
            s = lax.dot_general(qh[h], kblk, (((1,), (1,)), ((), ())),
                                preferred_element_type=F32) * (DSA_HEAD_DIM ** -0.5)
            s = jnp.where(sel, s, neg)
            m_new = jnp.maximum(ms[h], jnp.max(s, axis=1, keepdims=True))
            alpha = jnp.exp(ms[h] - m_new)
            p = jnp.where(sel, jnp.exp(s - m_new), 0.0)
            new_ls.append(alpha * ls[h] + jnp.sum(p, axis=1, keepdims=True))
            new_ms.append(m_new)
            acc_ref[h] = alpha * acc_ref[h] + jnp.dot(p.astype(BF16), vblk, preferred_element_type=F32)
        return seen, tuple(new_ms), tuple(new_ls)

    init = (jnp.zeros((tq, 1), F32),
            tuple(jnp.full((tq, 1), neg, F32) for _ in range(DSA_HEADS)),
            tuple(jnp.zeros((tq, 1), F32) for _ in range(DSA_HEADS)))
    _, _, ls = lax.fori_loop(0, nkc, att_body, init)
    out = jnp.zeros((tq, DSA_WIDTH), F32)
    for h in range(DSA_HEADS):
        m = (lane_w >= h * DSA_HEAD_DIM) & (lane_w < (h + 1) * DSA_HEAD_DIM)
        out = out + jnp.where(m, acc_ref[h] / ls[h], 0.0)
    o_ref[...] = _rms(out, ng_ref[...]).astype(o_ref.dtype)


def _dsa(p, norm_g, batch, seq, tq=128, kc=256):
    n_sel = min(DSA_TOPK, seq // 4)
    kc = min(kc, seq)
    r3 = lambda a: a.reshape(batch, seq, a.shape[-1])
    qblk = lambda w: pl.BlockSpec((None, tq, w), lambda b, i: (b, i, 0))
    kfull = lambda w: pl.BlockSpec((None, seq, w), lambda b, i: (b, 0, 0))
    out = pl.pallas_call(
        functools.partial(_dsa_kernel, tq=tq, kc=kc, n_sel=n_sel),
        grid=(batch, seq // tq),
        in_specs=[qblk(DSA_WIDTH), kfull(DSA_WIDTH), qblk(LANES), qblk(DSA_WIDTH),
                  kfull(DSA_WIDTH), kfull(DSA_WIDTH), pl.BlockSpec((1, DSA_WIDTH), lambda b, i: (0, 0))],
        out_specs=qblk(DSA_WIDTH),
        out_shape=jax.ShapeDtypeStruct((batch, seq, DSA_WIDTH), BF16),
        scratch_shapes=[
            pltpu.VMEM((DSA_IDX_HEADS * tq, DSA_WIDTH), BF16),
            pltpu.VMEM((seq // kc, tq, kc), I32),
            pltpu.VMEM((DSA_HEADS, tq, DSA_WIDTH), F32),
        ],
        compiler_params=_cparams("arbitrary", "arbitrary"),
        name="dsa",
    )(r3(p["iq"]), r3(p["ik"]), r3(p["iw"]), r3(p["dq"]), r3(p["dk"]), r3(p["dv"]),
      norm_g.reshape(1, DSA_WIDTH))
    return out.reshape(batch * seq, DSA_WIDTH)


def _s5_tables(a_re, a_im, log_step, b_re, b_im, c_re, c_im):
    hp = lax.Precision.HIGHEST
    s = S5_CHUNK
    dt = jnp.exp(log_step)[:, None]
    mag = jnp.exp(a_re * dt)
    lam_re, lam_im = mag * jnp.cos(a_im * dt), mag * jnp.sin(a_im * dt)
    den = a_re * a_re + a_im * a_im
    nr, ni = lam_re - 1.0, lam_im
    coef_re = (nr * a_re + ni * a_im) / den
    coef_im = (ni * a_re - nr * a_im) / den
    bb_re = coef_re[..., None] * b_re - coef_im[..., None] * b_im
    bb_im = coef_re[..., None] * b_im + coef_im[..., None] * b_re
    j = jnp.arange(s + 1, dtype=F32)[:, None, None]
    pmag = jnp.exp(a_re * dt * j)
    pw_re, pw_im = pmag * jnp.cos(a_im * dt * j), pmag * jnp.sin(a_im * dt * j)
    e_re = c_re[None] * pw_re[:, :, None, :] - c_im[None] * pw_im[:, :, None, :]
    e_im = c_re[None] * pw_im[:, :, None, :] + c_im[None] * pw_re[:, :, None, :]
    kk = (jnp.einsum("jgcp,gpd->jgcd", e_re, bb_re, precision=hp)
          - jnp.einsum("jgcp,gpd->jgcd", e_im, bb_im, precision=hp))
    lag = np.arange(s)[None, :] - np.arange(s)[:, None]
    toe = jnp.where((lag >= 0)[:, :, None, None, None], kk[np.clip(lag, 0, s)], 0.0)
    w_toe = toe.transpose(2, 0, 4, 1, 3).reshape(S5_GROUPS, s * S5_GROUP_CH, s * S5_GROUP_CH)
    rp_re, rp_im = pw_re[s - 1 - np.arange(s)], pw_im[s - 1 - np.arange(s)]
    x_re = rp_re[..., None] * bb_re[None] - rp_im[..., None] * bb_im[None]
    x_im = rp_re[..., None] * bb_im[None] + rp_im[..., None] * bb_re[None]
    w1_re = x_re.transpose(1, 0, 3, 2).reshape(S5_GROUPS, s * S5_GROUP_CH, S5_STATE)
    w1_im = x_im.transpose(1, 0, 3, 2).reshape(S5_GROUPS, s * S5_GROUP_CH, S5_STATE)
    f_re, f_im = e_re[1:], e_im[1:]
    w2_re = f_re.transpose(1, 3, 0, 2).reshape(S5_GROUPS, S5_STATE, s * S5_GROUP_CH)
    w2_im = (-f_im).transpose(1, 3, 0, 2).reshape(S5_GROUPS, S5_STATE, s * S5_GROUP_CH)

    def pair_diag(w):
        g, a, b = w.shape
        w = w.reshape(g // 2, 2, a, b)
        z = jnp.zeros_like(w[:, 0])
        return jnp.concatenate([jnp.concatenate([w[:, 0], z], axis=2),
                                jnp.concatenate([z, w[:, 1]], axis=2)], axis=1)

    tabs = dict(toe=pair_diag(w_toe), w1_re=pair_diag(w1_re), w1_im=pair_diag(w1_im),
                w2_re=pair_diag(w2_re), w2_im=pair_diag(w2_im))
    tabs = {k: v.astype(BF16) for k, v in tabs.items()}
    tabs["a_re"], tabs["a_im"] = pw_re[s].reshape(-1), pw_im[s].reshape(-1)
    return tabs


def _s5_state_in_kernel(u_ref, wr_ref, wi_ref, xr_ref, xi_ref):
    u = u_ref[...]
    xr_ref[...] = jnp.dot(u, wr_ref[...], preferred_element_type=F32)
    xi_ref[...] = jnp.dot(u, wi_ref[...], preferred_element_type=F32)


def _s5_scan_kernel(xr_ref, xi_ref, ar_ref, ai_ref, hr_ref, hi_ref, cr_ref, ci_ref, *, steps):
    @pl.when(pl.program_id(0) == 0)
    def _():
        cr_ref[...] = jnp.zeros_like(cr_ref)
        ci_ref[...] = jnp.zeros_like(ci_ref)

    ar, ai = ar_ref[...], ai_ref[...]

    def body(n, carry):
        hr, hi = carry
        hr_ref[n] = hr
        hi_ref[n] = hi
        return ar * hr - ai * hi + xr_ref[n], ar * hi + ai * hr + xi_ref[n]

    hr, hi = lax.fori_loop(0, steps, body, (cr_ref[...], ci_ref[...]))
    cr_ref[...] = hr
    ci_ref[...] = hi


def _s5_out_kernel(u_ref, toe_ref, hr_ref, hi_ref, w2r_ref, w2i_ref, y_ref):
    y = jnp.dot(u_ref[...], toe_ref[...], preferred_element_type=F32)
    y = y + jnp.dot(hr_ref[...].astype(BF16), w2r_ref[...], preferred_element_type=F32)
    y = y + jnp.dot(hi_ref[...].astype(BF16), w2i_ref[...], preferred_element_type=F32)
    y_ref[...] = y


def _s5_post_kernel(y_ref, u_ref, d_ref, gw_ref, gb_ref, ng_ref, o_ref):
    z = y_ref[...] + d_ref[...] * u_ref[...]
    z = 0.5 * z * (1.0 + jnp.tanh(math.sqrt(2.0 / math.pi) * (z + 0.044715 * (z * z * z))))
    gate = _sigmoid(jnp.dot(z.astype(BF16), gw_ref[...], preferred_element_type=F32) + gb_ref[...])
    o_ref[...] = _rms(z * gate, ng_ref[...]).astype(o_ref.dtype)


def _s5(u2, tabs, d, glu_w, glu_b, norm_g, batch, seq, tm=512):
    s, cg = S5_CHUNK, S5_GROUP_CH
    nch = seq // s
    rows = nch * batch
    gp = S5_GROUPS * S5_STATE
    up = u2.astype(BF16).reshape(batch, nch, s, S5_PAIRS, 2, cg).transpose(3, 1, 0, 4, 2, 5)
    up = up.reshape(S5_PAIRS, rows, S5_PAIR_W)
    pair_in = pl.BlockSpec((None, rows, S5_PAIR_W), lambda g: (g, 0, 0))
    wspec = lambda a: pl.BlockSpec((None,) + a.shape[1:], lambda g: (g, 0, 0))
    colblk = pl.BlockSpec((rows, LANES), lambda g: (0, g))
    xr, xi = pl.pallas_call(
        _s5_state_in_kernel,
        grid=(S5_PAIRS,),
        in_specs=[pair_in, wspec(tabs["w1_re"]), wspec(tabs["w1_im"])],
        out_specs=[colblk, colblk],
        out_shape=[jax.ShapeDtypeStruct((rows, gp), F32)] * 2,
        compiler_params=_cparams("arbitrary"),
        name="s5_state_in",
    )(up, tabs["w1_re"], tabs["w1_im"])
    parts = SUBLANES // batch
    wl = gp // parts
    slab = lambda a: a.reshape(nch, SUBLANES, wl)
    coef = lambda a: jnp.tile(a.reshape(parts, wl), (batch, 1))
    steps = min(64, nch)
    sblk = pl.BlockSpec((steps, SUBLANES, wl), lambda i: (i, 0, 0))
    cblk = pl.BlockSpec((SUBLANES, wl), lambda i: (0, 0))
    hr, hi = pl.pallas_call(
        functools.partial(_s5_scan_kernel, steps=steps),
        grid=(nch // steps,),
        in_specs=[sblk, sblk, cblk, cblk],
        out_specs=[sblk, sblk],
        out_shape=[jax.ShapeDtypeStruct((nch, SUBLANES, wl), F32)] * 2,
        scratch_shapes=[pltpu.VMEM((SUBLANES, wl), F32)] * 2,
        compiler_params=_cparams("arbitrary"),
        name="s5_scan",
    )(slab(xr), slab(xi), coef(tabs["a_re"]), coef(tabs["a_im"]))
    y = pl.pallas_call(
        _s5_out_kernel,
        grid=(S5_PAIRS,),
        in_specs=[pair_in, wspec(tabs["toe"]), colblk, colblk, wspec(tabs["w2_re"]), wspec(tabs["w2_im"])],
        out_specs=pl.BlockSpec((None, rows, S5_PAIR_W), lambda g: (g, 0, 0)),
        out_shape=jax.ShapeDtypeStruct((S5_PAIRS, rows, S5_PAIR_W), F32),
        compiler_params=_cparams("arbitrary"),
        name="s5_out",
    )(up, tabs["toe"], hr.reshape(rows, gp), hi.reshape(rows, gp), tabs["w2_re"], tabs["w2_im"])
    y = y.reshape(S5_PAIRS, nch, batch, 2, s, cg).transpose(2, 1, 4, 0, 3, 5).reshape(batch * seq, S5_WIDTH)
    t = batch * seq
    tm = min(tm, t)
    tok = pl.BlockSpec((tm, S5_WIDTH), lambda i: (i, 0))
    vec = pl.BlockSpec((1, S5_WIDTH), lambda i: (0, 0))
    return pl.pallas_call(
        _s5_post_kernel,
        grid=(t // tm,),
        in_specs=[tok, tok, vec, pl.BlockSpec((S5_WIDTH, S5_WIDTH), lambda i: (0, 0)), vec, vec],
        out_specs=tok,
        out_shape=jax.ShapeDtypeStruct((t, S5_WIDTH), BF16),
        compiler_params=_cparams("arbitrary"),
        name="s5_post",
    )(y, u2, d.reshape(1, -1), glu_w.astype(BF16), glu_b.reshape(1, -1), norm_g.reshape(1, -1))


def _mix_out_kernel(og_ref, od_ref, os_ref, wg_ref, wd_ref, ws_ref, x_ref, gt_ref, n2_ref, sc_ref,
                    sh_ref, rw1_ref, rw2_ref, rb_ref, x1_ref, h2_ref, route_ref, cnt_ref, carry_ref, *, tm):
    @pl.when(pl.program_id(0) == 0)
    def _():
        carry_ref[...] = jnp.zeros_like(carry_ref)

    mix = jnp.dot(og_ref[...], wg_ref[...], preferred_element_type=F32)
    mix = mix + jnp.dot(od_ref[...], wd_ref[...], preferred_element_type=F32)
    mix = mix + jnp.dot(os_ref[...], ws_ref[...], preferred_element_type=F32)
    x1 = x_ref[...] + gt_ref[...] * mix
    x1_ref[...] = x1
    h2 = _rms(x1, n2_ref[...]) * (1.0 + sc_ref[...]) + sh_ref[...]
    h2_ref[...] = h2
    ha = h2.astype(BF16)
    hb = (h2 - ha.astype(F32)).astype(BF16)
    logit = (jnp.dot(ha, rw1_ref[...], preferred_element_type=F32)
             + jnp.dot(ha, rw2_ref[...], preferred_element_type=F32)
             + jnp.dot(hb, rw1_ref[...], preferred_element_type=F32)) + rb_ref[...]
    lane = lax.broadcasted_iota(I32, (tm, LANES), 1)
    lanef = lane.astype(F32)
    ninf = jnp.float32(-jnp.inf)
    big = jnp.float32(LANES)
    gl = jnp.where(lane < MOE_GROUPS, logit, ninf)
    gmax = jnp.max(gl, axis=1, keepdims=True)
    grp_p = 1.0 / jnp.sum(jnp.exp(gl - gmax), axis=1, keepdims=True)
    grp_i = jnp.min(jnp.where(gl == gmax, lanef, big), axis=1, keepdims=True)
    lo = MOE_GROUPS + MOE_EXPERTS_PER_GROUP * grp_i
    el = jnp.where((lanef >= lo) & (lanef < lo + MOE_EXPERTS_PER_GROUP), logit, ninf)
    l1 = jnp.max(el, axis=1, keepdims=True)
    i1 = jnp.min(jnp.where(el == l1, lanef, big), axis=1, keepdims=True)
    el2 = jnp.where(lanef == i1, ninf, el)
    l2 = jnp.max(el2, axis=1, keepdims=True)
    i2 = jnp.min(jnp.where(el2 == l2, lanef, big), axis=1, keepdims=True)
    e21 = jnp.exp(l2 - l1)
    w1 = grp_p / (1.0 + e21)
    w2 = grp_p * e21 / (1.0 + e21)
    e1 = i1 - MOE_GROUPS
    e2 = i2 - MOE_GROUPS
    oh1 = lanef == e1
    oh2 = lanef == e2
    hot = jnp.where(oh1 | oh2, 1.0, 0.0)
    ri = lax.broadcasted_iota(I32, (tm, tm), 0)
    ci = lax.broadcasted_iota(I32, (tm, tm), 1)
    before = (ci < ri).astype(BF16)
    prior = carry_ref[...] + jnp.dot(before, hot.astype(BF16), preferred_element_type=F32)
    r1 = jnp.sum(jnp.where(oh1, prior, 0.0), axis=1, keepdims=True)
    r2 = jnp.sum(jnp.where(oh2, prior, 0.0), axis=1, keepdims=True)
    total = carry_ref[...] + jnp.sum(hot, axis=0, keepdims=True)
    carry_ref[...] = total
    cnt_ref[...] = jnp.broadcast_to(total, cnt_ref.shape)
    route = jnp.where(lane == 0, e1, jnp.where(lane == 1, e2, jnp.where(lane == 2, w1, jnp.where(
        lane == 3, w2, jnp.where(lane == 4, r1, jnp.where(lane == 5, r2, 0.0))))))
    route_ref[...] = route


def _mix_out(o_gla, o_dsa, o_s5, w_out, x2, gt1, norm2_g, sc2, sh2, rgw, rgb, rew, reb, seq, tm=256):
    t, d = x2.shape
    per_b = seq // tm
    wg = jnp.pad(w_out[:GLA_HEADS * GLA_DV].reshape(GLA_HEADS, GLA_DV, d),
                 ((0, 0), (0, HEAD_PAD - GLA_DV), (0, 0))).reshape(GLA_W, d).astype(BF16)
    wd = w_out[GLA_HEADS * GLA_DV:GLA_HEADS * GLA_DV + DSA_WIDTH].astype(BF16)
    ws = w_out[GLA_HEADS * GLA_DV + DSA_WIDTH:].astype(BF16)
    rw = _pad_cols(jnp.concatenate([rgw, rew], axis=1), LANES)
    rb = _pad_cols(jnp.concatenate([rgb, reb]).reshape(1, -1), LANES)
    rw1 = rw.astype(BF16)
    rw2 = (rw - rw1.astype(F32)).astype(BF16)
    tok = lambda w: pl.BlockSpec((tm, w), lambda i: (i, 0))
    full = lambda a: pl.BlockSpec(a.shape, lambda i: (0, 0))
    perb = pl.BlockSpec((None, 1, d), lambda i: (i // per_b, 0, 0))
    vec = pl.BlockSpec((1, d), lambda i: (0, 0))
    return pl.pallas_call(
        functools.partial(_mix_out_kernel, tm=tm),
        grid=(t // tm,),
        in_specs=[tok(GLA_W), tok(DSA_WIDTH), tok(S5_WIDTH), full(wg), full(wd), full(ws), tok(d),
                  perb, vec, perb, perb, full(rw1), full(rw2), full(rb)],
        out_specs=[tok(d), tok(d), tok(LANES), pl.BlockSpec((SUBLANES, LANES), lambda i: (0, 0))],
        out_shape=[jax.ShapeDtypeStruct((t, d), F32), jax.ShapeDtypeStruct((t, d), F32),
                   jax.ShapeDtypeStruct((t, LANES), F32), jax.ShapeDtypeStruct((SUBLANES, LANES), F32)],
        scratch_shapes=[pltpu.VMEM((1, LANES), F32)],
        compiler_params=_cparams("arbitrary"),
        name="mix_out",
    )(o_gla, o_dsa, o_s5, wg, wd, ws, x2, gt1[:, None, :], norm2_g.reshape(1, d), sc2[:, None, :],
      sh2[:, None, :], rw1, rw2, rb)


def _dispatch_kernel(dest_ref, h_ref, zero_ref, out_ref, sem, *, td):
    del zero_ref
    base = pl.program_id(0) * td

    def row_copy(r, which):
        return pltpu.make_async_copy(h_ref.at[pl.ds(base + r, 1), :],
                                     out_ref.at[pl.ds(dest_ref[which, r], 1), :], sem)

    def start(r, c):
        row_copy(r, 0).start()
        row_copy(r, 1).start()
        return c

    def wait(r, c):
        row_copy(r, 0).wait()
        row_copy(r, 1).wait()
        return c

    lax.fori_loop(0, td, start, 0)
    lax.fori_loop(0, td, wait, 0)


def _dispatch(h2, dest, n_rows, td=256):
    t, d = h2.shape
    dest3 = dest.reshape(2, t // td, td).transpose(1, 0, 2)
    zeros = jnp.zeros((n_rows, d), F32)
    return pl.pallas_call(
        functools.partial(_dispatch_kernel, td=td),
        grid=(t // td,),
        in_specs=[pl.BlockSpec((None, 2, td), lambda i: (i, 0, 0), memory_space=pltpu.SMEM),
                  pl.BlockSpec(memory_space=pl.ANY), pl.BlockSpec(memory_space=pl.ANY)],
        out_specs=pl.BlockSpec(memory_space=pl.ANY),
        out_shape=jax.ShapeDtypeStruct((n_rows, d), F32),
        scratch_shapes=[pltpu.SemaphoreType.DMA(())],
        input_output_aliases={2: 0},
        compiler_params=_cparams("arbitrary"),
        name="moe_dispatch",
    )(dest3, h2, zeros)


def _expert_kernel(be_ref, nb_ref, x_ref, wg_ref, wu_ref, wd_ref, o_ref):
    del be_ref
    i = pl.program_id(0)

    @pl.when(i < nb_ref[0])
    def _():
        xb = x_ref[...].astype(BF16)
        g = jnp.dot(xb, wg_ref[...], preferred_element_type=F32)
        u = jnp.dot(xb, wu_ref[...], preferred_element_type=F32)
        hid = (g * _sigmoid(g) * u).astype(BF16)
        o_ref[...] = jnp.dot(hid, wd_ref[...], preferred_element_type=F32)

    @pl.when(i >= nb_ref[0])
    def _():
        o_ref[...] = jnp.zeros_like(o_ref)


def _experts(xs, blk_e, n_used, wg, wu, wd):
    n_rows, d = xs.shape
    nb = n_rows // MOE_BM
    hdim = wg.shape[-1]
    return pl.pallas_call(
        _expert_kernel,
        grid_spec=pltpu.PrefetchScalarGridSpec(
            num_scalar_prefetch=2,
            grid=(nb,),
            in_specs=[
                pl.BlockSpec((MOE_BM, d), lambda i, be, nu: (i, 0)),
                pl.BlockSpec((None, d, hdim), lambda i, be, nu: (be[i], 0, 0)),
                pl.BlockSpec((None, d, hdim), lambda i, be, nu: (be[i], 0, 0)),
                pl.BlockSpec((None, hdim, d), lambda i, be, nu: (be[i], 0, 0)),
            ],
            out_specs=pl.BlockSpec((MOE_BM, d), lambda i, be, nu: (i, 0)),
        ),
        out_shape=jax.ShapeDtypeStruct((n_rows, d), F32),
        compiler_params=_cparams("arbitrary"),
        name="moe_experts",
    )(blk_e, n_used, xs, wg, wu, wd)


def _combine_kernel(dest_ref, x_ref, route_ref, gt_ref, fg_ref, eo_ref, o_ref, rows_ref, sem, *, tc, final):
    def row_copy(r, which):
        return pltpu.make_async_copy(eo_ref.at[pl.ds(dest_ref[which, r], 1), :],
                                     rows_ref.at[which, pl.ds(r, 1), :], sem)

    def start(r, c):
        row_copy(r, 0).start()
        row_copy(r, 1).start()
        return c

    def wait(r, c):
        row_copy(r, 0).wait()
        row_copy(r, 1).wait()
        return c

    lax.fori_loop(0, tc, start, 0)
    lax.fori_loop(0, tc, wait, 0)
    route = route_ref[...]
    lane = lax.broadcasted_iota(I32, route.shape, 1)
    w1 = jnp.sum(jnp.where(lane == 2, route, 0.0), axis=1, keepdims=True)
    w2 = jnp.sum(jnp.where(lane == 3, route, 0.0), axis=1, keepdims=True)
    y = rows_ref[0] * w1 + rows_ref[1] * w2
    x = x_ref[...] + gt_ref[...] * y
    if final:
        x = _rms(x, fg_ref[...])
    o_ref[...] = x


def _combine(x1, route, dest, eo, gt2, final_g, seq, final, tc=256):
    t, d = x1.shape
    per_b = seq // tc
    dest3 = dest.reshape(2, t // tc, tc).transpose(1, 0, 2)
    return pl.pallas_call(
        functools.partial(_combine_kernel, tc=tc, final=final),
        grid=(t // tc,),
        in_specs=[pl.BlockSpec((None, 2, tc), lambda i: (i, 0, 0), memory_space=pltpu.SMEM),
                  pl.BlockSpec((tc, d), lambda i: (i, 0)),
                  pl.BlockSpec((tc, LANES), lambda i: (i, 0)),
                  pl.BlockSpec((None, 1, d), lambda i: (i // per_b, 0, 0)),
                  pl.BlockSpec((1, d), lambda i: (0, 0)),
                  pl.BlockSpec(memory_space=pl.ANY)],
        out_specs=pl.BlockSpec((tc, d), lambda i: (i, 0)),
        out_shape=jax.ShapeDtypeStruct((t, d), F32),
        scratch_shapes=[pltpu.VMEM((2, tc, d), F32), pltpu.SemaphoreType.DMA(())],
        compiler_params=_cparams("arbitrary"),
        name="moe_combine",
    )(dest3, x1, route, gt2[:, None, :], final_g.reshape(1, d), eo)


def _moe_plan(route, counts, t):
    e = route[:, 0:2].astype(I32).T
    rank = route[:, 4:6].astype(I32).T
    cnt = counts[0, :MOE_EXPERTS].astype(I32)
    padded = (cnt + MOE_BM - 1) // MOE_BM * MOE_BM
    pend = jnp.cumsum(padded)
    pstart = pend - padded
    dest = pstart[e] + rank
    nb = (2 * t) // MOE_BM + MOE_EXPERTS
    n_used = (pend[-1] // MOE_BM).astype(I32)
    blk = jnp.arange(nb, dtype=I32)
    blk_e = jnp.searchsorted(pend, jnp.minimum(blk, n_used - 1) * MOE_BM, side="right").astype(I32)
    blk_e = jnp.minimum(blk_e, MOE_EXPERTS - 1)
    return dest, blk_e, n_used.reshape(1), nb * MOE_BM


def kernel(x, c, ada_w, ada_b, norm1_g, w_in, gla_gate_w, gla_gate_b, gla_norm_g, dsa_norm_g, s5_a_re, s5_a_im, s5_log_step, s5_b_re, s5_b_im, s5_c_re, s5_c_im, s5_d, s5_glu_w, s5_glu_b, s5_norm_g, w_out, norm2_g, router_grp_w, router_grp_b, router_exp_w, router_exp_b, exp_w_gate, exp_w_up, exp_w_down, final_norm_g):
    batch, seq, d = x.shape
    depth = ada_w.shape[0]
    t = batch * seq
    mod = _ada_mod(c, ada_w, ada_b)
    x2 = x.reshape(t, d)
    for l in range(depth):
        sh1, sc1, gt1, sh2, sc2, gt2 = (mod[l, j] for j in range(6))
        p = _in_proj(x2, norm1_g[l], sc1, sh1, _layout_w_in(w_in[l]), seq)
        o_gla = _gla(p, gla_gate_w[l], gla_gate_b[l], gla_norm_g[l], batch, seq)
        o_dsa = _dsa(p, dsa_norm_g[l], batch, seq)
        tabs = _s5_tables(s5_a_re[l], s5_a_im[l], s5_log_step[l], s5_b_re[l], s5_b_im[l],
                          s5_c_re[l], s5_c_im[l])
        o_s5 = _s5(p["su"], tabs, s5_d[l], s5_glu_w[l], s5_glu_b[l], s5_norm_g[l], batch, seq)
        x1, h2, route, counts = _mix_out(o_gla, o_dsa, o_s5, w_out[l], x2, gt1, norm2_g[l], sc2, sh2,
                                         router_grp_w[l], router_grp_b[l], router_exp_w[l],
                                         router_exp_b[l], seq)
        dest, blk_e, n_used, n_rows = _moe_plan(route, counts, t)
        xs = _dispatch(h2, dest, n_rows)
        eo = _experts(xs, blk_e, n_used, exp_w_gate[l].astype(BF16), exp_w_up[l].astype(BF16),
                      exp_w_down[l].astype(BF16))
        x2 = _combine(x1, route, dest, eo, gt2, final_norm_g, seq, final=(l == depth - 1))
    return x2.reshape(batch, seq, d)
```

```python
import functools
import math

import jax
import jax.numpy as jnp
import numpy as np
from jax import lax
from jax.experimental import pallas as pl
from jax.experimental.pallas import tpu as pltpu

F32 = jnp.float32
BF16 = jnp.bfloat16
I32 = jnp.int32

D_MODEL = 1024
GLA_HEADS = 4
GLA_DV = 96
GLA_DK = 48
GLA_RANK = 16
GLA_TAU = 16.0
GLA_CHUNK = 64
DSA_HEADS = 4
DSA_HEAD_DIM = 64
DSA_WIDTH = DSA_HEADS * DSA_HEAD_DIM
DSA_IDX_HEADS = 8
DSA_IDX_DIM = 32
DSA_TOPK = 256
S5_GROUPS = 24
S5_GROUP_CH = 16
S5_STATE = 64
S5_WIDTH = S5_GROUPS * S5_GROUP_CH
MOE_GROUPS = 4
MOE_EXPERTS_PER_GROUP = 8
MOE_EXPERTS = MOE_GROUPS * MOE_EXPERTS_PER_GROUP
MOE_HIDDEN = 512
RMS_EPS = 1e-6
IN_SIZES = (192, 192, 384, 16, 384, 256, 256, 256, 256, 32, 8, 384)

LANES = 128
SUBLANES = 8
VMEM_LIMIT = 56 * 1024 * 1024

HEAD_PAD = LANES
GLA_W = GLA_HEADS * HEAD_PAD
S5_CHUNK = 16
S5_TILE_G = LANES // S5_GROUP_CH
S5_TILES = S5_GROUPS // S5_TILE_G
S5_ROW_W = S5_CHUNK * LANES
S5_STATE_W = S5_TILE_G * S5_STATE
TOKEN_TILE = (SUBLANES, LANES)
MOE_BM = 256
INT_MIN = -2 ** 31


def _cparams(*sem):
    return pltpu.CompilerParams(dimension_semantics=tuple(sem), vmem_limit_bytes=VMEM_LIMIT)


def _sigmoid(x):
    return 1.0 / (1.0 + jnp.exp(-x))


def _rms(x, g, n=None):
    n = x.shape[-1] if n is None else n
    ms = jnp.sum(x * x, axis=-1, keepdims=True) * (1.0 / n)
    return x * lax.rsqrt(ms + RMS_EPS) * g


def _ada_kernel(c_ref, w_ref, b_ref, o_ref):
    c = c_ref[...]
    cond = (c * _sigmoid(c)).astype(BF16)
    o_ref[...] = jnp.dot(cond, w_ref[...].astype(BF16), preferred_element_type=F32) + b_ref[...]


def _ada_mod(c, ada_w, ada_b):
    depth, d, d6 = ada_w.shape
    nblk = d6 // d
    bp = -(-c.shape[0] // SUBLANES) * SUBLANES
    cp = jnp.pad(c, ((0, bp - c.shape[0]), (0, 0)))
    out = pl.pallas_call(
        _ada_kernel,
        grid=(depth, nblk),
        in_specs=[
            pl.BlockSpec((bp, d), lambda l, j: (0, 0)),
            pl.BlockSpec((None, d, d), lambda l, j: (l, 0, j)),
            pl.BlockSpec((None, None, 1, d), lambda l, j: (l, j, 0, 0)),
        ],
        out_specs=pl.BlockSpec((None, None, bp, d), lambda l, j: (l, j, 0, 0)),
        out_shape=jax.ShapeDtypeStruct((depth, nblk, bp, d), F32),
        compiler_params=_cparams("arbitrary", "arbitrary"),
        name="ada_mod",
    )(cp, ada_w, ada_b.reshape(depth, nblk, 1, d))
    return out[:, :, : c.shape[0], :]


PROJ_OUTS = (
    ("gq", GLA_W, BF16), ("gk", GLA_W, BF16), ("gv", GLA_W, BF16), ("gr", GLA_W, BF16),
    ("glr", LANES, BF16),
    ("dq", DSA_WIDTH, BF16), ("dk", DSA_WIDTH, BF16), ("dv", DSA_WIDTH, BF16),
    ("iq", DSA_IDX_HEADS * DSA_IDX_DIM, BF16), ("ik", DSA_IDX_HEADS * DSA_IDX_DIM, BF16),
    ("iw", LANES, F32), ("su0", LANES, F32), ("su1", LANES, F32), ("su2", LANES, F32),
)
PROJ_COLS = sum(w for _, w, _ in PROJ_OUTS)


def _pad_heads(w, heads, dim):
    d = w.shape[0]
    w = w.reshape(d, heads, dim)
    return jnp.pad(w, ((0, 0), (0, 0), (0, HEAD_PAD - dim))).reshape(d, heads * HEAD_PAD)


def _pad_cols(w, width):
    return jnp.pad(w, ((0, 0), (0, width - w.shape[1])))


def _layout_w_in(w_in):
    offs = np.cumsum((0,) + IN_SIZES)
    p = [w_in[:, offs[i]:offs[i + 1]] for i in range(len(IN_SIZES))]
    cols = [
        _pad_heads(p[0], GLA_HEADS, GLA_DK), _pad_heads(p[1], GLA_HEADS, GLA_DK),
        _pad_heads(p[2], GLA_HEADS, GLA_DV), _pad_heads(p[4], GLA_HEADS, GLA_DV),
        _pad_cols(p[3], LANES),
        p[5], p[6], p[7], p[8], jnp.tile(p[9], (1, DSA_IDX_HEADS)),
        _pad_cols(p[10], LANES), p[11],
    ]
    return jnp.concatenate(cols, axis=1).astype(BF16)


def _proj_kernel(x_ref, g_ref, sc_ref, sh_ref, w_ref, *o_refs):
    x = x_ref[...]
    h = _rms(x, g_ref[...]) * (1.0 + sc_ref[...]) + sh_ref[...]
    hb = h.astype(BF16)
    c0 = 0
    for o_ref, (_, width, _) in zip(o_refs, PROJ_OUTS):
        o_ref[...] = jnp.dot(hb, w_ref[:, c0:c0 + width], preferred_element_type=F32).astype(o_ref.dtype)
        c0 += width


def _in_proj(x2, norm_g, sc, sh, w_lay, seq, tm=512):
    t, d = x2.shape
    per_b = seq // tm
    outs = pl.pallas_call(
        _proj_kernel,
        grid=(t // tm,),
        in_specs=[
            pl.BlockSpec((tm, d), lambda i: (i, 0)),
            pl.BlockSpec((1, d), lambda i: (0, 0)),
            pl.BlockSpec((None, 1, d), lambda i: (i // per_b, 0, 0)),
            pl.BlockSpec((None, 1, d), lambda i: (i // per_b, 0, 0)),
            pl.BlockSpec((d, PROJ_COLS), lambda i: (0, 0)),
        ],
        out_specs=[pl.BlockSpec((tm, w), lambda i: (i, 0)) for _, w, _ in PROJ_OUTS],
        out_shape=[jax.ShapeDtypeStruct((t, w), dt) for _, w, dt in PROJ_OUTS],
        compiler_params=_cparams("arbitrary"),
        name="in_proj",
    )(x2, norm_g.reshape(1, d), sc[:, None, :], sh[:, None, :], w_lay)
    return {name: o for (name, _, _), o in zip(PROJ_OUTS, outs)}


def _gla_kernel(q_ref, k_ref, v_ref, r_ref, lr_ref, gw_ref, gb_ref, ng_ref, o_ref, s_ref, *, rows):
    @pl.when(pl.program_id(1) == 0)
    def _():
        s_ref[...] = jnp.zeros_like(s_ref)

    c = GLA_CHUNK
    gl = jnp.dot(lr_ref[...], gw_ref[...], preferred_element_type=F32) + gb_ref[...]
    g = (jnp.minimum(gl, 0.0) - jnp.log1p(jnp.exp(-jnp.abs(gl)))) * (1.0 / GLA_TAU)
    g1 = g.astype(BF16)
    rem = g - g1.astype(F32)
    g2 = rem.astype(BF16)
    g3 = (rem - g2.astype(F32)).astype(BF16)
    ri = lax.broadcasted_iota(I32, (c, c), 0)
    ci = lax.broadcasted_iota(I32, (c, c), 1)
    causal = ri >= ci
    tri = causal.astype(BF16)
    ng = ng_ref[...]
    for ch in range(rows // c):
        rs = slice(ch * c, (ch + 1) * c)
        bcum = (jnp.dot(tri, g1[rs], preferred_element_type=F32)
                + jnp.dot(tri, g2[rs], preferred_element_type=F32)
                + jnp.dot(tri, g3[rs], preferred_element_type=F32))
        blast = bcum[c - 1:c, :]
        e_pos = jnp.exp(bcum)
        kf = k_ref[rs, :].astype(F32)
        qd = (q_ref[rs, :].astype(F32) * (GLA_DK ** -0.5) * e_pos).astype(BF16)
        kd = (kf * jnp.exp(-bcum)).astype(BF16)
        ku = (kf * jnp.exp(blast - bcum)).astype(BF16)
        dec = jnp.exp(blast)
        for h in range(GLA_HEADS):
            hs = slice(h * HEAD_PAD, (h + 1) * HEAD_PAD)
            vh = v_ref[rs, hs]
            att = lax.dot_general(qd[:, hs], kd[:, hs], (((1,), (1,)), ((), ())),
                                  preferred_element_type=F32)
            att = jnp.where(causal, att, 0.0).astype(BF16)
            st = s_ref[h]
            o = jnp.dot(att, vh, preferred_element_type=F32)
            o = o + lax.dot_general(qd[:, hs], st.astype(BF16), (((1,), (1,)), ((), ())),
                                    preferred_element_type=F32)
            upd = lax.dot_general(vh, ku[:, hs], (((0,), (0,)), ((), ())),
                                  preferred_element_type=F32)
            s_ref[h] = st * dec[:, hs] + upd
            y = _rms(o, ng[:, hs], GLA_DV)
            rr = r_ref[rs, hs].astype(F32)
            o_ref[rs, hs] = (y * (rr * _sigmoid(rr))).astype(o_ref.dtype)


def _gla(p, gate_w, gate_b, norm_g, batch, seq, rows=512):
    gw = _pad_cols(jnp.pad(_pad_heads(gate_w, GLA_HEADS, GLA_DK), ((0, LANES - GLA_RANK), (0, 0))), GLA_W)
    gb = _pad_heads(gate_b.reshape(1, -1), GLA_HEADS, GLA_DK)
    ng = _pad_heads(jnp.tile(norm_g.reshape(1, GLA_DV), (1, GLA_HEADS)), GLA_HEADS, GLA_DV)
    rows = min(rows, seq)
    blk = lambda w: pl.BlockSpec((None, rows, w), lambda b, j: (b, j, 0))
    full = lambda a: pl.BlockSpec(a.shape, lambda b, j: (0, 0))
    r3 = lambda a: a.reshape(batch, seq, a.shape[-1])
    out = pl.pallas_call(
        functools.partial(_gla_kernel, rows=rows),
        grid=(batch, seq // rows),
        in_specs=[blk(GLA_W), blk(GLA_W), blk(GLA_W), blk(GLA_W), blk(LANES),
                  full(gw), full(gb), full(ng)],
        out_specs=blk(GLA_W),
        out_shape=jax.ShapeDtypeStruct((batch, seq, GLA_W), BF16),
        scratch_shapes=[pltpu.VMEM((GLA_HEADS, HEAD_PAD, HEAD_PAD), F32)],
        compiler_params=_cparams("arbitrary", "arbitrary"),
        name="gla",
    )(r3(p["gq"]), r3(p["gk"]), r3(p["gv"]), r3(p["gr"]), r3(p["glr"]), gw.astype(BF16), gb, ng)
    return out.reshape(batch * seq, GLA_W)


def _dsa_kernel(iq_ref, ik_ref, iw_ref, q_ref, k_ref, v_ref, ng_ref, o_ref,
                qs_ref, q4_ref, keys_ref, acc_ref, run_ref, *, tq, kc, sub, n_sel):
    qi = pl.program_id(1)
    nkc = ((qi + 1) * tq + kc - 1) // kc
    lane_w = lax.broadcasted_iota(I32, (tq, DSA_WIDTH), 1)
    iq = iq_ref[...]
    for h in range(DSA_IDX_HEADS):
        m = (lane_w >= h * DSA_IDX_DIM) & (lane_w < (h + 1) * DSA_IDX_DIM)
        qs_ref[h * tq:(h + 1) * tq, :] = jnp.where(m, iq, jnp.zeros_like(iq))
    iw = iw_ref[...] * (DSA_IDX_HEADS ** -0.5 * DSA_IDX_DIM ** -0.5)
    lane = lax.broadcasted_iota(I32, (tq, LANES), 1)
    w_cols = [jnp.sum(jnp.where(lane == h, iw, 0.0), axis=1, keepdims=True) for h in range(DSA_IDX_HEADS)]
    qpos = qi * tq + lax.broadcasted_iota(I32, (tq, sub), 0)
    kcol = lax.broadcasted_iota(I32, (tq, sub), 1)
    nt = (((1,), (1,)), ((), ()))

    def score_body(c, carry):
        for u in range(kc // sub):
            k0 = pl.multiple_of(c * kc + u * sub, sub)
            lg = lax.dot_general(qs_ref[...], ik_ref[pl.ds(k0, sub), :], nt,
                                 preferred_element_type=F32)
            sc = jnp.zeros((tq, sub), F32)
            for h in range(DSA_IDX_HEADS):
                sc = sc + w_cols[h] * jnp.maximum(lg[h * tq:(h + 1) * tq, :], 0.0)
            sc = sc + 0.0
            bits = lax.bitcast_convert_type(sc, I32)
            key = jnp.where(bits < 0, bits ^ jnp.int32(0x7FFFFFFF), bits)
            key = jnp.where(kcol + k0 <= qpos, key, jnp.int32(INT_MIN))
            keys_ref[c, :, u * sub:(u + 1) * sub] = key
        return carry

    lax.fori_loop(0, nkc, score_body, 0)

    def count_ge(cand):
        candb = jnp.broadcast_to(cand, (tq, LANES))

        def body(c, acc):
            blk = keys_ref[c]
            for j in range(kc // LANES):
                acc = acc + jnp.where(blk[:, j * LANES:(j + 1) * LANES] >= candb, 1.0, 0.0)
            return acc

        acc = lax.fori_loop(0, nkc, body, jnp.zeros((tq, LANES), F32))
        return jnp.sum(acc, axis=1, keepdims=True)

    need = jnp.float32(n_sel)
    t0 = jnp.full((tq, 1), INT_MIN, I32)
    zero = jnp.zeros((tq, 1), I32)
    t0 = jnp.where(count_ge(zero) >= need, zero, t0)

    def bit_body(i, t):
        cand = t | (jnp.int32(1) << (30 - i))
        return jnp.where(count_ge(cand) >= need, cand, t)

    thr = lax.fori_loop(0, 31, bit_body, t0)
    thr = jnp.maximum(thr, jnp.int32(INT_MIN + 1))
    n_gt = count_ge(thr + 1)
    n_tie = need - n_gt

    ui = lax.broadcasted_iota(I32, (kc, kc), 0)
    uj = lax.broadcasted_iota(I32, (kc, kc), 1)
    before = (ui < uj).astype(BF16)
    q = q_ref[...] * jnp.asarray(DSA_HEAD_DIM ** -0.5, BF16)
    for h in range(DSA_HEADS):
        m = (lane_w >= h * DSA_HEAD_DIM) & (lane_w < (h + 1) * DSA_HEAD_DIM)
        q4_ref[h * tq:(h + 1) * tq, :] = jnp.where(m, q, jnp.zeros_like(q))
    neg = jnp.float32(-1e30)
    hrows = lambda a, h: a[h * tq:(h + 1) * tq]

    def lane_fold(op, acc, s):
        for j in range(kc // LANES):
            acc = op(acc, s[:, j * LANES:(j + 1) * LANES])
        return acc

    run_ref[...] = jnp.full(run_ref.shape, neg, F32)

    def mask_body(c, seen):
        k0 = pl.multiple_of(c * kc, kc)
        key = keys_ref[c]
        eq = key == thr
        eqf = jnp.where(eq, 1.0, 0.0)
        rank = seen + jnp.dot(eqf.astype(BF16), before, preferred_element_type=F32)
        sel = (key > thr) | (eq & (rank < n_tie))
        bias = jnp.where(sel, 0.0, neg)
        keys_ref[c] = lax.bitcast_convert_type(bias, I32)
        s = lax.dot_general(q4_ref[...], k_ref[pl.ds(k0, kc), :], nt, preferred_element_type=F32)
        for h in range(DSA_HEADS):
            run_ref[h] = lane_fold(jnp.maximum, run_ref[h], hrows(s, h) + bias)
        return seen + jnp.sum(eqf, axis=1, keepdims=True)

    lax.fori_loop(0, nkc, mask_body, jnp.zeros((tq, 1), F32))
    ms = [jnp.max(run_ref[h], axis=1, keepdims=True) for h in range(DSA_HEADS)]

    run_ref[...] = jnp.zeros_like(run_ref)
    acc_ref[...] = jnp.zeros_like(acc_ref)

    def att_body(c, carry):
        k0 = pl.multiple_of(c * kc, kc)
        bias = lax.bitcast_convert_type(keys_ref[c], F32)
        s = lax.dot_general(q4_ref[...], k_ref[pl.ds(k0, kc), :], nt, preferred_element_type=F32)
        ps = []
        for h in range(DSA_HEADS):
            p = jnp.exp(hrows(s, h) + bias - ms[h])
            run_ref[h] = lane_fold(jnp.add, run_ref[h], p)
            ps.append(p.astype(BF16))
        acc_ref[...] += jnp.dot(jnp.concatenate(ps, axis=0), v_ref[pl.ds(k0, kc), :],
                                preferred_element_type=F32)
        return carry

    lax.fori_loop(0, nkc, att_body, 0)
    out = jnp.zeros((tq, DSA_WIDTH), F32)
    for h in range(DSA_HEADS):
        m = (lane_w >= h * DSA_HEAD_DIM) & (lane_w < (h + 1) * DSA_HEAD_DIM)
        out = out + jnp.where(m, acc_ref[h * tq:(h + 1) * tq, :] / jnp.sum(run_ref[h], axis=1, keepdims=True), 0.0)
    o_ref[...] = _rms(out, ng_ref[...]).astype(o_ref.dtype)


def _dsa(p, norm_g, batch, seq, tq=128, kc=512, sub=256):
    n_sel = min(DSA_TOPK, seq // 4)
    kc = min(kc, seq)
    sub = min(sub, kc)
    r3 = lambda a: a.reshape(batch, seq, a.shape[-1])
    qblk = lambda w: pl.BlockSpec((None, tq, w), lambda b, i: (b, i, 0))
    kfull = lambda w: pl.BlockSpec((None, seq, w), lambda b, i: (b, 0, 0))
    out = pl.pallas_call(
        functools.partial(_dsa_kernel, tq=tq, kc=kc, sub=sub, n_sel=n_sel),
        grid=(batch, seq // tq),
        in_specs=[qblk(DSA_WIDTH), kfull(DSA_WIDTH), qblk(LANES), qblk(DSA_WIDTH),
                  kfull(DSA_WIDTH), kfull(DSA_WIDTH), pl.BlockSpec((1, DSA_WIDTH), lambda b, i: (0, 0))],
        out_specs=qblk(DSA_WIDTH),
        out_shape=jax.ShapeDtypeStruct((batch, seq, DSA_WIDTH), BF16),
        scratch_shapes=[
            pltpu.VMEM((DSA_IDX_HEADS * tq, DSA_WIDTH), BF16),
            pltpu.VMEM((DSA_HEADS * tq, DSA_WIDTH), BF16),
            pltpu.VMEM((seq // kc, tq, kc), I32),
            pltpu.VMEM((DSA_HEADS * tq, DSA_WIDTH), F32),
            pltpu.VMEM((DSA_HEADS, tq, LANES), F32),
        ],
        compiler_params=_cparams("arbitrary", "arbitrary"),
        name="dsa",
    )(r3(p["iq"]), r3(p["ik"]), r3(p["iw"]), r3(p["dq"]), r3(p["dk"]), r3(p["dv"]),
      norm_g.reshape(1, DSA_WIDTH))
    return out.reshape(batch * seq, DSA_WIDTH)


def _s5_tables(a_re, a_im, log_step, b_re, b_im, c_re, c_im):
    hp = lax.Precision.HIGHEST
    s = S5_CHUNK
    dt = jnp.exp(log_step)[:, None]
    mag = jnp.exp(a_re * dt)
    lam_re, lam_im = mag * jnp.cos(a_im * dt), mag * jnp.sin(a_im * dt)
    den = a_re * a_re + a_im * a_im
    nr, ni = lam_re - 1.0, lam_im
    coef_re = (nr * a_re + ni * a_im) / den
    coef_im = (ni * a_re - nr * a_im) / den
    bb_re = coef_re[..., None] * b_re - coef_im[..., None] * b_im
    bb_im = coef_re[..., None] * b_im + coef_im[..., None] * b_re
    j = jnp.arange(s + 1, dtype=F32)[:, None, None]
    pmag = jnp.exp(a_re * dt * j)
    pw_re, pw_im = pmag * jnp.cos(a_im * dt * j), pmag * jnp.sin(a_im * dt * j)
    e_re = c_re[None] * pw_re[:, :, None, :] - c_im[None] * pw_im[:, :, None, :]
    e_im = c_re[None] * pw_im[:, :, None, :] + c_im[None] * pw_re[:, :, None, :]
    kk = (jnp.einsum("jgcp,gpd->jgcd", e_re, bb_re, precision=hp)
          - jnp.einsum("jgcp,gpd->jgcd", e_im, bb_im, precision=hp))
    lag = np.arange(s)[None, :] - np.arange(s)[:, None]
    toe = jnp.where((lag >= 0)[:, :, None, None, None], kk[np.clip(lag, 0, s)], 0.0)
    rp_re, rp_im = pw_re[s - 1 - np.arange(s)], pw_im[s - 1 - np.arange(s)]
    x_re = rp_re[..., None] * bb_re[None] - rp_im[..., None] * bb_im[None]
    x_im = rp_re[..., None] * bb_im[None] + rp_im[..., None] * bb_re[None]
    f_re, f_im = e_re[1:], e_im[1:]
    eye = jnp.eye(S5_TILE_G, dtype=F32)
    tg = lambda a, ax: a.reshape(a.shape[:ax] + (S5_TILES, S5_TILE_G) + a.shape[ax + 1:])
    w_toe = (tg(toe, 2).transpose(2, 0, 3, 5, 1, 4)[:, :, :, :, :, None, :]
             * eye[None, None, :, None, None, :, None]).reshape(S5_TILES, S5_ROW_W, S5_ROW_W)
    w1 = lambda x: (tg(x, 1).transpose(1, 0, 2, 4, 3)[:, :, :, :, None, :]
                    * eye[None, None, :, None, :, None]).reshape(S5_TILES, S5_ROW_W, S5_STATE_W)
    w2 = lambda f: (tg(f, 1).transpose(1, 2, 4, 0, 3)[:, :, :, :, None, :]
                    * eye[None, :, None, None, :, None]).reshape(S5_TILES, S5_STATE_W, S5_ROW_W)
    tabs = dict(toe=w_toe, w1_re=w1(x_re), w1_im=w1(x_im), w2_re=w2(f_re), w2_im=w2(-f_im))
    tabs = {k: v.astype(BF16) for k, v in tabs.items()}
    tabs["a_re"], tabs["a_im"] = pw_re[s].reshape(-1), pw_im[s].reshape(-1)
    return tabs


def _s5_state_in_kernel(u_ref, wr_ref, wi_ref, xr_ref, xi_ref):
    u = u_ref[...].astype(BF16)
    xr_ref[...] = jnp.dot(u, wr_ref[...], preferred_element_type=F32)
    xi_ref[...] = jnp.dot(u, wi_ref[...], preferred_element_type=F32)


def _s5_scan_kernel(xr_ref, xi_ref, ar_ref, ai_ref, hr_ref, hi_ref, cr_ref, ci_ref, *, steps):
    @pl.when(pl.program_id(0) == 0)
    def _():
        cr_ref[...] = jnp.zeros_like(cr_ref)
        ci_ref[...] = jnp.zeros_like(ci_ref)

    ar, ai = ar_ref[...], ai_ref[...]

    def body(n, carry):
        hr, hi = carry
        hr_ref[n] = hr
        hi_ref[n] = hi
        return ar * hr - ai * hi + xr_ref[n], ar * hi + ai * hr + xi_ref[n]

    hr, hi = lax.fori_loop(0, steps, body, (cr_ref[...], ci_ref[...]))
    cr_ref[...] = hr
    ci_ref[...] = hi


def _s5_out_kernel(u_ref, toe_ref, hr_ref, hi_ref, w2r_ref, w2i_ref, y_ref):
    y = jnp.dot(u_ref[...].astype(BF16), toe_ref[...], preferred_element_type=F32)
    y = y + jnp.dot(hr_ref[...].astype(BF16), w2r_ref[...], preferred_element_type=F32)
    y = y + jnp.dot(hi_ref[...].astype(BF16), w2i_ref[...], preferred_element_type=F32)
    y_ref[...] = y


def _s5_post_kernel(y_ref, u0_ref, u1_ref, u2_ref, d_ref, gw_ref, gb_ref, ng_ref, o_ref):
    y = jnp.concatenate([y_ref[q] for q in range(S5_TILES)], axis=1)
    u = jnp.concatenate([u0_ref[...], u1_ref[...], u2_ref[...]], axis=1)
    z = y + d_ref[...] * u
    z = 0.5 * z * (1.0 + jnp.tanh(math.sqrt(2.0 / math.pi) * (z + 0.044715 * (z * z * z))))
    gate = _sigmoid(jnp.dot(z.astype(BF16), gw_ref[...], preferred_element_type=F32) + gb_ref[...])
    o_ref[...] = _rms(z * gate, ng_ref[...]).astype(o_ref.dtype)


def _s5(us, tabs, d, glu_w, glu_b, norm_g, batch, seq, tm=512, rb=512):
    s = S5_CHUNK
    nch = seq // s
    rows = nch * batch
    rb = min(rb, rows)
    gp = S5_GROUPS * S5_STATE
    xq = jnp.stack([u.reshape(rows, S5_ROW_W) for u in us])
    row_in = pl.BlockSpec((None, rb, S5_ROW_W), lambda q, i: (q, i, 0))
    wspec = lambda a: pl.BlockSpec((None,) + a.shape[1:], lambda q, i: (q, 0, 0))
    colblk = pl.BlockSpec((rb, S5_STATE_W), lambda q, i: (i, q))
    xr, xi = pl.pallas_call(
        _s5_state_in_kernel,
        grid=(S5_TILES, rows // rb),
        in_specs=[row_in, wspec(tabs["w1_re"]), wspec(tabs["w1_im"])],
        out_specs=[colblk, colblk],
        out_shape=[jax.ShapeDtypeStruct((rows, gp), F32)] * 2,
        compiler_params=_cparams("arbitrary", "arbitrary"),
        name="s5_state_in",
    )(xq, tabs["w1_re"], tabs["w1_im"])
    parts = SUBLANES // batch
    wl = gp // parts
    slab = lambda a: a.reshape(batch, nch, gp).transpose(1, 0, 2).reshape(nch, SUBLANES, wl)
    unslab = lambda a: a.reshape(nch, batch, gp).transpose(1, 0, 2).reshape(rows, gp)
    coef = lambda a: jnp.tile(a.reshape(parts, wl), (batch, 1))
    steps = min(64, nch)
    sblk = pl.BlockSpec((steps, SUBLANES, wl), lambda i: (i, 0, 0))
    cblk = pl.BlockSpec((SUBLANES, wl), lambda i: (0, 0))
    hr, hi = pl.pallas_call(
        functools.partial(_s5_scan_kernel, steps=steps),
        grid=(nch // steps,),
        in_specs=[sblk, sblk, cblk, cblk],
        out_specs=[sblk, sblk],
        out_shape=[jax.ShapeDtypeStruct((nch, SUBLANES, wl), F32)] * 2,
        scratch_shapes=[pltpu.VMEM((SUBLANES, wl), F32)] * 2,
        compiler_params=_cparams("arbitrary"),
        name="s5_scan",
    )(slab(xr), slab(xi), coef(tabs["a_re"]), coef(tabs["a_im"]))
    y = pl.pallas_call(
        _s5_out_kernel,
        grid=(S5_TILES, rows // rb),
        in_specs=[row_in, wspec(tabs["toe"]), colblk, colblk, wspec(tabs["w2_re"]), wspec(tabs["w2_im"])],
        out_specs=row_in,
        out_shape=jax.ShapeDtypeStruct((S5_TILES, rows, S5_ROW_W), F32),
        compiler_params=_cparams("arbitrary", "arbitrary"),
        name="s5_out",
    )(xq, tabs["toe"], unslab(hr), unslab(hi), tabs["w2_re"], tabs["w2_im"])
    t = batch * seq
    y = y.reshape(S5_TILES, t, LANES)
    tm = min(tm, t)
    tok = lambda w: pl.BlockSpec((tm, w), lambda i: (i, 0))
    vec = pl.BlockSpec((1, S5_WIDTH), lambda i: (0, 0))
    return pl.pallas_call(
        _s5_post_kernel,
        grid=(t // tm,),
        in_specs=[pl.BlockSpec((S5_TILES, tm, LANES), lambda i: (0, i, 0)), tok(LANES), tok(LANES), tok(LANES),
                  vec, pl.BlockSpec((S5_WIDTH, S5_WIDTH), lambda i: (0, 0)), vec, vec],
        out_specs=tok(S5_WIDTH),
        out_shape=jax.ShapeDtypeStruct((t, S5_WIDTH), BF16),
        compiler_params=_cparams("arbitrary"),
        name="s5_post",
    )(y, us[0], us[1], us[2], d.reshape(1, -1), glu_w.astype(BF16), glu_b.reshape(1, -1),
      norm_g.reshape(1, -1))


def _mix_out_kernel(og_ref, od_ref, os_ref, wg_ref, wd_ref, ws_ref, x_ref, gt_ref, n2_ref, sc_ref,
                    sh_ref, rw1_ref, rw2_ref, rb_ref, x1_ref, h2_ref, route_ref, cnt_ref, carry_ref, *, tm):
    @pl.when(pl.program_id(0) == 0)
    def _():
        carry_ref[...] = jnp.zeros_like(carry_ref)

    mix = jnp.dot(og_ref[...], wg_ref[...], preferred_element_type=F32)
    mix = mix + jnp.dot(od_ref[...], wd_ref[...], preferred_element_type=F32)
    mix = mix + jnp.dot(os_ref[...], ws_ref[...], preferred_element_type=F32)
    x1 = x_ref[...] + gt_ref[...] * mix
    x1_ref[...] = x1
    h2 = _rms(x1, n2_ref[...]) * (1.0 + sc_ref[...]) + sh_ref[...]
    for s in range(SUBLANES):
        h2_ref[:, s, :] = h2[:, s * LANES:(s + 1) * LANES]
    ha = h2.astype(BF16)
    hb = (h2 - ha.astype(F32)).astype(BF16)
    logit = (jnp.dot(ha, rw1_ref[...], preferred_element_type=F32)
             + jnp.dot(ha, rw2_ref[...], preferred_element_type=F32)
             + jnp.dot(hb, rw1_ref[...], preferred_element_type=F32)) + rb_ref[...]
    lane = lax.broadcasted_iota(I32, (tm, LANES), 1)
    lanef = lane.astype(F32)
    ninf = jnp.float32(-jnp.inf)
    big = jnp.float32(LANES)
    gl = jnp.where(lane < MOE_GROUPS, logit, ninf)
    gmax = jnp.max(gl, axis=1, keepdims=True)
    grp_p = 1.0 / jnp.sum(jnp.exp(gl - gmax), axis=1, keepdims=True)
    grp_i = jnp.min(jnp.where(gl == gmax, lanef, big), axis=1, keepdims=True)
    lo = MOE_GROUPS + MOE_EXPERTS_PER_GROUP * grp_i
    el = jnp.where((lanef >= lo) & (lanef < lo + MOE_EXPERTS_PER_GROUP), logit, ninf)
    l1 = jnp.max(el, axis=1, keepdims=True)
    i1 = jnp.min(jnp.where(el == l1, lanef, big), axis=1, keepdims=True)
    el2 = jnp.where(lanef == i1, ninf, el)
    l2 = jnp.max(el2, axis=1, keepdims=True)
    i2 = jnp.min(jnp.where(el2 == l2, lanef, big), axis=1, keepdims=True)
    e21 = jnp.exp(l2 - l1)
    w1 = grp_p / (1.0 + e21)
    w2 = grp_p * e21 / (1.0 + e21)
    e1 = i1 - MOE_GROUPS
    e2 = i2 - MOE_GROUPS
    oh1 = lanef == e1
    oh2 = lanef == e2
    hot = jnp.where(oh1 | oh2, 1.0, 0.0)
    ri = lax.broadcasted_iota(I32, (tm, tm), 0)
    ci = lax.broadcasted_iota(I32, (tm, tm), 1)
    before = (ci < ri).astype(BF16)
    prior = carry_ref[...] + jnp.dot(before, hot.astype(BF16), preferred_element_type=F32)
    r1 = jnp.sum(jnp.where(oh1, prior, 0.0), axis=1, keepdims=True)
    r2 = jnp.sum(jnp.where(oh2, prior, 0.0), axis=1, keepdims=True)
    total = carry_ref[...] + jnp.sum(hot, axis=0, keepdims=True)
    carry_ref[...] = total
    cnt_ref[...] = jnp.broadcast_to(total, cnt_ref.shape)
    route = jnp.where(lane == 0, e1, jnp.where(lane == 1, e2, jnp.where(lane == 2, w1, jnp.where(
        lane == 3, w2, jnp.where(lane == 4, r1, jnp.where(lane == 5, r2, 0.0))))))
    route_ref[...] = route


def _mix_out(o_gla, o_dsa, o_s5, w_out, x2, gt1, norm2_g, sc2, sh2, rgw, rgb, rew, reb, seq, tm=256):
    t, d = x2.shape
    per_b = seq // tm
    wg = jnp.pad(w_out[:GLA_HEADS * GLA_DV].reshape(GLA_HEADS, GLA_DV, d),
                 ((0, 0), (0, HEAD_PAD - GLA_DV), (0, 0))).reshape(GLA_W, d).astype(BF16)
    wd = w_out[GLA_HEADS * GLA_DV:GLA_HEADS * GLA_DV + DSA_WIDTH].astype(BF16)
    ws = w_out[GLA_HEADS * GLA_DV + DSA_WIDTH:].astype(BF16)
    rw = _pad_cols(jnp.concatenate([rgw, rew], axis=1), LANES)
    rb = _pad_cols(jnp.concatenate([rgb, reb]).reshape(1, -1), LANES)
    rw1 = rw.astype(BF16)
    rw2 = (rw - rw1.astype(F32)).astype(BF16)
    tok = lambda w: pl.BlockSpec((tm, w), lambda i: (i, 0))
    full = lambda a: pl.BlockSpec(a.shape, lambda i: (0, 0))
    perb = pl.BlockSpec((None, 1, d), lambda i: (i // per_b, 0, 0))
    vec = pl.BlockSpec((1, d), lambda i: (0, 0))
    return pl.pallas_call(
        functools.partial(_mix_out_kernel, tm=tm),
        grid=(t // tm,),
        in_specs=[tok(GLA_W), tok(DSA_WIDTH), tok(S5_WIDTH), full(wg), full(wd), full(ws), tok(d),
                  perb, vec, perb, perb, full(rw1), full(rw2), full(rb)],
        out_specs=[tok(d), pl.BlockSpec((tm,) + TOKEN_TILE, lambda i: (i, 0, 0)), tok(LANES),
                   pl.BlockSpec((SUBLANES, LANES), lambda i: (0, 0))],
        out_shape=[jax.ShapeDtypeStruct((t, d), F32), jax.ShapeDtypeStruct((t,) + TOKEN_TILE, F32),
                   jax.ShapeDtypeStruct((t, LANES), F32), jax.ShapeDtypeStruct((SUBLANES, LANES), F32)],
        scratch_shapes=[pltpu.VMEM((1, LANES), F32)],
        compiler_params=_cparams("arbitrary"),
        name="mix_out",
    )(o_gla, o_dsa, o_s5, wg, wd, ws, x2, gt1[:, None, :], norm2_g.reshape(1, d), sc2[:, None, :],
      sh2[:, None, :], rw1, rw2, rb)


def _tile_rows_to_2d(ref):
    return jnp.concatenate([ref[:, s, :] for s in range(SUBLANES)], axis=1)


def _dispatch_kernel(dest_ref, h_ref, zero_ref, out_ref, sem, *, td):
    del zero_ref

    def start(r, c):
        pltpu.make_async_copy(h_ref.at[r], out_ref.at[dest_ref[0, r]], sem).start()
        pltpu.make_async_copy(h_ref.at[r], out_ref.at[dest_ref[1, r]], sem).start()
        return c

    lax.fori_loop(0, td, start, 0, unroll=8)
    for _ in range(2):
        pltpu.make_async_copy(h_ref, out_ref.at[pl.ds(0, td)], sem).wait()


def _dispatch(h2, dest, n_rows, td=256):
    t = h2.shape[0]
    dest3 = dest.reshape(2, t // td, td).transpose(1, 0, 2)
    zeros = jnp.zeros((n_rows,) + TOKEN_TILE, F32)
    return pl.pallas_call(
        functools.partial(_dispatch_kernel, td=td),
        grid=(t // td,),
        in_specs=[pl.BlockSpec((None, 2, td), lambda i: (i, 0, 0), memory_space=pltpu.SMEM),
                  pl.BlockSpec((td,) + TOKEN_TILE, lambda i: (i, 0, 0)),
                  pl.BlockSpec(memory_space=pl.ANY)],
        out_specs=pl.BlockSpec(memory_space=pl.ANY),
        out_shape=jax.ShapeDtypeStruct((n_rows,) + TOKEN_TILE, F32),
        scratch_shapes=[pltpu.SemaphoreType.DMA(())],
        input_output_aliases={2: 0},
        compiler_params=_cparams("arbitrary"),
        name="moe_dispatch",
    )(dest3, h2, zeros)


def _expert_kernel(be_ref, nb_ref, x_ref, wg_ref, wu_ref, wd_ref, o_ref):
    del be_ref
    i = pl.program_id(0)

    @pl.when(i < nb_ref[0])
    def _():
        xb = _tile_rows_to_2d(x_ref).astype(BF16)
        g = jnp.dot(xb, wg_ref[...], preferred_element_type=F32)
        u = jnp.dot(xb, wu_ref[...], preferred_element_type=F32)
        hid = (g * _sigmoid(g) * u).astype(BF16)
        out = jnp.dot(hid, wd_ref[...], preferred_element_type=F32)
        for s in range(SUBLANES):
            o_ref[:, s, :] = out[:, s * LANES:(s + 1) * LANES]

    @pl.when(i >= nb_ref[0])
    def _():
        o_ref[...] = jnp.zeros_like(o_ref)


def _experts(xs, blk_e, n_used, wg, wu, wd):
    n_rows = xs.shape[0]
    nb = n_rows // MOE_BM
    d, hdim = wg.shape[-2:]
    rows = pl.BlockSpec((MOE_BM,) + TOKEN_TILE, lambda i, be, nu: (i, 0, 0))
    return pl.pallas_call(
        _expert_kernel,
        grid_spec=pltpu.PrefetchScalarGridSpec(
            num_scalar_prefetch=2,
            grid=(nb,),
            in_specs=[
                rows,
                pl.BlockSpec((None, d, hdim), lambda i, be, nu: (be[i], 0, 0)),
                pl.BlockSpec((None, d, hdim), lambda i, be, nu: (be[i], 0, 0)),
                pl.BlockSpec((None, hdim, d), lambda i, be, nu: (be[i], 0, 0)),
            ],
            out_specs=rows,
        ),
        out_shape=jax.ShapeDtypeStruct((n_rows,) + TOKEN_TILE, F32),
        compiler_params=_cparams("arbitrary"),
        name="moe_experts",
    )(blk_e, n_used, xs, wg, wu, wd)


def _combine_kernel(dest_ref, x_ref, route_ref, gt_ref, fg_ref, eo_ref, o_ref, rows_ref, sem, *, tc, final):
    def start(r, c):
        pltpu.make_async_copy(eo_ref.at[dest_ref[0, r]], rows_ref.at[0, r], sem).start()
        pltpu.make_async_copy(eo_ref.at[dest_ref[1, r]], rows_ref.at[1, r], sem).start()
        return c

    lax.fori_loop(0, tc, start, 0, unroll=8)
    for which in range(2):
        pltpu.make_async_copy(eo_ref.at[pl.ds(0, tc)], rows_ref.at[which], sem).wait()
    route = route_ref[...]
    lane = lax.broadcasted_iota(I32, route.shape, 1)
    w1 = jnp.sum(jnp.where(lane == 2, route, 0.0), axis=1, keepdims=True)
    w2 = jnp.sum(jnp.where(lane == 3, route, 0.0), axis=1, keepdims=True)
    y = _tile_rows_to_2d(rows_ref.at[0]) * w1 + _tile_rows_to_2d(rows_ref.at[1]) * w2
    x = x_ref[...] + gt_ref[...] * y
    if final:
        x = _rms(x, fg_ref[...])
    o_ref[...] = x


def _combine(x1, route, dest, eo, gt2, final_g, seq, final, tc=256):
    t, d = x1.shape
    per_b = seq // tc
    dest3 = dest.reshape(2, t // tc, tc).transpose(1, 0, 2)
    return pl.pallas_call(
        functools.partial(_combine_kernel, tc=tc, final=final),
        grid=(t // tc,),
        in_specs=[pl.BlockSpec((None, 2, tc), lambda i: (i, 0, 0), memory_space=pltpu.SMEM),
                  pl.BlockSpec((tc, d), lambda i: (i, 0)),
                  pl.BlockSpec((tc, LANES), lambda i: (i, 0)),
                  pl.BlockSpec((None, 1, d), lambda i: (i // per_b, 0, 0)),
                  pl.BlockSpec((1, d), lambda i: (0, 0)),
                  pl.BlockSpec(memory_space=pl.ANY)],
        out_specs=pl.BlockSpec((tc, d), lambda i: (i, 0)),
        out_shape=jax.ShapeDtypeStruct((t, d), F32),
        scratch_shapes=[pltpu.VMEM((2, tc) + TOKEN_TILE, F32), pltpu.SemaphoreType.DMA(())],
        compiler_params=_cparams("arbitrary"),
        name="moe_combine",
    )(dest3, x1, route, gt2[:, None, :], final_g.reshape(1, d), eo)


def _moe_plan(route, counts, t):
    e = route[:, 0:2].astype(I32).T
    rank = route[:, 4:6].astype(I32).T
    cnt = counts[0, :MOE_EXPERTS].astype(I32)
    padded = (cnt + MOE_BM - 1) // MOE_BM * MOE_BM
    pend = jnp.cumsum(padded)
    pstart = pend - padded
    ids = jnp.arange(MOE_EXPERTS, dtype=I32)
    dest = rank + jnp.sum(jnp.where(e[..., None] == ids, pstart, 0), axis=-1)
    nb = (2 * t) // MOE_BM + MOE_EXPERTS
    n_used = (pend[-1] // MOE_BM).astype(I32)
    first_row = jnp.minimum(jnp.arange(nb, dtype=I32), n_used - 1) * MOE_BM
    blk_e = jnp.sum((pend[None, :] <= first_row[:, None]).astype(I32), axis=1)
    blk_e = jnp.minimum(blk_e, MOE_EXPERTS - 1)
    return dest, blk_e, n_used.reshape(1), nb * MOE_BM


def kernel(x, c, ada_w, ada_b, norm1_g, w_in, gla_gate_w, gla_gate_b, gla_norm_g, dsa_norm_g, s5_a_re, s5_a_im, s5_log_step, s5_b_re, s5_b_im, s5_c_re, s5_c_im, s5_d, s5_glu_w, s5_glu_b, s5_norm_g, w_out, norm2_g, router_grp_w, router_grp_b, router_exp_w, router_exp_b, exp_w_gate, exp_w_up, exp_w_down, final_norm_g):
    batch, seq, d = x.shape
    depth = ada_w.shape[0]
    t = batch * seq
    mod = _ada_mod(c, ada_w, ada_b)
    x2 = x.reshape(t, d)
    for l in range(depth):
        sh1, sc1, gt1, sh2, sc2, gt2 = (mod[l, j] for j in range(6))
        p = _in_proj(x2, norm1_g[l], sc1, sh1, _layout_w_in(w_in[l]), seq)
        o_gla = _gla(p, gla_gate_w[l], gla_gate_b[l], gla_norm_g[l], batch, seq)
        o_dsa = _dsa(p, dsa_norm_g[l], batch, seq)
        tabs = _s5_tables(s5_a_re[l], s5_a_im[l], s5_log_step[l], s5_b_re[l], s5_b_im[l],
                          s5_c_re[l], s5_c_im[l])
        o_s5 = _s5((p["su0"], p["su1"], p["su2"]), tabs, s5_d[l], s5_glu_w[l], s5_glu_b[l], s5_norm_g[l], batch, seq)
        x1, h2, route, counts = _mix_out(o_gla, o_dsa, o_s5, w_out[l], x2, gt1, norm2_g[l], sc2, sh2,
                                         router_grp_w[l], router_grp_b[l], router_exp_w[l],
                                         router_exp_b[l], seq)
        dest, blk_e, n_used, n_rows = _moe_plan(route, counts, t)
        xs = _dispatch(h2, dest, n_rows)
        eo = _experts(xs, blk_e, n_used, exp_w_gate[l].astype(BF16), exp_w_up[l].astype(BF16),
                      exp_w_down[l].astype(BF16))
        x2 = _combine(x1, route, dest, eo, gt2, final_norm_g, seq, final=(l == depth - 1))
    return x2.reshape(batch, seq, d)
```

```python
import functools
import math

import jax
import jax.numpy as jnp
import numpy as np
from jax import lax
from jax.experimental import pallas as pl
from jax.experimental.pallas import tpu as pltpu

F32 = jnp.float32
BF16 = jnp.bfloat16
I32 = jnp.int32

D_MODEL = 1024
GLA_HEADS = 4
GLA_DV = 96
GLA_DK = 48
GLA_RANK = 16
GLA_TAU = 16.0
GLA_CHUNK = 64
DSA_HEADS = 4
DSA_HEAD_DIM = 64
DSA_WIDTH = DSA_HEADS * DSA_HEAD_DIM
DSA_IDX_HEADS = 8
DSA_IDX_DIM = 32
DSA_TOPK = 256
S5_GROUPS = 24
S5_GROUP_CH = 16
S5_STATE = 64
S5_WIDTH = S5_GROUPS * S5_GROUP_CH
MOE_GROUPS = 4
MOE_EXPERTS_PER_GROUP = 8
MOE_EXPERTS = MOE_GROUPS * MOE_EXPERTS_PER_GROUP
MOE_HIDDEN = 512
RMS_EPS = 1e-6
IN_SIZES = (192, 192, 384, 16, 384, 256, 256, 256, 256, 32, 8, 384)

LANES = 128
SUBLANES = 8
VMEM_LIMIT = 56 * 1024 * 1024

HEAD_PAD = LANES
GLA_W = GLA_HEADS * HEAD_PAD
S5_CHUNK = 16
S5_TILE_G = LANES // S5_GROUP_CH
S5_TILES = S5_GROUPS // S5_TILE_G
S5_ROW_W = S5_CHUNK * LANES
S5_STATE_W = S5_TILE_G * S5_STATE
TOKEN_TILE = (SUBLANES, LANES)
MOE_BM = 256
INT_MIN = -2 ** 31


def _cparams(*sem):
    return pltpu.CompilerParams(dimension_semantics=tuple(sem), vmem_limit_bytes=VMEM_LIMIT)


def _sigmoid(x):
    return 1.0 / (1.0 + jnp.exp(-x))


def _rms(x, g, n=None):
    n = x.shape[-1] if n is None else n
    ms = jnp.sum(x * x, axis=-1, keepdims=True) * (1.0 / n)
    return x * lax.rsqrt(ms + RMS_EPS) * g


def _ada_kernel(c_ref, w_ref, b_ref, o_ref):
    c = c_ref[...]
    cond = (c * _sigmoid(c)).astype(BF16)
    o_ref[...] = jnp.dot(cond, w_ref[...].astype(BF16), preferred_element_type=F32) + b_ref[...]


def _ada_mod(c, ada_w, ada_b):
    depth, d, d6 = ada_w.shape
    nblk = d6 // d
    bp = -(-c.shape[0] // SUBLANES) * SUBLANES
    cp = jnp.pad(c, ((0, bp - c.shape[0]), (0, 0)))
    out = pl.pallas_call(
        _ada_kernel,
        grid=(depth, nblk),
        in_specs=[
            pl.BlockSpec((bp, d), lambda l, j: (0, 0)),
            pl.BlockSpec((None, d, d), lambda l, j: (l, 0, j)),
            pl.BlockSpec((None, None, 1, d), lambda l, j: (l, j, 0, 0)),
        ],
        out_specs=pl.BlockSpec((None, None, bp, d), lambda l, j: (l, j, 0, 0)),
        out_shape=jax.ShapeDtypeStruct((depth, nblk, bp, d), F32),
        compiler_params=_cparams("arbitrary", "arbitrary"),
        name="ada_mod",
    )(cp, ada_w, ada_b.reshape(depth, nblk, 1, d))
    return out[:, :, : c.shape[0], :]


PROJ_OUTS = (
    ("gq", GLA_W, BF16), ("gk", GLA_W, BF16), ("gv", GLA_W, BF16), ("gr", GLA_W, BF16),
    ("glr", LANES, BF16),
    ("dq", DSA_WIDTH, BF16), ("dk", DSA_WIDTH, BF16), ("dv", DSA_WIDTH, BF16),
    ("iq", DSA_IDX_HEADS * DSA_IDX_DIM, BF16), ("ik", DSA_IDX_HEADS * DSA_IDX_DIM, BF16),
    ("iw", LANES, F32), ("su0", LANES, F32), ("su1", LANES, F32), ("su2", LANES, F32),
)
PROJ_COLS = sum(w for _, w, _ in PROJ_OUTS)


def _pad_heads(w, heads, dim):
    d = w.shape[0]
    w = w.reshape(d, heads, dim)
    return jnp.pad(w, ((0, 0), (0, 0), (0, HEAD_PAD - dim))).reshape(d, heads * HEAD_PAD)


def _pad_cols(w, width):
    return jnp.pad(w, ((0, 0), (0, width - w.shape[1])))


def _layout_w_in(w_in):
    offs = np.cumsum((0,) + IN_SIZES)
    p = [w_in[:, offs[i]:offs[i + 1]] for i in range(len(IN_SIZES))]
    cols = [
        _pad_heads(p[0], GLA_HEADS, GLA_DK), _pad_heads(p[1], GLA_HEADS, GLA_DK),
        _pad_heads(p[2], GLA_HEADS, GLA_DV), _pad_heads(p[4], GLA_HEADS, GLA_DV),
        _pad_cols(p[3], LANES),
        p[5], p[6], p[7], p[8], jnp.tile(p[9], (1, DSA_IDX_HEADS)),
        _pad_cols(p[10], LANES), p[11],
    ]
    return jnp.concatenate(cols, axis=1).astype(BF16)


def _proj_kernel(x_ref, g_ref, sc_ref, sh_ref, w_ref, *o_refs):
    x = x_ref[...]
    h = _rms(x, g_ref[...]) * (1.0 + sc_ref[...]) + sh_ref[...]
    hb = h.astype(BF16)
    c0 = 0
    for o_ref, (_, width, _) in zip(o_refs, PROJ_OUTS):
        o_ref[...] = jnp.dot(hb, w_ref[:, c0:c0 + width], preferred_element_type=F32).astype(o_ref.dtype)
        c0 += width


def _in_proj(x2, norm_g, sc, sh, w_lay, seq, tm=512):
    t, d = x2.shape
    per_b = seq // tm
    outs = pl.pallas_call(
        _proj_kernel,
        grid=(t // tm,),
        in_specs=[
            pl.BlockSpec((tm, d), lambda i: (i, 0)),
            pl.BlockSpec((1, d), lambda i: (0, 0)),
            pl.BlockSpec((None, 1, d), lambda i: (i // per_b, 0, 0)),
            pl.BlockSpec((None, 1, d), lambda i: (i // per_b, 0, 0)),
            pl.BlockSpec((d, PROJ_COLS), lambda i: (0, 0)),
        ],
        out_specs=[pl.BlockSpec((tm, w), lambda i: (i, 0)) for _, w, _ in PROJ_OUTS],
        out_shape=[jax.ShapeDtypeStruct((t, w), dt) for _, w, dt in PROJ_OUTS],
        compiler_params=_cparams("arbitrary"),
        name="in_proj",
    )(x2, norm_g.reshape(1, d), sc[:, None, :], sh[:, None, :], w_lay)
    return {name: o for (name, _, _), o in zip(PROJ_OUTS, outs)}


def _gla_kernel(q_ref, k_ref, v_ref, r_ref, lr_ref, gw_ref, gb_ref, ng_ref, o_ref, s_ref, *, rows):
    @pl.when(pl.program_id(1) == 0)
    def _():
        s_ref[...] = jnp.zeros_like(s_ref)

    c = GLA_CHUNK
    gl = jnp.dot(lr_ref[...], gw_ref[...], preferred_element_type=F32) + gb_ref[...]
    g = (jnp.minimum(gl, 0.0) - jnp.log1p(jnp.exp(-jnp.abs(gl)))) * (1.0 / GLA_TAU)
    g1 = g.astype(BF16)
    rem = g - g1.astype(F32)
    g2 = rem.astype(BF16)
    g3 = (rem - g2.astype(F32)).astype(BF16)
    ri = lax.broadcasted_iota(I32, (c, c), 0)
    ci = lax.broadcasted_iota(I32, (c, c), 1)
    causal = ri >= ci
    tri = causal.astype(BF16)
    ng = ng_ref[...]
    for ch in range(rows // c):
        rs = slice(ch * c, (ch + 1) * c)
        bcum = (jnp.dot(tri, g1[rs], preferred_element_type=F32)
                + jnp.dot(tri, g2[rs], preferred_element_type=F32)
                + jnp.dot(tri, g3[rs], preferred_element_type=F32))
        blast = bcum[c - 1:c, :]
        e_pos = jnp.exp(bcum)
        kf = k_ref[rs, :].astype(F32)
        qd = (q_ref[rs, :].astype(F32) * (GLA_DK ** -0.5) * e_pos).astype(BF16)
        kd = (kf * jnp.exp(-bcum)).astype(BF16)
        ku = (kf * jnp.exp(blast - bcum)).astype(BF16)
        dec = jnp.exp(blast)
        for h in range(GLA_HEADS):
            hs = slice(h * HEAD_PAD, (h + 1) * HEAD_PAD)
            vh = v_ref[rs, hs]
            att = lax.dot_general(qd[:, hs], kd[:, hs], (((1,), (1,)), ((), ())),
                                  preferred_element_type=F32)
            att = jnp.where(causal, att, 0.0).astype(BF16)
            st = s_ref[h]
            o = jnp.dot(att, vh, preferred_element_type=F32)
            o = o + lax.dot_general(qd[:, hs], st.astype(BF16), (((1,), (1,)), ((), ())),
                                    preferred_element_type=F32)
            upd = lax.dot_general(vh, ku[:, hs], (((0,), (0,)), ((), ())),
                                  preferred_element_type=F32)
            s_ref[h] = st * dec[:, hs] + upd
            y = _rms(o, ng[:, hs], GLA_DV)
            rr = r_ref[rs, hs].astype(F32)
            o_ref[rs, hs] = (y * (rr * _sigmoid(rr))).astype(o_ref.dtype)


def _gla(p, gate_w, gate_b, norm_g, batch, seq, rows=512):
    gw = _pad_cols(jnp.pad(_pad_heads(gate_w, GLA_HEADS, GLA_DK), ((0, LANES - GLA_RANK), (0, 0))), GLA_W)
    gb = _pad_heads(gate_b.reshape(1, -1), GLA_HEADS, GLA_DK)
    ng = _pad_heads(jnp.tile(norm_g.reshape(1, GLA_DV), (1, GLA_HEADS)), GLA_HEADS, GLA_DV)
    rows = min(rows, seq)
    blk = lambda w: pl.BlockSpec((None, rows, w), lambda b, j: (b, j, 0))
    full = lambda a: pl.BlockSpec(a.shape, lambda b, j: (0, 0))
    r3 = lambda a: a.reshape(batch, seq, a.shape[-1])
    out = pl.pallas_call(
        functools.partial(_gla_kernel, rows=rows),
        grid=(batch, seq // rows),
        in_specs=[blk(GLA_W), blk(GLA_W), blk(GLA_W), blk(GLA_W), blk(LANES),
                  full(gw), full(gb), full(ng)],
        out_specs=blk(GLA_W),
        out_shape=jax.ShapeDtypeStruct((batch, seq, GLA_W), BF16),
        scratch_shapes=[pltpu.VMEM((GLA_HEADS, HEAD_PAD, HEAD_PAD), F32)],
        compiler_params=_cparams("arbitrary", "arbitrary"),
        name="gla",
    )(r3(p["gq"]), r3(p["gk"]), r3(p["gv"]), r3(p["gr"]), r3(p["glr"]), gw.astype(BF16), gb, ng)
    return out.reshape(batch * seq, GLA_W)


def _dsa_kernel(iq_ref, ik_ref, iw_ref, q_ref, k_ref, v_ref, ng_ref, o_ref,
                qs_ref, q4_ref, keys_ref, acc_ref, run_ref, top_ref, *, tq, kc, sub, n_sel, top_r):
    qi = pl.program_id(1)
    nkc = ((qi + 1) * tq + kc - 1) // kc
    lane_w = lax.broadcasted_iota(I32, (tq, DSA_WIDTH), 1)
    iq = iq_ref[...]
    for h in range(DSA_IDX_HEADS):
        m = (lane_w >= h * DSA_IDX_DIM) & (lane_w < (h + 1) * DSA_IDX_DIM)
        qs_ref[h * tq:(h + 1) * tq, :] = jnp.where(m, iq, jnp.zeros_like(iq))
    iw = iw_ref[...] * (DSA_IDX_HEADS ** -0.5 * DSA_IDX_DIM ** -0.5)
    lane = lax.broadcasted_iota(I32, (tq, LANES), 1)
    w_cols = [jnp.sum(jnp.where(lane == h, iw, 0.0), axis=1, keepdims=True) for h in range(DSA_IDX_HEADS)]
    qpos = qi * tq + lax.broadcasted_iota(I32, (tq, sub), 0)
    kcol = lax.broadcasted_iota(I32, (tq, sub), 1)
    nt = (((1,), (1,)), ((), ()))

    ninf = jnp.float32(-jnp.inf)

    def score_body(c, carry):
        for u in range(kc // sub):
            k0 = pl.multiple_of(c * kc + u * sub, sub)
            lg = lax.dot_general(qs_ref[...], ik_ref[pl.ds(k0, sub), :], nt,
                                 preferred_element_type=F32)
            sc = jnp.zeros((tq, sub), F32)
            for h in range(DSA_IDX_HEADS):
                sc = sc + w_cols[h] * jnp.maximum(lg[h * tq:(h + 1) * tq, :], 0.0)
            sc = sc + 0.0
            keys_ref[c, :, u * sub:(u + 1) * sub] = jnp.where(kcol + k0 <= qpos, sc, ninf)
        return carry

    lax.fori_loop(0, nkc, score_body, 0)

    def to_key(x):
        bits = lax.bitcast_convert_type(x, I32)
        key = jnp.where(bits < 0, bits ^ jnp.int32(0x7FFFFFFF), bits)
        return jnp.where(x == ninf, jnp.int32(INT_MIN), key)

    def count_ge(cand):
        candb = jnp.broadcast_to(cand, (tq, LANES))

        def body(c, acc):
            blk = keys_ref[c]
            for j in range(kc // LANES):
                acc = acc + jnp.where(to_key(blk[:, j * LANES:(j + 1) * LANES]) >= candb, 1.0, 0.0)
            return acc

        acc = lax.fori_loop(0, nkc, body, jnp.zeros((tq, LANES), F32))
        return jnp.sum(acc, axis=1, keepdims=True)

    need = jnp.float32(n_sel)

    def select(count):
        t0 = jnp.full((tq, 1), INT_MIN, I32)
        zero = jnp.zeros((tq, 1), I32)
        t0 = jnp.where(count(zero) >= need, zero, t0)

        def bit_body(i, t):
            cand = t | (jnp.int32(1) << (30 - i))
            return jnp.where(count(cand) >= need, cand, t)

        t = lax.fori_loop(0, 31, bit_body, t0)
        t = jnp.maximum(t, jnp.int32(INT_MIN + 1))
        return t, count(t + 1), count(t)

    lowest = jnp.full((SUBLANES, LANES), ninf, F32)
    for rt in range(0, tq // SUBLANES, 2):
        rsl = [slice((rt + a) * SUBLANES, (rt + a + 1) * SUBLANES) for a in range(2)]

        def top_body(c, ts, rsl=rsl):
            ts = [list(t) for t in ts]
            for j in range(kc // LANES):
                for a in range(2):
                    x = keys_ref[c, rsl[a], j * LANES:(j + 1) * LANES]
                    for r in range(top_r):
                        hi = jnp.maximum(ts[a][r], x)
                        x = jnp.minimum(ts[a][r], x)
                        ts[a][r] = hi
            return tuple(tuple(t) for t in ts)

        ts = lax.fori_loop(0, nkc, top_body, ((lowest,) * top_r,) * 2)
        for a in range(2):
            for r in range(top_r):
                top_ref[r, rsl[a], :] = to_key(ts[a][r])

    def count_cand(cand):
        candb = jnp.broadcast_to(cand, (tq, LANES))
        acc = jnp.zeros((tq, LANES), F32)
        for r in range(top_r):
            acc = acc + jnp.where(top_ref[r] >= candb, 1.0, 0.0)
        return jnp.sum(acc, axis=1, keepdims=True)

    thr_c, n_gt_c, n_ge_c = select(count_cand)
    hidden = jnp.max(jnp.where(top_ref[top_r - 1] >= thr_c, 1.0, 0.0)) > 0.0
    thr, n_gt, n_ge = lax.cond(hidden, lambda: select(count_ge), lambda: (thr_c, n_gt_c, n_ge_c))
    n_tie = need - n_gt
    any_tie = jnp.max(n_ge) > need
    thr_bits = jnp.where(thr < 0, thr ^ jnp.int32(0x7FFFFFFF), thr)
    thr_f = jnp.where(thr == jnp.int32(INT_MIN + 1), jnp.float32(jnp.finfo(jnp.float32).min),
                      lax.bitcast_convert_type(thr_bits, F32))

    ui = lax.broadcasted_iota(I32, (kc, kc), 0)
    uj = lax.broadcasted_iota(I32, (kc, kc), 1)
    before = (ui < uj).astype(BF16)
    q = q_ref[...] * jnp.asarray(DSA_HEAD_DIM ** -0.5, BF16)
    for h in range(DSA_HEADS):
        m = (lane_w >= h * DSA_HEAD_DIM) & (lane_w < (h + 1) * DSA_HEAD_DIM)
        q4_ref[h * tq:(h + 1) * tq, :] = jnp.where(m, q, jnp.zeros_like(q))
    neg = jnp.float32(-1e30)
    hrows = lambda a, h: a[h * tq:(h + 1) * tq]

    def lane_fold(op, acc, s):
        for j in range(kc // LANES):
            acc = op(acc, s[:, j * LANES:(j + 1) * LANES])
        return acc

    run_ref[...] = jnp.full(run_ref.shape, neg, F32)

    def mask_pass(with_ties):
        def mask_body(c, seen):
            k0 = pl.multiple_of(c * kc, kc)
            sc = keys_ref[c]
            if with_ties:
                eq = sc == thr_f
                eqf = jnp.where(eq, 1.0, 0.0)
                rank = seen + jnp.dot(eqf.astype(BF16), before, preferred_element_type=F32)
                sel = (sc > thr_f) | (eq & (rank < n_tie))
                seen = seen + jnp.sum(eqf, axis=1, keepdims=True)
            else:
                sel = sc >= thr_f
            bias = jnp.where(sel, 0.0, neg)
            keys_ref[c] = bias
            s = lax.dot_general(q4_ref[...], k_ref[pl.ds(k0, kc), :], nt, preferred_element_type=F32)
            for h in range(DSA_HEADS):
                run_ref[h] = lane_fold(jnp.maximum, run_ref[h], hrows(s, h) + bias)
            return seen

        lax.fori_loop(0, nkc, mask_body, jnp.zeros((tq, 1), F32))
        return 0

    lax.cond(any_tie, lambda: mask_pass(True), lambda: mask_pass(False))
    ms = [jnp.max(run_ref[h], axis=1, keepdims=True) for h in range(DSA_HEADS)]

    run_ref[...] = jnp.zeros_like(run_ref)
    acc_ref[...] = jnp.zeros_like(acc_ref)

    def att_body(c, carry):
        k0 = pl.multiple_of(c * kc, kc)
        bias = keys_ref[c]
        s = lax.dot_general(q4_ref[...], k_ref[pl.ds(k0, kc), :], nt, preferred_element_type=F32)
        ps = []
        for h in range(DSA_HEADS):
            p = jnp.exp(hrows(s, h) + bias - ms[h])
            run_ref[h] = lane_fold(jnp.add, run_ref[h], p)
            ps.append(p.astype(BF16))
        acc_ref[...] += jnp.dot(jnp.concatenate(ps, axis=0), v_ref[pl.ds(k0, kc), :],
                                preferred_element_type=F32)
        return carry

    lax.fori_loop(0, nkc, att_body, 0)
    out = jnp.zeros((tq, DSA_WIDTH), F32)
    for h in range(DSA_HEADS):
        m = (lane_w >= h * DSA_HEAD_DIM) & (lane_w < (h + 1) * DSA_HEAD_DIM)
        out = out + jnp.where(m, acc_ref[h * tq:(h + 1) * tq, :] / jnp.sum(run_ref[h], axis=1, keepdims=True), 0.0)
    o_ref[...] = _rms(out, ng_ref[...]).astype(o_ref.dtype)


def _dsa(p, norm_g, batch, seq, tq=128, kc=512, sub=256, top_r=12):
    n_sel = min(DSA_TOPK, seq // 4)
    kc = min(kc, seq)
    sub = min(sub, kc)
    r3 = lambda a: a.reshape(batch, seq, a.shape[-1])
    qblk = lambda w: pl.BlockSpec((None, tq, w), lambda b, i: (b, i, 0))
    kfull = lambda w: pl.BlockSpec((None, seq, w), lambda b, i: (b, 0, 0))
    out = pl.pallas_call(
        functools.partial(_dsa_kernel, tq=tq, kc=kc, sub=sub, n_sel=n_sel, top_r=top_r),
        grid=(batch, seq // tq),
        in_specs=[qblk(DSA_WIDTH), kfull(DSA_WIDTH), qblk(LANES), qblk(DSA_WIDTH),
                  kfull(DSA_WIDTH), kfull(DSA_WIDTH), pl.BlockSpec((1, DSA_WIDTH), lambda b, i: (0, 0))],
        out_specs=qblk(DSA_WIDTH),
        out_shape=jax.ShapeDtypeStruct((batch, seq, DSA_WIDTH), BF16),
        scratch_shapes=[
            pltpu.VMEM((DSA_IDX_HEADS * tq, DSA_WIDTH), BF16),
            pltpu.VMEM((DSA_HEADS * tq, DSA_WIDTH), BF16),
            pltpu.VMEM((seq // kc, tq, kc), F32),
            pltpu.VMEM((DSA_HEADS * tq, DSA_WIDTH), F32),
            pltpu.VMEM((DSA_HEADS, tq, LANES), F32),
            pltpu.VMEM((top_r, tq, LANES), I32),
        ],
        compiler_params=_cparams("arbitrary", "arbitrary"),
        name="dsa",
    )(r3(p["iq"]), r3(p["ik"]), r3(p["iw"]), r3(p["dq"]), r3(p["dk"]), r3(p["dv"]),
      norm_g.reshape(1, DSA_WIDTH))
    return out.reshape(batch * seq, DSA_WIDTH)


def _s5_tables(a_re, a_im, log_step, b_re, b_im, c_re, c_im):
    hp = lax.Precision.HIGHEST
    s = S5_CHUNK
    dt = jnp.exp(log_step)[:, None]
    mag = jnp.exp(a_re * dt)
    lam_re, lam_im = mag * jnp.cos(a_im * dt), mag * jnp.sin(a_im * dt)
    den = a_re * a_re + a_im * a_im
    nr, ni = lam_re - 1.0, lam_im
    coef_re = (nr * a_re + ni * a_im) / den
    coef_im = (ni * a_re - nr * a_im) / den
    bb_re = coef_re[..., None] * b_re - coef_im[..., None] * b_im
    bb_im = coef_re[..., None] * b_im + coef_im[..., None] * b_re
    j = jnp.arange(s + 1, dtype=F32)[:, None, None]
    pmag = jnp.exp(a_re * dt * j)
    pw_re, pw_im = pmag * jnp.cos(a_im * dt * j), pmag * jnp.sin(a_im * dt * j)
    e_re = c_re[None] * pw_re[:, :, None, :] - c_im[None] * pw_im[:, :, None, :]
    e_im = c_re[None] * pw_im[:, :, None, :] + c_im[None] * pw_re[:, :, None, :]
    kk = (jnp.einsum("jgcp,gpd->jgcd", e_re, bb_re, precision=hp)
          - jnp.einsum("jgcp,gpd->jgcd", e_im, bb_im, precision=hp))
    lag = np.arange(s)[None, :] - np.arange(s)[:, None]
    toe = jnp.where((lag >= 0)[:, :, None, None, None], kk[np.clip(lag, 0, s)], 0.0)
    rp_re, rp_im = pw_re[s - 1 - np.arange(s)], pw_im[s - 1 - np.arange(s)]
    x_re = rp_re[..., None] * bb_re[None] - rp_im[..., None] * bb_im[None]
    x_im = rp_re[..., None] * bb_im[None] + rp_im[..., None] * bb_re[None]
    f_re, f_im = e_re[1:], e_im[1:]
    eye = jnp.eye(S5_TILE_G, dtype=F32)
    tg = lambda a, ax: a.reshape(a.shape[:ax] + (S5_TILES, S5_TILE_G) + a.shape[ax + 1:])
    w_toe = (tg(toe, 2).transpose(2, 0, 3, 5, 1, 4)[:, :, :, :, :, None, :]
             * eye[None, None, :, None, None, :, None]).reshape(S5_TILES, S5_ROW_W, S5_ROW_W)
    w1 = lambda x: (tg(x, 1).transpose(1, 0, 2, 4, 3)[:, :, :, :, None, :]
                    * eye[None, None, :, None, :, None]).reshape(S5_TILES, S5_ROW_W, S5_STATE_W)
    w2 = lambda f: (tg(f, 1).transpose(1, 2, 4, 0, 3)[:, :, :, :, None, :]
                    * eye[None, :, None, None, :, None]).reshape(S5_TILES, S5_STATE_W, S5_ROW_W)
    tabs = dict(toe=w_toe, w1_re=w1(x_re), w1_im=w1(x_im), w2_re=w2(f_re), w2_im=w2(-f_im))
    tabs = {k: v.astype(BF16) for k, v in tabs.items()}
    tabs["a_re"], tabs["a_im"] = pw_re[s].reshape(-1), pw_im[s].reshape(-1)
    return tabs


def _s5_state_in_kernel(u_ref, wr_ref, wi_ref, xr_ref, xi_ref):
    u = u_ref[...].astype(BF16)
    xr_ref[...] = jnp.dot(u, wr_ref[...], preferred_element_type=F32)
    xi_ref[...] = jnp.dot(u, wi_ref[...], preferred_element_type=F32)


def _s5_scan_kernel(xr_ref, xi_ref, ar_ref, ai_ref, hr_ref, hi_ref, cr_ref, ci_ref, *, steps):
    @pl.when(pl.program_id(0) == 0)
    def _():
        cr_ref[...] = jnp.zeros_like(cr_ref)
        ci_ref[...] = jnp.zeros_like(ci_ref)

    ar, ai = ar_ref[...], ai_ref[...]

    def body(n, carry):
        hr, hi = carry
        hr_ref[n] = hr
        hi_ref[n] = hi
        return ar * hr - ai * hi + xr_ref[n], ar * hi + ai * hr + xi_ref[n]

    hr, hi = lax.fori_loop(0, steps, body, (cr_ref[...], ci_ref[...]))
    cr_ref[...] = hr
    ci_ref[...] = hi


def _s5_out_kernel(u_ref, toe_ref, hr_ref, hi_ref, w2r_ref, w2i_ref, y_ref):
    y = jnp.dot(u_ref[...].astype(BF16), toe_ref[...], preferred_element_type=F32)
    y = y + jnp.dot(hr_ref[...].astype(BF16), w2r_ref[...], preferred_element_type=F32)
    y = y + jnp.dot(hi_ref[...].astype(BF16), w2i_ref[...], preferred_element_type=F32)
    y_ref[...] = y


def _s5_post_kernel(y_ref, u0_ref, u1_ref, u2_ref, d_ref, gw_ref, gb_ref, ng_ref, o_ref):
    y = jnp.concatenate([y_ref[q] for q in range(S5_TILES)], axis=1)
    u = jnp.concatenate([u0_ref[...], u1_ref[...], u2_ref[...]], axis=1)
    z = y + d_ref[...] * u
    z = 0.5 * z * (1.0 + jnp.tanh(math.sqrt(2.0 / math.pi) * (z + 0.044715 * (z * z * z))))
    gate = _sigmoid(jnp.dot(z.astype(BF16), gw_ref[...], preferred_element_type=F32) + gb_ref[...])
    o_ref[...] = _rms(z * gate, ng_ref[...]).astype(o_ref.dtype)


def _s5(us, tabs, d, glu_w, glu_b, norm_g, batch, seq, tm=512, rb=512):
    s = S5_CHUNK
    nch = seq // s
    rows = nch * batch
    rb = min(rb, rows)
    gp = S5_GROUPS * S5_STATE
    xq = jnp.stack([u.reshape(rows, S5_ROW_W) for u in us])
    row_in = pl.BlockSpec((None, rb, S5_ROW_W), lambda q, i: (q, i, 0))
    wspec = lambda a: pl.BlockSpec((None,) + a.shape[1:], lambda q, i: (q, 0, 0))
    colblk = pl.BlockSpec((rb, S5_STATE_W), lambda q, i: (i, q))
    xr, xi = pl.pallas_call(
        _s5_state_in_kernel,
        grid=(S5_TILES, rows // rb),
        in_specs=[row_in, wspec(tabs["w1_re"]), wspec(tabs["w1_im"])],
        out_specs=[colblk, colblk],
        out_shape=[jax.ShapeDtypeStruct((rows, gp), F32)] * 2,
        compiler_params=_cparams("arbitrary", "arbitrary"),
        name="s5_state_in",
    )(xq, tabs["w1_re"], tabs["w1_im"])
    parts = SUBLANES // batch
    wl = gp // parts
    slab = lambda a: a.reshape(batch, nch, gp).transpose(1, 0, 2).reshape(nch, SUBLANES, wl)
    unslab = lambda a: a.reshape(nch, batch, gp).transpose(1, 0, 2).reshape(rows, gp)
    coef = lambda a: jnp.tile(a.reshape(parts, wl), (batch, 1))
    steps = min(64, nch)
    sblk = pl.BlockSpec((steps, SUBLANES, wl), lambda i: (i, 0, 0))
    cblk = pl.BlockSpec((SUBLANES, wl), lambda i: (0, 0))
    hr, hi = pl.pallas_call(
        functools.partial(_s5_scan_kernel, steps=steps),
        grid=(nch // steps,),
        in_specs=[sblk, sblk, cblk, cblk],
        out_specs=[sblk, sblk],
        out_shape=[jax.ShapeDtypeStruct((nch, SUBLANES, wl), F32)] * 2,
        scratch_shapes=[pltpu.VMEM((SUBLANES, wl), F32)] * 2,
        compiler_params=_cparams("arbitrary"),
        name="s5_scan",
    )(slab(xr), slab(xi), coef(tabs["a_re"]), coef(tabs["a_im"]))
    y = pl.pallas_call(
        _s5_out_kernel,
        grid=(S5_TILES, rows // rb),
        in_specs=[row_in, wspec(tabs["toe"]), colblk, colblk, wspec(tabs["w2_re"]), wspec(tabs["w2_im"])],
        out_specs=row_in,
        out_shape=jax.ShapeDtypeStruct((S5_TILES, rows, S5_ROW_W), F32),
        compiler_params=_cparams("arbitrary", "arbitrary"),
        name="s5_out",
    )(xq, tabs["toe"], unslab(hr), unslab(hi), tabs["w2_re"], tabs["w2_im"])
    t = batch * seq
    y = y.reshape(S5_TILES, t, LANES)
    tm = min(tm, t)
    tok = lambda w: pl.BlockSpec((tm, w), lambda i: (i, 0))
    vec = pl.BlockSpec((1, S5_WIDTH), lambda i: (0, 0))
    return pl.pallas_call(
        _s5_post_kernel,
        grid=(t // tm,),
        in_specs=[pl.BlockSpec((S5_TILES, tm, LANES), lambda i: (0, i, 0)), tok(LANES), tok(LANES), tok(LANES),
                  vec, pl.BlockSpec((S5_WIDTH, S5_WIDTH), lambda i: (0, 0)), vec, vec],
        out_specs=tok(S5_WIDTH),
        out_shape=jax.ShapeDtypeStruct((t, S5_WIDTH), BF16),
        compiler_params=_cparams("arbitrary"),
        name="s5_post",
    )(y, us[0], us[1], us[2], d.reshape(1, -1), glu_w.astype(BF16), glu_b.reshape(1, -1),
      norm_g.reshape(1, -1))


def _mix_out_kernel(og_ref, od_ref, os_ref, wg_ref, wd_ref, ws_ref, x_ref, gt_ref, n2_ref, sc_ref,
                    sh_ref, rw1_ref, rw2_ref, rb_ref, x1_ref, h2_ref, route_ref, cnt_ref, carry_ref, *, tm):
    @pl.when(pl.program_id(0) == 0)
    def _():
        carry_ref[...] = jnp.zeros_like(carry_ref)

    mix = jnp.dot(og_ref[...], wg_ref[...], preferred_element_type=F32)
    mix = mix + jnp.dot(od_ref[...], wd_ref[...], preferred_element_type=F32)
    mix = mix + jnp.dot(os_ref[...], ws_ref[...], preferred_element_type=F32)
    x1 = x_ref[...] + gt_ref[...] * mix
    x1_ref[...] = x1
    h2 = _rms(x1, n2_ref[...]) * (1.0 + sc_ref[...]) + sh_ref[...]
    for s in range(SUBLANES):
        h2_ref[:, s, :] = h2[:, s * LANES:(s + 1) * LANES]
    ha = h2.astype(BF16)
    hb = (h2 - ha.astype(F32)).astype(BF16)
    logit = (jnp.dot(ha, rw1_ref[...], preferred_element_type=F32)
             + jnp.dot(ha, rw2_ref[...], preferred_element_type=F32)
             + jnp.dot(hb, rw1_ref[...], preferred_element_type=F32)) + rb_ref[...]
    lane = lax.broadcasted_iota(I32, (tm, LANES), 1)
    lanef = lane.astype(F32)
    ninf = jnp.float32(-jnp.inf)
    big = jnp.float32(LANES)
    gl = jnp.where(lane < MOE_GROUPS, logit, ninf)
    gmax = jnp.max(gl, axis=1, keepdims=True)
    grp_p = 1.0 / jnp.sum(jnp.exp(gl - gmax), axis=1, keepdims=True)
    grp_i = jnp.min(jnp.where(gl == gmax, lanef, big), axis=1, keepdims=True)
    lo = MOE_GROUPS + MOE_EXPERTS_PER_GROUP * grp_i
    el = jnp.where((lanef >= lo) & (lanef < lo + MOE_EXPERTS_PER_GROUP), logit, ninf)
    l1 = jnp.max(el, axis=1, keepdims=True)
    i1 = jnp.min(jnp.where(el == l1, lanef, big), axis=1, keepdims=True)
    el2 = jnp.where(lanef == i1, ninf, el)
    l2 = jnp.max(el2, axis=1, keepdims=True)
    i2 = jnp.min(jnp.where(el2 == l2, lanef, big), axis=1, keepdims=True)
    e21 = jnp.exp(l2 - l1)
    w1 = grp_p / (1.0 + e21)
    w2 = grp_p * e21 / (1.0 + e21)
    e1 = i1 - MOE_GROUPS
    e2 = i2 - MOE_GROUPS
    oh1 = lanef == e1
    oh2 = lanef == e2
    hot = jnp.where(oh1 | oh2, 1.0, 0.0)
    ri = lax.broadcasted_iota(I32, (tm, tm), 0)
    ci = lax.broadcasted_iota(I32, (tm, tm), 1)
    before = (ci < ri).astype(BF16)
    prior = carry_ref[...] + jnp.dot(before, hot.astype(BF16), preferred_element_type=F32)
    r1 = jnp.sum(jnp.where(oh1, prior, 0.0), axis=1, keepdims=True)
    r2 = jnp.sum(jnp.where(oh2, prior, 0.0), axis=1, keepdims=True)
    total = carry_ref[...] + jnp.sum(hot, axis=0, keepdims=True)
    carry_ref[...] = total
    cnt_ref[...] = jnp.broadcast_to(total, cnt_ref.shape)
    route = jnp.where(lane == 0, e1, jnp.where(lane == 1, e2, jnp.where(lane == 2, w1, jnp.where(
        lane == 3, w2, jnp.where(lane == 4, r1, jnp.where(lane == 5, r2, 0.0))))))
    route_ref[...] = route


def _mix_out(o_gla, o_dsa, o_s5, w_out, x2, gt1, norm2_g, sc2, sh2, rgw, rgb, rew, reb, seq, tm=256):
    t, d = x2.shape
    per_b = seq // tm
    wg = jnp.pad(w_out[:GLA_HEADS * GLA_DV].reshape(GLA_HEADS, GLA_DV, d),
                 ((0, 0), (0, HEAD_PAD - GLA_DV), (0, 0))).reshape(GLA_W, d).astype(BF16)
    wd = w_out[GLA_HEADS * GLA_DV:GLA_HEADS * GLA_DV + DSA_WIDTH].astype(BF16)
    ws = w_out[GLA_HEADS * GLA_DV + DSA_WIDTH:].astype(BF16)
    rw = _pad_cols(jnp.concatenate([rgw, rew], axis=1), LANES)
    rb = _pad_cols(jnp.concatenate([rgb, reb]).reshape(1, -1), LANES)
    rw1 = rw.astype(BF16)
    rw2 = (rw - rw1.astype(F32)).astype(BF16)
    tok = lambda w: pl.BlockSpec((tm, w), lambda i: (i, 0))
    full = lambda a: pl.BlockSpec(a.shape, lambda i: (0, 0))
    perb = pl.BlockSpec((None, 1, d), lambda i: (i // per_b, 0, 0))
    vec = pl.BlockSpec((1, d), lambda i: (0, 0))
    return pl.pallas_call(
        functools.partial(_mix_out_kernel, tm=tm),
        grid=(t // tm,),
        in_specs=[tok(GLA_W), tok(DSA_WIDTH), tok(S5_WIDTH), full(wg), full(wd), full(ws), tok(d),
                  perb, vec, perb, perb, full(rw1), full(rw2), full(rb)],
        out_specs=[tok(d), pl.BlockSpec((tm,) + TOKEN_TILE, lambda i: (i, 0, 0)), tok(LANES),
                   pl.BlockSpec((SUBLANES, LANES), lambda i: (0, 0))],
        out_shape=[jax.ShapeDtypeStruct((t, d), F32), jax.ShapeDtypeStruct((t,) + TOKEN_TILE, F32),
                   jax.ShapeDtypeStruct((t, LANES), F32), jax.ShapeDtypeStruct((SUBLANES, LANES), F32)],
        scratch_shapes=[pltpu.VMEM((1, LANES), F32)],
        compiler_params=_cparams("arbitrary"),
        name="mix_out",
    )(o_gla, o_dsa, o_s5, wg, wd, ws, x2, gt1[:, None, :], norm2_g.reshape(1, d), sc2[:, None, :],
      sh2[:, None, :], rw1, rw2, rb)


def _tile_rows_to_2d(ref):
    return jnp.concatenate([ref[:, s, :] for s in range(SUBLANES)], axis=1)


def _dispatch_kernel(dest_ref, h_ref, zero_ref, out_ref, sem, *, td):
    del zero_ref

    def start(r, c):
        pltpu.make_async_copy(h_ref.at[r], out_ref.at[dest_ref[0, r]], sem).start()
        pltpu.make_async_copy(h_ref.at[r], out_ref.at[dest_ref[1, r]], sem).start()
        return c

    lax.fori_loop(0, td, start, 0, unroll=8)
    for _ in range(2):
        pltpu.make_async_copy(h_ref, out_ref.at[pl.ds(0, td)], sem).wait()


def _dispatch(h2, dest, n_rows, td=256):
    t = h2.shape[0]
    dest3 = dest.reshape(2, t // td, td).transpose(1, 0, 2)
    zeros = jnp.zeros((n_rows,) + TOKEN_TILE, F32)
    return pl.pallas_call(
        functools.partial(_dispatch_kernel, td=td),
        grid=(t // td,),
        in_specs=[pl.BlockSpec((None, 2, td), lambda i: (i, 0, 0), memory_space=pltpu.SMEM),
                  pl.BlockSpec((td,) + TOKEN_TILE, lambda i: (i, 0, 0)),
                  pl.BlockSpec(memory_space=pl.ANY)],
        out_specs=pl.BlockSpec(memory_space=pl.ANY),
        out_shape=jax.ShapeDtypeStruct((n_rows,) + TOKEN_TILE, F32),
        scratch_shapes=[pltpu.SemaphoreType.DMA(())],
        input_output_aliases={2: 0},
        compiler_params=_cparams("arbitrary"),
        name="moe_dispatch",
    )(dest3, h2, zeros)


def _expert_kernel(be_ref, nb_ref, x_ref, wg_ref, wu_ref, wd_ref, o_ref):
    del be_ref
    i = pl.program_id(0)

    @pl.when(i < nb_ref[0])
    def _():
        xb = _tile_rows_to_2d(x_ref).astype(BF16)
        g = jnp.dot(xb, wg_ref[...], preferred_element_type=F32)
        u = jnp.dot(xb, wu_ref[...], preferred_element_type=F32)
        hid = (g * _sigmoid(g) * u).astype(BF16)
        out = jnp.dot(hid, wd_ref[...], preferred_element_type=F32)
        for s in range(SUBLANES):
            o_ref[:, s, :] = out[:, s * LANES:(s + 1) * LANES]

    @pl.when(i >= nb_ref[0])
    def _():
        o_ref[...] = jnp.zeros_like(o_ref)


def _experts(xs, blk_e, n_used, wg, wu, wd):
    n_rows = xs.shape[0]
    nb = n_rows // MOE_BM
    d, hdim = wg.shape[-2:]
    rows = pl.BlockSpec((MOE_BM,) + TOKEN_TILE, lambda i, be, nu: (i, 0, 0))
    return pl.pallas_call(
        _expert_kernel,
        grid_spec=pltpu.PrefetchScalarGridSpec(
            num_scalar_prefetch=2,
            grid=(nb,),
            in_specs=[
                rows,
                pl.BlockSpec((None, d, hdim), lambda i, be, nu: (be[i], 0, 0)),
                pl.BlockSpec((None, d, hdim), lambda i, be, nu: (be[i], 0, 0)),
                pl.BlockSpec((None, hdim, d), lambda i, be, nu: (be[i], 0, 0)),
            ],
            out_specs=rows,
        ),
        out_shape=jax.ShapeDtypeStruct((n_rows,) + TOKEN_TILE, F32),
        compiler_params=_cparams("arbitrary"),
        name="moe_experts",
    )(blk_e, n_used, xs, wg, wu, wd)


def _combine_kernel(dest_ref, x_ref, route_ref, gt_ref, fg_ref, eo_ref, o_ref, rows_ref, sem, *, tc, final):
    def start(r, c):
        pltpu.make_async_copy(eo_ref.at[dest_ref[0, r]], rows_ref.at[0, r], sem).start()
        pltpu.make_async_copy(eo_ref.at[dest_ref[1, r]], rows_ref.at[1, r], sem).start()
        return c

    lax.fori_loop(0, tc, start, 0, unroll=8)
    for which in range(2):
        pltpu.make_async_copy(eo_ref.at[pl.ds(0, tc)], rows_ref.at[which], sem).wait()
    route = route_ref[...]
    lane = lax.broadcasted_iota(I32, route.shape, 1)
    w1 = jnp.sum(jnp.where(lane == 2, route, 0.0), axis=1, keepdims=True)
    w2 = jnp.sum(jnp.where(lane == 3, route, 0.0), axis=1, keepdims=True)
    y = _tile_rows_to_2d(rows_ref.at[0]) * w1 + _tile_rows_to_2d(rows_ref.at[1]) * w2
    x = x_ref[...] + gt_ref[...] * y
    if final:
        x = _rms(x, fg_ref[...])
    o_ref[...] = x


def _combine(x1, route, dest, eo, gt2, final_g, seq, final, tc=256):
    t, d = x1.shape
    per_b = seq // tc
    dest3 = dest.reshape(2, t // tc, tc).transpose(1, 0, 2)
    return pl.pallas_call(
        functools.partial(_combine_kernel, tc=tc, final=final),
        grid=(t // tc,),
        in_specs=[pl.BlockSpec((None, 2, tc), lambda i: (i, 0, 0), memory_space=pltpu.SMEM),
                  pl.BlockSpec((tc, d), lambda i: (i, 0)),
                  pl.BlockSpec((tc, LANES), lambda i: (i, 0)),
                  pl.BlockSpec((None, 1, d), lambda i: (i // per_b, 0, 0)),
                  pl.BlockSpec((1, d), lambda i: (0, 0)),
                  pl.BlockSpec(memory_space=pl.ANY)],
        out_specs=pl.BlockSpec((tc, d), lambda i: (i, 0)),
        out_shape=jax.ShapeDtypeStruct((t, d), F32),
        scratch_shapes=[pltpu.VMEM((2, tc) + TOKEN_TILE, F32), pltpu.SemaphoreType.DMA(())],
        compiler_params=_cparams("arbitrary"),
        name="moe_combine",
    )(dest3, x1, route, gt2[:, None, :], final_g.reshape(1, d), eo)


def _moe_plan(route, counts, t):
    e = route[:, 0:2].astype(I32).T
    rank = route[:, 4:6].astype(I32).T
    cnt = counts[0, :MOE_EXPERTS].astype(I32)
    padded = (cnt + MOE_BM - 1) // MOE_BM * MOE_BM
    pend = jnp.cumsum(padded)
    pstart = pend - padded
    ids = jnp.arange(MOE_EXPERTS, dtype=I32)
    dest = rank + jnp.sum(jnp.where(e[..., None] == ids, pstart, 0), axis=-1)
    nb = (2 * t) // MOE_BM + MOE_EXPERTS
    n_used = (pend[-1] // MOE_BM).astype(I32)
    first_row = jnp.minimum(jnp.arange(nb, dtype=I32), n_used - 1) * MOE_BM
    blk_e = jnp.sum((pend[None, :] <= first_row[:, None]).astype(I32), axis=1)
    blk_e = jnp.minimum(blk_e, MOE_EXPERTS - 1)
    return dest, blk_e, n_used.reshape(1), nb * MOE_BM


def kernel(x, c, ada_w, ada_b, norm1_g, w_in, gla_gate_w, gla_gate_b, gla_norm_g, dsa_norm_g, s5_a_re, s5_a_im, s5_log_step, s5_b_re, s5_b_im, s5_c_re, s5_c_im, s5_d, s5_glu_w, s5_glu_b, s5_norm_g, w_out, norm2_g, router_grp_w, router_grp_b, router_exp_w, router_exp_b, exp_w_gate, exp_w_up, exp_w_down, final_norm_g):
    batch, seq, d = x.shape
    depth = ada_w.shape[0]
    t = batch * seq
    mod = _ada_mod(c, ada_w, ada_b)
    x2 = x.reshape(t, d)
    for l in range(depth):
        sh1, sc1, gt1, sh2, sc2, gt2 = (mod[l, j] for j in range(6))
        p = _in_proj(x2, norm1_g[l], sc1, sh1, _layout_w_in(w_in[l]), seq)
        o_gla = _gla(p, gla_gate_w[l], gla_gate_b[l], gla_norm_g[l], batch, seq)
        o_dsa = _dsa(p, dsa_norm_g[l], batch, seq)
        tabs = _s5_tables(s5_a_re[l], s5_a_im[l], s5_log_step[l], s5_b_re[l], s5_b_im[l],
                          s5_c_re[l], s5_c_im[l])
        o_s5 = _s5((p["su0"], p["su1"], p["su2"]), tabs, s5_d[l], s5_glu_w[l], s5_glu_b[l], s5_norm_g[l], batch, seq)
        x1, h2, route, counts = _mix_out(o_gla, o_dsa, o_s5, w_out[l], x2, gt1, norm2_g[l], sc2, sh2,
                                         router_grp_w[l], router_grp_b[l], router_exp_w[l],
                                         router_exp_b[l], seq)
        dest, blk_e, n_used, n_rows = _moe_plan(route, counts, t)
        xs = _dispatch(h2, dest, n_rows)
        eo = _experts(xs, blk_e, n_used, exp_w_gate[l].astype(BF16), exp_w_up[l].astype(BF16),
                      exp_w_down[l].astype(BF16))
        x2 = _combine(x1, route, dest, eo, gt2, final_norm_g, seq, final=(l == depth - 1))
    return x2.reshape(batch, seq, d)
```

```python
import functools
import math

import jax
import jax.numpy as jnp
import numpy as np
from jax import lax
from jax.experimental import pallas as pl
from jax.experimental.pallas import tpu as pltpu

F32 = jnp.float32
BF16 = jnp.bfloat16
I32 = jnp.int32

D_MODEL = 1024
GLA_HEADS = 4
GLA_DV = 96
GLA_DK = 48
GLA_RANK = 16
GLA_TAU = 16.0
GLA_CHUNK = 64
DSA_HEADS = 4
DSA_HEAD_DIM = 64
DSA_WIDTH = DSA_HEADS * DSA_HEAD_DIM
DSA_IDX_HEADS = 8
DSA_IDX_DIM = 32
DSA_TOPK = 256
S5_GROUPS = 24
S5_GROUP_CH = 16
S5_STATE = 64
S5_WIDTH = S5_GROUPS * S5_GROUP_CH
MOE_GROUPS = 4
MOE_EXPERTS_PER_GROUP = 8
MOE_EXPERTS = MOE_GROUPS * MOE_EXPERTS_PER_GROUP
MOE_HIDDEN = 512
RMS_EPS = 1e-6
IN_SIZES = (192, 192, 384, 16, 384, 256, 256, 256, 256, 32, 8, 384)

LANES = 128
SUBLANES = 8
VMEM_LIMIT = 56 * 1024 * 1024

HEAD_PAD = LANES
GLA_W = GLA_HEADS * HEAD_PAD
S5_CHUNK = 16
S5_TILE_G = LANES // S5_GROUP_CH
S5_TILES = S5_GROUPS // S5_TILE_G
S5_ROW_W = S5_CHUNK * LANES
S5_STATE_W = S5_TILE_G * S5_STATE
MOE_BM = 512
INT_MIN = -2 ** 31


def _cparams(*sem):
    return pltpu.CompilerParams(dimension_semantics=tuple(sem), vmem_limit_bytes=VMEM_LIMIT)


def _sigmoid(x):
    return 0.5 * (jnp.tanh(0.5 * x) + 1.0)


def _rms(x, g, n=None):
    n = x.shape[-1] if n is None else n
    ms = jnp.sum(x * x, axis=-1, keepdims=True) * (1.0 / n)
    return x * lax.rsqrt(ms + RMS_EPS) * g


def _ada_kernel(c_ref, w_ref, b_ref, o_ref):
    c = c_ref[...]
    cond = (c * _sigmoid(c)).astype(BF16)
    o_ref[...] = jnp.dot(cond, w_ref[...].astype(BF16), preferred_element_type=F32) + b_ref[...]


def _ada_mod(c, ada_w, ada_b):
    depth, d, d6 = ada_w.shape
    nblk = d6 // d
    bp = -(-c.shape[0] // SUBLANES) * SUBLANES
    cp = jnp.pad(c, ((0, bp - c.shape[0]), (0, 0)))
    out = pl.pallas_call(
        _ada_kernel,
        grid=(depth, nblk),
        in_specs=[
            pl.BlockSpec((bp, d), lambda l, j: (0, 0)),
            pl.BlockSpec((None, d, d), lambda l, j: (l, 0, j)),
            pl.BlockSpec((None, None, 1, d), lambda l, j: (l, j, 0, 0)),
        ],
        out_specs=pl.BlockSpec((None, None, bp, d), lambda l, j: (l, j, 0, 0)),
        out_shape=jax.ShapeDtypeStruct((depth, nblk, bp, d), F32),
        compiler_params=_cparams("arbitrary", "arbitrary"),
        name="ada_mod",
    )(cp, ada_w, ada_b.reshape(depth, nblk, 1, d))
    return out[:, :, : c.shape[0], :]


PROJ_OUTS = (
    ("gq", GLA_W, BF16), ("gk", GLA_W, BF16), ("gv", GLA_W, BF16), ("gr", GLA_W, BF16),
    ("glr", LANES, BF16),
    ("dq", DSA_WIDTH, BF16), ("dk", DSA_WIDTH, BF16), ("dv", DSA_WIDTH, BF16),
    ("iq", DSA_IDX_HEADS * DSA_IDX_DIM, BF16), ("ik", DSA_IDX_HEADS * DSA_IDX_DIM, BF16),
    ("iw", LANES, F32), ("su0", LANES, F32), ("su1", LANES, F32), ("su2", LANES, F32),
)
PROJ_COLS = sum(w for _, w, _ in PROJ_OUTS)


def _pad_heads(w, heads, dim):
    d = w.shape[0]
    w = w.reshape(d, heads, dim)
    return jnp.pad(w, ((0, 0), (0, 0), (0, HEAD_PAD - dim))).reshape(d, heads * HEAD_PAD)


def _pad_cols(w, width):
    return jnp.pad(w, ((0, 0), (0, width - w.shape[1])))


def _layout_w_in(w_in):
    offs = np.cumsum((0,) + IN_SIZES)
    p = [w_in[:, offs[i]:offs[i + 1]] for i in range(len(IN_SIZES))]
    cols = [
        _pad_heads(p[0], GLA_HEADS, GLA_DK), _pad_heads(p[1], GLA_HEADS, GLA_DK),
        _pad_heads(p[2], GLA_HEADS, GLA_DV), _pad_heads(p[4], GLA_HEADS, GLA_DV),
        _pad_cols(p[3], LANES),
        p[5], p[6], p[7], p[8], jnp.tile(p[9], (1, DSA_IDX_HEADS)),
        _pad_cols(p[10], LANES), p[11],
    ]
    return jnp.concatenate(cols, axis=1).astype(BF16)


def _proj_kernel(x_ref, g_ref, sc_ref, sh_ref, w_ref, *o_refs):
    x = x_ref[...]
    h = _rms(x, g_ref[...]) * (1.0 + sc_ref[...]) + sh_ref[...]
    hb = h.astype(BF16)
    c0 = 0
    for o_ref, (_, width, _) in zip(o_refs, PROJ_OUTS):
        o_ref[...] = jnp.dot(hb, w_ref[:, c0:c0 + width], preferred_element_type=F32).astype(o_ref.dtype)
        c0 += width


def _in_proj(x2, norm_g, sc, sh, w_lay, seq, tm=512):
    t, d = x2.shape
    per_b = seq // tm
    outs = pl.pallas_call(
        _proj_kernel,
        grid=(t // tm,),
        in_specs=[
            pl.BlockSpec((tm, d), lambda i: (i, 0)),
            pl.BlockSpec((1, d), lambda i: (0, 0)),
            pl.BlockSpec((None, 1, d), lambda i: (i // per_b, 0, 0)),
            pl.BlockSpec((None, 1, d), lambda i: (i // per_b, 0, 0)),
            pl.BlockSpec((d, PROJ_COLS), lambda i: (0, 0)),
        ],
        out_specs=[pl.BlockSpec((tm, w), lambda i: (i, 0)) for _, w, _ in PROJ_OUTS],
        out_shape=[jax.ShapeDtypeStruct((t, w), dt) for _, w, dt in PROJ_OUTS],
        compiler_params=_cparams("arbitrary"),
        name="in_proj",
    )(x2, norm_g.reshape(1, d), sc[:, None, :], sh[:, None, :], w_lay)
    return {name: o for (name, _, _), o in zip(PROJ_OUTS, outs)}


def _gla_kernel(q_ref, k_ref, v_ref, r_ref, lr_ref, gw_ref, gb_ref, ng_ref, o_ref, s_ref, *, rows):
    @pl.when(pl.program_id(1) == 0)
    def _():
        s_ref[...] = jnp.zeros_like(s_ref)

    c = GLA_CHUNK
    gl = jnp.dot(lr_ref[...], gw_ref[...], preferred_element_type=F32) + gb_ref[...]
    g = (jnp.minimum(gl, 0.0) - jnp.log1p(jnp.exp(-jnp.abs(gl)))) * (1.0 / GLA_TAU)
    g1 = g.astype(BF16)
    rem = g - g1.astype(F32)
    g2 = rem.astype(BF16)
    g3 = (rem - g2.astype(F32)).astype(BF16)
    ri = lax.broadcasted_iota(I32, (c, c), 0)
    ci = lax.broadcasted_iota(I32, (c, c), 1)
    causal = ri >= ci
    tri = causal.astype(BF16)
    ng = ng_ref[...]
    for ch in range(rows // c):
        rs = slice(ch * c, (ch + 1) * c)
        bcum = (jnp.dot(tri, g1[rs], preferred_element_type=F32)
                + jnp.dot(tri, g2[rs], preferred_element_type=F32)
                + jnp.dot(tri, g3[rs], preferred_element_type=F32))
        blast = bcum[c - 1:c, :]
        e_pos = jnp.exp(bcum)
        kf = k_ref[rs, :].astype(F32)
        qd = (q_ref[rs, :].astype(F32) * (GLA_DK ** -0.5) * e_pos).astype(BF16)
        kd = (kf * jnp.exp(-bcum)).astype(BF16)
        ku = (kf * jnp.exp(blast - bcum)).astype(BF16)
        dec = jnp.exp(blast)
        for h in range(GLA_HEADS):
            hs = slice(h * HEAD_PAD, (h + 1) * HEAD_PAD)
            vh = v_ref[rs, hs]
            att = lax.dot_general(qd[:, hs], kd[:, hs], (((1,), (1,)), ((), ())),
                                  preferred_element_type=F32)
            att = jnp.where(causal, att, 0.0).astype(BF16)
            st = s_ref[h]
            o = jnp.dot(att, vh, preferred_element_type=F32)
            o = o + lax.dot_general(qd[:, hs], st.astype(BF16), (((1,), (1,)), ((), ())),
                                    preferred_element_type=F32)
            upd = lax.dot_general(vh, ku[:, hs], (((0,), (0,)), ((), ())),
                                  preferred_element_type=F32)
            s_ref[h] = st * dec[:, hs] + upd
            y = _rms(o, ng[:, hs], GLA_DV)
            rr = r_ref[rs, hs].astype(F32)
            o_ref[rs, hs] = (y * (rr * _sigmoid(rr))).astype(o_ref.dtype)


def _gla(p, gate_w, gate_b, norm_g, batch, seq, rows=512):
    gw = _pad_cols(jnp.pad(_pad_heads(gate_w, GLA_HEADS, GLA_DK), ((0, LANES - GLA_RANK), (0, 0))), GLA_W)
    gb = _pad_heads(gate_b.reshape(1, -1), GLA_HEADS, GLA_DK)
    ng = _pad_heads(jnp.tile(norm_g.reshape(1, GLA_DV), (1, GLA_HEADS)), GLA_HEADS, GLA_DV)
    rows = min(rows, seq)
    blk = lambda w: pl.BlockSpec((None, rows, w), lambda b, j: (b, j, 0))
    full = lambda a: pl.BlockSpec(a.shape, lambda b, j: (0, 0))
    r3 = lambda a: a.reshape(batch, seq, a.shape[-1])
    out = pl.pallas_call(
        functools.partial(_gla_kernel, rows=rows),
        grid=(batch, seq // rows),
        in_specs=[blk(GLA_W), blk(GLA_W), blk(GLA_W), blk(GLA_W), blk(LANES),
                  full(gw), full(gb), full(ng)],
        out_specs=blk(GLA_W),
        out_shape=jax.ShapeDtypeStruct((batch, seq, GLA_W), BF16),
        scratch_shapes=[pltpu.VMEM((GLA_HEADS, HEAD_PAD, HEAD_PAD), F32)],
        compiler_params=_cparams("arbitrary", "arbitrary"),
        name="gla",
    )(r3(p["gq"]), r3(p["gk"]), r3(p["gv"]), r3(p["gr"]), r3(p["glr"]), gw.astype(BF16), gb, ng)
    return out.reshape(batch * seq, GLA_W)


def _dsa_kernel(iq_ref, ik_ref, iw_ref, q_ref, k_ref, v_ref, ng_ref, o_ref,
                qs_ref, q4_ref, keys_ref, acc_ref, run_ref, top_ref, *, tq, kc, sub, n_sel, top_r):
    qi = pl.program_id(1)
    nkc = ((qi + 1) * tq + kc - 1) // kc
    lane_w = lax.broadcasted_iota(I32, (tq, DSA_WIDTH), 1)
    iq = iq_ref[...]
    for h in range(DSA_IDX_HEADS):
        m = (lane_w >= h * DSA_IDX_DIM) & (lane_w < (h + 1) * DSA_IDX_DIM)
        qs_ref[h * tq:(h + 1) * tq, :] = jnp.where(m, iq, jnp.zeros_like(iq))
    iw = iw_ref[...] * (DSA_IDX_HEADS ** -0.5 * DSA_IDX_DIM ** -0.5)
    lane = lax.broadcasted_iota(I32, (tq, LANES), 1)
    w_cols = [jnp.sum(jnp.where(lane == h, iw, 0.0), axis=1, keepdims=True) for h in range(DSA_IDX_HEADS)]
    qpos = qi * tq + lax.broadcasted_iota(I32, (tq, sub), 0)
    kcol = lax.broadcasted_iota(I32, (tq, sub), 1)
    nt = (((1,), (1,)), ((), ()))

    ninf = jnp.float32(-jnp.inf)

    def score_body(c, carry):
        for u in range(kc // sub):
            k0 = pl.multiple_of(c * kc + u * sub, sub)
            lg = lax.dot_general(qs_ref[...], ik_ref[pl.ds(k0, sub), :], nt,
                                 preferred_element_type=F32)
            sc = jnp.zeros((tq, sub), F32)
            for h in range(DSA_IDX_HEADS):
                sc = sc + w_cols[h] * jnp.maximum(lg[h * tq:(h + 1) * tq, :], 0.0)
            sc = sc + 0.0
            keys_ref[c, :, u * sub:(u + 1) * sub] = jnp.where(kcol + k0 <= qpos, sc, ninf)
        return carry

    lax.fori_loop(0, nkc, score_body, 0)

    def to_key(x):
        bits = lax.bitcast_convert_type(x, I32)
        key = jnp.where(bits < 0, bits ^ jnp.int32(0x7FFFFFFF), bits)
        return jnp.where(x == ninf, jnp.int32(INT_MIN), key)

    def count_ge(cand):
        candb = jnp.broadcast_to(cand, (tq, LANES))

        def body(c, acc):
            blk = keys_ref[c]
            for j in range(kc // LANES):
                acc = acc + jnp.where(to_key(blk[:, j * LANES:(j + 1) * LANES]) >= candb, 1.0, 0.0)
            return acc

        acc = lax.fori_loop(0, nkc, body, jnp.zeros((tq, LANES), F32))
        return jnp.sum(acc, axis=1, keepdims=True)

    need = jnp.float32(n_sel)

    def select(count):
        t0 = jnp.full((tq, 1), INT_MIN, I32)
        zero = jnp.zeros((tq, 1), I32)
        t0 = jnp.where(count(zero) >= need, zero, t0)

        def bit_body(i, t):
            cand = t | (jnp.int32(1) << (30 - i))
            return jnp.where(count(cand) >= need, cand, t)

        t = lax.fori_loop(0, 31, bit_body, t0)
        t = jnp.maximum(t, jnp.int32(INT_MIN + 1))
        return t, count(t + 1), count(t)

    lowest = jnp.full((SUBLANES, LANES), ninf, F32)
    for rt in range(0, tq // SUBLANES, 2):
        rsl = [slice((rt + a) * SUBLANES, (rt + a + 1) * SUBLANES) for a in range(2)]

        def top_body(c, ts, rsl=rsl):
            ts = [list(t) for t in ts]
            for j in range(kc // LANES):
                for a in range(2):
                    x = keys_ref[c, rsl[a], j * LANES:(j + 1) * LANES]
                    for r in range(top_r):
                        hi = jnp.maximum(ts[a][r], x)
                        x = jnp.minimum(ts[a][r], x)
                        ts[a][r] = hi
            return tuple(tuple(t) for t in ts)

        ts = lax.fori_loop(0, nkc, top_body, ((lowest,) * top_r,) * 2)
        for a in range(2):
            for r in range(top_r):
                top_ref[r, rsl[a], :] = to_key(ts[a][r])

    def count_cand(cand):
        candb = jnp.broadcast_to(cand, (tq, LANES))
        acc = jnp.zeros((tq, LANES), F32)
        for r in range(top_r):
            acc = acc + jnp.where(top_ref[r] >= candb, 1.0, 0.0)
        return jnp.sum(acc, axis=1, keepdims=True)

    thr_c, n_gt_c, n_ge_c = select(count_cand)
    hidden = jnp.max(jnp.where(top_ref[top_r - 1] >= thr_c, 1.0, 0.0)) > 0.0
    thr, n_gt, n_ge = lax.cond(hidden, lambda: select(count_ge), lambda: (thr_c, n_gt_c, n_ge_c))
    n_tie = need - n_gt
    any_tie = jnp.max(n_ge) > need
    thr_bits = jnp.where(thr < 0, thr ^ jnp.int32(0x7FFFFFFF), thr)
    thr_f = jnp.where(thr == jnp.int32(INT_MIN + 1), jnp.float32(jnp.finfo(jnp.float32).min),
                      lax.bitcast_convert_type(thr_bits, F32))

    ui = lax.broadcasted_iota(I32, (kc, kc), 0)
    uj = lax.broadcasted_iota(I32, (kc, kc), 1)
    before = (ui < uj).astype(BF16)
    q = q_ref[...] * jnp.asarray(DSA_HEAD_DIM ** -0.5, BF16)
    for h in range(DSA_HEADS):
        m = (lane_w >= h * DSA_HEAD_DIM) & (lane_w < (h + 1) * DSA_HEAD_DIM)
        q4_ref[h * tq:(h + 1) * tq, :] = jnp.where(m, q, jnp.zeros_like(q))
    neg = jnp.float32(-1e30)
    hrows = lambda a, h: a[h * tq:(h + 1) * tq]

    def lane_fold(op, acc, s):
        for j in range(kc // LANES):
            acc = op(acc, s[:, j * LANES:(j + 1) * LANES])
        return acc

    run_ref[...] = jnp.full(run_ref.shape, neg, F32)

    def mask_pass(with_ties):
        def mask_body(c, seen):
            k0 = pl.multiple_of(c * kc, kc)
            sc = keys_ref[c]
            if with_ties:
                eq = sc == thr_f
                eqf = jnp.where(eq, 1.0, 0.0)
                rank = seen + jnp.dot(eqf.astype(BF16), before, preferred_element_type=F32)
                sel = (sc > thr_f) | (eq & (rank < n_tie))
                seen = seen + jnp.sum(eqf, axis=1, keepdims=True)
            else:
                sel = sc >= thr_f
            bias = jnp.where(sel, 0.0, neg)
            keys_ref[c] = bias
            s = lax.dot_general(q4_ref[...], k_ref[pl.ds(k0, kc), :], nt, preferred_element_type=F32)
            for h in range(DSA_HEADS):
                run_ref[h] = lane_fold(jnp.maximum, run_ref[h], hrows(s, h) + bias)
            return seen

        lax.fori_loop(0, nkc, mask_body, jnp.zeros((tq, 1), F32))
        return 0

    lax.cond(any_tie, lambda: mask_pass(True), lambda: mask_pass(False))
    ms = [jnp.max(run_ref[h], axis=1, keepdims=True) for h in range(DSA_HEADS)]

    run_ref[...] = jnp.zeros_like(run_ref)
    acc_ref[...] = jnp.zeros_like(acc_ref)

    def att_body(c, carry):
        k0 = pl.multiple_of(c * kc, kc)
        bias = keys_ref[c]
        s = lax.dot_general(q4_ref[...], k_ref[pl.ds(k0, kc), :], nt, preferred_element_type=F32)
        ps = []
        for h in range(DSA_HEADS):
            p = jnp.exp(hrows(s, h) + bias - ms[h])
            run_ref[h] = lane_fold(jnp.add, run_ref[h], p)
            ps.append(p.astype(BF16))
        acc_ref[...] += jnp.dot(jnp.concatenate(ps, axis=0), v_ref[pl.ds(k0, kc), :],
                                preferred_element_type=F32)
        return carry

    lax.fori_loop(0, nkc, att_body, 0)
    out = jnp.zeros((tq, DSA_WIDTH), F32)
    for h in range(DSA_HEADS):
        m = (lane_w >= h * DSA_HEAD_DIM) & (lane_w < (h + 1) * DSA_HEAD_DIM)
        out = out + jnp.where(m, acc_ref[h * tq:(h + 1) * tq, :] / jnp.sum(run_ref[h], axis=1, keepdims=True), 0.0)
    o_ref[...] = _rms(out, ng_ref[...]).astype(o_ref.dtype)


def _dsa(p, norm_g, batch, seq, tq=256, kc=512, sub=256, top_r=12):
    n_sel = min(DSA_TOPK, seq // 4)
    kc = min(kc, seq)
    sub = min(sub, kc)
    r3 = lambda a: a.reshape(batch, seq, a.shape[-1])
    qblk = lambda w: pl.BlockSpec((None, tq, w), lambda b, i: (b, i, 0))
    kfull = lambda w: pl.BlockSpec((None, seq, w), lambda b, i: (b, 0, 0))
    out = pl.pallas_call(
        functools.partial(_dsa_kernel, tq=tq, kc=kc, sub=sub, n_sel=n_sel, top_r=top_r),
        grid=(batch, seq // tq),
        in_specs=[qblk(DSA_WIDTH), kfull(DSA_WIDTH), qblk(LANES), qblk(DSA_WIDTH),
                  kfull(DSA_WIDTH), kfull(DSA_WIDTH), pl.BlockSpec((1, DSA_WIDTH), lambda b, i: (0, 0))],
        out_specs=qblk(DSA_WIDTH),
        out_shape=jax.ShapeDtypeStruct((batch, seq, DSA_WIDTH), BF16),
        scratch_shapes=[
            pltpu.VMEM((DSA_IDX_HEADS * tq, DSA_WIDTH), BF16),
            pltpu.VMEM((DSA_HEADS * tq, DSA_WIDTH), BF16),
            pltpu.VMEM((seq // kc, tq, kc), F32),
            pltpu.VMEM((DSA_HEADS * tq, DSA_WIDTH), F32),
            pltpu.VMEM((DSA_HEADS, tq, LANES), F32),
            pltpu.VMEM((top_r, tq, LANES), I32),
        ],
        compiler_params=_cparams("arbitrary", "arbitrary"),
        name="dsa",
    )(r3(p["iq"]), r3(p["ik"]), r3(p["iw"]), r3(p["dq"]), r3(p["dk"]), r3(p["dv"]),
      norm_g.reshape(1, DSA_WIDTH))
    return out.reshape(batch * seq, DSA_WIDTH)


def _s5_tables(a_re, a_im, log_step, b_re, b_im, c_re, c_im):
    hp = lax.Precision.HIGHEST
    s = S5_CHUNK
    dt = jnp.exp(log_step)[:, None]
    mag = jnp.exp(a_re * dt)
    lam_re, lam_im = mag * jnp.cos(a_im * dt), mag * jnp.sin(a_im * dt)
    den = a_re * a_re + a_im * a_im
    nr, ni = lam_re - 1.0, lam_im
    coef_re = (nr * a_re + ni * a_im) / den
    coef_im = (ni * a_re - nr * a_im) / den
    bb_re = coef_re[..., None] * b_re - coef_im[..., None] * b_im
    bb_im = coef_re[..., None] * b_im + coef_im[..., None] * b_re
    j = jnp.arange(s + 1, dtype=F32)[:, None, None]
    pmag = jnp.exp(a_re * dt * j)
    pw_re, pw_im = pmag * jnp.cos(a_im * dt * j), pmag * jnp.sin(a_im * dt * j)
    e_re = c_re[None] * pw_re[:, :, None, :] - c_im[None] * pw_im[:, :, None, :]
    e_im = c_re[None] * pw_im[:, :, None, :] + c_im[None] * pw_re[:, :, None, :]
    kk = (jnp.einsum("jgcp,gpd->jgcd", e_re, bb_re, precision=hp)
          - jnp.einsum("jgcp,gpd->jgcd", e_im, bb_im, precision=hp))
    lag = np.arange(s)[None, :] - np.arange(s)[:, None]
    toe = jnp.where((lag >= 0)[:, :, None, None, None], kk[np.clip(lag, 0, s)], 0.0)
    rp_re, rp_im = pw_re[s - 1 - np.arange(s)], pw_im[s - 1 - np.arange(s)]
    x_re = rp_re[..., None] * bb_re[None] - rp_im[..., None] * bb_im[None]
    x_im = rp_re[..., None] * bb_im[None] + rp_im[..., None] * bb_re[None]
    f_re, f_im = e_re[1:], e_im[1:]
    eye = jnp.eye(S5_TILE_G, dtype=F32)
    tg = lambda a, ax: a.reshape(a.shape[:ax] + (S5_TILES, S5_TILE_G) + a.shape[ax + 1:])
    w_toe = (tg(toe, 2).transpose(2, 0, 3, 5, 1, 4)[:, :, :, :, :, None, :]
             * eye[None, None, :, None, None, :, None]).reshape(S5_TILES, S5_ROW_W, S5_ROW_W)
    w1 = lambda x: (tg(x, 1).transpose(1, 0, 2, 4, 3)[:, :, :, :, None, :]
                    * eye[None, None, :, None, :, None]).reshape(S5_TILES, S5_ROW_W, S5_STATE_W)
    w2 = lambda f: (tg(f, 1).transpose(1, 2, 4, 0, 3)[:, :, :, :, None, :]
                    * eye[None, :, None, None, :, None]).reshape(S5_TILES, S5_STATE_W, S5_ROW_W)
    tabs = dict(toe=w_toe, w1_re=w1(x_re), w1_im=w1(x_im), w2_re=w2(f_re), w2_im=w2(-f_im))
    tabs = {k: v.astype(BF16) for k, v in tabs.items()}
    tabs["a_re"], tabs["a_im"] = pw_re[s].reshape(-1), pw_im[s].reshape(-1)
    return tabs


def _s5_state_in_kernel(u_ref, wr_ref, wi_ref, xr_ref, xi_ref):
    u = u_ref[...].astype(BF16)
    xr_ref[...] = jnp.dot(u, wr_ref[...], preferred_element_type=F32)
    xi_ref[...] = jnp.dot(u, wi_ref[...], preferred_element_type=F32)


def _s5_scan_kernel(xr_ref, xi_ref, ar_ref, ai_ref, hr_ref, hi_ref, cr_ref, ci_ref, *, steps):
    @pl.when(pl.program_id(0) == 0)
    def _():
        cr_ref[...] = jnp.zeros_like(cr_ref)
        ci_ref[...] = jnp.zeros_like(ci_ref)

    ar, ai = ar_ref[...], ai_ref[...]

    def body(n, carry):
        hr, hi = carry
        hr_ref[n] = hr
        hi_ref[n] = hi
        return ar * hr - ai * hi + xr_ref[n], ar * hi + ai * hr + xi_ref[n]

    hr, hi = lax.fori_loop(0, steps, body, (cr_ref[...], ci_ref[...]))
    cr_ref[...] = hr
    ci_ref[...] = hi


def _s5_out_kernel(u_ref, toe_ref, hr_ref, hi_ref, w2r_ref, w2i_ref, y_ref):
    y = jnp.dot(u_ref[...].astype(BF16), toe_ref[...], preferred_element_type=F32)
    y = y + jnp.dot(hr_ref[...].astype(BF16), w2r_ref[...], preferred_element_type=F32)
    y = y + jnp.dot(hi_ref[...].astype(BF16), w2i_ref[...], preferred_element_type=F32)
    y_ref[...] = y


def _s5_post_kernel(y_ref, u0_ref, u1_ref, u2_ref, d_ref, gw_ref, gb_ref, ng_ref, o_ref):
    y = jnp.concatenate([y_ref[q] for q in range(S5_TILES)], axis=1)
    u = jnp.concatenate([u0_ref[...], u1_ref[...], u2_ref[...]], axis=1)
    z = y + d_ref[...] * u
    z = 0.5 * z * (1.0 + jnp.tanh(math.sqrt(2.0 / math.pi) * (z + 0.044715 * (z * z * z))))
    gate = _sigmoid(jnp.dot(z.astype(BF16), gw_ref[...], preferred_element_type=F32) + gb_ref[...])
    o_ref[...] = _rms(z * gate, ng_ref[...]).astype(o_ref.dtype)


def _s5(us, tabs, d, glu_w, glu_b, norm_g, batch, seq, tm=512, rb=512):
    s = S5_CHUNK
    nch = seq // s
    rows = nch * batch
    rb = min(rb, rows)
    gp = S5_GROUPS * S5_STATE
    xq = jnp.stack([u.reshape(rows, S5_ROW_W) for u in us])
    row_in = pl.BlockSpec((None, rb, S5_ROW_W), lambda q, i: (q, i, 0))
    wspec = lambda a: pl.BlockSpec((None,) + a.shape[1:], lambda q, i: (q, 0, 0))
    colblk = pl.BlockSpec((rb, S5_STATE_W), lambda q, i: (i, q))
    xr, xi = pl.pallas_call(
        _s5_state_in_kernel,
        grid=(S5_TILES, rows // rb),
        in_specs=[row_in, wspec(tabs["w1_re"]), wspec(tabs["w1_im"])],
        out_specs=[colblk, colblk],
        out_shape=[jax.ShapeDtypeStruct((rows, gp), F32)] * 2,
        compiler_params=_cparams("arbitrary", "arbitrary"),
        name="s5_state_in",
    )(xq, tabs["w1_re"], tabs["w1_im"])
    parts = SUBLANES // batch
    wl = gp // parts
    slab = lambda a: a.reshape(batch, nch, gp).transpose(1, 0, 2).reshape(nch, SUBLANES, wl)
    unslab = lambda a: a.reshape(nch, batch, gp).transpose(1, 0, 2).reshape(rows, gp)
    coef = lambda a: jnp.tile(a.reshape(parts, wl), (batch, 1))
    steps = min(64, nch)
    sblk = pl.BlockSpec((steps, SUBLANES, wl), lambda i: (i, 0, 0))
    cblk = pl.BlockSpec((SUBLANES, wl), lambda i: (0, 0))
    hr, hi = pl.pallas_call(
        functools.partial(_s5_scan_kernel, steps=steps),
        grid=(nch // steps,),
        in_specs=[sblk, sblk, cblk, cblk],
        out_specs=[sblk, sblk],
        out_shape=[jax.ShapeDtypeStruct((nch, SUBLANES, wl), F32)] * 2,
        scratch_shapes=[pltpu.VMEM((SUBLANES, wl), F32)] * 2,
        compiler_params=_cparams("arbitrary"),
        name="s5_scan",
    )(slab(xr), slab(xi), coef(tabs["a_re"]), coef(tabs["a_im"]))
    y = pl.pallas_call(
        _s5_out_kernel,
        grid=(S5_TILES, rows // rb),
        in_specs=[row_in, wspec(tabs["toe"]), colblk, colblk, wspec(tabs["w2_re"]), wspec(tabs["w2_im"])],
        out_specs=row_in,
        out_shape=jax.ShapeDtypeStruct((S5_TILES, rows, S5_ROW_W), F32),
        compiler_params=_cparams("arbitrary", "arbitrary"),
        name="s5_out",
    )(xq, tabs["toe"], unslab(hr), unslab(hi), tabs["w2_re"], tabs["w2_im"])
    t = batch * seq
    y = y.reshape(S5_TILES, t, LANES)
    tm = min(tm, t)
    tok = lambda w: pl.BlockSpec((tm, w), lambda i: (i, 0))
    vec = pl.BlockSpec((1, S5_WIDTH), lambda i: (0, 0))
    return pl.pallas_call(
        _s5_post_kernel,
        grid=(t // tm,),
        in_specs=[pl.BlockSpec((S5_TILES, tm, LANES), lambda i: (0, i, 0)), tok(LANES), tok(LANES), tok(LANES),
                  vec, pl.BlockSpec((S5_WIDTH, S5_WIDTH), lambda i: (0, 0)), vec, vec],
        out_specs=tok(S5_WIDTH),
        out_shape=jax.ShapeDtypeStruct((t, S5_WIDTH), BF16),
        compiler_params=_cparams("arbitrary"),
        name="s5_post",
    )(y, us[0], us[1], us[2], d.reshape(1, -1), glu_w.astype(BF16), glu_b.reshape(1, -1),
      norm_g.reshape(1, -1))


def _mix_out_kernel(og_ref, od_ref, os_ref, wg_ref, wd_ref, ws_ref, x_ref, gt_ref, n2_ref, sc_ref,
                    sh_ref, rw1_ref, rw2_ref, rb_ref, x1_ref, h2_ref, route_ref, cnt_ref, carry_ref, *, tm):
    @pl.when(pl.program_id(0) == 0)
    def _():
        carry_ref[...] = jnp.zeros_like(carry_ref)

    mix = jnp.dot(og_ref[...], wg_ref[...], preferred_element_type=F32)
    mix = mix + jnp.dot(od_ref[...], wd_ref[...], preferred_element_type=F32)
    mix = mix + jnp.dot(os_ref[...], ws_ref[...], preferred_element_type=F32)
    x1 = x_ref[...] + gt_ref[...] * mix
    x1_ref[...] = x1
    h2 = _rms(x1, n2_ref[...]) * (1.0 + sc_ref[...]) + sh_ref[...]
    h2_ref[...] = h2
    ha = h2.astype(BF16)
    hb = (h2 - ha.astype(F32)).astype(BF16)
    logit = (jnp.dot(ha, rw1_ref[...], preferred_element_type=F32)
             + jnp.dot(ha, rw2_ref[...], preferred_element_type=F32)
             + jnp.dot(hb, rw1_ref[...], preferred_element_type=F32)) + rb_ref[...]
    lane = lax.broadcasted_iota(I32, (tm, LANES), 1)
    lanef = lane.astype(F32)
    ninf = jnp.float32(-jnp.inf)
    big = jnp.float32(LANES)
    gl = jnp.where(lane < MOE_GROUPS, logit, ninf)
    gmax = jnp.max(gl, axis=1, keepdims=True)
    grp_p = 1.0 / jnp.sum(jnp.exp(gl - gmax), axis=1, keepdims=True)
    grp_i = jnp.min(jnp.where(gl == gmax, lanef, big), axis=1, keepdims=True)
    lo = MOE_GROUPS + MOE_EXPERTS_PER_GROUP * grp_i
    el = jnp.where((lanef >= lo) & (lanef < lo + MOE_EXPERTS_PER_GROUP), logit, ninf)
    l1 = jnp.max(el, axis=1, keepdims=True)
    i1 = jnp.min(jnp.where(el == l1, lanef, big), axis=1, keepdims=True)
    el2 = jnp.where(lanef == i1, ninf, el)
    l2 = jnp.max(el2, axis=1, keepdims=True)
    i2 = jnp.min(jnp.where(el2 == l2, lanef, big), axis=1, keepdims=True)
    e21 = jnp.exp(l2 - l1)
    w1 = grp_p / (1.0 + e21)
    w2 = grp_p * e21 / (1.0 + e21)
    e1 = i1 - MOE_GROUPS
    e2 = i2 - MOE_GROUPS
    oh1 = lanef == e1
    oh2 = lanef == e2
    hot = jnp.where(oh1 | oh2, 1.0, 0.0)
    ri = lax.broadcasted_iota(I32, (tm, tm), 0)
    ci = lax.broadcasted_iota(I32, (tm, tm), 1)
    before = (ci < ri).astype(BF16)
    prior = carry_ref[...] + jnp.dot(before, hot.astype(BF16), preferred_element_type=F32)
    r1 = jnp.sum(jnp.where(oh1, prior, 0.0), axis=1, keepdims=True)
    r2 = jnp.sum(jnp.where(oh2, prior, 0.0), axis=1, keepdims=True)
    total = carry_ref[...] + jnp.sum(hot, axis=0, keepdims=True)
    carry_ref[...] = total
    cnt_ref[...] = jnp.broadcast_to(total, cnt_ref.shape)
    route = jnp.where(lane == 0, e1, jnp.where(lane == 1, e2, jnp.where(lane == 2, w1, jnp.where(
        lane == 3, w2, jnp.where(lane == 4, r1, jnp.where(lane == 5, r2, 0.0))))))
    route_ref[...] = route


def _mix_out(o_gla, o_dsa, o_s5, w_out, x2, gt1, norm2_g, sc2, sh2, rgw, rgb, rew, reb, seq, tm=512):
    t, d = x2.shape
    per_b = seq // tm
    wg = jnp.pad(w_out[:GLA_HEADS * GLA_DV].reshape(GLA_HEADS, GLA_DV, d),
                 ((0, 0), (0, HEAD_PAD - GLA_DV), (0, 0))).reshape(GLA_W, d).astype(BF16)
    wd = w_out[GLA_HEADS * GLA_DV:GLA_HEADS * GLA_DV + DSA_WIDTH].astype(BF16)
    ws = w_out[GLA_HEADS * GLA_DV + DSA_WIDTH:].astype(BF16)
    rw = _pad_cols(jnp.concatenate([rgw, rew], axis=1), LANES)
    rb = _pad_cols(jnp.concatenate([rgb, reb]).reshape(1, -1), LANES)
    rw1 = rw.astype(BF16)
    rw2 = (rw - rw1.astype(F32)).astype(BF16)
    tok = lambda w: pl.BlockSpec((tm, w), lambda i: (i, 0))
    full = lambda a: pl.BlockSpec(a.shape, lambda i: (0, 0))
    perb = pl.BlockSpec((None, 1, d), lambda i: (i // per_b, 0, 0))
    vec = pl.BlockSpec((1, d), lambda i: (0, 0))
    return pl.pallas_call(
        functools.partial(_mix_out_kernel, tm=tm),
        grid=(t // tm,),
        in_specs=[tok(GLA_W), tok(DSA_WIDTH), tok(S5_WIDTH), full(wg), full(wd), full(ws), tok(d),
                  perb, vec, perb, perb, full(rw1), full(rw2), full(rb)],
        out_specs=[tok(d), tok(d), tok(LANES), pl.BlockSpec((SUBLANES, LANES), lambda i: (0, 0))],
        out_shape=[jax.ShapeDtypeStruct((t, d), F32), jax.ShapeDtypeStruct((t, d), F32),
                   jax.ShapeDtypeStruct((t, LANES), F32), jax.ShapeDtypeStruct((SUBLANES, LANES), F32)],
        scratch_shapes=[pltpu.VMEM((1, LANES), F32)],
        compiler_params=_cparams("arbitrary"),
        name="mix_out",
    )(o_gla, o_dsa, o_s5, wg, wd, ws, x2, gt1[:, None, :], norm2_g.reshape(1, d), sc2[:, None, :],
      sh2[:, None, :], rw1, rw2, rb)


def _row(ref, r):
    return ref.at[pl.ds(r, 1), :]


def _dispatch_kernel(dest_ref, h_ref, zero_ref, out_ref, sem, *, td):
    del zero_ref

    def start(g, c):
        for j in range(SUBLANES):
            r = g * SUBLANES + j
            pltpu.make_async_copy(_row(h_ref, r), _row(out_ref, dest_ref[0, r]), sem).start()
            pltpu.make_async_copy(_row(h_ref, r), _row(out_ref, dest_ref[1, r]), sem).start()
        return c

    lax.fori_loop(0, td // SUBLANES, start, 0)
    for _ in range(2):
        pltpu.make_async_copy(h_ref, out_ref.at[pl.ds(0, td), :], sem).wait()


def _dispatch(h2, dest, n_rows, td=256):
    t, d = h2.shape
    dest3 = dest.reshape(2, t // td, td).transpose(1, 0, 2)
    zeros = jnp.zeros((n_rows, d), F32)
    return pl.pallas_call(
        functools.partial(_dispatch_kernel, td=td),
        grid=(t // td,),
        in_specs=[pl.BlockSpec((None, 2, td), lambda i: (i, 0, 0), memory_space=pltpu.SMEM),
                  pl.BlockSpec((td, d), lambda i: (i, 0)),
                  pl.BlockSpec(memory_space=pl.ANY)],
        out_specs=pl.BlockSpec(memory_space=pl.ANY),
        out_shape=jax.ShapeDtypeStruct((n_rows, d), F32),
        scratch_shapes=[pltpu.SemaphoreType.DMA(())],
        input_output_aliases={2: 0},
        compiler_params=_cparams("arbitrary"),
        name="moe_dispatch",
    )(dest3, h2, zeros)


def _expert_kernel(be_ref, nb_ref, x_ref, wg_ref, wu_ref, wd_ref, o_ref):
    del be_ref
    i = pl.program_id(0)

    @pl.when(i < nb_ref[0])
    def _():
        xb = x_ref[...].astype(BF16)
        g = jnp.dot(xb, wg_ref[...], preferred_element_type=F32)
        u = jnp.dot(xb, wu_ref[...], preferred_element_type=F32)
        hid = (g * _sigmoid(g) * u).astype(BF16)
        o_ref[...] = jnp.dot(hid, wd_ref[...], preferred_element_type=F32)

    @pl.when(i >= nb_ref[0])
    def _():
        o_ref[...] = jnp.zeros_like(o_ref)


def _experts(xs, blk_e, n_used, wg, wu, wd):
    n_rows = xs.shape[0]
    nb = n_rows // MOE_BM
    d, hdim = wg.shape[-2:]
    rows = pl.BlockSpec((MOE_BM, d), lambda i, be, nu: (i, 0))
    return pl.pallas_call(
        _expert_kernel,
        grid_spec=pltpu.PrefetchScalarGridSpec(
            num_scalar_prefetch=2,
            grid=(nb,),
            in_specs=[
                rows,
                pl.BlockSpec((None, d, hdim), lambda i, be, nu: (be[i], 0, 0)),
                pl.BlockSpec((None, d, hdim), lambda i, be, nu: (be[i], 0, 0)),
                pl.BlockSpec((None, hdim, d), lambda i, be, nu: (be[i], 0, 0)),
            ],
            out_specs=rows,
        ),
        out_shape=jax.ShapeDtypeStruct((n_rows, d), F32),
        compiler_params=_cparams("arbitrary"),
        name="moe_experts",
    )(blk_e, n_used, xs, wg, wu, wd)


def _combine_kernel(dest_ref, x_ref, route_ref, gt_ref, fg_ref, eo_ref, o_ref, rows_ref, sem, *, tc, final):
    def start(g, c):
        for j in range(SUBLANES):
            r = g * SUBLANES + j
            pltpu.make_async_copy(_row(eo_ref, dest_ref[0, r]), _row(rows_ref.at[0], r), sem).start()
            pltpu.make_async_copy(_row(eo_ref, dest_ref[1, r]), _row(rows_ref.at[1], r), sem).start()
        return c

    lax.fori_loop(0, tc // SUBLANES, start, 0)
    for which in range(2):
        pltpu.make_async_copy(eo_ref.at[pl.ds(0, tc), :], rows_ref.at[which], sem).wait()
    route = route_ref[...]
    lane = lax.broadcasted_iota(I32, route.shape, 1)
    w1 = jnp.sum(jnp.where(lane == 2, route, 0.0), axis=1, keepdims=True)
    w2 = jnp.sum(jnp.where(lane == 3, route, 0.0), axis=1, keepdims=True)
    y = rows_ref[0] * w1 + rows_ref[1] * w2
    x = x_ref[...] + gt_ref[...] * y
    if final:
        x = _rms(x, fg_ref[...])
    o_ref[...] = x


def _combine(x1, route, dest, eo, gt2, final_g, seq, final, tc=256):
    t, d = x1.shape
    per_b = seq // tc
    dest3 = dest.reshape(2, t // tc, tc).transpose(1, 0, 2)
    return pl.pallas_call(
        functools.partial(_combine_kernel, tc=tc, final=final),
        grid=(t // tc,),
        in_specs=[pl.BlockSpec((None, 2, tc), lambda i: (i, 0, 0), memory_space=pltpu.SMEM),
                  pl.BlockSpec((tc, d), lambda i: (i, 0)),
                  pl.BlockSpec((tc, LANES), lambda i: (i, 0)),
                  pl.BlockSpec((None, 1, d), lambda i: (i // per_b, 0, 0)),
                  pl.BlockSpec((1, d), lambda i: (0, 0)),
                  pl.BlockSpec(memory_space=pl.ANY)],
        out_specs=pl.BlockSpec((tc, d), lambda i: (i, 0)),
        out_shape=jax.ShapeDtypeStruct((t, d), F32),
        scratch_shapes=[pltpu.VMEM((2, tc, d), F32), pltpu.SemaphoreType.DMA(())],
        compiler_params=_cparams("arbitrary"),
        name="moe_combine",
    )(dest3, x1, route, gt2[:, None, :], final_g.reshape(1, d), eo)


def _moe_plan(route, counts, t):
    e = route[:, 0:2].astype(I32).T
    rank = route[:, 4:6].astype(I32).T
    cnt = counts[0, :MOE_EXPERTS].astype(I32)
    padded = (cnt + MOE_BM - 1) // MOE_BM * MOE_BM
    pend = jnp.cumsum(padded)
    pstart = pend - padded
    ids = jnp.arange(MOE_EXPERTS, dtype=I32)
    dest = rank + jnp.sum(jnp.where(e[..., None] == ids, pstart, 0), axis=-1)
    nb = (2 * t) // MOE_BM + MOE_EXPERTS
    n_used = (pend[-1] // MOE_BM).astype(I32)
    first_row = jnp.minimum(jnp.arange(nb, dtype=I32), n_used - 1) * MOE_BM
    blk_e = jnp.sum((pend[None, :] <= first_row[:, None]).astype(I32), axis=1)
    blk_e = jnp.minimum(blk_e, MOE_EXPERTS - 1)
    return dest, blk_e, n_used.reshape(1), nb * MOE_BM


def kernel(x, c, ada_w, ada_b, norm1_g, w_in, gla_gate_w, gla_gate_b, gla_norm_g, dsa_norm_g, s5_a_re, s5_a_im, s5_log_step, s5_b_re, s5_b_im, s5_c_re, s5_c_im, s5_d, s5_glu_w, s5_glu_b, s5_norm_g, w_out, norm2_g, router_grp_w, router_grp_b, router_exp_w, router_exp_b, exp_w_gate, exp_w_up, exp_w_down, final_norm_g):
    batch, seq, d = x.shape
    depth = ada_w.shape[0]
    t = batch * seq
    mod = _ada_mod(c, ada_w, ada_b)
    x2 = x.reshape(t, d)
    for l in range(depth):
        sh1, sc1, gt1, sh2, sc2, gt2 = (mod[l, j] for j in range(6))
        p = _in_proj(x2, norm1_g[l], sc1, sh1, _layout_w_in(w_in[l]), seq)
        o_gla = _gla(p, gla_gate_w[l], gla_gate_b[l], gla_norm_g[l], batch, seq)
        o_dsa = _dsa(p, dsa_norm_g[l], batch, seq)
        tabs = _s5_tables(s5_a_re[l], s5_a_im[l], s5_log_step[l], s5_b_re[l], s5_b_im[l],
                          s5_c_re[l], s5_c_im[l])
        o_s5 = _s5((p["su0"], p["su1"], p["su2"]), tabs, s5_d[l], s5_glu_w[l], s5_glu_b[l], s5_norm_g[l], batch, seq)
        x1, h2, route, counts = _mix_out(o_gla, o_dsa, o_s5, w_out[l], x2, gt1, norm2_g[l], sc2, sh2,
                                         router_grp_w[l], router_grp_b[l], router_exp_w[l],
                                         router_exp_b[l], seq)
        dest, blk_e, n_used, n_rows = _moe_plan(route, counts, t)
        xs = _dispatch(h2, dest, n_rows)
        eo = _experts(xs, blk_e, n_used, exp_w_gate[l].astype(BF16), exp_w_up[l].astype(BF16),
                      exp_w_down[l].astype(BF16))
        x2 = _combine(x1, route, dest, eo, gt2, final_norm_g, seq, final=(l == depth - 1))
    return x2.reshape(batch, seq, d)
```

```python
import functools
import math

import jax
import jax.numpy as jnp
import numpy as np
from jax import lax
from jax.experimental import pallas as pl
from jax.experimental.pallas import tpu as pltpu

F32 = jnp.float32
BF16 = jnp.bfloat16
I32 = jnp.int32

D_MODEL = 1024
GLA_HEADS = 4
GLA_DV = 96
GLA_DK = 48
GLA_RANK = 16
GLA_TAU = 16.0
GLA_CHUNK = 64
DSA_HEADS = 4
DSA_HEAD_DIM = 64
DSA_WIDTH = DSA_HEADS * DSA_HEAD_DIM
DSA_IDX_HEADS = 8
DSA_IDX_DIM = 32
DSA_TOPK = 256
S5_GROUPS = 24
S5_GROUP_CH = 16
S5_STATE = 64
S5_WIDTH = S5_GROUPS * S5_GROUP_CH
MOE_GROUPS = 4
MOE_EXPERTS_PER_GROUP = 8
MOE_EXPERTS = MOE_GROUPS * MOE_EXPERTS_PER_GROUP
MOE_HIDDEN = 512
RMS_EPS = 1e-6
IN_SIZES = (192, 192, 384, 16, 384, 256, 256, 256, 256, 32, 8, 384)

LANES = 128
SUBLANES = 8
VMEM_LIMIT = 56 * 1024 * 1024

HEAD_PAD = LANES
GLA_W = GLA_HEADS * HEAD_PAD
S5_CHUNK = 16
S5_TILE_G = LANES // S5_GROUP_CH
S5_TILES = S5_GROUPS // S5_TILE_G
S5_ROW_W = S5_CHUNK * LANES
S5_STATE_W = S5_TILE_G * S5_STATE
MOE_BM = 512
INT_MIN = -2 ** 31


def _cparams(*sem):
    return pltpu.CompilerParams(dimension_semantics=tuple(sem), vmem_limit_bytes=VMEM_LIMIT)


def _sigmoid(x):
    return 0.5 * (jnp.tanh(0.5 * x) + 1.0)


def _rms(x, g, n=None):
    n = x.shape[-1] if n is None else n
    ms = jnp.sum(x * x, axis=-1, keepdims=True) * (1.0 / n)
    return x * lax.rsqrt(ms + RMS_EPS) * g


def _ada_kernel(c_ref, w_ref, b_ref, o_ref):
    c = c_ref[...]
    cond = (c * _sigmoid(c)).astype(BF16)
    o_ref[...] = jnp.dot(cond, w_ref[...].astype(BF16), preferred_element_type=F32) + b_ref[...]


def _ada_mod(c, ada_w, ada_b):
    depth, d, d6 = ada_w.shape
    nblk = d6 // d
    bp = -(-c.shape[0] // SUBLANES) * SUBLANES
    cp = jnp.pad(c, ((0, bp - c.shape[0]), (0, 0)))
    out = pl.pallas_call(
        _ada_kernel,
        grid=(depth, nblk),
        in_specs=[
            pl.BlockSpec((bp, d), lambda l, j: (0, 0)),
            pl.BlockSpec((None, d, d), lambda l, j: (l, 0, j)),
            pl.BlockSpec((None, None, 1, d), lambda l, j: (l, j, 0, 0)),
        ],
        out_specs=pl.BlockSpec((None, None, bp, d), lambda l, j: (l, j, 0, 0)),
        out_shape=jax.ShapeDtypeStruct((depth, nblk, bp, d), F32),
        compiler_params=_cparams("arbitrary", "arbitrary"),
        name="ada_mod",
    )(cp, ada_w, ada_b.reshape(depth, nblk, 1, d))
    return out[:, :, : c.shape[0], :]


PROJ_OUTS = (
    ("gq", GLA_W, BF16), ("gk", GLA_W, BF16), ("gv", GLA_W, BF16), ("gr", GLA_W, BF16),
    ("glr", LANES, BF16),
    ("dq", DSA_WIDTH, BF16), ("dk", DSA_WIDTH, BF16), ("dv", DSA_WIDTH, BF16),
    ("iq", DSA_IDX_HEADS * DSA_IDX_DIM, BF16), ("ik", DSA_IDX_HEADS * DSA_IDX_DIM, BF16),
    ("iw", LANES, F32), ("su0", LANES, F32), ("su1", LANES, F32), ("su2", LANES, F32),
)
PROJ_COLS = sum(w for _, w, _ in PROJ_OUTS)


def _pad_heads(w, heads, dim):
    d = w.shape[0]
    w = w.reshape(d, heads, dim)
    return jnp.pad(w, ((0, 0), (0, 0), (0, HEAD_PAD - dim))).reshape(d, heads * HEAD_PAD)


def _pad_cols(w, width):
    return jnp.pad(w, ((0, 0), (0, width - w.shape[1])))


def _layout_w_in(w_in):
    offs = np.cumsum((0,) + IN_SIZES)
    p = [w_in[:, offs[i]:offs[i + 1]] for i in range(len(IN_SIZES))]
    cols = [
        _pad_heads(p[0], GLA_HEADS, GLA_DK), _pad_heads(p[1], GLA_HEADS, GLA_DK),
        _pad_heads(p[2], GLA_HEADS, GLA_DV), _pad_heads(p[4], GLA_HEADS, GLA_DV),
        _pad_cols(p[3], LANES),
        p[5], p[6], p[7], p[8], jnp.tile(p[9], (1, DSA_IDX_HEADS)),
        _pad_cols(p[10], LANES), p[11],
    ]
    return jnp.concatenate(cols, axis=1).astype(BF16)


def _proj_kernel(x_ref, g_ref, sc_ref, sh_ref, w_ref, *o_refs):
    x = x_ref[...]
    h = _rms(x, g_ref[...]) * (1.0 + sc_ref[...]) + sh_ref[...]
    hb = h.astype(BF16)
    c0 = 0
    for o_ref, (_, width, _) in zip(o_refs, PROJ_OUTS):
        o_ref[...] = jnp.dot(hb, w_ref[:, c0:c0 + width], preferred_element_type=F32).astype(o_ref.dtype)
        c0 += width
    sx_ref = o_refs[len(PROJ_OUTS)]
    n_chunks = x.shape[0] // S5_CHUNK
    for q in range(S5_TILES):
        su_ref = o_refs[len(PROJ_OUTS) - S5_TILES + q]
        for s in range(S5_CHUNK):
            sx_ref[q, :, s * LANES:(s + 1) * LANES] = su_ref[pl.ds(s, n_chunks, stride=S5_CHUNK), :]


def _in_proj(x2, norm_g, sc, sh, w_lay, seq, tm=512):
    t, d = x2.shape
    per_b = seq // tm
    outs = pl.pallas_call(
        _proj_kernel,
        grid=(t // tm,),
        in_specs=[
            pl.BlockSpec((tm, d), lambda i: (i, 0)),
            pl.BlockSpec((1, d), lambda i: (0, 0)),
            pl.BlockSpec((None, 1, d), lambda i: (i // per_b, 0, 0)),
            pl.BlockSpec((None, 1, d), lambda i: (i // per_b, 0, 0)),
            pl.BlockSpec((d, PROJ_COLS), lambda i: (0, 0)),
        ],
        out_specs=[pl.BlockSpec((tm, w), lambda i: (i, 0)) for _, w, _ in PROJ_OUTS]
        + [pl.BlockSpec((S5_TILES, tm // S5_CHUNK, S5_ROW_W), lambda i: (0, i, 0))],
        out_shape=[jax.ShapeDtypeStruct((t, w), dt) for _, w, dt in PROJ_OUTS]
        + [jax.ShapeDtypeStruct((S5_TILES, t // S5_CHUNK, S5_ROW_W), F32)],
        compiler_params=_cparams("arbitrary"),
        name="in_proj",
    )(x2, norm_g.reshape(1, d), sc[:, None, :], sh[:, None, :], w_lay)
    p = {name: o for (name, _, _), o in zip(PROJ_OUTS, outs)}
    p["sx"] = outs[-1]
    return p


def _gla_kernel(q_ref, k_ref, v_ref, r_ref, lr_ref, gw_ref, gb_ref, ng_ref, o_ref, s_ref, *, rows):
    @pl.when(pl.program_id(1) == 0)
    def _():
        s_ref[...] = jnp.zeros_like(s_ref)

    c = GLA_CHUNK
    gl = jnp.dot(lr_ref[...], gw_ref[...], preferred_element_type=F32) + gb_ref[...]
    g = (jnp.minimum(gl, 0.0) - jnp.log1p(jnp.exp(-jnp.abs(gl)))) * (1.0 / GLA_TAU)
    g1 = g.astype(BF16)
    rem = g - g1.astype(F32)
    g2 = rem.astype(BF16)
    g3 = (rem - g2.astype(F32)).astype(BF16)
    ri = lax.broadcasted_iota(I32, (c, c), 0)
    ci = lax.broadcasted_iota(I32, (c, c), 1)
    causal = ri >= ci
    tri = causal.astype(BF16)
    ng = ng_ref[...]
    for ch in range(rows // c):
        rs = slice(ch * c, (ch + 1) * c)
        bcum = (jnp.dot(tri, g1[rs], preferred_element_type=F32)
                + jnp.dot(tri, g2[rs], preferred_element_type=F32)
                + jnp.dot(tri, g3[rs], preferred_element_type=F32))
        blast = bcum[c - 1:c, :]
        e_pos = jnp.exp(bcum)
        kf = k_ref[rs, :].astype(F32)
        qd = (q_ref[rs, :].astype(F32) * (GLA_DK ** -0.5) * e_pos).astype(BF16)
        kd = (kf * jnp.exp(-bcum)).astype(BF16)
        ku = (kf * jnp.exp(blast - bcum)).astype(BF16)
        dec = jnp.exp(blast)
        for h in range(GLA_HEADS):
            hs = slice(h * HEAD_PAD, (h + 1) * HEAD_PAD)
            vh = v_ref[rs, hs]
            att = lax.dot_general(qd[:, hs], kd[:, hs], (((1,), (1,)), ((), ())),
                                  preferred_element_type=F32)
            att = jnp.where(causal, att, 0.0).astype(BF16)
            st = s_ref[h]
            o = jnp.dot(att, vh, preferred_element_type=F32)
            o = o + lax.dot_general(qd[:, hs], st.astype(BF16), (((1,), (1,)), ((), ())),
                                    preferred_element_type=F32)
            upd = lax.dot_general(vh, ku[:, hs], (((0,), (0,)), ((), ())),
                                  preferred_element_type=F32)
            s_ref[h] = st * dec[:, hs] + upd
            y = _rms(o, ng[:, hs], GLA_DV)
            rr = r_ref[rs, hs].astype(F32)
            o_ref[rs, hs] = (y * (rr * _sigmoid(rr))).astype(o_ref.dtype)


def _gla(p, gate_w, gate_b, norm_g, batch, seq, rows=512):
    gw = _pad_cols(jnp.pad(_pad_heads(gate_w, GLA_HEADS, GLA_DK), ((0, LANES - GLA_RANK), (0, 0))), GLA_W)
    gb = _pad_heads(gate_b.reshape(1, -1), GLA_HEADS, GLA_DK)
    ng = _pad_heads(jnp.tile(norm_g.reshape(1, GLA_DV), (1, GLA_HEADS)), GLA_HEADS, GLA_DV)
    rows = min(rows, seq)
    blk = lambda w: pl.BlockSpec((None, rows, w), lambda b, j: (b, j, 0))
    full = lambda a: pl.BlockSpec(a.shape, lambda b, j: (0, 0))
    r3 = lambda a: a.reshape(batch, seq, a.shape[-1])
    out = pl.pallas_call(
        functools.partial(_gla_kernel, rows=rows),
        grid=(batch, seq // rows),
        in_specs=[blk(GLA_W), blk(GLA_W), blk(GLA_W), blk(GLA_W), blk(LANES),
                  full(gw), full(gb), full(ng)],
        out_specs=blk(GLA_W),
        out_shape=jax.ShapeDtypeStruct((batch, seq, GLA_W), BF16),
        scratch_shapes=[pltpu.VMEM((GLA_HEADS, HEAD_PAD, HEAD_PAD), F32)],
        compiler_params=_cparams("arbitrary", "arbitrary"),
        name="gla",
    )(r3(p["gq"]), r3(p["gk"]), r3(p["gv"]), r3(p["gr"]), r3(p["glr"]), gw.astype(BF16), gb, ng)
    return out.reshape(batch * seq, GLA_W)


def _dsa_kernel(iq_ref, ik_ref, iw_ref, q_ref, k_ref, v_ref, ng_ref, o_ref,
                qs_ref, q4_ref, keys_ref, acc_ref, run_ref, top_ref, sa_ref, sb_ref, kmax_ref,
                *, tq, kc, sub, n_sel, top_r):
    assert kc == 2 * sub
    qi = pl.program_id(1)
    nkc = ((qi + 1) * tq + kc - 1) // kc
    lane_w = lax.broadcasted_iota(I32, (tq, DSA_WIDTH), 1)
    iq = iq_ref[...]
    for h in range(DSA_IDX_HEADS):
        m = (lane_w >= h * DSA_IDX_DIM) & (lane_w < (h + 1) * DSA_IDX_DIM)
        qs_ref[h * tq:(h + 1) * tq, :] = jnp.where(m, iq, jnp.zeros_like(iq))
    iw = iw_ref[...] * (DSA_IDX_HEADS ** -0.5 * DSA_IDX_DIM ** -0.5)
    lane = lax.broadcasted_iota(I32, (tq, LANES), 1)
    w_cols = [iw[:, h:h + 1] for h in range(DSA_IDX_HEADS)]
    qpos = qi * tq + lax.broadcasted_iota(I32, (tq, sub), 0)
    kcol = lax.broadcasted_iota(I32, (tq, sub), 1)
    nt = (((1,), (1,)), ((), ()))

    ninf = jnp.float32(-jnp.inf)

    def score_body(c, carry):
        for u in range(kc // sub):
            k0 = pl.multiple_of(c * kc + u * sub, sub)
            lg = lax.dot_general(qs_ref[...], ik_ref[pl.ds(k0, sub), :], nt,
                                 preferred_element_type=F32)
            sc = jnp.zeros((tq, sub), F32)
            for h in range(DSA_IDX_HEADS):
                sc = sc + w_cols[h] * jnp.maximum(lg[h * tq:(h + 1) * tq, :], 0.0)
            sc = sc + 0.0
            keys_ref[c, :, u * sub:(u + 1) * sub] = jnp.where(kcol + k0 <= qpos, sc, ninf)
        return carry

    lax.fori_loop(0, nkc, score_body, 0)

    def to_key(x):
        bits = lax.bitcast_convert_type(x, I32)
        key = jnp.where(bits < 0, bits ^ jnp.int32(0x7FFFFFFF), bits)
        return jnp.where(x == ninf, jnp.int32(INT_MIN), key)

    def count_ge(cand):
        candb = jnp.broadcast_to(cand, (tq, LANES))

        def body(c, acc):
            blk = keys_ref[c]
            for j in range(kc // LANES):
                acc = acc + jnp.where(to_key(blk[:, j * LANES:(j + 1) * LANES]) >= candb, 1.0, 0.0)
            return acc

        acc = lax.fori_loop(0, nkc, body, jnp.zeros((tq, LANES), F32))
        return jnp.sum(acc, axis=1, keepdims=True)

    need = jnp.float32(n_sel)

    def select(count):
        t0 = jnp.full((tq, 1), INT_MIN, I32)
        zero = jnp.zeros((tq, 1), I32)
        t0 = jnp.where(count(zero) >= need, zero, t0)

        def bit_body(i, t):
            cand = t | (jnp.int32(1) << (30 - i))
            return jnp.where(count(cand) >= need, cand, t)

        t = lax.fori_loop(0, 31, bit_body, t0)
        t = jnp.maximum(t, jnp.int32(INT_MIN + 1))
        return t, count(t + 1), count(t)

    lowest = jnp.full((SUBLANES, LANES), ninf, F32)
    for rt in range(0, tq // SUBLANES, 2):
        rsl = [slice((rt + a) * SUBLANES, (rt + a + 1) * SUBLANES) for a in range(2)]

        def top_body(c, ts, rsl=rsl):
            ts = [list(t) for t in ts]
            for j in range(kc // LANES):
                for a in range(2):
                    x = keys_ref[c, rsl[a], j * LANES:(j + 1) * LANES]
                    for r in range(top_r):
                        hi = jnp.maximum(ts[a][r], x)
                        x = jnp.minimum(ts[a][r], x)
                        ts[a][r] = hi
            return tuple(tuple(t) for t in ts)

        ts = lax.fori_loop(0, nkc, top_body, ((lowest,) * top_r,) * 2)
        for a in range(2):
            for r in range(top_r):
                top_ref[r, rsl[a], :] = to_key(ts[a][r])

    def count_cand(cand):
        candb = jnp.broadcast_to(cand, (tq, LANES))
        acc = jnp.zeros((tq, LANES), F32)
        for r in range(top_r):
            acc = acc + jnp.where(top_ref[r] >= candb, 1.0, 0.0)
        return jnp.sum(acc, axis=1, keepdims=True)

    thr_c, n_gt_c, n_ge_c = select(count_cand)
    hidden = jnp.max(jnp.where(top_ref[top_r - 1] >= thr_c, 1.0, 0.0)) > 0.0
    thr, n_gt, n_ge = lax.cond(hidden, lambda: select(count_ge), lambda: (thr_c, n_gt_c, n_ge_c))
    n_tie = need - n_gt
    any_tie = jnp.max(n_ge) > need
    thr_bits = jnp.where(thr < 0, thr ^ jnp.int32(0x7FFFFFFF), thr)
    thr_f = jnp.where(thr == jnp.int32(INT_MIN + 1), jnp.float32(jnp.finfo(jnp.float32).min),
                      lax.bitcast_convert_type(thr_bits, F32))

    ui = lax.broadcasted_iota(I32, (kc, kc), 0)
    uj = lax.broadcasted_iota(I32, (kc, kc), 1)
    before = (ui < uj).astype(BF16)
    q = q_ref[...] * jnp.asarray(DSA_HEAD_DIM ** -0.5, BF16)
    for h in range(DSA_HEADS):
        m = (lane_w >= h * DSA_HEAD_DIM) & (lane_w < (h + 1) * DSA_HEAD_DIM)
        q4_ref[h * tq:(h + 1) * tq, :] = jnp.where(m, q, jnp.zeros_like(q))
    neg = jnp.float32(-1e30)

    def lane_fold(op, acc, s):
        for j in range(s.shape[1] // LANES):
            acc = op(acc, s[:, j * LANES:(j + 1) * LANES])
        return acc

    @pl.when(qi == 0)
    def _():
        kmax_ref[...] = jnp.zeros_like(kmax_ref)

    hl = lax.broadcasted_iota(I32, (DSA_WIDTH, LANES), 0) // DSA_HEAD_DIM
    head_of = (hl == lax.broadcasted_iota(I32, (DSA_WIDTH, LANES), 1)).astype(BF16)
    kb = k_ref[pl.ds(pl.multiple_of(qi * tq, tq), tq), :].astype(F32)
    kn2 = jnp.dot((kb * kb).astype(BF16), head_of, preferred_element_type=F32)
    kmax = jnp.maximum(kmax_ref[0:1, :], jnp.max(kn2, axis=0, keepdims=True))
    kmax_ref[...] = jnp.broadcast_to(kmax, kmax_ref.shape)
    qf = q.astype(F32)
    qn2 = jnp.dot((qf * qf).astype(BF16), head_of, preferred_element_type=F32)
    bound = jnp.sqrt(qn2 * kmax) * 1.02
    ms_bound = [jnp.sum(jnp.where(lane == h, bound, 0.0), axis=1, keepdims=True) for h in range(DSA_HEADS)]

    def tie_bias_pass():
        def body(c, seen):
            sc = keys_ref[c]
            eq = sc == thr_f
            eqf = jnp.where(eq, 1.0, 0.0)
            rank = seen + jnp.dot(eqf.astype(BF16), before, preferred_element_type=F32)
            sel = (sc > thr_f) | (eq & (rank < n_tie))
            keys_ref[c] = jnp.where(sel, 0.0, neg)
            return seen + jnp.sum(eqf, axis=1, keepdims=True)

        lax.fori_loop(0, nkc, body, jnp.zeros((tq, 1), F32))

    def bias_of(c, stored):
        sc = keys_ref[c]
        return sc if stored else jnp.where(sc >= thr_f, 0.0, neg)

    def qk(k0):
        return lax.dot_general(q4_ref[...], k_ref[pl.ds(k0, sub), :], nt, preferred_element_type=F32)

    def max_pass(stored):
        run_ref[...] = jnp.full(run_ref.shape, neg, F32)

        def body(c, carry):
            bias = bias_of(c, stored)
            for u in range(kc // sub):
                s = qk(pl.multiple_of(c * kc + u * sub, sub))
                for h in range(DSA_HEADS):
                    run_ref[h] = lane_fold(jnp.maximum, run_ref[h],
                                           s[h * tq:(h + 1) * tq] + bias[:, u * sub:(u + 1) * sub])
            return carry

        lax.fori_loop(0, nkc, body, 0)
        return [jnp.max(run_ref[h], axis=1, keepdims=True) for h in range(DSA_HEADS)]

    def att_pass(stored, ms):
        run_ref[...] = jnp.zeros_like(run_ref)
        acc_ref[...] = jnp.zeros_like(acc_ref)

        def half(s_ref, bias, k0):
            ps = []
            for h in range(DSA_HEADS):
                p = jnp.exp(s_ref[h * tq:(h + 1) * tq, :] + bias - ms[h])
                run_ref[h] = lane_fold(jnp.add, run_ref[h], p)
                ps.append(p.astype(BF16))
            acc_ref[...] += jnp.dot(jnp.concatenate(ps, axis=0), v_ref[pl.ds(k0, sub), :],
                                    preferred_element_type=F32)

        sa_ref[...] = qk(0)

        def body(c, carry):
            k0 = pl.multiple_of(c * kc, kc)
            sb_ref[...] = qk(k0 + sub)
            bias = bias_of(c, stored)
            half(sa_ref, bias[:, :sub], k0)
            sa_ref[...] = qk(pl.multiple_of(jnp.minimum(c + 1, nkc - 1) * kc, kc))
            half(sb_ref, bias[:, sub:], k0 + sub)
            return carry

        lax.fori_loop(0, nkc, body, 0)

    def finish(stored):
        if stored:
            tie_bias_pass()
        att_pass(stored, ms_bound)
        lmin = jnp.min(jnp.sum(run_ref[0], axis=1, keepdims=True))
        for h in range(1, DSA_HEADS):
            lmin = jnp.minimum(lmin, jnp.min(jnp.sum(run_ref[h], axis=1, keepdims=True)))

        def redo():
            att_pass(stored, max_pass(stored))
            return 0

        lax.cond(lmin >= 1e-26, lambda: 0, redo)
        return 0

    lax.cond(any_tie, lambda: finish(True), lambda: finish(False))
    out = jnp.zeros((tq, DSA_WIDTH), F32)
    for h in range(DSA_HEADS):
        m = (lane_w >= h * DSA_HEAD_DIM) & (lane_w < (h + 1) * DSA_HEAD_DIM)
        out = out + jnp.where(m, acc_ref[h * tq:(h + 1) * tq, :] / jnp.sum(run_ref[h], axis=1, keepdims=True), 0.0)
    o_ref[...] = _rms(out, ng_ref[...]).astype(o_ref.dtype)


def _dsa(p, norm_g, batch, seq, tq=256, kc=512, top_r=12):
    n_sel = min(DSA_TOPK, seq // 4)
    kc = min(kc, seq)
    sub = kc // 2
    r3 = lambda a: a.reshape(batch, seq, a.shape[-1])
    qblk = lambda w: pl.BlockSpec((None, tq, w), lambda b, i: (b, i, 0))
    kfull = lambda w: pl.BlockSpec((None, seq, w), lambda b, i: (b, 0, 0))
    out = pl.pallas_call(
        functools.partial(_dsa_kernel, tq=tq, kc=kc, sub=sub, n_sel=n_sel, top_r=top_r),
        grid=(batch, seq // tq),
        in_specs=[qblk(DSA_WIDTH), kfull(DSA_WIDTH), qblk(LANES), qblk(DSA_WIDTH),
                  kfull(DSA_WIDTH), kfull(DSA_WIDTH), pl.BlockSpec((1, DSA_WIDTH), lambda b, i: (0, 0))],
        out_specs=qblk(DSA_WIDTH),
        out_shape=jax.ShapeDtypeStruct((batch, seq, DSA_WIDTH), BF16),
        scratch_shapes=[
            pltpu.VMEM((DSA_IDX_HEADS * tq, DSA_WIDTH), BF16),
            pltpu.VMEM((DSA_HEADS * tq, DSA_WIDTH), BF16),
            pltpu.VMEM((seq // kc, tq, kc), F32),
            pltpu.VMEM((DSA_HEADS * tq, DSA_WIDTH), F32),
            pltpu.VMEM((DSA_HEADS, tq, LANES), F32),
            pltpu.VMEM((top_r, tq, LANES), I32),
            pltpu.VMEM((DSA_HEADS * tq, sub), F32),
            pltpu.VMEM((DSA_HEADS * tq, sub), F32),
            pltpu.VMEM((SUBLANES, LANES), F32),
        ],
        compiler_params=_cparams("arbitrary", "arbitrary"),
        name="dsa",
    )(r3(p["iq"]), r3(p["ik"]), r3(p["iw"]), r3(p["dq"]), r3(p["dk"]), r3(p["dv"]),
      norm_g.reshape(1, DSA_WIDTH))
    return out.reshape(batch * seq, DSA_WIDTH)


def _s5_tables(a_re, a_im, log_step, b_re, b_im, c_re, c_im):
    hp = lax.Precision.HIGHEST
    s = S5_CHUNK
    dt = jnp.exp(log_step)[:, None]
    mag = jnp.exp(a_re * dt)
    lam_re, lam_im = mag * jnp.cos(a_im * dt), mag * jnp.sin(a_im * dt)
    den = a_re * a_re + a_im * a_im
    nr, ni = lam_re - 1.0, lam_im
    coef_re = (nr * a_re + ni * a_im) / den
    coef_im = (ni * a_re - nr * a_im) / den
    bb_re = coef_re[..., None] * b_re - coef_im[..., None] * b_im
    bb_im = coef_re[..., None] * b_im + coef_im[..., None] * b_re
    j = jnp.arange(s + 1, dtype=F32)[:, None, None]
    pmag = jnp.exp(a_re * dt * j)
    pw_re, pw_im = pmag * jnp.cos(a_im * dt * j), pmag * jnp.sin(a_im * dt * j)
    e_re = c_re[None] * pw_re[:, :, None, :] - c_im[None] * pw_im[:, :, None, :]
    e_im = c_re[None] * pw_im[:, :, None, :] + c_im[None] * pw_re[:, :, None, :]
    kk = (jnp.einsum("jgcp,gpd->jgcd", e_re, bb_re, precision=hp)
          - jnp.einsum("jgcp,gpd->jgcd", e_im, bb_im, precision=hp))
    lag = np.arange(s)[None, :] - np.arange(s)[:, None]
    toe = jnp.where((lag >= 0)[:, :, None, None, None], kk[np.clip(lag, 0, s)], 0.0)
    rp_re, rp_im = pw_re[s - 1 - np.arange(s)], pw_im[s - 1 - np.arange(s)]
    x_re = rp_re[..., None] * bb_re[None] - rp_im[..., None] * bb_im[None]
    x_im = rp_re[..., None] * bb_im[None] + rp_im[..., None] * bb_re[None]
    f_re, f_im = e_re[1:], e_im[1:]
    eye = jnp.eye(S5_TILE_G, dtype=F32)
    tg = lambda a, ax: a.reshape(a.shape[:ax] + (S5_TILES, S5_TILE_G) + a.shape[ax + 1:])
    w_toe = (tg(toe, 2).transpose(2, 0, 3, 5, 1, 4)[:, :, :, :, :, None, :]
             * eye[None, None, :, None, None, :, None]).reshape(S5_TILES, S5_ROW_W, S5_ROW_W)
    w1 = lambda x: (tg(x, 1).transpose(1, 0, 2, 4, 3)[:, :, :, :, None, :]
                    * eye[None, None, :, None, :, None]).reshape(S5_TILES, S5_ROW_W, S5_STATE_W)
    w2 = lambda f: (tg(f, 1).transpose(1, 2, 4, 0, 3)[:, :, :, :, None, :]
                    * eye[None, :, None, None, :, None]).reshape(S5_TILES, S5_STATE_W, S5_ROW_W)
    tabs = dict(toe=w_toe, w1_re=w1(x_re), w1_im=w1(x_im), w2_re=w2(f_re), w2_im=w2(-f_im))
    tabs = {k: v.astype(BF16) for k, v in tabs.items()}
    tabs["a_re"], tabs["a_im"] = pw_re[s].reshape(-1), pw_im[s].reshape(-1)
    return tabs


def _s5_state_in_kernel(u_ref, wr_ref, wi_ref, xr_ref, xi_ref):
    u = u_ref[...].astype(BF16)
    xr_ref[...] = jnp.dot(u, wr_ref[...], preferred_element_type=F32)
    xi_ref[...] = jnp.dot(u, wi_ref[...], preferred_element_type=F32)


def _s5_scan_kernel(xr_ref, xi_ref, ar_ref, ai_ref, hr_ref, hi_ref, cr_ref, ci_ref, *, steps):
    @pl.when(pl.program_id(0) == 0)
    def _():
        cr_ref[...] = jnp.zeros_like(cr_ref)
        ci_ref[...] = jnp.zeros_like(ci_ref)

    ar, ai = ar_ref[...], ai_ref[...]

    def body(n, carry):
        hr, hi = carry
        hr_ref[n] = hr
        hi_ref[n] = hi
        return ar * hr - ai * hi + xr_ref[n], ar * hi + ai * hr + xi_ref[n]

    hr, hi = lax.fori_loop(0, steps, body, (cr_ref[...], ci_ref[...]))
    cr_ref[...] = hr
    ci_ref[...] = hi


def _s5_out_kernel(u_ref, toe_ref, hr_ref, hi_ref, w2r_ref, w2i_ref, y_ref):
    y = jnp.dot(u_ref[...].astype(BF16), toe_ref[...], preferred_element_type=F32)
    y = y + jnp.dot(hr_ref[...].astype(BF16), w2r_ref[...], preferred_element_type=F32)
    y = y + jnp.dot(hi_ref[...].astype(BF16), w2i_ref[...], preferred_element_type=F32)
    y_ref[...] = y


def _s5_post_kernel(y_ref, u0_ref, u1_ref, u2_ref, d_ref, gw_ref, gb_ref, ng_ref, o_ref, yt_ref):
    n_chunks = yt_ref.shape[1] // S5_CHUNK
    for q in range(S5_TILES):
        for s in range(S5_CHUNK):
            yt_ref[q, pl.ds(s, n_chunks, stride=S5_CHUNK), :] = y_ref[q, :, s * LANES:(s + 1) * LANES]
    y = jnp.concatenate([yt_ref[q] for q in range(S5_TILES)], axis=1)
    u = jnp.concatenate([u0_ref[...], u1_ref[...], u2_ref[...]], axis=1)
    z = y + d_ref[...] * u
    z = 0.5 * z * (1.0 + jnp.tanh(math.sqrt(2.0 / math.pi) * (z + 0.044715 * (z * z * z))))
    gate = _sigmoid(jnp.dot(z.astype(BF16), gw_ref[...], preferred_element_type=F32) + gb_ref[...])
    o_ref[...] = _rms(z * gate, ng_ref[...]).astype(o_ref.dtype)


def _s5(us, xq, tabs, d, glu_w, glu_b, norm_g, batch, seq, tm=512, rb=512):
    s = S5_CHUNK
    nch = seq // s
    rows = nch * batch
    rb = min(rb, rows)
    gp = S5_GROUPS * S5_STATE
    row_in = pl.BlockSpec((None, rb, S5_ROW_W), lambda q, i: (q, i, 0))
    wspec = lambda a: pl.BlockSpec((None,) + a.shape[1:], lambda q, i: (q, 0, 0))
    colblk = pl.BlockSpec((rb, S5_STATE_W), lambda q, i: (i, q))
    xr, xi = pl.pallas_call(
        _s5_state_in_kernel,
        grid=(S5_TILES, rows // rb),
        in_specs=[row_in, wspec(tabs["w1_re"]), wspec(tabs["w1_im"])],
        out_specs=[colblk, colblk],
        out_shape=[jax.ShapeDtypeStruct((rows, gp), F32)] * 2,
        compiler_params=_cparams("arbitrary", "arbitrary"),
        name="s5_state_in",
    )(xq, tabs["w1_re"], tabs["w1_im"])
    parts = SUBLANES // batch
    wl = gp // parts
    slab = lambda a: a.reshape(batch, nch, gp).transpose(1, 0, 2).reshape(nch, SUBLANES, wl)
    unslab = lambda a: a.reshape(nch, batch, gp).transpose(1, 0, 2).reshape(rows, gp)
    coef = lambda a: jnp.tile(a.reshape(parts, wl), (batch, 1))
    steps = min(64, nch)
    sblk = pl.BlockSpec((steps, SUBLANES, wl), lambda i: (i, 0, 0))
    cblk = pl.BlockSpec((SUBLANES, wl), lambda i: (0, 0))
    hr, hi = pl.pallas_call(
        functools.partial(_s5_scan_kernel, steps=steps),
        grid=(nch // steps,),
        in_specs=[sblk, sblk, cblk, cblk],
        out_specs=[sblk, sblk],
        out_shape=[jax.ShapeDtypeStruct((nch, SUBLANES, wl), F32)] * 2,
        scratch_shapes=[pltpu.VMEM((SUBLANES, wl), F32)] * 2,
        compiler_params=_cparams("arbitrary"),
        name="s5_scan",
    )(slab(xr), slab(xi), coef(tabs["a_re"]), coef(tabs["a_im"]))
    y = pl.pallas_call(
        _s5_out_kernel,
        grid=(S5_TILES, rows // rb),
        in_specs=[row_in, wspec(tabs["toe"]), colblk, colblk, wspec(tabs["w2_re"]), wspec(tabs["w2_im"])],
        out_specs=row_in,
        out_shape=jax.ShapeDtypeStruct((S5_TILES, rows, S5_ROW_W), F32),
        compiler_params=_cparams("arbitrary", "arbitrary"),
        name="s5_out",
    )(xq, tabs["toe"], unslab(hr), unslab(hi), tabs["w2_re"], tabs["w2_im"])
    t = batch * seq
    tm = min(tm, t)
    tok = lambda w: pl.BlockSpec((tm, w), lambda i: (i, 0))
    vec = pl.BlockSpec((1, S5_WIDTH), lambda i: (0, 0))
    return pl.pallas_call(
        _s5_post_kernel,
        grid=(t // tm,),
        in_specs=[pl.BlockSpec((S5_TILES, tm // s, S5_ROW_W), lambda i: (0, i, 0)), tok(LANES), tok(LANES),
                  tok(LANES), vec, pl.BlockSpec((S5_WIDTH, S5_WIDTH), lambda i: (0, 0)), vec, vec],
        out_specs=tok(S5_WIDTH),
        out_shape=jax.ShapeDtypeStruct((t, S5_WIDTH), BF16),
        scratch_shapes=[pltpu.VMEM((S5_TILES, tm, LANES), F32)],
        compiler_params=_cparams("arbitrary"),
        name="s5_post",
    )(y, us[0], us[1], us[2], d.reshape(1, -1), glu_w.astype(BF16), glu_b.reshape(1, -1),
      norm_g.reshape(1, -1))


def _mix_out_kernel(og_ref, od_ref, os_ref, wg_ref, wd_ref, ws_ref, x_ref, gt_ref, n2_ref, sc_ref,
                    sh_ref, rw1_ref, rw2_ref, rb_ref, x1_ref, h2_ref, route_ref, cnt_ref, carry_ref, *, tm):
    @pl.when(pl.program_id(0) == 0)
    def _():
        carry_ref[...] = jnp.zeros_like(carry_ref)

    mix = jnp.dot(og_ref[...], wg_ref[...], preferred_element_type=F32)
    mix = mix + jnp.dot(od_ref[...], wd_ref[...], preferred_element_type=F32)
    mix = mix + jnp.dot(os_ref[...], ws_ref[...], preferred_element_type=F32)
    x1 = x_ref[...] + gt_ref[...] * mix
    x1_ref[...] = x1
    h2 = _rms(x1, n2_ref[...]) * (1.0 + sc_ref[...]) + sh_ref[...]
    h2_ref[...] = h2
    ha = h2.astype(BF16)
    hb = (h2 - ha.astype(F32)).astype(BF16)
    logit = (jnp.dot(ha, rw1_ref[...], preferred_element_type=F32)
             + jnp.dot(ha, rw2_ref[...], preferred_element_type=F32)
             + jnp.dot(hb, rw1_ref[...], preferred_element_type=F32)) + rb_ref[...]
    lane = lax.broadcasted_iota(I32, (tm, LANES), 1)
    lanef = lane.astype(F32)
    ninf = jnp.float32(-jnp.inf)
    big = jnp.float32(LANES)
    gl = jnp.where(lane < MOE_GROUPS, logit, ninf)
    gmax = jnp.max(gl, axis=1, keepdims=True)
    grp_p = 1.0 / jnp.sum(jnp.exp(gl - gmax), axis=1, keepdims=True)
    grp_i = jnp.min(jnp.where(gl == gmax, lanef, big), axis=1, keepdims=True)
    lo = MOE_GROUPS + MOE_EXPERTS_PER_GROUP * grp_i
    el = jnp.where((lanef >= lo) & (lanef < lo + MOE_EXPERTS_PER_GROUP), logit, ninf)
    l1 = jnp.max(el, axis=1, keepdims=True)
    i1 = jnp.min(jnp.where(el == l1, lanef, big), axis=1, keepdims=True)
    el2 = jnp.where(lanef == i1, ninf, el)
    l2 = jnp.max(el2, axis=1, keepdims=True)
    i2 = jnp.min(jnp.where(el2 == l2, lanef, big), axis=1, keepdims=True)
    e21 = jnp.exp(l2 - l1)
    w1 = grp_p / (1.0 + e21)
    w2 = grp_p * e21 / (1.0 + e21)
    e1 = i1 - MOE_GROUPS
    e2 = i2 - MOE_GROUPS
    oh1 = lanef == e1
    oh2 = lanef == e2
    hot = jnp.where(oh1 | oh2, 1.0, 0.0)
    ri = lax.broadcasted_iota(I32, (tm, tm), 0)
    ci = lax.broadcasted_iota(I32, (tm, tm), 1)
    before = (ci < ri).astype(BF16)
    prior = carry_ref[...] + jnp.dot(before, hot.astype(BF16), preferred_element_type=F32)
    r1 = jnp.sum(jnp.where(oh1, prior, 0.0), axis=1, keepdims=True)
    r2 = jnp.sum(jnp.where(oh2, prior, 0.0), axis=1, keepdims=True)
    total = carry_ref[...] + jnp.sum(hot, axis=0, keepdims=True)
    carry_ref[...] = total
    cnt_ref[...] = jnp.broadcast_to(total, cnt_ref.shape)
    route = jnp.where(lane == 0, e1, jnp.where(lane == 1, e2, jnp.where(lane == 2, w1, jnp.where(
        lane == 3, w2, jnp.where(lane == 4, r1, jnp.where(lane == 5, r2, 0.0))))))
    route_ref[...] = route


def _mix_out(o_gla, o_dsa, o_s5, w_out, x2, gt1, norm2_g, sc2, sh2, rgw, rgb, rew, reb, seq, tm=512):
    t, d = x2.shape
    per_b = seq // tm
    wg = jnp.pad(w_out[:GLA_HEADS * GLA_DV].reshape(GLA_HEADS, GLA_DV, d),
                 ((0, 0), (0, HEAD_PAD - GLA_DV), (0, 0))).reshape(GLA_W, d).astype(BF16)
    wd = w_out[GLA_HEADS * GLA_DV:GLA_HEADS * GLA_DV + DSA_WIDTH].astype(BF16)
    ws = w_out[GLA_HEADS * GLA_DV + DSA_WIDTH:].astype(BF16)
    rw = _pad_cols(jnp.concatenate([rgw, rew], axis=1), LANES)
    rb = _pad_cols(jnp.concatenate([rgb, reb]).reshape(1, -1), LANES)
    rw1 = rw.astype(BF16)
    rw2 = (rw - rw1.astype(F32)).astype(BF16)
    tok = lambda w: pl.BlockSpec((tm, w), lambda i: (i, 0))
    full = lambda a: pl.BlockSpec(a.shape, lambda i: (0, 0))
    perb = pl.BlockSpec((None, 1, d), lambda i: (i // per_b, 0, 0))
    vec = pl.BlockSpec((1, d), lambda i: (0, 0))
    return pl.pallas_call(
        functools.partial(_mix_out_kernel, tm=tm),
        grid=(t // tm,),
        in_specs=[tok(GLA_W), tok(DSA_WIDTH), tok(S5_WIDTH), full(wg), full(wd), full(ws), tok(d),
                  perb, vec, perb, perb, full(rw1), full(rw2), full(rb)],
        out_specs=[tok(d), tok(d), tok(LANES), pl.BlockSpec((SUBLANES, LANES), lambda i: (0, 0))],
        out_shape=[jax.ShapeDtypeStruct((t, d), F32), jax.ShapeDtypeStruct((t, d), F32),
                   jax.ShapeDtypeStruct((t, LANES), F32), jax.ShapeDtypeStruct((SUBLANES, LANES), F32)],
        scratch_shapes=[pltpu.VMEM((1, LANES), F32)],
        compiler_params=_cparams("arbitrary"),
        name="mix_out",
    )(o_gla, o_dsa, o_s5, wg, wd, ws, x2, gt1[:, None, :], norm2_g.reshape(1, d), sc2[:, None, :],
      sh2[:, None, :], rw1, rw2, rb)


def _row(ref, r):
    return ref.at[pl.ds(r, 1), :]


def _dispatch_kernel(dest_ref, h_ref, zero_ref, out_ref, sem, *, td):
    del zero_ref

    def start(g, c):
        for j in range(SUBLANES):
            r = g * SUBLANES + j
            pltpu.make_async_copy(_row(h_ref, r), _row(out_ref, dest_ref[0, r]), sem).start()
            pltpu.make_async_copy(_row(h_ref, r), _row(out_ref, dest_ref[1, r]), sem).start()
        return c

    lax.fori_loop(0, td // SUBLANES, start, 0)
    for _ in range(2):
        pltpu.make_async_copy(h_ref, out_ref.at[pl.ds(0, td), :], sem).wait()


def _dispatch(h2, dest, n_rows, td=256):
    t, d = h2.shape
    dest3 = dest.reshape(2, t // td, td).transpose(1, 0, 2)
    zeros = jnp.zeros((n_rows, d), F32)
    return pl.pallas_call(
        functools.partial(_dispatch_kernel, td=td),
        grid=(t // td,),
        in_specs=[pl.BlockSpec((None, 2, td), lambda i: (i, 0, 0), memory_space=pltpu.SMEM),
                  pl.BlockSpec((td, d), lambda i: (i, 0)),
                  pl.BlockSpec(memory_space=pl.ANY)],
        out_specs=pl.BlockSpec(memory_space=pl.ANY),
        out_shape=jax.ShapeDtypeStruct((n_rows, d), F32),
        scratch_shapes=[pltpu.SemaphoreType.DMA(())],
        input_output_aliases={2: 0},
        compiler_params=_cparams("arbitrary"),
        name="moe_dispatch",
    )(dest3, h2, zeros)


def _expert_kernel(be_ref, nb_ref, x_ref, wg_ref, wu_ref, wd_ref, o_ref):
    del be_ref
    i = pl.program_id(0)

    @pl.when(i < nb_ref[0])
    def _():
        xb = x_ref[...].astype(BF16)
        g = jnp.dot(xb, wg_ref[...], preferred_element_type=F32)
        u = jnp.dot(xb, wu_ref[...], preferred_element_type=F32)
        hid = (g * _sigmoid(g) * u).astype(BF16)
        o_ref[...] = jnp.dot(hid, wd_ref[...], preferred_element_type=F32)

    @pl.when(i >= nb_ref[0])
    def _():
        o_ref[...] = jnp.zeros_like(o_ref)


def _experts(xs, blk_e, n_used, wg, wu, wd):
    n_rows = xs.shape[0]
    nb = n_rows // MOE_BM
    d, hdim = wg.shape[-2:]
    rows = pl.BlockSpec((MOE_BM, d), lambda i, be, nu: (i, 0))
    return pl.pallas_call(
        _expert_kernel,
        grid_spec=pltpu.PrefetchScalarGridSpec(
            num_scalar_prefetch=2,
            grid=(nb,),
            in_specs=[
                rows,
                pl.BlockSpec((None, d, hdim), lambda i, be, nu: (be[i], 0, 0)),
                pl.BlockSpec((None, d, hdim), lambda i, be, nu: (be[i], 0, 0)),
                pl.BlockSpec((None, hdim, d), lambda i, be, nu: (be[i], 0, 0)),
            ],
            out_specs=rows,
        ),
        out_shape=jax.ShapeDtypeStruct((n_rows, d), F32),
        compiler_params=_cparams("arbitrary"),
        name="moe_experts",
    )(blk_e, n_used, xs, wg, wu, wd)


def _combine_kernel(dest_ref, x_ref, route_ref, gt_ref, fg_ref, eo_ref, o_ref, rows_ref, sem, *, tc, final):
    def start(g, c):
        for j in range(SUBLANES):
            r = g * SUBLANES + j
            pltpu.make_async_copy(_row(eo_ref, dest_ref[0, r]), _row(rows_ref.at[0], r), sem).start()
            pltpu.make_async_copy(_row(eo_ref, dest_ref[1, r]), _row(rows_ref.at[1], r), sem).start()
        return c

    lax.fori_loop(0, tc // SUBLANES, start, 0)
    for which in range(2):
        pltpu.make_async_copy(eo_ref.at[pl.ds(0, tc), :], rows_ref.at[which], sem).wait()
    route = route_ref[...]
    lane = lax.broadcasted_iota(I32, route.shape, 1)
    w1 = jnp.sum(jnp.where(lane == 2, route, 0.0), axis=1, keepdims=True)
    w2 = jnp.sum(jnp.where(lane == 3, route, 0.0), axis=1, keepdims=True)
    y = rows_ref[0] * w1 + rows_ref[1] * w2
    x = x_ref[...] + gt_ref[...] * y
    if final:
        x = _rms(x, fg_ref[...])
    o_ref[...] = x


def _combine(x1, route, dest, eo, gt2, final_g, seq, final, tc=256):
    t, d = x1.shape
    per_b = seq // tc
    dest3 = dest.reshape(2, t // tc, tc).transpose(1, 0, 2)
    return pl.pallas_call(
        functools.partial(_combine_kernel, tc=tc, final=final),
        grid=(t // tc,),
        in_specs=[pl.BlockSpec((None, 2, tc), lambda i: (i, 0, 0), memory_space=pltpu.SMEM),
                  pl.BlockSpec((tc, d), lambda i: (i, 0)),
                  pl.BlockSpec((tc, LANES), lambda i: (i, 0)),
                  pl.BlockSpec((None, 1, d), lambda i: (i // per_b, 0, 0)),
                  pl.BlockSpec((1, d), lambda i: (0, 0)),
                  pl.BlockSpec(memory_space=pl.ANY)],
        out_specs=pl.BlockSpec((tc, d), lambda i: (i, 0)),
        out_shape=jax.ShapeDtypeStruct((t, d), F32),
        scratch_shapes=[pltpu.VMEM((2, tc, d), F32), pltpu.SemaphoreType.DMA(())],
        compiler_params=_cparams("arbitrary"),
        name="moe_combine",
    )(dest3, x1, route, gt2[:, None, :], final_g.reshape(1, d), eo)


def _moe_plan(route, counts, t):
    e = route[:, 0:2].astype(I32).T
    rank = route[:, 4:6].astype(I32).T
    cnt = counts[0, :MOE_EXPERTS].astype(I32)
    padded = (cnt + MOE_BM - 1) // MOE_BM * MOE_BM
    pend = jnp.cumsum(padded)
    pstart = pend - padded
    ids = jnp.arange(MOE_EXPERTS, dtype=I32)
    dest = rank + jnp.sum(jnp.where(e[..., None] == ids, pstart, 0), axis=-1)
    nb = (2 * t) // MOE_BM + MOE_EXPERTS
    n_used = (pend[-1] // MOE_BM).astype(I32)
    first_row = jnp.minimum(jnp.arange(nb, dtype=I32), n_used - 1) * MOE_BM
    blk_e = jnp.sum((pend[None, :] <= first_row[:, None]).astype(I32), axis=1)
    blk_e = jnp.minimum(blk_e, MOE_EXPERTS - 1)
    return dest, blk_e, n_used.reshape(1), nb * MOE_BM


def kernel(x, c, ada_w, ada_b, norm1_g, w_in, gla_gate_w, gla_gate_b, gla_norm_g, dsa_norm_g, s5_a_re, s5_a_im, s5_log_step, s5_b_re, s5_b_im, s5_c_re, s5_c_im, s5_d, s5_glu_w, s5_glu_b, s5_norm_g, w_out, norm2_g, router_grp_w, router_grp_b, router_exp_w, router_exp_b, exp_w_gate, exp_w_up, exp_w_down, final_norm_g):
    batch, seq, d = x.shape
    depth = ada_w.shape[0]
    t = batch * seq
    mod = _ada_mod(c, ada_w, ada_b)
    x2 = x.reshape(t, d)
    for l in range(depth):
        sh1, sc1, gt1, sh2, sc2, gt2 = (mod[l, j] for j in range(6))
        p = _in_proj(x2, norm1_g[l], sc1, sh1, _layout_w_in(w_in[l]), seq)
        o_gla = _gla(p, gla_gate_w[l], gla_gate_b[l], gla_norm_g[l], batch, seq)
        o_dsa = _dsa(p, dsa_norm_g[l], batch, seq)
        tabs = _s5_tables(s5_a_re[l], s5_a_im[l], s5_log_step[l], s5_b_re[l], s5_b_im[l],
                          s5_c_re[l], s5_c_im[l])
        o_s5 = _s5((p["su0"], p["su1"], p["su2"]), p["sx"], tabs, s5_d[l], s5_glu_w[l], s5_glu_b[l], s5_norm_g[l], batch, seq)
        x1, h2, route, counts = _mix_out(o_gla, o_dsa, o_s5, w_out[l], x2, gt1, norm2_g[l], sc2, sh2,
                                         router_grp_w[l], router_grp_b[l], router_exp_w[l],
                                         router_exp_b[l], seq)
        dest, blk_e, n_used, n_rows = _moe_plan(route, counts, t)
        xs = _dispatch(h2, dest, n_rows)
        eo = _experts(xs, blk_e, n_used, exp_w_gate[l].astype(BF16), exp_w_up[l].astype(BF16),
                      exp_w_down[l].astype(BF16))
        x2 = _combine(x1, route, dest, eo, gt2, final_norm_g, seq, final=(l == depth - 1))
    return x2.reshape(batch, seq, d)
```

```python
import functools
import math

import jax
import jax.numpy as jnp
import numpy as np
from jax import lax
from jax.experimental import pallas as pl
from jax.experimental.pallas import tpu as pltpu

F32 = jnp.float32
BF16 = jnp.bfloat16
I32 = jnp.int32

D_MODEL = 1024
GLA_HEADS = 4
GLA_DV = 96
GLA_DK = 48
GLA_RANK = 16
GLA_TAU = 16.0
GLA_CHUNK = 64
DSA_HEADS = 4
DSA_HEAD_DIM = 64
DSA_WIDTH = DSA_HEADS * DSA_HEAD_DIM
DSA_IDX_HEADS = 8
DSA_IDX_DIM = 32
DSA_TOPK = 256
S5_GROUPS = 24
S5_GROUP_CH = 16
S5_STATE = 64
S5_WIDTH = S5_GROUPS * S5_GROUP_CH
MOE_GROUPS = 4
MOE_EXPERTS_PER_GROUP = 8
MOE_EXPERTS = MOE_GROUPS * MOE_EXPERTS_PER_GROUP
MOE_HIDDEN = 512
RMS_EPS = 1e-6
IN_SIZES = (192, 192, 384, 16, 384, 256, 256, 256, 256, 32, 8, 384)

LANES = 128
SUBLANES = 8
VMEM_LIMIT = 56 * 1024 * 1024

HEAD_PAD = LANES
GLA_W = GLA_HEADS * HEAD_PAD
S5_CHUNK = 16
S5_TILE_G = LANES // S5_GROUP_CH
S5_TILES = S5_GROUPS // S5_TILE_G
S5_ROW_W = S5_CHUNK * LANES
S5_STATE_W = S5_TILE_G * S5_STATE
MOE_BM = 512
INT_MIN = -2 ** 31


def _cparams(*sem):
    return pltpu.CompilerParams(dimension_semantics=tuple(sem), vmem_limit_bytes=VMEM_LIMIT)


def _sigmoid(x):
    return 0.5 * (jnp.tanh(0.5 * x) + 1.0)


def _rms(x, g, n=None):
    n = x.shape[-1] if n is None else n
    ms = jnp.sum(x * x, axis=-1, keepdims=True) * (1.0 / n)
    return x * lax.rsqrt(ms + RMS_EPS) * g


def _ada_kernel(c_ref, w_ref, b_ref, o_ref):
    c = c_ref[...]
    cond = (c * _sigmoid(c)).astype(BF16)
    o_ref[...] = jnp.dot(cond, w_ref[...].astype(BF16), preferred_element_type=F32) + b_ref[...]


def _ada_mod(c, ada_w, ada_b):
    depth, d, d6 = ada_w.shape
    nblk = d6 // d
    bp = -(-c.shape[0] // SUBLANES) * SUBLANES
    cp = jnp.pad(c, ((0, bp - c.shape[0]), (0, 0)))
    out = pl.pallas_call(
        _ada_kernel,
        grid=(depth, nblk),
        in_specs=[
            pl.BlockSpec((bp, d), lambda l, j: (0, 0)),
            pl.BlockSpec((None, d, d), lambda l, j: (l, 0, j)),
            pl.BlockSpec((None, None, 1, d), lambda l, j: (l, j, 0, 0)),
        ],
        out_specs=pl.BlockSpec((None, None, bp, d), lambda l, j: (l, j, 0, 0)),
        out_shape=jax.ShapeDtypeStruct((depth, nblk, bp, d), F32),
        compiler_params=_cparams("arbitrary", "arbitrary"),
        name="ada_mod",
    )(cp, ada_w, ada_b.reshape(depth, nblk, 1, d))
    return out[:, :, : c.shape[0], :]


PROJ_OUTS = (
    ("gq", GLA_W, BF16), ("gk", GLA_W, BF16), ("gv", GLA_W, BF16), ("gr", GLA_W, BF16),
    ("glr", LANES, BF16),
    ("dq", DSA_WIDTH, BF16), ("dk", DSA_WIDTH, BF16), ("dv", DSA_WIDTH, BF16),
    ("iq", DSA_IDX_HEADS * DSA_IDX_DIM, BF16), ("ik", DSA_IDX_HEADS * DSA_IDX_DIM, BF16),
    ("iw", LANES, F32), ("su0", LANES, F32), ("su1", LANES, F32), ("su2", LANES, F32),
)
PROJ_COLS = sum(w for _, w, _ in PROJ_OUTS)


def _pad_heads(w, heads, dim):
    d = w.shape[0]
    w = w.reshape(d, heads, dim)
    return jnp.pad(w, ((0, 0), (0, 0), (0, HEAD_PAD - dim))).reshape(d, heads * HEAD_PAD)


def _pad_cols(w, width):
    return jnp.pad(w, ((0, 0), (0, width - w.shape[1])))


def _layout_w_in(w_in):
    offs = np.cumsum((0,) + IN_SIZES)
    p = [w_in[:, offs[i]:offs[i + 1]] for i in range(len(IN_SIZES))]
    cols = [
        _pad_heads(p[0], GLA_HEADS, GLA_DK), _pad_heads(p[1], GLA_HEADS, GLA_DK),
        _pad_heads(p[2], GLA_HEADS, GLA_DV), _pad_heads(p[4], GLA_HEADS, GLA_DV),
        _pad_cols(p[3], LANES),
        p[5], p[6], p[7], p[8], jnp.tile(p[9], (1, DSA_IDX_HEADS)),
        _pad_cols(p[10], LANES), p[11],
    ]
    return jnp.concatenate(cols, axis=1).astype(BF16)


def _proj_kernel(x_ref, g_ref, sc_ref, sh_ref, w_ref, *o_refs):
    x = x_ref[...]
    h = _rms(x, g_ref[...]) * (1.0 + sc_ref[...]) + sh_ref[...]
    hb = h.astype(BF16)
    c0 = 0
    for o_ref, (_, width, _) in zip(o_refs, PROJ_OUTS):
        o_ref[...] = jnp.dot(hb, w_ref[:, c0:c0 + width], preferred_element_type=F32).astype(o_ref.dtype)
        c0 += width
    sx_ref = o_refs[len(PROJ_OUTS)]
    n_chunks = x.shape[0] // S5_CHUNK
    for q in range(S5_TILES):
        su_ref = o_refs[len(PROJ_OUTS) - S5_TILES + q]
        for s in range(S5_CHUNK):
            sx_ref[q, :, s * LANES:(s + 1) * LANES] = su_ref[pl.ds(s, n_chunks, stride=S5_CHUNK), :]


def _in_proj(x2, norm_g, sc, sh, w_lay, seq, tm=512):
    t, d = x2.shape
    per_b = seq // tm
    outs = pl.pallas_call(
        _proj_kernel,
        grid=(t // tm,),
        in_specs=[
            pl.BlockSpec((tm, d), lambda i: (i, 0)),
            pl.BlockSpec((1, d), lambda i: (0, 0)),
            pl.BlockSpec((None, 1, d), lambda i: (i // per_b, 0, 0)),
            pl.BlockSpec((None, 1, d), lambda i: (i // per_b, 0, 0)),
            pl.BlockSpec((d, PROJ_COLS), lambda i: (0, 0)),
        ],
        out_specs=[pl.BlockSpec((tm, w), lambda i: (i, 0)) for _, w, _ in PROJ_OUTS]
        + [pl.BlockSpec((S5_TILES, tm // S5_CHUNK, S5_ROW_W), lambda i: (0, i, 0))],
        out_shape=[jax.ShapeDtypeStruct((t, w), dt) for _, w, dt in PROJ_OUTS]
        + [jax.ShapeDtypeStruct((S5_TILES, t // S5_CHUNK, S5_ROW_W), F32)],
        compiler_params=_cparams("arbitrary"),
        name="in_proj",
    )(x2, norm_g.reshape(1, d), sc[:, None, :], sh[:, None, :], w_lay)
    p = {name: o for (name, _, _), o in zip(PROJ_OUTS, outs)}
    p["sx"] = outs[-1]
    return p


def _gla_kernel(q_ref, k_ref, v_ref, r_ref, lr_ref, gw_ref, gb_ref, ng_ref, o_ref, s_ref, *, rows):
    @pl.when(pl.program_id(1) == 0)
    def _():
        s_ref[...] = jnp.zeros_like(s_ref)

    c = GLA_CHUNK
    gl = jnp.dot(lr_ref[...], gw_ref[...], preferred_element_type=F32) + gb_ref[...]
    g = (jnp.minimum(gl, 0.0) - jnp.log1p(jnp.exp(-jnp.abs(gl)))) * (1.0 / GLA_TAU)
    g1 = g.astype(BF16)
    rem = g - g1.astype(F32)
    g2 = rem.astype(BF16)
    g3 = (rem - g2.astype(F32)).astype(BF16)
    ri = lax.broadcasted_iota(I32, (c, c), 0)
    ci = lax.broadcasted_iota(I32, (c, c), 1)
    causal = ri >= ci
    tri = causal.astype(BF16)
    ng = ng_ref[...]
    for ch in range(rows // c):
        rs = slice(ch * c, (ch + 1) * c)
        bcum = (jnp.dot(tri, g1[rs], preferred_element_type=F32)
                + jnp.dot(tri, g2[rs], preferred_element_type=F32)
                + jnp.dot(tri, g3[rs], preferred_element_type=F32))
        blast = bcum[c - 1:c, :]
        e_pos = jnp.exp(bcum)
        kf = k_ref[rs, :].astype(F32)
        qd = (q_ref[rs, :].astype(F32) * (GLA_DK ** -0.5) * e_pos).astype(BF16)
        kd = (kf * jnp.exp(-bcum)).astype(BF16)
        ku = (kf * jnp.exp(blast - bcum)).astype(BF16)
        dec = jnp.exp(blast)
        for h in range(GLA_HEADS):
            hs = slice(h * HEAD_PAD, (h + 1) * HEAD_PAD)
            vh = v_ref[rs, hs]
            att = lax.dot_general(qd[:, hs], kd[:, hs], (((1,), (1,)), ((), ())),
                                  preferred_element_type=F32)
            att = jnp.where(causal, att, 0.0).astype(BF16)
            st = s_ref[h]
            o = jnp.dot(att, vh, preferred_element_type=F32)
            o = o + lax.dot_general(qd[:, hs], st.astype(BF16), (((1,), (1,)), ((), ())),
                                    preferred_element_type=F32)
            upd = lax.dot_general(vh, ku[:, hs], (((0,), (0,)), ((), ())),
                                  preferred_element_type=F32)
            s_ref[h] = st * dec[:, hs] + upd
            y = _rms(o, ng[:, hs], GLA_DV)
            rr = r_ref[rs, hs].astype(F32)
            o_ref[rs, hs] = (y * (rr * _sigmoid(rr))).astype(o_ref.dtype)


def _gla(p, gate_w, gate_b, norm_g, batch, seq, rows=512):
    gw = _pad_cols(jnp.pad(_pad_heads(gate_w, GLA_HEADS, GLA_DK), ((0, LANES - GLA_RANK), (0, 0))), GLA_W)
    gb = _pad_heads(gate_b.reshape(1, -1), GLA_HEADS, GLA_DK)
    ng = _pad_heads(jnp.tile(norm_g.reshape(1, GLA_DV), (1, GLA_HEADS)), GLA_HEADS, GLA_DV)
    rows = min(rows, seq)
    blk = lambda w: pl.BlockSpec((None, rows, w), lambda b, j: (b, j, 0))
    full = lambda a: pl.BlockSpec(a.shape, lambda b, j: (0, 0))
    r3 = lambda a: a.reshape(batch, seq, a.shape[-1])
    out = pl.pallas_call(
        functools.partial(_gla_kernel, rows=rows),
        grid=(batch, seq // rows),
        in_specs=[blk(GLA_W), blk(GLA_W), blk(GLA_W), blk(GLA_W), blk(LANES),
                  full(gw), full(gb), full(ng)],
        out_specs=blk(GLA_W),
        out_shape=jax.ShapeDtypeStruct((batch, seq, GLA_W), BF16),
        scratch_shapes=[pltpu.VMEM((GLA_HEADS, HEAD_PAD, HEAD_PAD), F32)],
        compiler_params=_cparams("arbitrary", "arbitrary"),
        name="gla",
    )(r3(p["gq"]), r3(p["gk"]), r3(p["gv"]), r3(p["gr"]), r3(p["glr"]), gw.astype(BF16), gb, ng)
    return out.reshape(batch * seq, GLA_W)


def _dsa_kernel(iq_ref, ik_ref, iw_ref, q_ref, k_ref, v_ref, ng_ref, o_ref,
                qs_ref, q4_ref, keys_ref, acc_ref, run_ref, top_ref, sa_ref, sb_ref, kmax_ref,
                *, tq, kc, sub, n_sel, top_r):
    assert kc == 2 * sub
    qi = pl.program_id(1)
    nkc = ((qi + 1) * tq + kc - 1) // kc
    lane_w = lax.broadcasted_iota(I32, (tq, DSA_WIDTH), 1)
    iq = iq_ref[...]
    for h in range(DSA_IDX_HEADS):
        m = (lane_w >= h * DSA_IDX_DIM) & (lane_w < (h + 1) * DSA_IDX_DIM)
        qs_ref[h * tq:(h + 1) * tq, :] = jnp.where(m, iq, jnp.zeros_like(iq))
    iw = iw_ref[...] * (DSA_IDX_HEADS ** -0.5 * DSA_IDX_DIM ** -0.5)
    lane = lax.broadcasted_iota(I32, (tq, LANES), 1)
    w_cols = [iw[:, h:h + 1] for h in range(DSA_IDX_HEADS)]
    qpos = qi * tq + lax.broadcasted_iota(I32, (tq, sub), 0)
    kcol = lax.broadcasted_iota(I32, (tq, sub), 1)
    nt = (((1,), (1,)), ((), ()))

    ninf = jnp.float32(-jnp.inf)

    def score_body(c, carry):
        for u in range(kc // sub):
            k0 = pl.multiple_of(c * kc + u * sub, sub)
            lg = lax.dot_general(qs_ref[...], ik_ref[pl.ds(k0, sub), :], nt,
                                 preferred_element_type=F32)
            sc = jnp.zeros((tq, sub), F32)
            for h in range(DSA_IDX_HEADS):
                sc = sc + w_cols[h] * jnp.maximum(lg[h * tq:(h + 1) * tq, :], 0.0)
            sc = sc + 0.0
            keys_ref[c, :, u * sub:(u + 1) * sub] = jnp.where(kcol + k0 <= qpos, sc, ninf)
        return carry

    lax.fori_loop(0, nkc, score_body, 0)

    def to_key(x):
        bits = lax.bitcast_convert_type(x, I32)
        key = jnp.where(bits < 0, bits ^ jnp.int32(0x7FFFFFFF), bits)
        return jnp.where(x == ninf, jnp.int32(INT_MIN), key)

    def count_ge(cand):
        candb = jnp.broadcast_to(cand, (tq, LANES))

        def body(c, acc):
            blk = keys_ref[c]
            for j in range(kc // LANES):
                acc = acc + jnp.where(to_key(blk[:, j * LANES:(j + 1) * LANES]) >= candb, 1.0, 0.0)
            return acc

        acc = lax.fori_loop(0, nkc, body, jnp.zeros((tq, LANES), F32))
        return jnp.sum(acc, axis=1, keepdims=True)

    need = jnp.float32(n_sel)

    def select(count):
        t0 = jnp.full((tq, 1), INT_MIN, I32)
        zero = jnp.zeros((tq, 1), I32)
        t0 = jnp.where(count(zero) >= need, zero, t0)

        def bit_body(i, t):
            cand = t | (jnp.int32(1) << (30 - i))
            return jnp.where(count(cand) >= need, cand, t)

        t = lax.fori_loop(0, 31, bit_body, t0)
        t = jnp.maximum(t, jnp.int32(INT_MIN + 1))
        return t, count(t + 1), count(t)

    lowest = jnp.full((SUBLANES, LANES), ninf, F32)
    for rt in range(0, tq // SUBLANES, 2):
        rsl = [slice((rt + a) * SUBLANES, (rt + a + 1) * SUBLANES) for a in range(2)]

        def top_body(c, ts, rsl=rsl):
            ts = [list(t) for t in ts]
            for j in range(kc // LANES):
                for a in range(2):
                    x = keys_ref[c, rsl[a], j * LANES:(j + 1) * LANES]
                    for r in range(top_r):
                        hi = jnp.maximum(ts[a][r], x)
                        x = jnp.minimum(ts[a][r], x)
                        ts[a][r] = hi
            return tuple(tuple(t) for t in ts)

        ts = lax.fori_loop(0, nkc, top_body, ((lowest,) * top_r,) * 2)
        for a in range(2):
            for r in range(top_r):
                top_ref[r, rsl[a], :] = to_key(ts[a][r])

    def count_cand(cand):
        candb = jnp.broadcast_to(cand, (tq, LANES))
        acc = jnp.zeros((tq, LANES), F32)
        for r in range(top_r):
            acc = acc + jnp.where(top_ref[r] >= candb, 1.0, 0.0)
        return jnp.sum(acc, axis=1, keepdims=True)

    thr_c, n_gt_c, n_ge_c = select(count_cand)
    hidden = jnp.max(jnp.where(top_ref[top_r - 1] >= thr_c, 1.0, 0.0)) > 0.0
    thr, n_gt, n_ge = lax.cond(hidden, lambda: select(count_ge), lambda: (thr_c, n_gt_c, n_ge_c))
    n_tie = need - n_gt
    any_tie = jnp.max(n_ge) > need
    thr_bits = jnp.where(thr < 0, thr ^ jnp.int32(0x7FFFFFFF), thr)
    thr_f = jnp.where(thr == jnp.int32(INT_MIN + 1), jnp.float32(jnp.finfo(jnp.float32).min),
                      lax.bitcast_convert_type(thr_bits, F32))

    ui = lax.broadcasted_iota(I32, (kc, kc), 0)
    uj = lax.broadcasted_iota(I32, (kc, kc), 1)
    before = (ui < uj).astype(BF16)
    q = q_ref[...] * jnp.asarray(DSA_HEAD_DIM ** -0.5, BF16)
    for h in range(DSA_HEADS):
        m = (lane_w >= h * DSA_HEAD_DIM) & (lane_w < (h + 1) * DSA_HEAD_DIM)
        q4_ref[h * tq:(h + 1) * tq, :] = jnp.where(m, q, jnp.zeros_like(q))
    neg = jnp.float32(-1e30)

    def lane_fold(op, acc, s):
        for j in range(s.shape[1] // LANES):
            acc = op(acc, s[:, j * LANES:(j + 1) * LANES])
        return acc

    @pl.when(qi == 0)
    def _():
        kmax_ref[...] = jnp.zeros_like(kmax_ref)

    hl = lax.broadcasted_iota(I32, (DSA_WIDTH, LANES), 0) // DSA_HEAD_DIM
    head_of = (hl == lax.broadcasted_iota(I32, (DSA_WIDTH, LANES), 1)).astype(BF16)
    kb = k_ref[pl.ds(pl.multiple_of(qi * tq, tq), tq), :].astype(F32)
    kn2 = jnp.dot((kb * kb).astype(BF16), head_of, preferred_element_type=F32)
    kmax = jnp.maximum(kmax_ref[0:1, :], jnp.max(kn2, axis=0, keepdims=True))
    kmax_ref[...] = jnp.broadcast_to(kmax, kmax_ref.shape)
    qf = q.astype(F32)
    qn2 = jnp.dot((qf * qf).astype(BF16), head_of, preferred_element_type=F32)
    bound = jnp.sqrt(qn2 * kmax) * 1.02
    ms_bound = [jnp.sum(jnp.where(lane == h, bound, 0.0), axis=1, keepdims=True) for h in range(DSA_HEADS)]

    def tie_bias_pass():
        def body(c, seen):
            sc = keys_ref[c]
            eq = sc == thr_f
            eqf = jnp.where(eq, 1.0, 0.0)
            rank = seen + jnp.dot(eqf.astype(BF16), before, preferred_element_type=F32)
            sel = (sc > thr_f) | (eq & (rank < n_tie))
            keys_ref[c] = jnp.where(sel, 0.0, neg)
            return seen + jnp.sum(eqf, axis=1, keepdims=True)

        lax.fori_loop(0, nkc, body, jnp.zeros((tq, 1), F32))

    def bias_of(c, stored):
        sc = keys_ref[c]
        return sc if stored else jnp.where(sc >= thr_f, 0.0, neg)

    def qk(k0):
        return lax.dot_general(q4_ref[...], k_ref[pl.ds(k0, sub), :], nt, preferred_element_type=F32)

    def max_pass(stored):
        run_ref[...] = jnp.full(run_ref.shape, neg, F32)

        def body(c, carry):
            bias = bias_of(c, stored)
            for u in range(kc // sub):
                s = qk(pl.multiple_of(c * kc + u * sub, sub))
                for h in range(DSA_HEADS):
                    run_ref[h] = lane_fold(jnp.maximum, run_ref[h],
                                           s[h * tq:(h + 1) * tq] + bias[:, u * sub:(u + 1) * sub])
            return carry

        lax.fori_loop(0, nkc, body, 0)
        return [jnp.max(run_ref[h], axis=1, keepdims=True) for h in range(DSA_HEADS)]

    def att_pass(stored, ms):
        run_ref[...] = jnp.zeros_like(run_ref)
        acc_ref[...] = jnp.zeros_like(acc_ref)

        def half(s_ref, bias, k0):
            ps = []
            for h in range(DSA_HEADS):
                p = jnp.exp(s_ref[h * tq:(h + 1) * tq, :] + bias - ms[h])
                run_ref[h] = lane_fold(jnp.add, run_ref[h], p)
                ps.append(p.astype(BF16))
            acc_ref[...] += jnp.dot(jnp.concatenate(ps, axis=0), v_ref[pl.ds(k0, sub), :],
                                    preferred_element_type=F32)

        sa_ref[...] = qk(0)

        def body(c, carry):
            k0 = pl.multiple_of(c * kc, kc)
            sb_ref[...] = qk(k0 + sub)
            bias = bias_of(c, stored)
            half(sa_ref, bias[:, :sub], k0)
            sa_ref[...] = qk(pl.multiple_of(jnp.minimum(c + 1, nkc - 1) * kc, kc))
            half(sb_ref, bias[:, sub:], k0 + sub)
            return carry

        lax.fori_loop(0, nkc, body, 0)

    def finish(stored):
        if stored:
            tie_bias_pass()
        att_pass(stored, ms_bound)
        lmin = jnp.min(jnp.sum(run_ref[0], axis=1, keepdims=True))
        for h in range(1, DSA_HEADS):
            lmin = jnp.minimum(lmin, jnp.min(jnp.sum(run_ref[h], axis=1, keepdims=True)))

        def redo():
            att_pass(stored, max_pass(stored))
            return 0

        lax.cond(lmin >= 1e-26, lambda: 0, redo)
        return 0

    lax.cond(any_tie, lambda: finish(True), lambda: finish(False))
    out = jnp.zeros((tq, DSA_WIDTH), F32)
    for h in range(DSA_HEADS):
        m = (lane_w >= h * DSA_HEAD_DIM) & (lane_w < (h + 1) * DSA_HEAD_DIM)
        out = out + jnp.where(m, acc_ref[h * tq:(h + 1) * tq, :] / jnp.sum(run_ref[h], axis=1, keepdims=True), 0.0)
    o_ref[...] = _rms(out, ng_ref[...]).astype(o_ref.dtype)


def _dsa(p, norm_g, batch, seq, tq=256, kc=512, top_r=12):
    n_sel = min(DSA_TOPK, seq // 4)
    kc = min(kc, seq)
    sub = kc // 2
    r3 = lambda a: a.reshape(batch, seq, a.shape[-1])
    qblk = lambda w: pl.BlockSpec((None, tq, w), lambda b, i: (b, i, 0))
    kfull = lambda w: pl.BlockSpec((None, seq, w), lambda b, i: (b, 0, 0))
    out = pl.pallas_call(
        functools.partial(_dsa_kernel, tq=tq, kc=kc, sub=sub, n_sel=n_sel, top_r=top_r),
        grid=(batch, seq // tq),
        in_specs=[qblk(DSA_WIDTH), kfull(DSA_WIDTH), qblk(LANES), qblk(DSA_WIDTH),
                  kfull(DSA_WIDTH), kfull(DSA_WIDTH), pl.BlockSpec((1, DSA_WIDTH), lambda b, i: (0, 0))],
        out_specs=qblk(DSA_WIDTH),
        out_shape=jax.ShapeDtypeStruct((batch, seq, DSA_WIDTH), BF16),
        scratch_shapes=[
            pltpu.VMEM((DSA_IDX_HEADS * tq, DSA_WIDTH), BF16),
            pltpu.VMEM((DSA_HEADS * tq, DSA_WIDTH), BF16),
            pltpu.VMEM((seq // kc, tq, kc), F32),
            pltpu.VMEM((DSA_HEADS * tq, DSA_WIDTH), F32),
            pltpu.VMEM((DSA_HEADS, tq, LANES), F32),
            pltpu.VMEM((top_r, tq, LANES), I32),
            pltpu.VMEM((DSA_HEADS * tq, sub), F32),
            pltpu.VMEM((DSA_HEADS * tq, sub), F32),
            pltpu.VMEM((SUBLANES, LANES), F32),
        ],
        compiler_params=_cparams("arbitrary", "arbitrary"),
        name="dsa",
    )(r3(p["iq"]), r3(p["ik"]), r3(p["iw"]), r3(p["dq"]), r3(p["dk"]), r3(p["dv"]),
      norm_g.reshape(1, DSA_WIDTH))
    return out.reshape(batch * seq, DSA_WIDTH)


def _s5_tables(a_re, a_im, log_step, b_re, b_im, c_re, c_im):
    hp = lax.Precision.HIGHEST
    s = S5_CHUNK
    dt = jnp.exp(log_step)[:, None]
    mag = jnp.exp(a_re * dt)
    lam_re, lam_im = mag * jnp.cos(a_im * dt), mag * jnp.sin(a_im * dt)
    den = a_re * a_re + a_im * a_im
    nr, ni = lam_re - 1.0, lam_im
    coef_re = (nr * a_re + ni * a_im) / den
    coef_im = (ni * a_re - nr * a_im) / den
    bb_re = coef_re[..., None] * b_re - coef_im[..., None] * b_im
    bb_im = coef_re[..., None] * b_im + coef_im[..., None] * b_re
    j = jnp.arange(s + 1, dtype=F32)[:, None, None]
    pmag = jnp.exp(a_re * dt * j)
    pw_re, pw_im = pmag * jnp.cos(a_im * dt * j), pmag * jnp.sin(a_im * dt * j)
    e_re = c_re[None] * pw_re[:, :, None, :] - c_im[None] * pw_im[:, :, None, :]
    e_im = c_re[None] * pw_im[:, :, None, :] + c_im[None] * pw_re[:, :, None, :]
    kk = (jnp.einsum("jgcp,gpd->jgcd", e_re, bb_re, precision=hp)
          - jnp.einsum("jgcp,gpd->jgcd", e_im, bb_im, precision=hp))
    lag = np.arange(s)[None, :] - np.arange(s)[:, None]
    toe = jnp.where((lag >= 0)[:, :, None, None, None], kk[np.clip(lag, 0, s)], 0.0)
    rp_re, rp_im = pw_re[s - 1 - np.arange(s)], pw_im[s - 1 - np.arange(s)]
    x_re = rp_re[..., None] * bb_re[None] - rp_im[..., None] * bb_im[None]
    x_im = rp_re[..., None] * bb_im[None] + rp_im[..., None] * bb_re[None]
    f_re, f_im = e_re[1:], e_im[1:]
    sc_w = s * S5_GROUP_CH
    g_ids = jnp.arange(S5_TILE_G)[:, None, None]

    def placement(inner, width):
        a = jnp.arange(width // S5_TILE_G)[None, :, None]
        dst = (a // inner) * (S5_TILE_G * inner) + g_ids * inner + a % inner
        return (dst == jnp.arange(width)[None, None, :]).astype(BF16)

    p_row = placement(S5_GROUP_CH, S5_ROW_W)
    p_state = placement(S5_STATE, S5_STATE_W)
    per_tile = lambda a: a.astype(BF16).reshape((S5_TILES, S5_TILE_G) + a.shape[1:])

    def place(tab, p_in, p_out):
        t = jnp.einsum("qgab,gbc->qgac", per_tile(tab), p_out, preferred_element_type=F32).astype(BF16)
        return jnp.einsum("gar,qgac->qrc", p_in, t, preferred_element_type=F32).astype(BF16)

    g_toe = toe.transpose(2, 0, 4, 1, 3).reshape(S5_GROUPS, sc_w, sc_w)
    g_w1 = lambda x: x.transpose(1, 0, 3, 2).reshape(S5_GROUPS, sc_w, S5_STATE)
    g_w2 = lambda f: f.transpose(1, 3, 0, 2).reshape(S5_GROUPS, S5_STATE, sc_w)
    tabs = dict(toe=place(g_toe, p_row, p_row),
                w1_re=place(g_w1(x_re), p_row, p_state), w1_im=place(g_w1(x_im), p_row, p_state),
                w2_re=place(g_w2(f_re), p_state, p_row), w2_im=place(g_w2(-f_im), p_state, p_row))
    tabs["a_re"], tabs["a_im"] = pw_re[s].reshape(-1), pw_im[s].reshape(-1)
    return tabs


def _s5_state_in_kernel(u_ref, wr_ref, wi_ref, xr_ref, xi_ref):
    u = u_ref[...].astype(BF16)
    xr_ref[...] = jnp.dot(u, wr_ref[...], preferred_element_type=F32)
    xi_ref[...] = jnp.dot(u, wi_ref[...], preferred_element_type=F32)


def _s5_scan_kernel(xr_ref, xi_ref, ar_ref, ai_ref, hr_ref, hi_ref, cr_ref, ci_ref, *, steps):
    @pl.when(pl.program_id(0) == 0)
    def _():
        cr_ref[...] = jnp.zeros_like(cr_ref)
        ci_ref[...] = jnp.zeros_like(ci_ref)

    ar, ai = ar_ref[...], ai_ref[...]

    def body(n, carry):
        hr, hi = carry
        hr_ref[n] = hr
        hi_ref[n] = hi
        return ar * hr - ai * hi + xr_ref[n], ar * hi + ai * hr + xi_ref[n]

    hr, hi = lax.fori_loop(0, steps, body, (cr_ref[...], ci_ref[...]))
    cr_ref[...] = hr
    ci_ref[...] = hi


def _s5_out_kernel(u_ref, toe_ref, hr_ref, hi_ref, w2r_ref, w2i_ref, y_ref):
    y = jnp.dot(u_ref[...].astype(BF16), toe_ref[...], preferred_element_type=F32)
    y = y + jnp.dot(hr_ref[...].astype(BF16), w2r_ref[...], preferred_element_type=F32)
    y = y + jnp.dot(hi_ref[...].astype(BF16), w2i_ref[...], preferred_element_type=F32)
    y_ref[...] = y


def _s5_post_kernel(y_ref, u0_ref, u1_ref, u2_ref, d_ref, gw_ref, gb_ref, ng_ref, o_ref, yt_ref):
    n_chunks = yt_ref.shape[1] // S5_CHUNK
    for q in range(S5_TILES):
        for s in range(S5_CHUNK):
            yt_ref[q, pl.ds(s, n_chunks, stride=S5_CHUNK), :] = y_ref[q, :, s * LANES:(s + 1) * LANES]
    y = jnp.concatenate([yt_ref[q] for q in range(S5_TILES)], axis=1)
    u = jnp.concatenate([u0_ref[...], u1_ref[...], u2_ref[...]], axis=1)
    z = y + d_ref[...] * u
    z = 0.5 * z * (1.0 + jnp.tanh(math.sqrt(2.0 / math.pi) * (z + 0.044715 * (z * z * z))))
    gate = _sigmoid(jnp.dot(z.astype(BF16), gw_ref[...], preferred_element_type=F32) + gb_ref[...])
    o_ref[...] = _rms(z * gate, ng_ref[...]).astype(o_ref.dtype)


def _s5(us, xq, tabs, d, glu_w, glu_b, norm_g, batch, seq, tm=512, rb=512):
    s = S5_CHUNK
    nch = seq // s
    rows = nch * batch
    rb = min(rb, rows)
    gp = S5_GROUPS * S5_STATE
    row_in = pl.BlockSpec((None, rb, S5_ROW_W), lambda q, i: (q, i, 0))
    wspec = lambda a: pl.BlockSpec((None,) + a.shape[1:], lambda q, i: (q, 0, 0))
    colblk = pl.BlockSpec((rb, S5_STATE_W), lambda q, i: (i, q))
    xr, xi = pl.pallas_call(
        _s5_state_in_kernel,
        grid=(S5_TILES, rows // rb),
        in_specs=[row_in, wspec(tabs["w1_re"]), wspec(tabs["w1_im"])],
        out_specs=[colblk, colblk],
        out_shape=[jax.ShapeDtypeStruct((rows, gp), F32)] * 2,
        compiler_params=_cparams("arbitrary", "arbitrary"),
        name="s5_state_in",
    )(xq, tabs["w1_re"], tabs["w1_im"])
    parts = SUBLANES // batch
    wl = gp // parts
    slab = lambda a: a.reshape(batch, nch, gp).transpose(1, 0, 2).reshape(nch, SUBLANES, wl)
    unslab = lambda a: a.reshape(nch, batch, gp).transpose(1, 0, 2).reshape(rows, gp)
    coef = lambda a: jnp.tile(a.reshape(parts, wl), (batch, 1))
    steps = min(64, nch)
    sblk = pl.BlockSpec((steps, SUBLANES, wl), lambda i: (i, 0, 0))
    cblk = pl.BlockSpec((SUBLANES, wl), lambda i: (0, 0))
    hr, hi = pl.pallas_call(
        functools.partial(_s5_scan_kernel, steps=steps),
        grid=(nch // steps,),
        in_specs=[sblk, sblk, cblk, cblk],
        out_specs=[sblk, sblk],
        out_shape=[jax.ShapeDtypeStruct((nch, SUBLANES, wl), F32)] * 2,
        scratch_shapes=[pltpu.VMEM((SUBLANES, wl), F32)] * 2,
        compiler_params=_cparams("arbitrary"),
        name="s5_scan",
    )(slab(xr), slab(xi), coef(tabs["a_re"]), coef(tabs["a_im"]))
    y = pl.pallas_call(
        _s5_out_kernel,
        grid=(S5_TILES, rows // rb),
        in_specs=[row_in, wspec(tabs["toe"]), colblk, colblk, wspec(tabs["w2_re"]), wspec(tabs["w2_im"])],
        out_specs=row_in,
        out_shape=jax.ShapeDtypeStruct((S5_TILES, rows, S5_ROW_W), F32),
        compiler_params=_cparams("arbitrary", "arbitrary"),
        name="s5_out",
    )(xq, tabs["toe"], unslab(hr), unslab(hi), tabs["w2_re"], tabs["w2_im"])
    t = batch * seq
    tm = min(tm, t)
    tok = lambda w: pl.BlockSpec((tm, w), lambda i: (i, 0))
    vec = pl.BlockSpec((1, S5_WIDTH), lambda i: (0, 0))
    return pl.pallas_call(
        _s5_post_kernel,
        grid=(t // tm,),
        in_specs=[pl.BlockSpec((S5_TILES, tm // s, S5_ROW_W), lambda i: (0, i, 0)), tok(LANES), tok(LANES),
                  tok(LANES), vec, pl.BlockSpec((S5_WIDTH, S5_WIDTH), lambda i: (0, 0)), vec, vec],
        out_specs=tok(S5_WIDTH),
        out_shape=jax.ShapeDtypeStruct((t, S5_WIDTH), BF16),
        scratch_shapes=[pltpu.VMEM((S5_TILES, tm, LANES), F32)],
        compiler_params=_cparams("arbitrary"),
        name="s5_post",
    )(y, us[0], us[1], us[2], d.reshape(1, -1), glu_w.astype(BF16), glu_b.reshape(1, -1),
      norm_g.reshape(1, -1))


def _mix_out_kernel(og_ref, od_ref, os_ref, wg_ref, wd_ref, ws_ref, x_ref, gt_ref, n2_ref, sc_ref,
                    sh_ref, rw1_ref, rw2_ref, rb_ref, x1_ref, h2_ref, route_ref, cnt_ref, carry_ref, *, tm):
    @pl.when(pl.program_id(0) == 0)
    def _():
        carry_ref[...] = jnp.zeros_like(carry_ref)

    mix = jnp.dot(og_ref[...], wg_ref[...], preferred_element_type=F32)
    mix = mix + jnp.dot(od_ref[...], wd_ref[...], preferred_element_type=F32)
    mix = mix + jnp.dot(os_ref[...], ws_ref[...], preferred_element_type=F32)
    x1 = x_ref[...] + gt_ref[...] * mix
    x1_ref[...] = x1
    h2 = _rms(x1, n2_ref[...]) * (1.0 + sc_ref[...]) + sh_ref[...]
    h2_ref[...] = h2
    ha = h2.astype(BF16)
    hb = (h2 - ha.astype(F32)).astype(BF16)
    logit = (jnp.dot(ha, rw1_ref[...], preferred_element_type=F32)
             + jnp.dot(ha, rw2_ref[...], preferred_element_type=F32)
             + jnp.dot(hb, rw1_ref[...], preferred_element_type=F32)) + rb_ref[...]
    lane = lax.broadcasted_iota(I32, (tm, LANES), 1)
    lanef = lane.astype(F32)
    ninf = jnp.float32(-jnp.inf)
    big = jnp.float32(LANES)
    gl = jnp.where(lane < MOE_GROUPS, logit, ninf)
    gmax = jnp.max(gl, axis=1, keepdims=True)
    grp_p = 1.0 / jnp.sum(jnp.exp(gl - gmax), axis=1, keepdims=True)
    grp_i = jnp.min(jnp.where(gl == gmax, lanef, big), axis=1, keepdims=True)
    lo = MOE_GROUPS + MOE_EXPERTS_PER_GROUP * grp_i
    el = jnp.where((lanef >= lo) & (lanef < lo + MOE_EXPERTS_PER_GROUP), logit, ninf)
    l1 = jnp.max(el, axis=1, keepdims=True)
    i1 = jnp.min(jnp.where(el == l1, lanef, big), axis=1, keepdims=True)
    el2 = jnp.where(lanef == i1, ninf, el)
    l2 = jnp.max(el2, axis=1, keepdims=True)
    i2 = jnp.min(jnp.where(el2 == l2, lanef, big), axis=1, keepdims=True)
    e21 = jnp.exp(l2 - l1)
    w1 = grp_p / (1.0 + e21)
    w2 = grp_p * e21 / (1.0 + e21)
    e1 = i1 - MOE_GROUPS
    e2 = i2 - MOE_GROUPS
    oh1 = lanef == e1
    oh2 = lanef == e2
    hot = jnp.where(oh1 | oh2, 1.0, 0.0)
    ri = lax.broadcasted_iota(I32, (tm, tm), 0)
    ci = lax.broadcasted_iota(I32, (tm, tm), 1)
    before = (ci < ri).astype(BF16)
    prior = carry_ref[...] + jnp.dot(before, hot.astype(BF16), preferred_element_type=F32)
    r1 = jnp.sum(jnp.where(oh1, prior, 0.0), axis=1, keepdims=True)
    r2 = jnp.sum(jnp.where(oh2, prior, 0.0), axis=1, keepdims=True)
    total = carry_ref[...] + jnp.sum(hot, axis=0, keepdims=True)
    carry_ref[...] = total
    cnt_ref[...] = jnp.broadcast_to(total, cnt_ref.shape)
    route = jnp.where(lane == 0, e1, jnp.where(lane == 1, e2, jnp.where(lane == 2, w1, jnp.where(
        lane == 3, w2, jnp.where(lane == 4, r1, jnp.where(lane == 5, r2, 0.0))))))
    route_ref[...] = route


def _mix_out(o_gla, o_dsa, o_s5, w_out, x2, gt1, norm2_g, sc2, sh2, rgw, rgb, rew, reb, seq, tm=512):
    t, d = x2.shape
    per_b = seq // tm
    wg = jnp.pad(w_out[:GLA_HEADS * GLA_DV].reshape(GLA_HEADS, GLA_DV, d),
                 ((0, 0), (0, HEAD_PAD - GLA_DV), (0, 0))).reshape(GLA_W, d).astype(BF16)
    wd = w_out[GLA_HEADS * GLA_DV:GLA_HEADS * GLA_DV + DSA_WIDTH].astype(BF16)
    ws = w_out[GLA_HEADS * GLA_DV + DSA_WIDTH:].astype(BF16)
    rw = _pad_cols(jnp.concatenate([rgw, rew], axis=1), LANES)
    rb = _pad_cols(jnp.concatenate([rgb, reb]).reshape(1, -1), LANES)
    rw1 = rw.astype(BF16)
    rw2 = (rw - rw1.astype(F32)).astype(BF16)
    tok = lambda w: pl.BlockSpec((tm, w), lambda i: (i, 0))
    full = lambda a: pl.BlockSpec(a.shape, lambda i: (0, 0))
    perb = pl.BlockSpec((None, 1, d), lambda i: (i // per_b, 0, 0))
    vec = pl.BlockSpec((1, d), lambda i: (0, 0))
    return pl.pallas_call(
        functools.partial(_mix_out_kernel, tm=tm),
        grid=(t // tm,),
        in_specs=[tok(GLA_W), tok(DSA_WIDTH), tok(S5_WIDTH), full(wg), full(wd), full(ws), tok(d),
                  perb, vec, perb, perb, full(rw1), full(rw2), full(rb)],
        out_specs=[tok(d), tok(d), tok(LANES), pl.BlockSpec((SUBLANES, LANES), lambda i: (0, 0))],
        out_shape=[jax.ShapeDtypeStruct((t, d), F32), jax.ShapeDtypeStruct((t, d), F32),
                   jax.ShapeDtypeStruct((t, LANES), F32), jax.ShapeDtypeStruct((SUBLANES, LANES), F32)],
        scratch_shapes=[pltpu.VMEM((1, LANES), F32)],
        compiler_params=_cparams("arbitrary"),
        name="mix_out",
    )(o_gla, o_dsa, o_s5, wg, wd, ws, x2, gt1[:, None, :], norm2_g.reshape(1, d), sc2[:, None, :],
      sh2[:, None, :], rw1, rw2, rb)


def _row(ref, r):
    return ref.at[pl.ds(r, 1), :]


def _dispatch_kernel(dest_ref, h_ref, zero_ref, out_ref, sem, *, td):
    del zero_ref

    def start(g, c):
        for j in range(SUBLANES):
            r = g * SUBLANES + j
            pltpu.make_async_copy(_row(h_ref, r), _row(out_ref, dest_ref[0, r]), sem).start()
            pltpu.make_async_copy(_row(h_ref, r), _row(out_ref, dest_ref[1, r]), sem).start()
        return c

    lax.fori_loop(0, td // SUBLANES, start, 0)
    for _ in range(2):
        pltpu.make_async_copy(h_ref, out_ref.at[pl.ds(0, td), :], sem).wait()


def _dispatch(h2, dest, n_rows, td=256):
    t, d = h2.shape
    dest3 = dest.reshape(2, t // td, td).transpose(1, 0, 2)
    zeros = jnp.zeros((n_rows, d), F32)
    return pl.pallas_call(
        functools.partial(_dispatch_kernel, td=td),
        grid=(t // td,),
        in_specs=[pl.BlockSpec((None, 2, td), lambda i: (i, 0, 0), memory_space=pltpu.SMEM),
                  pl.BlockSpec((td, d), lambda i: (i, 0)),
                  pl.BlockSpec(memory_space=pl.ANY)],
        out_specs=pl.BlockSpec(memory_space=pl.ANY),
        out_shape=jax.ShapeDtypeStruct((n_rows, d), F32),
        scratch_shapes=[pltpu.SemaphoreType.DMA(())],
        input_output_aliases={2: 0},
        compiler_params=_cparams("arbitrary"),
        name="moe_dispatch",
    )(dest3, h2, zeros)


def _expert_kernel(be_ref, nb_ref, x_ref, wg_ref, wu_ref, wd_ref, o_ref):
    del be_ref
    i = pl.program_id(0)

    @pl.when(i < nb_ref[0])
    def _():
        xb = x_ref[...].astype(BF16)
        g = jnp.dot(xb, wg_ref[...].astype(BF16), preferred_element_type=F32)
        u = jnp.dot(xb, wu_ref[...].astype(BF16), preferred_element_type=F32)
        hid = (g * _sigmoid(g) * u).astype(BF16)
        o_ref[...] = jnp.dot(hid, wd_ref[...].astype(BF16), preferred_element_type=F32)

    @pl.when(i >= nb_ref[0])
    def _():
        o_ref[...] = jnp.zeros_like(o_ref)


def _experts(xs, blk_e, n_used, wg, wu, wd, layer):
    n_rows = xs.shape[0]
    nb = n_rows // MOE_BM
    d, hdim = wg.shape[-2:]
    rows = pl.BlockSpec((MOE_BM, d), lambda i, be, nu: (i, 0))
    wspec = lambda a, b: pl.BlockSpec((None, None, a, b), lambda i, be, nu: (layer, be[i], 0, 0))
    return pl.pallas_call(
        _expert_kernel,
        grid_spec=pltpu.PrefetchScalarGridSpec(
            num_scalar_prefetch=2,
            grid=(nb,),
            in_specs=[rows, wspec(d, hdim), wspec(d, hdim), wspec(hdim, d)],
            out_specs=rows,
        ),
        out_shape=jax.ShapeDtypeStruct((n_rows, d), F32),
        compiler_params=_cparams("arbitrary"),
        name="moe_experts",
    )(blk_e, n_used, xs, wg, wu, wd)


def _combine_kernel(dest_ref, x_ref, route_ref, gt_ref, fg_ref, eo_ref, o_ref, rows_ref, sem, *, tc, final):
    def start(g, c):
        for j in range(SUBLANES):
            r = g * SUBLANES + j
            pltpu.make_async_copy(_row(eo_ref, dest_ref[0, r]), _row(rows_ref.at[0], r), sem).start()
            pltpu.make_async_copy(_row(eo_ref, dest_ref[1, r]), _row(rows_ref.at[1], r), sem).start()
        return c

    lax.fori_loop(0, tc // SUBLANES, start, 0)
    for which in range(2):
        pltpu.make_async_copy(eo_ref.at[pl.ds(0, tc), :], rows_ref.at[which], sem).wait()
    route = route_ref[...]
    lane = lax.broadcasted_iota(I32, route.shape, 1)
    w1 = jnp.sum(jnp.where(lane == 2, route, 0.0), axis=1, keepdims=True)
    w2 = jnp.sum(jnp.where(lane == 3, route, 0.0), axis=1, keepdims=True)
    y = rows_ref[0] * w1 + rows_ref[1] * w2
    x = x_ref[...] + gt_ref[...] * y
    if final:
        x = _rms(x, fg_ref[...])
    o_ref[...] = x


def _combine(x1, route, dest, eo, gt2, final_g, seq, final, tc=256):
    t, d = x1.shape
    per_b = seq // tc
    dest3 = dest.reshape(2, t // tc, tc).transpose(1, 0, 2)
    return pl.pallas_call(
        functools.partial(_combine_kernel, tc=tc, final=final),
        grid=(t // tc,),
        in_specs=[pl.BlockSpec((None, 2, tc), lambda i: (i, 0, 0), memory_space=pltpu.SMEM),
                  pl.BlockSpec((tc, d), lambda i: (i, 0)),
                  pl.BlockSpec((tc, LANES), lambda i: (i, 0)),
                  pl.BlockSpec((None, 1, d), lambda i: (i // per_b, 0, 0)),
                  pl.BlockSpec((1, d), lambda i: (0, 0)),
                  pl.BlockSpec(memory_space=pl.ANY)],
        out_specs=pl.BlockSpec((tc, d), lambda i: (i, 0)),
        out_shape=jax.ShapeDtypeStruct((t, d), F32),
        scratch_shapes=[pltpu.VMEM((2, tc, d), F32), pltpu.SemaphoreType.DMA(())],
        compiler_params=_cparams("arbitrary"),
        name="moe_combine",
    )(dest3, x1, route, gt2[:, None, :], final_g.reshape(1, d), eo)


def _moe_plan(route, counts, t):
    e = route[:, 0:2].astype(I32).T
    rank = route[:, 4:6].astype(I32).T
    cnt = counts[0, :MOE_EXPERTS].astype(I32)
    padded = (cnt + MOE_BM - 1) // MOE_BM * MOE_BM
    pend = jnp.cumsum(padded)
    pstart = pend - padded
    ids = jnp.arange(MOE_EXPERTS, dtype=I32)
    dest = rank + jnp.sum(jnp.where(e[..., None] == ids, pstart, 0), axis=-1)
    nb = (2 * t) // MOE_BM + MOE_EXPERTS
    n_used = (pend[-1] // MOE_BM).astype(I32)
    first_row = jnp.minimum(jnp.arange(nb, dtype=I32), n_used - 1) * MOE_BM
    blk_e = jnp.sum((pend[None, :] <= first_row[:, None]).astype(I32), axis=1)
    blk_e = jnp.minimum(blk_e, MOE_EXPERTS - 1)
    return dest, blk_e, n_used.reshape(1), nb * MOE_BM


def kernel(x, c, ada_w, ada_b, norm1_g, w_in, gla_gate_w, gla_gate_b, gla_norm_g, dsa_norm_g, s5_a_re, s5_a_im, s5_log_step, s5_b_re, s5_b_im, s5_c_re, s5_c_im, s5_d, s5_glu_w, s5_glu_b, s5_norm_g, w_out, norm2_g, router_grp_w, router_grp_b, router_exp_w, router_exp_b, exp_w_gate, exp_w_up, exp_w_down, final_norm_g):
    batch, seq, d = x.shape
    depth = ada_w.shape[0]
    t = batch * seq
    mod = _ada_mod(c, ada_w, ada_b)
    x2 = x.reshape(t, d)
    for l in range(depth):
        sh1, sc1, gt1, sh2, sc2, gt2 = (mod[l, j] for j in range(6))
        p = _in_proj(x2, norm1_g[l], sc1, sh1, _layout_w_in(w_in[l]), seq)
        o_gla = _gla(p, gla_gate_w[l], gla_gate_b[l], gla_norm_g[l], batch, seq)
        o_dsa = _dsa(p, dsa_norm_g[l], batch, seq)
        tabs = _s5_tables(s5_a_re[l], s5_a_im[l], s5_log_step[l], s5_b_re[l], s5_b_im[l],
                          s5_c_re[l], s5_c_im[l])
        o_s5 = _s5((p["su0"], p["su1"], p["su2"]), p["sx"], tabs, s5_d[l], s5_glu_w[l], s5_glu_b[l], s5_norm_g[l], batch, seq)
        x1, h2, route, counts = _mix_out(o_gla, o_dsa, o_s5, w_out[l], x2, gt1, norm2_g[l], sc2, sh2,
                                         router_grp_w[l], router_grp_b[l], router_exp_w[l],
                                         router_exp_b[l], seq)
        dest, blk_e, n_used, n_rows = _moe_plan(route, counts, t)
        xs = _dispatch(h2, dest, n_rows)
        eo = _experts(xs, blk_e, n_used, exp_w_gate, exp_w_up, exp_w_down, l)
        x2 = _combine(x1, route, dest, eo, gt2, final_norm_g, seq, final=(l == depth - 1))
    return x2.reshape(batch, seq, d)
```

```python
import functools
import math

import jax
import jax.numpy as jnp
import numpy as np
from jax import lax
from jax.experimental import pallas as pl
from jax.experimental.pallas import tpu as pltpu

F32 = jnp.float32
BF16 = jnp.bfloat16
I32 = jnp.int32

D_MODEL = 1024
GLA_HEADS = 4
GLA_DV = 96
GLA_DK = 48
GLA_RANK = 16
GLA_TAU = 16.0
GLA_CHUNK = 64
DSA_HEADS = 4
DSA_HEAD_DIM = 64
DSA_WIDTH = DSA_HEADS * DSA_HEAD_DIM
DSA_IDX_HEADS = 8
DSA_IDX_DIM = 32
DSA_TOPK = 256
S5_GROUPS = 24
S5_GROUP_CH = 16
S5_STATE = 64
S5_WIDTH = S5_GROUPS * S5_GROUP_CH
MOE_GROUPS = 4
MOE_EXPERTS_PER_GROUP = 8
MOE_EXPERTS = MOE_GROUPS * MOE_EXPERTS_PER_GROUP
MOE_HIDDEN = 512
RMS_EPS = 1e-6
IN_SIZES = (192, 192, 384, 16, 384, 256, 256, 256, 256, 32, 8, 384)

LANES = 128
SUBLANES = 8
VMEM_LIMIT = 56 * 1024 * 1024

HEAD_PAD = LANES
GLA_W = GLA_HEADS * HEAD_PAD
S5_CHUNK = 16
S5_TILE_G = LANES // S5_GROUP_CH
S5_TILES = S5_GROUPS // S5_TILE_G
S5_ROW_W = S5_CHUNK * LANES
S5_STATE_W = S5_TILE_G * S5_STATE
MOE_BM = 512
INT_MIN = -2 ** 31


def _cparams(*sem):
    return pltpu.CompilerParams(dimension_semantics=tuple(sem), vmem_limit_bytes=VMEM_LIMIT)


def _sigmoid(x):
    return 0.5 * (jnp.tanh(0.5 * x) + 1.0)


def _rms(x, g, n=None):
    n = x.shape[-1] if n is None else n
    ms = jnp.sum(x * x, axis=-1, keepdims=True) * (1.0 / n)
    return x * lax.rsqrt(ms + RMS_EPS) * g


def _ada_kernel(c_ref, w_ref, b_ref, o_ref):
    c = c_ref[...]
    cond = (c * _sigmoid(c)).astype(BF16)
    o_ref[...] = jnp.dot(cond, w_ref[...].astype(BF16), preferred_element_type=F32) + b_ref[...]


def _ada_mod(c, ada_w, ada_b):
    depth, d, d6 = ada_w.shape
    nblk = d6 // d
    bp = -(-c.shape[0] // SUBLANES) * SUBLANES
    cp = jnp.pad(c, ((0, bp - c.shape[0]), (0, 0)))
    out = pl.pallas_call(
        _ada_kernel,
        grid=(depth, nblk),
        in_specs=[
            pl.BlockSpec((bp, d), lambda l, j: (0, 0)),
            pl.BlockSpec((None, d, d), lambda l, j: (l, 0, j)),
            pl.BlockSpec((None, None, 1, d), lambda l, j: (l, j, 0, 0)),
        ],
        out_specs=pl.BlockSpec((None, None, bp, d), lambda l, j: (l, j, 0, 0)),
        out_shape=jax.ShapeDtypeStruct((depth, nblk, bp, d), F32),
        compiler_params=_cparams("arbitrary", "arbitrary"),
        name="ada_mod",
    )(cp, ada_w, ada_b.reshape(depth, nblk, 1, d))
    return out[:, :, : c.shape[0], :]


PROJ_OUTS = (
    ("gq", GLA_W, BF16), ("gk", GLA_W, BF16), ("gv", GLA_W, BF16), ("gr", GLA_W, BF16),
    ("glr", LANES, BF16),
    ("dq", DSA_WIDTH, BF16), ("dk", DSA_WIDTH, BF16), ("dv", DSA_WIDTH, BF16),
    ("iq", DSA_IDX_HEADS * DSA_IDX_DIM, BF16), ("ik", DSA_IDX_HEADS * DSA_IDX_DIM, BF16),
    ("iw", LANES, F32), ("su0", LANES, F32), ("su1", LANES, F32), ("su2", LANES, F32),
)
PROJ_COLS = sum(w for _, w, _ in PROJ_OUTS)


def _pad_heads(w, heads, dim):
    d = w.shape[0]
    w = w.reshape(d, heads, dim)
    return jnp.pad(w, ((0, 0), (0, 0), (0, HEAD_PAD - dim))).reshape(d, heads * HEAD_PAD)


def _pad_cols(w, width):
    return jnp.pad(w, ((0, 0), (0, width - w.shape[1])))


def _layout_w_in(w_in):
    offs = np.cumsum((0,) + IN_SIZES)
    p = [w_in[:, offs[i]:offs[i + 1]] for i in range(len(IN_SIZES))]
    cols = [
        _pad_heads(p[0], GLA_HEADS, GLA_DK), _pad_heads(p[1], GLA_HEADS, GLA_DK),
        _pad_heads(p[2], GLA_HEADS, GLA_DV), _pad_heads(p[4], GLA_HEADS, GLA_DV),
        _pad_cols(p[3], LANES),
        p[5], p[6], p[7], p[8], jnp.tile(p[9], (1, DSA_IDX_HEADS)),
        _pad_cols(p[10], LANES), p[11],
    ]
    return jnp.concatenate(cols, axis=1).astype(BF16)


def _proj_kernel(x_ref, g_ref, sc_ref, sh_ref, w_ref, *o_refs):
    x = x_ref[...]
    h = _rms(x, g_ref[...]) * (1.0 + sc_ref[...]) + sh_ref[...]
    hb = h.astype(BF16)
    c0 = 0
    for o_ref, (_, width, _) in zip(o_refs, PROJ_OUTS):
        o_ref[...] = jnp.dot(hb, w_ref[:, c0:c0 + width], preferred_element_type=F32).astype(o_ref.dtype)
        c0 += width
    sx_ref = o_refs[len(PROJ_OUTS)]
    n_chunks = x.shape[0] // S5_CHUNK
    for q in range(S5_TILES):
        su_ref = o_refs[len(PROJ_OUTS) - S5_TILES + q]
        for s in range(S5_CHUNK):
            sx_ref[q, :, s * LANES:(s + 1) * LANES] = su_ref[pl.ds(s, n_chunks, stride=S5_CHUNK), :]


def _in_proj(x2, norm_g, sc, sh, w_lay, seq, tm=512):
    t, d = x2.shape
    per_b = seq // tm
    outs = pl.pallas_call(
        _proj_kernel,
        grid=(t // tm,),
        in_specs=[
            pl.BlockSpec((tm, d), lambda i: (i, 0)),
            pl.BlockSpec((1, d), lambda i: (0, 0)),
            pl.BlockSpec((None, 1, d), lambda i: (i // per_b, 0, 0)),
            pl.BlockSpec((None, 1, d), lambda i: (i // per_b, 0, 0)),
            pl.BlockSpec((d, PROJ_COLS), lambda i: (0, 0)),
        ],
        out_specs=[pl.BlockSpec((tm, w), lambda i: (i, 0)) for _, w, _ in PROJ_OUTS]
        + [pl.BlockSpec((S5_TILES, tm // S5_CHUNK, S5_ROW_W), lambda i: (0, i, 0))],
        out_shape=[jax.ShapeDtypeStruct((t, w), dt) for _, w, dt in PROJ_OUTS]
        + [jax.ShapeDtypeStruct((S5_TILES, t // S5_CHUNK, S5_ROW_W), F32)],
        compiler_params=_cparams("arbitrary"),
        name="in_proj",
    )(x2, norm_g.reshape(1, d), sc[:, None, :], sh[:, None, :], w_lay)
    p = {name: o for (name, _, _), o in zip(PROJ_OUTS, outs)}
    p["sx"] = outs[-1]
    return p


def _gla_kernel(q_ref, k_ref, v_ref, r_ref, lr_ref, gw_ref, gb_ref, ng_ref, o_ref, s_ref, *, rows):
    @pl.when(pl.program_id(1) == 0)
    def _():
        s_ref[...] = jnp.zeros_like(s_ref)

    c = GLA_CHUNK
    gl = jnp.dot(lr_ref[...], gw_ref[...], preferred_element_type=F32) + gb_ref[...]
    g = (jnp.minimum(gl, 0.0) - jnp.log1p(jnp.exp(-jnp.abs(gl)))) * (1.0 / GLA_TAU)
    g1 = g.astype(BF16)
    rem = g - g1.astype(F32)
    g2 = rem.astype(BF16)
    g3 = (rem - g2.astype(F32)).astype(BF16)
    ri = lax.broadcasted_iota(I32, (c, c), 0)
    ci = lax.broadcasted_iota(I32, (c, c), 1)
    causal = ri >= ci
    tri = causal.astype(BF16)
    ng = ng_ref[...]
    for ch in range(rows // c):
        rs = slice(ch * c, (ch + 1) * c)
        bcum = (jnp.dot(tri, g1[rs], preferred_element_type=F32)
                + jnp.dot(tri, g2[rs], preferred_element_type=F32)
                + jnp.dot(tri, g3[rs], preferred_element_type=F32))
        blast = bcum[c - 1:c, :]
        e_pos = jnp.exp(bcum)
        kf = k_ref[rs, :].astype(F32)
        qd = (q_ref[rs, :].astype(F32) * (GLA_DK ** -0.5) * e_pos).astype(BF16)
        kd = (kf * jnp.exp(-bcum)).astype(BF16)
        ku = (kf * jnp.exp(blast - bcum)).astype(BF16)
        dec = jnp.exp(blast)
        for h in range(GLA_HEADS):
            hs = slice(h * HEAD_PAD, (h + 1) * HEAD_PAD)
            vh = v_ref[rs, hs]
            att = lax.dot_general(qd[:, hs], kd[:, hs], (((1,), (1,)), ((), ())),
                                  preferred_element_type=F32)
            att = jnp.where(causal, att, 0.0).astype(BF16)
            st = s_ref[h]
            o = jnp.dot(att, vh, preferred_element_type=F32)
            o = o + lax.dot_general(qd[:, hs], st.astype(BF16), (((1,), (1,)), ((), ())),
                                    preferred_element_type=F32)
            upd = lax.dot_general(vh, ku[:, hs], (((0,), (0,)), ((), ())),
                                  preferred_element_type=F32)
            s_ref[h] = st * dec[:, hs] + upd
            y = _rms(o, ng[:, hs], GLA_DV)
            rr = r_ref[rs, hs].astype(F32)
            o_ref[rs, hs] = (y * (rr * _sigmoid(rr))).astype(o_ref.dtype)


def _gla(p, gate_w, gate_b, norm_g, batch, seq, rows=512):
    gw = _pad_cols(jnp.pad(_pad_heads(gate_w, GLA_HEADS, GLA_DK), ((0, LANES - GLA_RANK), (0, 0))), GLA_W)
    gb = _pad_heads(gate_b.reshape(1, -1), GLA_HEADS, GLA_DK)
    ng = _pad_heads(jnp.tile(norm_g.reshape(1, GLA_DV), (1, GLA_HEADS)), GLA_HEADS, GLA_DV)
    rows = min(rows, seq)
    blk = lambda w: pl.BlockSpec((None, rows, w), lambda b, j: (b, j, 0))
    full = lambda a: pl.BlockSpec(a.shape, lambda b, j: (0, 0))
    r3 = lambda a: a.reshape(batch, seq, a.shape[-1])
    out = pl.pallas_call(
        functools.partial(_gla_kernel, rows=rows),
        grid=(batch, seq // rows),
        in_specs=[blk(GLA_W), blk(GLA_W), blk(GLA_W), blk(GLA_W), blk(LANES),
                  full(gw), full(gb), full(ng)],
        out_specs=blk(GLA_W),
        out_shape=jax.ShapeDtypeStruct((batch, seq, GLA_W), BF16),
        scratch_shapes=[pltpu.VMEM((GLA_HEADS, HEAD_PAD, HEAD_PAD), F32)],
        compiler_params=_cparams("arbitrary", "arbitrary"),
        name="gla",
    )(r3(p["gq"]), r3(p["gk"]), r3(p["gv"]), r3(p["gr"]), r3(p["glr"]), gw.astype(BF16), gb, ng)
    return out.reshape(batch * seq, GLA_W)


def _dsa_kernel(iq_ref, ik_ref, iw_ref, q_ref, k_ref, v_ref, ng_ref, o_ref,
                qs_ref, q4_ref, keys_ref, acc_ref, run_ref, top_ref, sa_ref, sb_ref, kmax_ref,
                *, tq, kc, sub, n_sel, top_r):
    assert kc == 2 * sub
    qi = pl.program_id(1)
    nkc = ((qi + 1) * tq + kc - 1) // kc
    lane_w = lax.broadcasted_iota(I32, (tq, DSA_WIDTH), 1)
    iq = iq_ref[...]
    for h in range(DSA_IDX_HEADS):
        m = (lane_w >= h * DSA_IDX_DIM) & (lane_w < (h + 1) * DSA_IDX_DIM)
        qs_ref[h * tq:(h + 1) * tq, :] = jnp.where(m, iq, jnp.zeros_like(iq))
    iw = iw_ref[...] * (DSA_IDX_HEADS ** -0.5 * DSA_IDX_DIM ** -0.5)
    lane = lax.broadcasted_iota(I32, (tq, LANES), 1)
    w_cols = [iw[:, h:h + 1] for h in range(DSA_IDX_HEADS)]
    qpos = qi * tq + lax.broadcasted_iota(I32, (tq, sub), 0)
    kcol = lax.broadcasted_iota(I32, (tq, sub), 1)
    nt = (((1,), (1,)), ((), ()))

    ninf = jnp.float32(-jnp.inf)

    def score_body(c, carry):
        for u in range(kc // sub):
            k0 = pl.multiple_of(c * kc + u * sub, sub)
            lg = lax.dot_general(qs_ref[...], ik_ref[pl.ds(k0, sub), :], nt,
                                 preferred_element_type=F32)
            sc = jnp.zeros((tq, sub), F32)
            for h in range(DSA_IDX_HEADS):
                sc = sc + w_cols[h] * jnp.maximum(lg[h * tq:(h + 1) * tq, :], 0.0)
            sc = sc + 0.0
            keys_ref[c, :, u * sub:(u + 1) * sub] = jnp.where(kcol + k0 <= qpos, sc, ninf)
        return carry

    lax.fori_loop(0, nkc, score_body, 0)

    def to_key(x):
        bits = lax.bitcast_convert_type(x, I32)
        key = jnp.where(bits < 0, bits ^ jnp.int32(0x7FFFFFFF), bits)
        return jnp.where(x == ninf, jnp.int32(INT_MIN), key)

    def count_ge(cand):
        candb = jnp.broadcast_to(cand, (tq, LANES))

        def body(c, acc):
            blk = keys_ref[c]
            for j in range(kc // LANES):
                acc = acc + jnp.where(to_key(blk[:, j * LANES:(j + 1) * LANES]) >= candb, 1.0, 0.0)
            return acc

        acc = lax.fori_loop(0, nkc, body, jnp.zeros((tq, LANES), F32))
        return jnp.sum(acc, axis=1, keepdims=True)

    need = jnp.float32(n_sel)

    def select(count):
        t0 = jnp.full((tq, 1), INT_MIN, I32)
        zero = jnp.zeros((tq, 1), I32)
        t0 = jnp.where(count(zero) >= need, zero, t0)

        def bit_body(i, t):
            cand = t | (jnp.int32(1) << (30 - i))
            return jnp.where(count(cand) >= need, cand, t)

        t = lax.fori_loop(0, 31, bit_body, t0)
        t = jnp.maximum(t, jnp.int32(INT_MIN + 1))
        return t, count(t + 1), count(t)

    lowest = jnp.full((SUBLANES, LANES), ninf, F32)
    for rt in range(0, tq // SUBLANES, 2):
        rsl = [slice((rt + a) * SUBLANES, (rt + a + 1) * SUBLANES) for a in range(2)]

        def top_body(c, ts, rsl=rsl):
            ts = [list(t) for t in ts]
            for j in range(kc // LANES):
                for a in range(2):
                    x = keys_ref[c, rsl[a], j * LANES:(j + 1) * LANES]
                    for r in range(top_r):
                        hi = jnp.maximum(ts[a][r], x)
                        x = jnp.minimum(ts[a][r], x)
                        ts[a][r] = hi
            return tuple(tuple(t) for t in ts)

        ts = lax.fori_loop(0, nkc, top_body, ((lowest,) * top_r,) * 2)
        for a in range(2):
            for r in range(top_r):
                top_ref[r, rsl[a], :] = to_key(ts[a][r])

    def count_cand(cand):
        candb = jnp.broadcast_to(cand, (tq, LANES))
        acc = jnp.zeros((tq, LANES), F32)
        for r in range(top_r):
            acc = acc + jnp.where(top_ref[r] >= candb, 1.0, 0.0)
        return jnp.sum(acc, axis=1, keepdims=True)

    thr_c, n_gt_c, n_ge_c = select(count_cand)
    hidden = jnp.max(jnp.where(top_ref[top_r - 1] >= thr_c, 1.0, 0.0)) > 0.0
    thr, n_gt, n_ge = lax.cond(hidden, lambda: select(count_ge), lambda: (thr_c, n_gt_c, n_ge_c))
    n_tie = need - n_gt
    any_tie = jnp.max(n_ge) > need
    thr_bits = jnp.where(thr < 0, thr ^ jnp.int32(0x7FFFFFFF), thr)
    thr_f = jnp.where(thr == jnp.int32(INT_MIN + 1), jnp.float32(jnp.finfo(jnp.float32).min),
                      lax.bitcast_convert_type(thr_bits, F32))

    ui = lax.broadcasted_iota(I32, (kc, kc), 0)
    uj = lax.broadcasted_iota(I32, (kc, kc), 1)
    before = (ui < uj).astype(BF16)
    q = q_ref[...] * jnp.asarray(DSA_HEAD_DIM ** -0.5, BF16)
    for h in range(DSA_HEADS):
        m = (lane_w >= h * DSA_HEAD_DIM) & (lane_w < (h + 1) * DSA_HEAD_DIM)
        q4_ref[h * tq:(h + 1) * tq, :] = jnp.where(m, q, jnp.zeros_like(q))
    neg = jnp.float32(-1e30)

    def lane_fold(op, acc, s):
        for j in range(s.shape[1] // LANES):
            acc = op(acc, s[:, j * LANES:(j + 1) * LANES])
        return acc

    @pl.when(qi == 0)
    def _():
        kmax_ref[...] = jnp.zeros_like(kmax_ref)

    hl = lax.broadcasted_iota(I32, (DSA_WIDTH, LANES), 0) // DSA_HEAD_DIM
    head_of = (hl == lax.broadcasted_iota(I32, (DSA_WIDTH, LANES), 1)).astype(BF16)
    kb = k_ref[pl.ds(pl.multiple_of(qi * tq, tq), tq), :].astype(F32)
    kn2 = jnp.dot((kb * kb).astype(BF16), head_of, preferred_element_type=F32)
    kmax = jnp.maximum(kmax_ref[0:1, :], jnp.max(kn2, axis=0, keepdims=True))
    kmax_ref[...] = jnp.broadcast_to(kmax, kmax_ref.shape)
    qf = q.astype(F32)
    qn2 = jnp.dot((qf * qf).astype(BF16), head_of, preferred_element_type=F32)
    bound = jnp.sqrt(qn2 * kmax) * 1.02
    ms_bound = [jnp.sum(jnp.where(lane == h, bound, 0.0), axis=1, keepdims=True) for h in range(DSA_HEADS)]

    def tie_bias_pass():
        def body(c, seen):
            sc = keys_ref[c]
            eq = sc == thr_f
            eqf = jnp.where(eq, 1.0, 0.0)
            rank = seen + jnp.dot(eqf.astype(BF16), before, preferred_element_type=F32)
            sel = (sc > thr_f) | (eq & (rank < n_tie))
            keys_ref[c] = jnp.where(sel, 0.0, neg)
            return seen + jnp.sum(eqf, axis=1, keepdims=True)

        lax.fori_loop(0, nkc, body, jnp.zeros((tq, 1), F32))

    def bias_of(c, stored):
        sc = keys_ref[c]
        return sc if stored else jnp.where(sc >= thr_f, 0.0, neg)

    def qk(k0):
        return lax.dot_general(q4_ref[...], k_ref[pl.ds(k0, sub), :], nt, preferred_element_type=F32)

    def max_pass(stored):
        run_ref[...] = jnp.full(run_ref.shape, neg, F32)

        def body(c, carry):
            bias = bias_of(c, stored)
            for u in range(kc // sub):
                s = qk(pl.multiple_of(c * kc + u * sub, sub))
                for h in range(DSA_HEADS):
                    run_ref[h] = lane_fold(jnp.maximum, run_ref[h],
                                           s[h * tq:(h + 1) * tq] + bias[:, u * sub:(u + 1) * sub])
            return carry

        lax.fori_loop(0, nkc, body, 0)
        return [jnp.max(run_ref[h], axis=1, keepdims=True) for h in range(DSA_HEADS)]

    def att_pass(stored, ms):
        run_ref[...] = jnp.zeros_like(run_ref)
        acc_ref[...] = jnp.zeros_like(acc_ref)

        def half(s_ref, bias, k0):
            ps = []
            for h in range(DSA_HEADS):
                p = jnp.exp(s_ref[h * tq:(h + 1) * tq, :] + bias - ms[h])
                run_ref[h] = lane_fold(jnp.add, run_ref[h], p)
                ps.append(p.astype(BF16))
            acc_ref[...] += jnp.dot(jnp.concatenate(ps, axis=0), v_ref[pl.ds(k0, sub), :],
                                    preferred_element_type=F32)

        sa_ref[...] = qk(0)

        def body(c, carry):
            k0 = pl.multiple_of(c * kc, kc)
            sb_ref[...] = qk(k0 + sub)
            bias = bias_of(c, stored)
            half(sa_ref, bias[:, :sub], k0)
            sa_ref[...] = qk(pl.multiple_of(jnp.minimum(c + 1, nkc - 1) * kc, kc))
            half(sb_ref, bias[:, sub:], k0 + sub)
            return carry

        lax.fori_loop(0, nkc, body, 0)

    def finish(stored):
        if stored:
            tie_bias_pass()
        att_pass(stored, ms_bound)
        lmin = jnp.min(jnp.sum(run_ref[0], axis=1, keepdims=True))
        for h in range(1, DSA_HEADS):
            lmin = jnp.minimum(lmin, jnp.min(jnp.sum(run_ref[h], axis=1, keepdims=True)))

        def redo():
            att_pass(stored, max_pass(stored))
            return 0

        lax.cond(lmin >= 1e-26, lambda: 0, redo)
        return 0

    lax.cond(any_tie, lambda: finish(True), lambda: finish(False))
    out = jnp.zeros((tq, DSA_WIDTH), F32)
    for h in range(DSA_HEADS):
        m = (lane_w >= h * DSA_HEAD_DIM) & (lane_w < (h + 1) * DSA_HEAD_DIM)
        out = out + jnp.where(m, acc_ref[h * tq:(h + 1) * tq, :] / jnp.sum(run_ref[h], axis=1, keepdims=True), 0.0)
    o_ref[...] = _rms(out, ng_ref[...]).astype(o_ref.dtype)


def _dsa(p, norm_g, batch, seq, tq=512, kc=512, top_r=12):
    n_sel = min(DSA_TOPK, seq // 4)
    kc = min(kc, seq)
    sub = kc // 2
    r3 = lambda a: a.reshape(batch, seq, a.shape[-1])
    qblk = lambda w: pl.BlockSpec((None, tq, w), lambda b, i: (b, i, 0))
    kfull = lambda w: pl.BlockSpec((None, seq, w), lambda b, i: (b, 0, 0), pipeline_mode=pl.Buffered(1))
    out = pl.pallas_call(
        functools.partial(_dsa_kernel, tq=tq, kc=kc, sub=sub, n_sel=n_sel, top_r=top_r),
        grid=(batch, seq // tq),
        in_specs=[qblk(DSA_WIDTH), kfull(DSA_WIDTH), qblk(LANES), qblk(DSA_WIDTH),
                  kfull(DSA_WIDTH), kfull(DSA_WIDTH), pl.BlockSpec((1, DSA_WIDTH), lambda b, i: (0, 0))],
        out_specs=qblk(DSA_WIDTH),
        out_shape=jax.ShapeDtypeStruct((batch, seq, DSA_WIDTH), BF16),
        scratch_shapes=[
            pltpu.VMEM((DSA_IDX_HEADS * tq, DSA_WIDTH), BF16),
            pltpu.VMEM((DSA_HEADS * tq, DSA_WIDTH), BF16),
            pltpu.VMEM((seq // kc, tq, kc), F32),
            pltpu.VMEM((DSA_HEADS * tq, DSA_WIDTH), F32),
            pltpu.VMEM((DSA_HEADS, tq, LANES), F32),
            pltpu.VMEM((top_r, tq, LANES), I32),
            pltpu.VMEM((DSA_HEADS * tq, sub), F32),
            pltpu.VMEM((DSA_HEADS * tq, sub), F32),
            pltpu.VMEM((SUBLANES, LANES), F32),
        ],
        compiler_params=_cparams("arbitrary", "arbitrary"),
        name="dsa",
    )(r3(p["iq"]), r3(p["ik"]), r3(p["iw"]), r3(p["dq"]), r3(p["dk"]), r3(p["dv"]),
      norm_g.reshape(1, DSA_WIDTH))
    return out.reshape(batch * seq, DSA_WIDTH)


def _s5_tables(a_re, a_im, log_step, b_re, b_im, c_re, c_im):
    hp = lax.Precision.HIGHEST
    s = S5_CHUNK
    dt = jnp.exp(log_step)[:, None]
    mag = jnp.exp(a_re * dt)
    lam_re, lam_im = mag * jnp.cos(a_im * dt), mag * jnp.sin(a_im * dt)
    den = a_re * a_re + a_im * a_im
    nr, ni = lam_re - 1.0, lam_im
    coef_re = (nr * a_re + ni * a_im) / den
    coef_im = (ni * a_re - nr * a_im) / den
    bb_re = coef_re[..., None] * b_re - coef_im[..., None] * b_im
    bb_im = coef_re[..., None] * b_im + coef_im[..., None] * b_re
    j = jnp.arange(s + 1, dtype=F32)[:, None, None]
    pmag = jnp.exp(a_re * dt * j)
    pw_re, pw_im = pmag * jnp.cos(a_im * dt * j), pmag * jnp.sin(a_im * dt * j)
    e_re = c_re[None] * pw_re[:, :, None, :] - c_im[None] * pw_im[:, :, None, :]
    e_im = c_re[None] * pw_im[:, :, None, :] + c_im[None] * pw_re[:, :, None, :]
    kk = (jnp.einsum("jgcp,gpd->jgcd", e_re, bb_re, precision=hp)
          - jnp.einsum("jgcp,gpd->jgcd", e_im, bb_im, precision=hp))
    lag = np.arange(s)[None, :] - np.arange(s)[:, None]
    toe = jnp.where((lag >= 0)[:, :, None, None, None], kk[np.clip(lag, 0, s)], 0.0)
    rp_re, rp_im = pw_re[s - 1 - np.arange(s)], pw_im[s - 1 - np.arange(s)]
    x_re = rp_re[..., None] * bb_re[None] - rp_im[..., None] * bb_im[None]
    x_im = rp_re[..., None] * bb_im[None] + rp_im[..., None] * bb_re[None]
    f_re, f_im = e_re[1:], e_im[1:]
    sc_w = s * S5_GROUP_CH
    g_ids = jnp.arange(S5_TILE_G)[:, None, None]

    def placement(inner, width):
        a = jnp.arange(width // S5_TILE_G)[None, :, None]
        dst = (a // inner) * (S5_TILE_G * inner) + g_ids * inner + a % inner
        return (dst == jnp.arange(width)[None, None, :]).astype(F32)

    p_row = placement(S5_GROUP_CH, S5_ROW_W)
    p_state = placement(S5_STATE, S5_STATE_W)
    per_tile = lambda a: a.astype(BF16).astype(F32).reshape((S5_TILES, S5_TILE_G) + a.shape[1:])

    def place(tab, p_in, p_out):
        t = jnp.einsum("qgab,gbc->qgac", per_tile(tab), p_out)
        return jnp.einsum("gar,qgac->qrc", p_in, t).astype(BF16)

    g_toe = toe.transpose(2, 0, 4, 1, 3).reshape(S5_GROUPS, sc_w, sc_w)
    g_w1 = lambda x: x.transpose(1, 0, 3, 2).reshape(S5_GROUPS, sc_w, S5_STATE)
    g_w2 = lambda f: f.transpose(1, 3, 0, 2).reshape(S5_GROUPS, S5_STATE, sc_w)
    tabs = dict(toe=place(g_toe, p_row, p_row),
                w1_re=place(g_w1(x_re), p_row, p_state), w1_im=place(g_w1(x_im), p_row, p_state),
                w2_re=place(g_w2(f_re), p_state, p_row), w2_im=place(g_w2(-f_im), p_state, p_row))
    tabs["a_re"], tabs["a_im"] = pw_re[s].reshape(-1), pw_im[s].reshape(-1)
    return tabs


def _s5_state_in_kernel(u_ref, wr_ref, wi_ref, xr_ref, xi_ref):
    u = u_ref[...].astype(BF16)
    xr_ref[...] = jnp.dot(u, wr_ref[...], preferred_element_type=F32)
    xi_ref[...] = jnp.dot(u, wi_ref[...], preferred_element_type=F32)


def _s5_scan_kernel(xr_ref, xi_ref, ar_ref, ai_ref, hr_ref, hi_ref, cr_ref, ci_ref, *, steps):
    @pl.when(pl.program_id(0) == 0)
    def _():
        cr_ref[...] = jnp.zeros_like(cr_ref)
        ci_ref[...] = jnp.zeros_like(ci_ref)

    ar, ai = ar_ref[...], ai_ref[...]

    def body(n, carry):
        hr, hi = carry
        hr_ref[n] = hr
        hi_ref[n] = hi
        return ar * hr - ai * hi + xr_ref[n], ar * hi + ai * hr + xi_ref[n]

    hr, hi = lax.fori_loop(0, steps, body, (cr_ref[...], ci_ref[...]))
    cr_ref[...] = hr
    ci_ref[...] = hi


def _s5_out_kernel(u_ref, toe_ref, hr_ref, hi_ref, w2r_ref, w2i_ref, y_ref):
    y = jnp.dot(u_ref[...].astype(BF16), toe_ref[...], preferred_element_type=F32)
    y = y + jnp.dot(hr_ref[...].astype(BF16), w2r_ref[...], preferred_element_type=F32)
    y = y + jnp.dot(hi_ref[...].astype(BF16), w2i_ref[...], preferred_element_type=F32)
    y_ref[...] = y


def _s5_post_kernel(y_ref, u0_ref, u1_ref, u2_ref, d_ref, gw_ref, gb_ref, ng_ref, o_ref, yt_ref):
    n_chunks = yt_ref.shape[1] // S5_CHUNK
    for q in range(S5_TILES):
        for s in range(S5_CHUNK):
            yt_ref[q, pl.ds(s, n_chunks, stride=S5_CHUNK), :] = y_ref[q, :, s * LANES:(s + 1) * LANES]
    y = jnp.concatenate([yt_ref[q] for q in range(S5_TILES)], axis=1)
    u = jnp.concatenate([u0_ref[...], u1_ref[...], u2_ref[...]], axis=1)
    z = y + d_ref[...] * u
    z = 0.5 * z * (1.0 + jnp.tanh(math.sqrt(2.0 / math.pi) * (z + 0.044715 * (z * z * z))))
    gate = _sigmoid(jnp.dot(z.astype(BF16), gw_ref[...], preferred_element_type=F32) + gb_ref[...])
    o_ref[...] = _rms(z * gate, ng_ref[...]).astype(o_ref.dtype)


def _s5(us, xq, tabs, d, glu_w, glu_b, norm_g, batch, seq, tm=512, rb=512):
    s = S5_CHUNK
    nch = seq // s
    rows = nch * batch
    rb = min(rb, rows)
    gp = S5_GROUPS * S5_STATE
    row_in = pl.BlockSpec((None, rb, S5_ROW_W), lambda q, i: (q, i, 0))
    wspec = lambda a: pl.BlockSpec((None,) + a.shape[1:], lambda q, i: (q, 0, 0))
    colblk = pl.BlockSpec((rb, S5_STATE_W), lambda q, i: (i, q))
    xr, xi = pl.pallas_call(
        _s5_state_in_kernel,
        grid=(S5_TILES, rows // rb),
        in_specs=[row_in, wspec(tabs["w1_re"]), wspec(tabs["w1_im"])],
        out_specs=[colblk, colblk],
        out_shape=[jax.ShapeDtypeStruct((rows, gp), F32)] * 2,
        compiler_params=_cparams("arbitrary", "arbitrary"),
        name="s5_state_in",
    )(xq, tabs["w1_re"], tabs["w1_im"])
    parts = SUBLANES // batch
    wl = gp // parts
    slab = lambda a: a.reshape(batch, nch, gp).transpose(1, 0, 2).reshape(nch, SUBLANES, wl)
    unslab = lambda a: a.reshape(nch, batch, gp).transpose(1, 0, 2).reshape(rows, gp)
    coef = lambda a: jnp.tile(a.reshape(parts, wl), (batch, 1))
    steps = min(64, nch)
    sblk = pl.BlockSpec((steps, SUBLANES, wl), lambda i: (i, 0, 0))
    cblk = pl.BlockSpec((SUBLANES, wl), lambda i: (0, 0))
    hr, hi = pl.pallas_call(
        functools.partial(_s5_scan_kernel, steps=steps),
        grid=(nch // steps,),
        in_specs=[sblk, sblk, cblk, cblk],
        out_specs=[sblk, sblk],
        out_shape=[jax.ShapeDtypeStruct((nch, SUBLANES, wl), F32)] * 2,
        scratch_shapes=[pltpu.VMEM((SUBLANES, wl), F32)] * 2,
        compiler_params=_cparams("arbitrary"),
        name="s5_scan",
    )(slab(xr), slab(xi), coef(tabs["a_re"]), coef(tabs["a_im"]))
    y = pl.pallas_call(
        _s5_out_kernel,
        grid=(S5_TILES, rows // rb),
        in_specs=[row_in, wspec(tabs["toe"]), colblk, colblk, wspec(tabs["w2_re"]), wspec(tabs["w2_im"])],
        out_specs=row_in,
        out_shape=jax.ShapeDtypeStruct((S5_TILES, rows, S5_ROW_W), F32),
        compiler_params=_cparams("arbitrary", "arbitrary"),
        name="s5_out",
    )(xq, tabs["toe"], unslab(hr), unslab(hi), tabs["w2_re"], tabs["w2_im"])
    t = batch * seq
    tm = min(tm, t)
    tok = lambda w: pl.BlockSpec((tm, w), lambda i: (i, 0))
    vec = pl.BlockSpec((1, S5_WIDTH), lambda i: (0, 0))
    return pl.pallas_call(
        _s5_post_kernel,
        grid=(t // tm,),
        in_specs=[pl.BlockSpec((S5_TILES, tm // s, S5_ROW_W), lambda i: (0, i, 0)), tok(LANES), tok(LANES),
                  tok(LANES), vec, pl.BlockSpec((S5_WIDTH, S5_WIDTH), lambda i: (0, 0)), vec, vec],
        out_specs=tok(S5_WIDTH),
        out_shape=jax.ShapeDtypeStruct((t, S5_WIDTH), BF16),
        scratch_shapes=[pltpu.VMEM((S5_TILES, tm, LANES), F32)],
        compiler_params=_cparams("arbitrary"),
        name="s5_post",
    )(y, us[0], us[1], us[2], d.reshape(1, -1), glu_w.astype(BF16), glu_b.reshape(1, -1),
      norm_g.reshape(1, -1))


def _mix_out_kernel(og_ref, od_ref, os_ref, wg_ref, wd_ref, ws_ref, x_ref, gt_ref, n2_ref, sc_ref,
                    sh_ref, rw1_ref, rw2_ref, rb_ref, x1_ref, h2_ref, route_ref, cnt_ref, carry_ref, *, tm):
    @pl.when(pl.program_id(0) == 0)
    def _():
        carry_ref[...] = jnp.zeros_like(carry_ref)

    mix = jnp.dot(og_ref[...], wg_ref[...], preferred_element_type=F32)
    mix = mix + jnp.dot(od_ref[...], wd_ref[...], preferred_element_type=F32)
    mix = mix + jnp.dot(os_ref[...], ws_ref[...], preferred_element_type=F32)
    x1 = x_ref[...] + gt_ref[...] * mix
    x1_ref[...] = x1
    h2 = _rms(x1, n2_ref[...]) * (1.0 + sc_ref[...]) + sh_ref[...]
    h2_ref[...] = h2
    ha = h2.astype(BF16)
    hb = (h2 - ha.astype(F32)).astype(BF16)
    logit = (jnp.dot(ha, rw1_ref[...], preferred_element_type=F32)
             + jnp.dot(ha, rw2_ref[...], preferred_element_type=F32)
             + jnp.dot(hb, rw1_ref[...], preferred_element_type=F32)) + rb_ref[...]
    lane = lax.broadcasted_iota(I32, (tm, LANES), 1)
    lanef = lane.astype(F32)
    ninf = jnp.float32(-jnp.inf)
    big = jnp.float32(LANES)
    gl = jnp.where(lane < MOE_GROUPS, logit, ninf)
    gmax = jnp.max(gl, axis=1, keepdims=True)
    grp_p = 1.0 / jnp.sum(jnp.exp(gl - gmax), axis=1, keepdims=True)
    grp_i = jnp.min(jnp.where(gl == gmax, lanef, big), axis=1, keepdims=True)
    lo = MOE_GROUPS + MOE_EXPERTS_PER_GROUP * grp_i
    el = jnp.where((lanef >= lo) & (lanef < lo + MOE_EXPERTS_PER_GROUP), logit, ninf)
    l1 = jnp.max(el, axis=1, keepdims=True)
    i1 = jnp.min(jnp.where(el == l1, lanef, big), axis=1, keepdims=True)
    el2 = jnp.where(lanef == i1, ninf, el)
    l2 = jnp.max(el2, axis=1, keepdims=True)
    i2 = jnp.min(jnp.where(el2 == l2, lanef, big), axis=1, keepdims=True)
    e21 = jnp.exp(l2 - l1)
    w1 = grp_p / (1.0 + e21)
    w2 = grp_p * e21 / (1.0 + e21)
    e1 = i1 - MOE_GROUPS
    e2 = i2 - MOE_GROUPS
    oh1 = lanef == e1
    oh2 = lanef == e2
    hot = jnp.where(oh1 | oh2, 1.0, 0.0)
    ri = lax.broadcasted_iota(I32, (tm, tm), 0)
    ci = lax.broadcasted_iota(I32, (tm, tm), 1)
    before = (ci < ri).astype(BF16)
    prior = carry_ref[...] + jnp.dot(before, hot.astype(BF16), preferred_element_type=F32)
    r1 = jnp.sum(jnp.where(oh1, prior, 0.0), axis=1, keepdims=True)
    r2 = jnp.sum(jnp.where(oh2, prior, 0.0), axis=1, keepdims=True)
    total = carry_ref[...] + jnp.sum(hot, axis=0, keepdims=True)
    carry_ref[...] = total
    cnt_ref[...] = jnp.broadcast_to(total, cnt_ref.shape)
    route = jnp.where(lane == 0, e1, jnp.where(lane == 1, e2, jnp.where(lane == 2, w1, jnp.where(
        lane == 3, w2, jnp.where(lane == 4, r1, jnp.where(lane == 5, r2, 0.0))))))
    route_ref[...] = route


def _mix_out(o_gla, o_dsa, o_s5, w_out, x2, gt1, norm2_g, sc2, sh2, rgw, rgb, rew, reb, seq, tm=512):
    t, d = x2.shape
    per_b = seq // tm
    wg = jnp.pad(w_out[:GLA_HEADS * GLA_DV].reshape(GLA_HEADS, GLA_DV, d),
                 ((0, 0), (0, HEAD_PAD - GLA_DV), (0, 0))).reshape(GLA_W, d).astype(BF16)
    wd = w_out[GLA_HEADS * GLA_DV:GLA_HEADS * GLA_DV + DSA_WIDTH].astype(BF16)
    ws = w_out[GLA_HEADS * GLA_DV + DSA_WIDTH:].astype(BF16)
    rw = _pad_cols(jnp.concatenate([rgw, rew], axis=1), LANES)
    rb = _pad_cols(jnp.concatenate([rgb, reb]).reshape(1, -1), LANES)
    rw1 = rw.astype(BF16)
    rw2 = (rw - rw1.astype(F32)).astype(BF16)
    tok = lambda w: pl.BlockSpec((tm, w), lambda i: (i, 0))
    full = lambda a: pl.BlockSpec(a.shape, lambda i: (0, 0))
    perb = pl.BlockSpec((None, 1, d), lambda i: (i // per_b, 0, 0))
    vec = pl.BlockSpec((1, d), lambda i: (0, 0))
    return pl.pallas_call(
        functools.partial(_mix_out_kernel, tm=tm),
        grid=(t // tm,),
        in_specs=[tok(GLA_W), tok(DSA_WIDTH), tok(S5_WIDTH), full(wg), full(wd), full(ws), tok(d),
                  perb, vec, perb, perb, full(rw1), full(rw2), full(rb)],
        out_specs=[tok(d), tok(d), tok(LANES), pl.BlockSpec((SUBLANES, LANES), lambda i: (0, 0))],
        out_shape=[jax.ShapeDtypeStruct((t, d), F32), jax.ShapeDtypeStruct((t, d), F32),
                   jax.ShapeDtypeStruct((t, LANES), F32), jax.ShapeDtypeStruct((SUBLANES, LANES), F32)],
        scratch_shapes=[pltpu.VMEM((1, LANES), F32)],
        compiler_params=_cparams("arbitrary"),
        name="mix_out",
    )(o_gla, o_dsa, o_s5, wg, wd, ws, x2, gt1[:, None, :], norm2_g.reshape(1, d), sc2[:, None, :],
      sh2[:, None, :], rw1, rw2, rb)


def _row(ref, r):
    return ref.at[pl.ds(r, 1), :]


def _dispatch_kernel(dest_ref, h_ref, zero_ref, out_ref, sem, *, td):
    del zero_ref

    def start(g, c):
        for j in range(SUBLANES):
            r = g * SUBLANES + j
            pltpu.make_async_copy(_row(h_ref, r), _row(out_ref, dest_ref[0, r]), sem).start()
            pltpu.make_async_copy(_row(h_ref, r), _row(out_ref, dest_ref[1, r]), sem).start()
        return c

    lax.fori_loop(0, td // SUBLANES, start, 0)
    for _ in range(2):
        pltpu.make_async_copy(h_ref, out_ref.at[pl.ds(0, td), :], sem).wait()


def _dispatch(h2, dest, n_rows, td=256):
    t, d = h2.shape
    dest3 = dest.reshape(2, t // td, td).transpose(1, 0, 2)
    zeros = jnp.zeros((n_rows, d), F32)
    return pl.pallas_call(
        functools.partial(_dispatch_kernel, td=td),
        grid=(t // td,),
        in_specs=[pl.BlockSpec((None, 2, td), lambda i: (i, 0, 0), memory_space=pltpu.SMEM),
                  pl.BlockSpec((td, d), lambda i: (i, 0)),
                  pl.BlockSpec(memory_space=pl.ANY)],
        out_specs=pl.BlockSpec(memory_space=pl.ANY),
        out_shape=jax.ShapeDtypeStruct((n_rows, d), F32),
        scratch_shapes=[pltpu.SemaphoreType.DMA(())],
        input_output_aliases={2: 0},
        compiler_params=_cparams("arbitrary"),
        name="moe_dispatch",
    )(dest3, h2, zeros)


def _expert_kernel(be_ref, nb_ref, x_ref, wg_ref, wu_ref, wd_ref, o_ref):
    del be_ref
    i = pl.program_id(0)

    @pl.when(i < nb_ref[0])
    def _():
        xb = x_ref[...].astype(BF16)
        g = jnp.dot(xb, wg_ref[...].astype(BF16), preferred_element_type=F32)
        u = jnp.dot(xb, wu_ref[...].astype(BF16), preferred_element_type=F32)
        hid = (g * _sigmoid(g) * u).astype(BF16)
        o_ref[...] = jnp.dot(hid, wd_ref[...].astype(BF16), preferred_element_type=F32)

    @pl.when(i >= nb_ref[0])
    def _():
        o_ref[...] = jnp.zeros_like(o_ref)


def _experts(xs, blk_e, n_used, wg, wu, wd, layer):
    n_rows = xs.shape[0]
    nb = n_rows // MOE_BM
    d, hdim = wg.shape[-2:]
    rows = pl.BlockSpec((MOE_BM, d), lambda i, be, nu: (i, 0))
    wspec = lambda a, b: pl.BlockSpec((None, None, a, b), lambda i, be, nu: (layer, be[i], 0, 0))
    return pl.pallas_call(
        _expert_kernel,
        grid_spec=pltpu.PrefetchScalarGridSpec(
            num_scalar_prefetch=2,
            grid=(nb,),
            in_specs=[rows, wspec(d, hdim), wspec(d, hdim), wspec(hdim, d)],
            out_specs=rows,
        ),
        out_shape=jax.ShapeDtypeStruct((n_rows, d), F32),
        compiler_params=_cparams("arbitrary"),
        name="moe_experts",
    )(blk_e, n_used, xs, wg, wu, wd)


def _combine_kernel(dest_ref, x_ref, route_ref, gt_ref, fg_ref, eo_ref, o_ref, rows_ref, sem, *, tc, final):
    def start(g, c):
        for j in range(SUBLANES):
            r = g * SUBLANES + j
            pltpu.make_async_copy(_row(eo_ref, dest_ref[0, r]), _row(rows_ref.at[0], r), sem).start()
            pltpu.make_async_copy(_row(eo_ref, dest_ref[1, r]), _row(rows_ref.at[1], r), sem).start()
        return c

    lax.fori_loop(0, tc // SUBLANES, start, 0)
    for which in range(2):
        pltpu.make_async_copy(eo_ref.at[pl.ds(0, tc), :], rows_ref.at[which], sem).wait()
    route = route_ref[...]
    lane = lax.broadcasted_iota(I32, route.shape, 1)
    w1 = jnp.sum(jnp.where(lane == 2, route, 0.0), axis=1, keepdims=True)
    w2 = jnp.sum(jnp.where(lane == 3, route, 0.0), axis=1, keepdims=True)
    y = rows_ref[0] * w1 + rows_ref[1] * w2
    x = x_ref[...] + gt_ref[...] * y
    if final:
        x = _rms(x, fg_ref[...])
    o_ref[...] = x


def _combine(x1, route, dest, eo, gt2, final_g, seq, final, tc=256):
    t, d = x1.shape
    per_b = seq // tc
    dest3 = dest.reshape(2, t // tc, tc).transpose(1, 0, 2)
    return pl.pallas_call(
        functools.partial(_combine_kernel, tc=tc, final=final),
        grid=(t // tc,),
        in_specs=[pl.BlockSpec((None, 2, tc), lambda i: (i, 0, 0), memory_space=pltpu.SMEM),
                  pl.BlockSpec((tc, d), lambda i: (i, 0)),
                  pl.BlockSpec((tc, LANES), lambda i: (i, 0)),
                  pl.BlockSpec((None, 1, d), lambda i: (i // per_b, 0, 0)),
                  pl.BlockSpec((1, d), lambda i: (0, 0)),
                  pl.BlockSpec(memory_space=pl.ANY)],
        out_specs=pl.BlockSpec((tc, d), lambda i: (i, 0)),
        out_shape=jax.ShapeDtypeStruct((t, d), F32),
        scratch_shapes=[pltpu.VMEM((2, tc, d), F32), pltpu.SemaphoreType.DMA(())],
        compiler_params=_cparams("arbitrary"),
        name="moe_combine",
    )(dest3, x1, route, gt2[:, None, :], final_g.reshape(1, d), eo)


def _moe_plan(route, counts, t):
    e = route[:, 0:2].astype(I32).T
    rank = route[:, 4:6].astype(I32).T
    cnt = counts[0, :MOE_EXPERTS].astype(I32)
    padded = (cnt + MOE_BM - 1) // MOE_BM * MOE_BM
    pend = jnp.cumsum(padded)
    pstart = pend - padded
    ids = jnp.arange(MOE_EXPERTS, dtype=I32)
    dest = rank + jnp.sum(jnp.where(e[..., None] == ids, pstart, 0), axis=-1)
    nb = (2 * t) // MOE_BM + MOE_EXPERTS
    n_used = (pend[-1] // MOE_BM).astype(I32)
    first_row = jnp.minimum(jnp.arange(nb, dtype=I32), n_used - 1) * MOE_BM
    blk_e = jnp.sum((pend[None, :] <= first_row[:, None]).astype(I32), axis=1)
    blk_e = jnp.minimum(blk_e, MOE_EXPERTS - 1)
    return dest, blk_e, n_used.reshape(1), nb * MOE_BM


def kernel(x, c, ada_w, ada_b, norm1_g, w_in, gla_gate_w, gla_gate_b, gla_norm_g, dsa_norm_g, s5_a_re, s5_a_im, s5_log_step, s5_b_re, s5_b_im, s5_c_re, s5_c_im, s5_d, s5_glu_w, s5_glu_b, s5_norm_g, w_out, norm2_g, router_grp_w, router_grp_b, router_exp_w, router_exp_b, exp_w_gate, exp_w_up, exp_w_down, final_norm_g):
    batch, seq, d = x.shape
    depth = ada_w.shape[0]
    t = batch * seq
    mod = _ada_mod(c, ada_w, ada_b)
    x2 = x.reshape(t, d)
    for l in range(depth):
        sh1, sc1, gt1, sh2, sc2, gt2 = (mod[l, j] for j in range(6))
        p = _in_proj(x2, norm1_g[l], sc1, sh1, _layout_w_in(w_in[l]), seq)
        o_gla = _gla(p, gla_gate_w[l], gla_gate_b[l], gla_norm_g[l], batch, seq)
        o_dsa = _dsa(p, dsa_norm_g[l], batch, seq)
        tabs = _s5_tables(s5_a_re[l], s5_a_im[l], s5_log_step[l], s5_b_re[l], s5_b_im[l],
                          s5_c_re[l], s5_c_im[l])
        o_s5 = _s5((p["su0"], p["su1"], p["su2"]), p["sx"], tabs, s5_d[l], s5_glu_w[l], s5_glu_b[l], s5_norm_g[l], batch, seq)
        x1, h2, route, counts = _mix_out(o_gla, o_dsa, o_s5, w_out[l], x2, gt1, norm2_g[l], sc2, sh2,
                                         router_grp_w[l], router_grp_b[l], router_exp_w[l],
                                         router_exp_b[l], seq)
        dest, blk_e, n_used, n_rows = _moe_plan(route, counts, t)
        xs = _dispatch(h2, dest, n_rows)
        eo = _experts(xs, blk_e, n_used, exp_w_gate, exp_w_up, exp_w_down, l)
        x2 = _combine(x1, route, dest, eo, gt2, final_norm_g, seq, final=(l == depth - 1))
    return x2.reshape(batch, seq, d)
```

```python
import functools
import math

import jax
import jax.numpy as jnp
import numpy as np
from jax import lax
from jax.experimental import pallas as pl
from jax.experimental.pallas import tpu as pltpu

F32 = jnp.float32
BF16 = jnp.bfloat16
I32 = jnp.int32

D_MODEL = 1024
GLA_HEADS = 4
GLA_DV = 96
GLA_DK = 48
GLA_RANK = 16
GLA_TAU = 16.0
GLA_CHUNK = 64
DSA_HEADS = 4
DSA_HEAD_DIM = 64
DSA_WIDTH = DSA_HEADS * DSA_HEAD_DIM
DSA_IDX_HEADS = 8
DSA_IDX_DIM = 32
DSA_TOPK = 256
S5_GROUPS = 24
S5_GROUP_CH = 16
S5_STATE = 64
S5_WIDTH = S5_GROUPS * S5_GROUP_CH
MOE_GROUPS = 4
MOE_EXPERTS_PER_GROUP = 8
MOE_EXPERTS = MOE_GROUPS * MOE_EXPERTS_PER_GROUP
MOE_HIDDEN = 512
RMS_EPS = 1e-6
IN_SIZES = (192, 192, 384, 16, 384, 256, 256, 256, 256, 32, 8, 384)

LANES = 128
SUBLANES = 8
VMEM_LIMIT = 56 * 1024 * 1024

HEAD_PAD = LANES
GLA_W = GLA_HEADS * HEAD_PAD
S5_CHUNK = 16
S5_TILE_G = LANES // S5_GROUP_CH
S5_TILES = S5_GROUPS // S5_TILE_G
S5_ROW_W = S5_CHUNK * LANES
S5_STATE_W = S5_TILE_G * S5_STATE
MOE_BM = 512
INT_MIN = -2 ** 31


def _cparams(*sem):
    return pltpu.CompilerParams(dimension_semantics=tuple(sem), vmem_limit_bytes=VMEM_LIMIT)


def _sigmoid(x):
    return 0.5 * (jnp.tanh(0.5 * x) + 1.0)


def _rms(x, g, n=None):
    n = x.shape[-1] if n is None else n
    ms = jnp.sum(x * x, axis=-1, keepdims=True) * (1.0 / n)
    return x * lax.rsqrt(ms + RMS_EPS) * g


def _ada_kernel(c_ref, w_ref, b_ref, o_ref):
    c = c_ref[...]
    cond = (c * _sigmoid(c)).astype(BF16)
    o_ref[...] = jnp.dot(cond, w_ref[...].astype(BF16), preferred_element_type=F32) + b_ref[...]


def _ada_mod(c, ada_w, ada_b):
    depth, d, d6 = ada_w.shape
    nblk = d6 // d
    bp = -(-c.shape[0] // SUBLANES) * SUBLANES
    cp = jnp.pad(c, ((0, bp - c.shape[0]), (0, 0)))
    out = pl.pallas_call(
        _ada_kernel,
        grid=(depth, nblk),
        in_specs=[
            pl.BlockSpec((bp, d), lambda l, j: (0, 0)),
            pl.BlockSpec((None, d, d), lambda l, j: (l, 0, j)),
            pl.BlockSpec((None, None, 1, d), lambda l, j: (l, j, 0, 0)),
        ],
        out_specs=pl.BlockSpec((None, None, bp, d), lambda l, j: (l, j, 0, 0)),
        out_shape=jax.ShapeDtypeStruct((depth, nblk, bp, d), F32),
        compiler_params=_cparams("arbitrary", "arbitrary"),
        name="ada_mod",
    )(cp, ada_w, ada_b.reshape(depth, nblk, 1, d))
    return out[:, :, : c.shape[0], :]


PROJ_OUTS = (
    ("gq", GLA_W, BF16), ("gk", GLA_W, BF16), ("gv", GLA_W, BF16), ("gr", GLA_W, BF16),
    ("glr", LANES, BF16),
    ("dq", DSA_WIDTH, BF16), ("dk", DSA_WIDTH, BF16), ("dv", DSA_WIDTH, BF16),
    ("iq", DSA_IDX_HEADS * DSA_IDX_DIM, BF16), ("ik", DSA_IDX_HEADS * DSA_IDX_DIM, BF16),
    ("iw", LANES, F32), ("su0", LANES, F32), ("su1", LANES, F32), ("su2", LANES, F32),
)
PROJ_COLS = sum(w for _, w, _ in PROJ_OUTS)


def _pad_heads(w, heads, dim):
    d = w.shape[0]
    w = w.reshape(d, heads, dim)
    return jnp.pad(w, ((0, 0), (0, 0), (0, HEAD_PAD - dim))).reshape(d, heads * HEAD_PAD)


def _pad_cols(w, width):
    return jnp.pad(w, ((0, 0), (0, width - w.shape[1])))


def _layout_w_in(w_in):
    offs = np.cumsum((0,) + IN_SIZES)
    p = [w_in[:, offs[i]:offs[i + 1]] for i in range(len(IN_SIZES))]
    cols = [
        _pad_heads(p[0], GLA_HEADS, GLA_DK), _pad_heads(p[1], GLA_HEADS, GLA_DK),
        _pad_heads(p[2], GLA_HEADS, GLA_DV), _pad_heads(p[4], GLA_HEADS, GLA_DV),
        _pad_cols(p[3], LANES),
        p[5], p[6], p[7], p[8], jnp.tile(p[9], (1, DSA_IDX_HEADS)),
        _pad_cols(p[10], LANES), p[11],
    ]
    return jnp.concatenate(cols, axis=1).astype(BF16)


def _proj_kernel(x_ref, g_ref, sc_ref, sh_ref, w_ref, *o_refs):
    x = x_ref[...]
    h = _rms(x, g_ref[...]) * (1.0 + sc_ref[...]) + sh_ref[...]
    hb = h.astype(BF16)
    c0 = 0
    for o_ref, (_, width, _) in zip(o_refs, PROJ_OUTS):
        o_ref[...] = jnp.dot(hb, w_ref[:, c0:c0 + width], preferred_element_type=F32).astype(o_ref.dtype)
        c0 += width
    sx_ref = o_refs[len(PROJ_OUTS)]
    n_chunks = x.shape[0] // S5_CHUNK
    for q in range(S5_TILES):
        su_ref = o_refs[len(PROJ_OUTS) - S5_TILES + q]
        for s in range(S5_CHUNK):
            sx_ref[q, :, s * LANES:(s + 1) * LANES] = su_ref[pl.ds(s, n_chunks, stride=S5_CHUNK), :]


def _in_proj(x2, norm_g, sc, sh, w_lay, seq, tm=512):
    t, d = x2.shape
    per_b = seq // tm
    outs = pl.pallas_call(
        _proj_kernel,
        grid=(t // tm,),
        in_specs=[
            pl.BlockSpec((tm, d), lambda i: (i, 0)),
            pl.BlockSpec((1, d), lambda i: (0, 0)),
            pl.BlockSpec((None, 1, d), lambda i: (i // per_b, 0, 0)),
            pl.BlockSpec((None, 1, d), lambda i: (i // per_b, 0, 0)),
            pl.BlockSpec((d, PROJ_COLS), lambda i: (0, 0)),
        ],
        out_specs=[pl.BlockSpec((tm, w), lambda i: (i, 0)) for _, w, _ in PROJ_OUTS]
        + [pl.BlockSpec((S5_TILES, tm // S5_CHUNK, S5_ROW_W), lambda i: (0, i, 0))],
        out_shape=[jax.ShapeDtypeStruct((t, w), dt) for _, w, dt in PROJ_OUTS]
        + [jax.ShapeDtypeStruct((S5_TILES, t // S5_CHUNK, S5_ROW_W), F32)],
        compiler_params=_cparams("arbitrary"),
        name="in_proj",
    )(x2, norm_g.reshape(1, d), sc[:, None, :], sh[:, None, :], w_lay)
    p = {name: o for (name, _, _), o in zip(PROJ_OUTS, outs)}
    p["sx"] = outs[-1]
    return p


def _gla_kernel(q_ref, k_ref, v_ref, r_ref, lr_ref, gw_ref, gb_ref, ng_ref, o_ref, s_ref, *, rows):
    @pl.when(pl.program_id(1) == 0)
    def _():
        s_ref[...] = jnp.zeros_like(s_ref)

    c = GLA_CHUNK
    gl = jnp.dot(lr_ref[...], gw_ref[...], preferred_element_type=F32) + gb_ref[...]
    g = (jnp.minimum(gl, 0.0) - jnp.log1p(jnp.exp(-jnp.abs(gl)))) * (1.0 / GLA_TAU)
    g1 = g.astype(BF16)
    rem = g - g1.astype(F32)
    g2 = rem.astype(BF16)
    g3 = (rem - g2.astype(F32)).astype(BF16)
    ri = lax.broadcasted_iota(I32, (c, c), 0)
    ci = lax.broadcasted_iota(I32, (c, c), 1)
    causal = ri >= ci
    tri = causal.astype(BF16)
    ng = ng_ref[...]
    for ch in range(rows // c):
        rs = slice(ch * c, (ch + 1) * c)
        bcum = (jnp.dot(tri, g1[rs], preferred_element_type=F32)
                + jnp.dot(tri, g2[rs], preferred_element_type=F32)
                + jnp.dot(tri, g3[rs], preferred_element_type=F32))
        blast = bcum[c - 1:c, :]
        e_pos = jnp.exp(bcum)
        kf = k_ref[rs, :].astype(F32)
        qd = (q_ref[rs, :].astype(F32) * (GLA_DK ** -0.5) * e_pos).astype(BF16)
        kd = (kf * jnp.exp(-bcum)).astype(BF16)
        ku = (kf * jnp.exp(blast - bcum)).astype(BF16)
        dec = jnp.exp(blast)
        for h in range(GLA_HEADS):
            hs = slice(h * HEAD_PAD, (h + 1) * HEAD_PAD)
            vh = v_ref[rs, hs]
            att = lax.dot_general(qd[:, hs], kd[:, hs], (((1,), (1,)), ((), ())),
                                  preferred_element_type=F32)
            att = jnp.where(causal, att, 0.0).astype(BF16)
            st = s_ref[h]
            o = jnp.dot(att, vh, preferred_element_type=F32)
            o = o + lax.dot_general(qd[:, hs], st.astype(BF16), (((1,), (1,)), ((), ())),
                                    preferred_element_type=F32)
            upd = lax.dot_general(vh, ku[:, hs], (((0,), (0,)), ((), ())),
                                  preferred_element_type=F32)
            s_ref[h] = st * dec[:, hs] + upd
            y = _rms(o, ng[:, hs], GLA_DV)
            rr = r_ref[rs, hs].astype(F32)
            o_ref[rs, hs] = (y * (rr * _sigmoid(rr))).astype(o_ref.dtype)


def _gla(p, gate_w, gate_b, norm_g, batch, seq, rows=512):
    gw = _pad_cols(jnp.pad(_pad_heads(gate_w, GLA_HEADS, GLA_DK), ((0, LANES - GLA_RANK), (0, 0))), GLA_W)
    gb = _pad_heads(gate_b.reshape(1, -1), GLA_HEADS, GLA_DK)
    ng = _pad_heads(jnp.tile(norm_g.reshape(1, GLA_DV), (1, GLA_HEADS)), GLA_HEADS, GLA_DV)
    rows = min(rows, seq)
    blk = lambda w: pl.BlockSpec((None, rows, w), lambda b, j: (b, j, 0))
    full = lambda a: pl.BlockSpec(a.shape, lambda b, j: (0, 0))
    r3 = lambda a: a.reshape(batch, seq, a.shape[-1])
    out = pl.pallas_call(
        functools.partial(_gla_kernel, rows=rows),
        grid=(batch, seq // rows),
        in_specs=[blk(GLA_W), blk(GLA_W), blk(GLA_W), blk(GLA_W), blk(LANES),
                  full(gw), full(gb), full(ng)],
        out_specs=blk(GLA_W),
        out_shape=jax.ShapeDtypeStruct((batch, seq, GLA_W), BF16),
        scratch_shapes=[pltpu.VMEM((GLA_HEADS, HEAD_PAD, HEAD_PAD), F32)],
        compiler_params=_cparams("arbitrary", "arbitrary"),
        name="gla",
    )(r3(p["gq"]), r3(p["gk"]), r3(p["gv"]), r3(p["gr"]), r3(p["glr"]), gw.astype(BF16), gb, ng)
    return out.reshape(batch * seq, GLA_W)


def _dsa_kernel(iq_ref, ik_ref, iw_ref, q_ref, k_ref, v_ref, ng_ref, o_ref,
                qs_ref, q4_ref, keys_ref, acc_ref, run_ref, top_ref, sa_ref, sb_ref, kmax_ref, lvl_ref,
                *, tq, kc, sub, n_sel, top_r):
    assert kc == 2 * sub
    qi = pl.program_id(1)
    nkc = ((qi + 1) * tq + kc - 1) // kc
    lane_w = lax.broadcasted_iota(I32, (tq, DSA_WIDTH), 1)
    iq = iq_ref[...]
    for h in range(DSA_IDX_HEADS):
        m = (lane_w >= h * DSA_IDX_DIM) & (lane_w < (h + 1) * DSA_IDX_DIM)
        qs_ref[h * tq:(h + 1) * tq, :] = jnp.where(m, iq, jnp.zeros_like(iq))
    iw = iw_ref[...] * (DSA_IDX_HEADS ** -0.5 * DSA_IDX_DIM ** -0.5)
    lane = lax.broadcasted_iota(I32, (tq, LANES), 1)
    w_cols = [iw[:, h:h + 1] for h in range(DSA_IDX_HEADS)]
    qpos = qi * tq + lax.broadcasted_iota(I32, (tq, sub), 0)
    kcol = lax.broadcasted_iota(I32, (tq, sub), 1)
    nt = (((1,), (1,)), ((), ()))

    ninf = jnp.float32(-jnp.inf)

    def score_body(c, carry):
        for u in range(kc // sub):
            k0 = pl.multiple_of(c * kc + u * sub, sub)
            lg = lax.dot_general(qs_ref[...], ik_ref[pl.ds(k0, sub), :], nt,
                                 preferred_element_type=F32)
            sc = jnp.zeros((tq, sub), F32)
            for h in range(DSA_IDX_HEADS):
                sc = sc + w_cols[h] * jnp.maximum(lg[h * tq:(h + 1) * tq, :], 0.0)
            sc = sc + 0.0
            keys_ref[c, :, u * sub:(u + 1) * sub] = jnp.where(kcol + k0 <= qpos, sc, ninf)
        return carry

    lax.fori_loop(0, nkc, score_body, 0)

    def to_key(x):
        bits = lax.bitcast_convert_type(x, I32)
        key = jnp.where(bits < 0, bits ^ jnp.int32(0x7FFFFFFF), bits)
        return jnp.where(x == ninf, jnp.int32(INT_MIN), key)

    def count_ge(cand):
        candb = jnp.broadcast_to(cand, (tq, LANES))

        def body(c, acc):
            blk = keys_ref[c]
            for j in range(kc // LANES):
                acc = acc + jnp.where(to_key(blk[:, j * LANES:(j + 1) * LANES]) >= candb, 1.0, 0.0)
            return acc

        acc = lax.fori_loop(0, nkc, body, jnp.zeros((tq, LANES), F32))
        return jnp.sum(acc, axis=1, keepdims=True)

    need = jnp.float32(n_sel)
    assert n_sel % LANES == 0 and n_sel // LANES <= top_r
    lvl = n_sel // LANES - 1

    def select(count, bracket=None):
        def full():
            t0 = jnp.full((tq, 1), INT_MIN, I32)
            zero = jnp.zeros((tq, 1), I32)
            t0 = jnp.where(count(zero) >= need, zero, t0)

            def bit_body(i, t):
                cand = t | (jnp.int32(1) << (30 - i))
                return jnp.where(count(cand) >= need, cand, t)

            return lax.fori_loop(0, 31, bit_body, t0)

        if bracket is None:
            t = full()
        else:
            lo, hi = bracket
            lo_f = lo.astype(F32)
            span = hi.astype(F32) - lo_f
            e = (lax.bitcast_convert_type(jnp.maximum(span, 1.0), I32) >> 23) - 127
            nbits = jnp.max((e + 2).astype(F32)).astype(I32)
            fits = (nbits <= 29) & (jnp.max(lo_f) < 1.5e9) & (jnp.min(lo_f) > -2.0e9)

            def offsets():
                def bit_body(i, off):
                    cand = off | (jnp.int32(1) << (nbits - 1 - i))
                    return jnp.where(count(lo + cand) >= need, cand, off)

                return lo + lax.fori_loop(0, nbits, bit_body, jnp.zeros((tq, 1), I32))

            t = lax.cond(fits, offsets, full)
        t = jnp.maximum(t, jnp.int32(INT_MIN + 1))
        return t, count(t + 1), count(t)

    lowest = jnp.full((SUBLANES, LANES), ninf, F32)
    for rt in range(0, tq // SUBLANES, 2):
        rsl = [slice((rt + a) * SUBLANES, (rt + a + 1) * SUBLANES) for a in range(2)]

        def top_body(c, ts, rsl=rsl):
            ts = [list(t) for t in ts]
            for j in range(kc // LANES):
                for a in range(2):
                    x = keys_ref[c, rsl[a], j * LANES:(j + 1) * LANES]
                    for r in range(top_r):
                        hi = jnp.maximum(ts[a][r], x)
                        x = jnp.minimum(ts[a][r], x)
                        ts[a][r] = hi
            return tuple(tuple(t) for t in ts)

        ts = lax.fori_loop(0, nkc, top_body, ((lowest,) * top_r,) * 2)
        for a in range(2):
            for r in range(top_r):
                top_ref[r, rsl[a], :] = to_key(ts[a][r])
            lvl_ref[rsl[a], :] = ts[a][lvl]

    def count_cand(cand):
        candb = jnp.broadcast_to(cand, (tq, LANES))
        acc = jnp.zeros((tq, LANES), F32)
        for r in range(top_r):
            acc = acc + jnp.where(top_ref[r] >= candb, 1.0, 0.0)
        return jnp.sum(acc, axis=1, keepdims=True)

    level = lvl_ref[...]
    bracket = (to_key(jnp.min(level, axis=1, keepdims=True)), to_key(jnp.max(level, axis=1, keepdims=True)))
    thr_c, n_gt_c, n_ge_c = select(count_cand, bracket)
    hidden = jnp.max(jnp.where(top_ref[top_r - 1] >= thr_c, 1.0, 0.0)) > 0.0
    thr, n_gt, n_ge = lax.cond(hidden, lambda: select(count_ge), lambda: (thr_c, n_gt_c, n_ge_c))
    n_tie = need - n_gt
    any_tie = jnp.max(n_ge) > need
    thr_bits = jnp.where(thr < 0, thr ^ jnp.int32(0x7FFFFFFF), thr)
    thr_f = jnp.where(thr == jnp.int32(INT_MIN + 1), jnp.float32(jnp.finfo(jnp.float32).min),
                      lax.bitcast_convert_type(thr_bits, F32))

    ui = lax.broadcasted_iota(I32, (kc, kc), 0)
    uj = lax.broadcasted_iota(I32, (kc, kc), 1)
    before = (ui < uj).astype(BF16)
    q = q_ref[...] * jnp.asarray(DSA_HEAD_DIM ** -0.5, BF16)
    for h in range(DSA_HEADS):
        m = (lane_w >= h * DSA_HEAD_DIM) & (lane_w < (h + 1) * DSA_HEAD_DIM)
        q4_ref[h * tq:(h + 1) * tq, :] = jnp.where(m, q, jnp.zeros_like(q))
    neg = jnp.float32(-1e30)

    def lane_fold(op, acc, s):
        for j in range(s.shape[1] // LANES):
            acc = op(acc, s[:, j * LANES:(j + 1) * LANES])
        return acc

    @pl.when(qi == 0)
    def _():
        kmax_ref[...] = jnp.zeros_like(kmax_ref)

    hl = lax.broadcasted_iota(I32, (DSA_WIDTH, LANES), 0) // DSA_HEAD_DIM
    head_of = (hl == lax.broadcasted_iota(I32, (DSA_WIDTH, LANES), 1)).astype(BF16)
    kb = k_ref[pl.ds(pl.multiple_of(qi * tq, tq), tq), :].astype(F32)
    kn2 = jnp.dot((kb * kb).astype(BF16), head_of, preferred_element_type=F32)
    kmax = jnp.maximum(kmax_ref[0:1, :], jnp.max(kn2, axis=0, keepdims=True))
    kmax_ref[...] = jnp.broadcast_to(kmax, kmax_ref.shape)
    qf = q.astype(F32)
    qn2 = jnp.dot((qf * qf).astype(BF16), head_of, preferred_element_type=F32)
    bound = jnp.sqrt(qn2 * kmax) * 1.02
    ms_bound = [jnp.sum(jnp.where(lane == h, bound, 0.0), axis=1, keepdims=True) for h in range(DSA_HEADS)]

    def tie_bias_pass():
        def body(c, seen):
            sc = keys_ref[c]
            eq = sc == thr_f
            eqf = jnp.where(eq, 1.0, 0.0)
            rank = seen + jnp.dot(eqf.astype(BF16), before, preferred_element_type=F32)
            sel = (sc > thr_f) | (eq & (rank < n_tie))
            keys_ref[c] = jnp.where(sel, 0.0, neg)
            return seen + jnp.sum(eqf, axis=1, keepdims=True)

        lax.fori_loop(0, nkc, body, jnp.zeros((tq, 1), F32))

    def bias_of(c, stored):
        sc = keys_ref[c]
        return sc if stored else jnp.where(sc >= thr_f, 0.0, neg)

    def qk(k0):
        return lax.dot_general(q4_ref[...], k_ref[pl.ds(k0, sub), :], nt, preferred_element_type=F32)

    def max_pass(stored):
        run_ref[...] = jnp.full(run_ref.shape, neg, F32)

        def body(c, carry):
            bias = bias_of(c, stored)
            for u in range(kc // sub):
                s = qk(pl.multiple_of(c * kc + u * sub, sub))
                for h in range(DSA_HEADS):
                    run_ref[h] = lane_fold(jnp.maximum, run_ref[h],
                                           s[h * tq:(h + 1) * tq] + bias[:, u * sub:(u + 1) * sub])
            return carry

        lax.fori_loop(0, nkc, body, 0)
        return [jnp.max(run_ref[h], axis=1, keepdims=True) for h in range(DSA_HEADS)]

    def att_pass(stored, ms):
        run_ref[...] = jnp.zeros_like(run_ref)
        acc_ref[...] = jnp.zeros_like(acc_ref)

        def half(s_ref, bias, k0):
            ps = []
            for h in range(DSA_HEADS):
                p = jnp.exp(s_ref[h * tq:(h + 1) * tq, :] + bias - ms[h])
                run_ref[h] = lane_fold(jnp.add, run_ref[h], p)
                ps.append(p.astype(BF16))
            acc_ref[...] += jnp.dot(jnp.concatenate(ps, axis=0), v_ref[pl.ds(k0, sub), :],
                                    preferred_element_type=F32)

        sa_ref[...] = qk(0)

        def body(c, carry):
            k0 = pl.multiple_of(c * kc, kc)
            sb_ref[...] = qk(k0 + sub)
            bias = bias_of(c, stored)
            half(sa_ref, bias[:, :sub], k0)
            sa_ref[...] = qk(pl.multiple_of(jnp.minimum(c + 1, nkc - 1) * kc, kc))
            half(sb_ref, bias[:, sub:], k0 + sub)
            return carry

        lax.fori_loop(0, nkc, body, 0)

    def finish(stored):
        if stored:
            tie_bias_pass()
        att_pass(stored, ms_bound)
        lmin = jnp.min(jnp.sum(run_ref[0], axis=1, keepdims=True))
        for h in range(1, DSA_HEADS):
            lmin = jnp.minimum(lmin, jnp.min(jnp.sum(run_ref[h], axis=1, keepdims=True)))

        def redo():
            att_pass(stored, max_pass(stored))
            return 0

        lax.cond(lmin >= 1e-26, lambda: 0, redo)
        return 0

    lax.cond(any_tie, lambda: finish(True), lambda: finish(False))
    out = jnp.zeros((tq, DSA_WIDTH), F32)
    for h in range(DSA_HEADS):
        m = (lane_w >= h * DSA_HEAD_DIM) & (lane_w < (h + 1) * DSA_HEAD_DIM)
        out = out + jnp.where(m, acc_ref[h * tq:(h + 1) * tq, :] / jnp.sum(run_ref[h], axis=1, keepdims=True), 0.0)
    o_ref[...] = _rms(out, ng_ref[...]).astype(o_ref.dtype)


def _dsa(p, norm_g, batch, seq, tq=512, kc=512, top_r=12):
    n_sel = min(DSA_TOPK, seq // 4)
    kc = min(kc, seq)
    sub = kc // 2
    r3 = lambda a: a.reshape(batch, seq, a.shape[-1])
    qblk = lambda w: pl.BlockSpec((None, tq, w), lambda b, i: (b, i, 0))
    kfull = lambda w: pl.BlockSpec((None, seq, w), lambda b, i: (b, 0, 0), pipeline_mode=pl.Buffered(1))
    out = pl.pallas_call(
        functools.partial(_dsa_kernel, tq=tq, kc=kc, sub=sub, n_sel=n_sel, top_r=top_r),
        grid=(batch, seq // tq),
        in_specs=[qblk(DSA_WIDTH), kfull(DSA_WIDTH), qblk(LANES), qblk(DSA_WIDTH),
                  kfull(DSA_WIDTH), kfull(DSA_WIDTH), pl.BlockSpec((1, DSA_WIDTH), lambda b, i: (0, 0))],
        out_specs=qblk(DSA_WIDTH),
        out_shape=jax.ShapeDtypeStruct((batch, seq, DSA_WIDTH), BF16),
        scratch_shapes=[
            pltpu.VMEM((DSA_IDX_HEADS * tq, DSA_WIDTH), BF16),
            pltpu.VMEM((DSA_HEADS * tq, DSA_WIDTH), BF16),
            pltpu.VMEM((seq // kc, tq, kc), F32),
            pltpu.VMEM((DSA_HEADS * tq, DSA_WIDTH), F32),
            pltpu.VMEM((DSA_HEADS, tq, LANES), F32),
            pltpu.VMEM((top_r, tq, LANES), I32),
            pltpu.VMEM((DSA_HEADS * tq, sub), F32),
            pltpu.VMEM((DSA_HEADS * tq, sub), F32),
            pltpu.VMEM((SUBLANES, LANES), F32),
            pltpu.VMEM((tq, LANES), F32),
        ],
        compiler_params=_cparams("arbitrary", "arbitrary"),
        name="dsa",
    )(r3(p["iq"]), r3(p["ik"]), r3(p["iw"]), r3(p["dq"]), r3(p["dk"]), r3(p["dv"]),
      norm_g.reshape(1, DSA_WIDTH))
    return out.reshape(batch * seq, DSA_WIDTH)


def _s5_tables(a_re, a_im, log_step, b_re, b_im, c_re, c_im):
    hp = lax.Precision.HIGHEST
    s = S5_CHUNK
    dt = jnp.exp(log_step)[:, None]
    mag = jnp.exp(a_re * dt)
    lam_re, lam_im = mag * jnp.cos(a_im * dt), mag * jnp.sin(a_im * dt)
    den = a_re * a_re + a_im * a_im
    nr, ni = lam_re - 1.0, lam_im
    coef_re = (nr * a_re + ni * a_im) / den
    coef_im = (ni * a_re - nr * a_im) / den
    bb_re = coef_re[..., None] * b_re - coef_im[..., None] * b_im
    bb_im = coef_re[..., None] * b_im + coef_im[..., None] * b_re
    j = jnp.arange(s + 1, dtype=F32)[:, None, None]
    pmag = jnp.exp(a_re * dt * j)
    pw_re, pw_im = pmag * jnp.cos(a_im * dt * j), pmag * jnp.sin(a_im * dt * j)
    e_re = c_re[None] * pw_re[:, :, None, :] - c_im[None] * pw_im[:, :, None, :]
    e_im = c_re[None] * pw_im[:, :, None, :] + c_im[None] * pw_re[:, :, None, :]
    kk = (jnp.einsum("jgcp,gpd->jgcd", e_re, bb_re, precision=hp)
          - jnp.einsum("jgcp,gpd->jgcd", e_im, bb_im, precision=hp))
    lag = np.arange(s)[None, :] - np.arange(s)[:, None]
    toe = jnp.where((lag >= 0)[:, :, None, None, None], kk[np.clip(lag, 0, s)], 0.0)
    rp_re, rp_im = pw_re[s - 1 - np.arange(s)], pw_im[s - 1 - np.arange(s)]
    x_re = rp_re[..., None] * bb_re[None] - rp_im[..., None] * bb_im[None]
    x_im = rp_re[..., None] * bb_im[None] + rp_im[..., None] * bb_re[None]
    f_re, f_im = e_re[1:], e_im[1:]
    sc_w = s * S5_GROUP_CH
    g_ids = jnp.arange(S5_TILE_G)[:, None, None]

    def placement(inner, width):
        a = jnp.arange(width // S5_TILE_G)[None, :, None]
        dst = (a // inner) * (S5_TILE_G * inner) + g_ids * inner + a % inner
        return (dst == jnp.arange(width)[None, None, :]).astype(F32)

    p_row = placement(S5_GROUP_CH, S5_ROW_W)
    p_state = placement(S5_STATE, S5_STATE_W)
    per_tile = lambda a: a.astype(BF16).astype(F32).reshape((S5_TILES, S5_TILE_G) + a.shape[1:])

    def place(tab, p_in, p_out):
        t = jnp.einsum("qgab,gbc->qgac", per_tile(tab), p_out)
        return jnp.einsum("gar,qgac->qrc", p_in, t).astype(BF16)

    g_toe = toe.transpose(2, 0, 4, 1, 3).reshape(S5_GROUPS, sc_w, sc_w)
    g_w1 = lambda x: x.transpose(1, 0, 3, 2).reshape(S5_GROUPS, sc_w, S5_STATE)
    g_w2 = lambda f: f.transpose(1, 3, 0, 2).reshape(S5_GROUPS, S5_STATE, sc_w)
    tabs = dict(toe=place(g_toe, p_row, p_row),
                w1_re=place(g_w1(x_re), p_row, p_state), w1_im=place(g_w1(x_im), p_row, p_state),
                w2_re=place(g_w2(f_re), p_state, p_row), w2_im=place(g_w2(-f_im), p_state, p_row))
    tabs["a_re"], tabs["a_im"] = pw_re[s].reshape(-1), pw_im[s].reshape(-1)
    return tabs


def _s5_state_in_kernel(u_ref, wr_ref, wi_ref, xr_ref, xi_ref):
    u = u_ref[...].astype(BF16)
    xr_ref[...] = jnp.dot(u, wr_ref[...], preferred_element_type=F32)
    xi_ref[...] = jnp.dot(u, wi_ref[...], preferred_element_type=F32)


def _s5_scan_kernel(xr_ref, xi_ref, ar_ref, ai_ref, hr_ref, hi_ref, cr_ref, ci_ref, *, steps):
    @pl.when(pl.program_id(0) == 0)
    def _():
        cr_ref[...] = jnp.zeros_like(cr_ref)
        ci_ref[...] = jnp.zeros_like(ci_ref)

    ar, ai = ar_ref[...], ai_ref[...]

    def body(n, carry):
        hr, hi = carry
        hr_ref[n] = hr
        hi_ref[n] = hi
        return ar * hr - ai * hi + xr_ref[n], ar * hi + ai * hr + xi_ref[n]

    hr, hi = lax.fori_loop(0, steps, body, (cr_ref[...], ci_ref[...]))
    cr_ref[...] = hr
    ci_ref[...] = hi


def _s5_out_kernel(u_ref, toe_ref, hr_ref, hi_ref, w2r_ref, w2i_ref, y_ref):
    y = jnp.dot(u_ref[...].astype(BF16), toe_ref[...], preferred_element_type=F32)
    y = y + jnp.dot(hr_ref[...].astype(BF16), w2r_ref[...], preferred_element_type=F32)
    y = y + jnp.dot(hi_ref[...].astype(BF16), w2i_ref[...], preferred_element_type=F32)
    y_ref[...] = y


def _s5_post_kernel(y_ref, u0_ref, u1_ref, u2_ref, d_ref, gw_ref, gb_ref, ng_ref, o_ref, yt_ref):
    n_chunks = yt_ref.shape[1] // S5_CHUNK
    for q in range(S5_TILES):
        for s in range(S5_CHUNK):
            yt_ref[q, pl.ds(s, n_chunks, stride=S5_CHUNK), :] = y_ref[q, :, s * LANES:(s + 1) * LANES]
    y = jnp.concatenate([yt_ref[q] for q in range(S5_TILES)], axis=1)
    u = jnp.concatenate([u0_ref[...], u1_ref[...], u2_ref[...]], axis=1)
    z = y + d_ref[...] * u
    z = 0.5 * z * (1.0 + jnp.tanh(math.sqrt(2.0 / math.pi) * (z + 0.044715 * (z * z * z))))
    gate = _sigmoid(jnp.dot(z.astype(BF16), gw_ref[...], preferred_element_type=F32) + gb_ref[...])
    o_ref[...] = _rms(z * gate, ng_ref[...]).astype(o_ref.dtype)


def _s5(us, xq, tabs, d, glu_w, glu_b, norm_g, batch, seq, tm=512, rb=512):
    s = S5_CHUNK
    nch = seq // s
    rows = nch * batch
    rb = min(rb, rows)
    gp = S5_GROUPS * S5_STATE
    row_in = pl.BlockSpec((None, rb, S5_ROW_W), lambda q, i: (q, i, 0))
    wspec = lambda a: pl.BlockSpec((None,) + a.shape[1:], lambda q, i: (q, 0, 0))
    colblk = pl.BlockSpec((rb, S5_STATE_W), lambda q, i: (i, q))
    xr, xi = pl.pallas_call(
        _s5_state_in_kernel,
        grid=(S5_TILES, rows // rb),
        in_specs=[row_in, wspec(tabs["w1_re"]), wspec(tabs["w1_im"])],
        out_specs=[colblk, colblk],
        out_shape=[jax.ShapeDtypeStruct((rows, gp), F32)] * 2,
        compiler_params=_cparams("arbitrary", "arbitrary"),
        name="s5_state_in",
    )(xq, tabs["w1_re"], tabs["w1_im"])
    parts = SUBLANES // batch
    wl = gp // parts
    slab = lambda a: a.reshape(batch, nch, gp).transpose(1, 0, 2).reshape(nch, SUBLANES, wl)
    unslab = lambda a: a.reshape(nch, batch, gp).transpose(1, 0, 2).reshape(rows, gp)
    coef = lambda a: jnp.tile(a.reshape(parts, wl), (batch, 1))
    steps = min(64, nch)
    sblk = pl.BlockSpec((steps, SUBLANES, wl), lambda i: (i, 0, 0))
    cblk = pl.BlockSpec((SUBLANES, wl), lambda i: (0, 0))
    hr, hi = pl.pallas_call(
        functools.partial(_s5_scan_kernel, steps=steps),
        grid=(nch // steps,),
        in_specs=[sblk, sblk, cblk, cblk],
        out_specs=[sblk, sblk],
        out_shape=[jax.ShapeDtypeStruct((nch, SUBLANES, wl), F32)] * 2,
        scratch_shapes=[pltpu.VMEM((SUBLANES, wl), F32)] * 2,
        compiler_params=_cparams("arbitrary"),
        name="s5_scan",
    )(slab(xr), slab(xi), coef(tabs["a_re"]), coef(tabs["a_im"]))
    y = pl.pallas_call(
        _s5_out_kernel,
        grid=(S5_TILES, rows // rb),
        in_specs=[row_in, wspec(tabs["toe"]), colblk, colblk, wspec(tabs["w2_re"]), wspec(tabs["w2_im"])],
        out_specs=row_in,
        out_shape=jax.ShapeDtypeStruct((S5_TILES, rows, S5_ROW_W), F32),
        compiler_params=_cparams("arbitrary", "arbitrary"),
        name="s5_out",
    )(xq, tabs["toe"], unslab(hr), unslab(hi), tabs["w2_re"], tabs["w2_im"])
    t = batch * seq
    tm = min(tm, t)
    tok = lambda w: pl.BlockSpec((tm, w), lambda i: (i, 0))
    vec = pl.BlockSpec((1, S5_WIDTH), lambda i: (0, 0))
    return pl.pallas_call(
        _s5_post_kernel,
        grid=(t // tm,),
        in_specs=[pl.BlockSpec((S5_TILES, tm // s, S5_ROW_W), lambda i: (0, i, 0)), tok(LANES), tok(LANES),
                  tok(LANES), vec, pl.BlockSpec((S5_WIDTH, S5_WIDTH), lambda i: (0, 0)), vec, vec],
        out_specs=tok(S5_WIDTH),
        out_shape=jax.ShapeDtypeStruct((t, S5_WIDTH), BF16),
        scratch_shapes=[pltpu.VMEM((S5_TILES, tm, LANES), F32)],
        compiler_params=_cparams("arbitrary"),
        name="s5_post",
    )(y, us[0], us[1], us[2], d.reshape(1, -1), glu_w.astype(BF16), glu_b.reshape(1, -1),
      norm_g.reshape(1, -1))


def _mix_out_kernel(og_ref, od_ref, os_ref, wg_ref, wd_ref, ws_ref, x_ref, gt_ref, n2_ref, sc_ref,
                    sh_ref, rw1_ref, rw2_ref, rb_ref, x1_ref, h2_ref, route_ref, cnt_ref, carry_ref, *, tm):
    @pl.when(pl.program_id(0) == 0)
    def _():
        carry_ref[...] = jnp.zeros_like(carry_ref)

    mix = jnp.dot(og_ref[...], wg_ref[...], preferred_element_type=F32)
    mix = mix + jnp.dot(od_ref[...], wd_ref[...], preferred_element_type=F32)
    mix = mix + jnp.dot(os_ref[...], ws_ref[...], preferred_element_type=F32)
    x1 = x_ref[...] + gt_ref[...] * mix
    x1_ref[...] = x1
    h2 = _rms(x1, n2_ref[...]) * (1.0 + sc_ref[...]) + sh_ref[...]
    h2_ref[...] = h2
    ha = h2.astype(BF16)
    hb = (h2 - ha.astype(F32)).astype(BF16)
    logit = (jnp.dot(ha, rw1_ref[...], preferred_element_type=F32)
             + jnp.dot(ha, rw2_ref[...], preferred_element_type=F32)
             + jnp.dot(hb, rw1_ref[...], preferred_element_type=F32)) + rb_ref[...]
    lane = lax.broadcasted_iota(I32, (tm, LANES), 1)
    lanef = lane.astype(F32)
    ninf = jnp.float32(-jnp.inf)
    big = jnp.float32(LANES)
    gl = jnp.where(lane < MOE_GROUPS, logit, ninf)
    gmax = jnp.max(gl, axis=1, keepdims=True)
    grp_p = 1.0 / jnp.sum(jnp.exp(gl - gmax), axis=1, keepdims=True)
    grp_i = jnp.min(jnp.where(gl == gmax, lanef, big), axis=1, keepdims=True)
    lo = MOE_GROUPS + MOE_EXPERTS_PER_GROUP * grp_i
    el = jnp.where((lanef >= lo) & (lanef < lo + MOE_EXPERTS_PER_GROUP), logit, ninf)
    l1 = jnp.max(el, axis=1, keepdims=True)
    i1 = jnp.min(jnp.where(el == l1, lanef, big), axis=1, keepdims=True)
    el2 = jnp.where(lanef == i1, ninf, el)
    l2 = jnp.max(el2, axis=1, keepdims=True)
    i2 = jnp.min(jnp.where(el2 == l2, lanef, big), axis=1, keepdims=True)
    e21 = jnp.exp(l2 - l1)
    w1 = grp_p / (1.0 + e21)
    w2 = grp_p * e21 / (1.0 + e21)
    e1 = i1 - MOE_GROUPS
    e2 = i2 - MOE_GROUPS
    oh1 = lanef == e1
    oh2 = lanef == e2
    hot = jnp.where(oh1 | oh2, 1.0, 0.0)
    ri = lax.broadcasted_iota(I32, (tm, tm), 0)
    ci = lax.broadcasted_iota(I32, (tm, tm), 1)
    before = (ci < ri).astype(BF16)
    prior = carry_ref[...] + jnp.dot(before, hot.astype(BF16), preferred_element_type=F32)
    r1 = jnp.sum(jnp.where(oh1, prior, 0.0), axis=1, keepdims=True)
    r2 = jnp.sum(jnp.where(oh2, prior, 0.0), axis=1, keepdims=True)
    total = carry_ref[...] + jnp.sum(hot, axis=0, keepdims=True)
    carry_ref[...] = total
    cnt_ref[...] = jnp.broadcast_to(total, cnt_ref.shape)
    route = jnp.where(lane == 0, e1, jnp.where(lane == 1, e2, jnp.where(lane == 2, w1, jnp.where(
        lane == 3, w2, jnp.where(lane == 4, r1, jnp.where(lane == 5, r2, 0.0))))))
    route_ref[...] = route


def _mix_out(o_gla, o_dsa, o_s5, w_out, x2, gt1, norm2_g, sc2, sh2, rgw, rgb, rew, reb, seq, tm=512):
    t, d = x2.shape
    per_b = seq // tm
    wg = jnp.pad(w_out[:GLA_HEADS * GLA_DV].reshape(GLA_HEADS, GLA_DV, d),
                 ((0, 0), (0, HEAD_PAD - GLA_DV), (0, 0))).reshape(GLA_W, d).astype(BF16)
    wd = w_out[GLA_HEADS * GLA_DV:GLA_HEADS * GLA_DV + DSA_WIDTH].astype(BF16)
    ws = w_out[GLA_HEADS * GLA_DV + DSA_WIDTH:].astype(BF16)
    rw = _pad_cols(jnp.concatenate([rgw, rew], axis=1), LANES)
    rb = _pad_cols(jnp.concatenate([rgb, reb]).reshape(1, -1), LANES)
    rw1 = rw.astype(BF16)
    rw2 = (rw - rw1.astype(F32)).astype(BF16)
    tok = lambda w: pl.BlockSpec((tm, w), lambda i: (i, 0))
    full = lambda a: pl.BlockSpec(a.shape, lambda i: (0, 0))
    perb = pl.BlockSpec((None, 1, d), lambda i: (i // per_b, 0, 0))
    vec = pl.BlockSpec((1, d), lambda i: (0, 0))
    return pl.pallas_call(
        functools.partial(_mix_out_kernel, tm=tm),
        grid=(t // tm,),
        in_specs=[tok(GLA_W), tok(DSA_WIDTH), tok(S5_WIDTH), full(wg), full(wd), full(ws), tok(d),
                  perb, vec, perb, perb, full(rw1), full(rw2), full(rb)],
        out_specs=[tok(d), tok(d), tok(LANES), pl.BlockSpec((SUBLANES, LANES), lambda i: (0, 0))],
        out_shape=[jax.ShapeDtypeStruct((t, d), F32), jax.ShapeDtypeStruct((t, d), F32),
                   jax.ShapeDtypeStruct((t, LANES), F32), jax.ShapeDtypeStruct((SUBLANES, LANES), F32)],
        scratch_shapes=[pltpu.VMEM((1, LANES), F32)],
        compiler_params=_cparams("arbitrary"),
        name="mix_out",
    )(o_gla, o_dsa, o_s5, wg, wd, ws, x2, gt1[:, None, :], norm2_g.reshape(1, d), sc2[:, None, :],
      sh2[:, None, :], rw1, rw2, rb)


def _row(ref, r):
    return ref.at[pl.ds(r, 1), :]


def _dispatch_kernel(dest_ref, h_ref, zero_ref, out_ref, sem, *, td):
    del zero_ref

    def start(g, c):
        for j in range(SUBLANES):
            r = g * SUBLANES + j
            pltpu.make_async_copy(_row(h_ref, r), _row(out_ref, dest_ref[0, r]), sem).start()
            pltpu.make_async_copy(_row(h_ref, r), _row(out_ref, dest_ref[1, r]), sem).start()
        return c

    lax.fori_loop(0, td // SUBLANES, start, 0)
    for _ in range(2):
        pltpu.make_async_copy(h_ref, out_ref.at[pl.ds(0, td), :], sem).wait()


def _dispatch(h2, dest, n_rows, td=256):
    t, d = h2.shape
    dest3 = dest.reshape(2, t // td, td).transpose(1, 0, 2)
    zeros = jnp.zeros((n_rows, d), F32)
    return pl.pallas_call(
        functools.partial(_dispatch_kernel, td=td),
        grid=(t // td,),
        in_specs=[pl.BlockSpec((None, 2, td), lambda i: (i, 0, 0), memory_space=pltpu.SMEM),
                  pl.BlockSpec((td, d), lambda i: (i, 0)),
                  pl.BlockSpec(memory_space=pl.ANY)],
        out_specs=pl.BlockSpec(memory_space=pl.ANY),
        out_shape=jax.ShapeDtypeStruct((n_rows, d), F32),
        scratch_shapes=[pltpu.SemaphoreType.DMA(())],
        input_output_aliases={2: 0},
        compiler_params=_cparams("arbitrary"),
        name="moe_dispatch",
    )(dest3, h2, zeros)


def _expert_kernel(be_ref, nb_ref, x_ref, wg_ref, wu_ref, wd_ref, o_ref):
    del be_ref
    i = pl.program_id(0)

    @pl.when(i < nb_ref[0])
    def _():
        xb = x_ref[...].astype(BF16)
        g = jnp.dot(xb, wg_ref[...].astype(BF16), preferred_element_type=F32)
        u = jnp.dot(xb, wu_ref[...].astype(BF16), preferred_element_type=F32)
        hid = (g * _sigmoid(g) * u).astype(BF16)
        o_ref[...] = jnp.dot(hid, wd_ref[...].astype(BF16), preferred_element_type=F32)

    @pl.when(i >= nb_ref[0])
    def _():
        o_ref[...] = jnp.zeros_like(o_ref)


def _experts(xs, blk_e, n_used, wg, wu, wd, layer):
    n_rows = xs.shape[0]
    nb = n_rows // MOE_BM
    d, hdim = wg.shape[-2:]
    rows = pl.BlockSpec((MOE_BM, d), lambda i, be, nu: (i, 0))
    wspec = lambda a, b: pl.BlockSpec((None, None, a, b), lambda i, be, nu: (layer, be[i], 0, 0))
    return pl.pallas_call(
        _expert_kernel,
        grid_spec=pltpu.PrefetchScalarGridSpec(
            num_scalar_prefetch=2,
            grid=(nb,),
            in_specs=[rows, wspec(d, hdim), wspec(d, hdim), wspec(hdim, d)],
            out_specs=rows,
        ),
        out_shape=jax.ShapeDtypeStruct((n_rows, d), F32),
        compiler_params=_cparams("arbitrary"),
        name="moe_experts",
    )(blk_e, n_used, xs, wg, wu, wd)


def _combine_kernel(dest_ref, x_ref, route_ref, gt_ref, fg_ref, eo_ref, o_ref, rows_ref, sem, *, tc, final):
    def start(g, c):
        for j in range(SUBLANES):
            r = g * SUBLANES + j
            pltpu.make_async_copy(_row(eo_ref, dest_ref[0, r]), _row(rows_ref.at[0], r), sem).start()
            pltpu.make_async_copy(_row(eo_ref, dest_ref[1, r]), _row(rows_ref.at[1], r), sem).start()
        return c

    lax.fori_loop(0, tc // SUBLANES, start, 0)
    for which in range(2):
        pltpu.make_async_copy(eo_ref.at[pl.ds(0, tc), :], rows_ref.at[which], sem).wait()
    route = route_ref[...]
    lane = lax.broadcasted_iota(I32, route.shape, 1)
    w1 = jnp.sum(jnp.where(lane == 2, route, 0.0), axis=1, keepdims=True)
    w2 = jnp.sum(jnp.where(lane == 3, route, 0.0), axis=1, keepdims=True)
    y = rows_ref[0] * w1 + rows_ref[1] * w2
    x = x_ref[...] + gt_ref[...] * y
    if final:
        x = _rms(x, fg_ref[...])
    o_ref[...] = x


def _combine(x1, route, dest, eo, gt2, final_g, seq, final, tc=256):
    t, d = x1.shape
    per_b = seq // tc
    dest3 = dest.reshape(2, t // tc, tc).transpose(1, 0, 2)
    return pl.pallas_call(
        functools.partial(_combine_kernel, tc=tc, final=final),
        grid=(t // tc,),
        in_specs=[pl.BlockSpec((None, 2, tc), lambda i: (i, 0, 0), memory_space=pltpu.SMEM),
                  pl.BlockSpec((tc, d), lambda i: (i, 0)),
                  pl.BlockSpec((tc, LANES), lambda i: (i, 0)),
                  pl.BlockSpec((None, 1, d), lambda i: (i // per_b, 0, 0)),
                  pl.BlockSpec((1, d), lambda i: (0, 0)),
                  pl.BlockSpec(memory_space=pl.ANY)],
        out_specs=pl.BlockSpec((tc, d), lambda i: (i, 0)),
        out_shape=jax.ShapeDtypeStruct((t, d), F32),
        scratch_shapes=[pltpu.VMEM((2, tc, d), F32), pltpu.SemaphoreType.DMA(())],
        compiler_params=_cparams("arbitrary"),
        name="moe_combine",
    )(dest3, x1, route, gt2[:, None, :], final_g.reshape(1, d), eo)


def _moe_plan(route, counts, t):
    e = route[:, 0:2].astype(I32).T
    rank = route[:, 4:6].astype(I32).T
    cnt = counts[0, :MOE_EXPERTS].astype(I32)
    padded = (cnt + MOE_BM - 1) // MOE_BM * MOE_BM
    pend = jnp.cumsum(padded)
    pstart = pend - padded
    ids = jnp.arange(MOE_EXPERTS, dtype=I32)
    dest = rank + jnp.sum(jnp.where(e[..., None] == ids, pstart, 0), axis=-1)
    nb = (2 * t) // MOE_BM + MOE_EXPERTS
    n_used = (pend[-1] // MOE_BM).astype(I32)
    first_row = jnp.minimum(jnp.arange(nb, dtype=I32), n_used - 1) * MOE_BM
    blk_e = jnp.sum((pend[None, :] <= first_row[:, None]).astype(I32), axis=1)
    blk_e = jnp.minimum(blk_e, MOE_EXPERTS - 1)
    return dest, blk_e, n_used.reshape(1), nb * MOE_BM


def kernel(x, c, ada_w, ada_b, norm1_g, w_in, gla_gate_w, gla_gate_b, gla_norm_g, dsa_norm_g, s5_a_re, s5_a_im, s5_log_step, s5_b_re, s5_b_im, s5_c_re, s5_c_im, s5_d, s5_glu_w, s5_glu_b, s5_norm_g, w_out, norm2_g, router_grp_w, router_grp_b, router_exp_w, router_exp_b, exp_w_gate, exp_w_up, exp_w_down, final_norm_g):
    batch, seq, d = x.shape
    depth = ada_w.shape[0]
    t = batch * seq
    mod = _ada_mod(c, ada_w, ada_b)
    x2 = x.reshape(t, d)
    for l in range(depth):
        sh1, sc1, gt1, sh2, sc2, gt2 = (mod[l, j] for j in range(6))
        p = _in_proj(x2, norm1_g[l], sc1, sh1, _layout_w_in(w_in[l]), seq)
        o_gla = _gla(p, gla_gate_w[l], gla_gate_b[l], gla_norm_g[l], batch, seq)
        o_dsa = _dsa(p, dsa_norm_g[l], batch, seq)
        tabs = _s5_tables(s5_a_re[l], s5_a_im[l], s5_log_step[l], s5_b_re[l], s5_b_im[l],
                          s5_c_re[l], s5_c_im[l])
        o_s5 = _s5((p["su0"], p["su1"], p["su2"]), p["sx"], tabs, s5_d[l], s5_glu_w[l], s5_glu_b[l], s5_norm_g[l], batch, seq)
        x1, h2, route, counts = _mix_out(o_gla, o_dsa, o_s5, w_out[l], x2, gt1, norm2_g[l], sc2, sh2,
                                         router_grp_w[l], router_grp_b[l], router_exp_w[l],
                                         router_exp_b[l], seq)
        dest, blk_e, n_used, n_rows = _moe_plan(route, counts, t)
        xs = _dispatch(h2, dest, n_rows)
        eo = _experts(xs, blk_e, n_used, exp_w_gate, exp_w_up, exp_w_down, l)
        x2 = _combine(x1, route, dest, eo, gt2, final_norm_g, seq, final=(l == depth - 1))
    return x2.reshape(batch, seq, d)
```

```python
import functools
import math

import jax
import jax.numpy as jnp
import numpy as np
from jax import lax
from jax.experimental import pallas as pl
from jax.experimental.pallas import tpu as pltpu

F32 = jnp.float32
BF16 = jnp.bfloat16
I32 = jnp.int32

D_MODEL = 1024
GLA_HEADS = 4
GLA_DV = 96
GLA_DK = 48
GLA_RANK = 16
GLA_TAU = 16.0
GLA_CHUNK = 64
DSA_HEADS = 4
DSA_HEAD_DIM = 64
DSA_WIDTH = DSA_HEADS * DSA_HEAD_DIM
DSA_IDX_HEADS = 8
DSA_IDX_DIM = 32
DSA_TOPK = 256
S5_GROUPS = 24
S5_GROUP_CH = 16
S5_STATE = 64
S5_WIDTH = S5_GROUPS * S5_GROUP_CH
MOE_GROUPS = 4
MOE_EXPERTS_PER_GROUP = 8
MOE_EXPERTS = MOE_GROUPS * MOE_EXPERTS_PER_GROUP
MOE_HIDDEN = 512
RMS_EPS = 1e-6
IN_SIZES = (192, 192, 384, 16, 384, 256, 256, 256, 256, 32, 8, 384)

LANES = 128
SUBLANES = 8
VMEM_LIMIT = 56 * 1024 * 1024

HEAD_PAD = LANES
GLA_W = GLA_HEADS * HEAD_PAD
S5_CHUNK = 16
S5_TILE_G = LANES // S5_GROUP_CH
S5_TILES = S5_GROUPS // S5_TILE_G
S5_ROW_W = S5_CHUNK * LANES
S5_STATE_W = S5_TILE_G * S5_STATE
MOE_BM = 512
INT_MIN = -2 ** 31


def _cparams(*sem):
    return pltpu.CompilerParams(dimension_semantics=tuple(sem), vmem_limit_bytes=VMEM_LIMIT)


def _sigmoid(x):
    return 0.5 * (jnp.tanh(0.5 * x) + 1.0)


def _rms(x, g, n=None):
    n = x.shape[-1] if n is None else n
    ms = jnp.sum(x * x, axis=-1, keepdims=True) * (1.0 / n)
    return x * lax.rsqrt(ms + RMS_EPS) * g


def _ada_kernel(c_ref, w_ref, b_ref, o_ref):
    c = c_ref[...]
    cond = (c * _sigmoid(c)).astype(BF16)
    o_ref[...] = jnp.dot(cond, w_ref[...].astype(BF16), preferred_element_type=F32) + b_ref[...]


def _ada_mod(c, ada_w, ada_b):
    depth, d, d6 = ada_w.shape
    nblk = d6 // d
    bp = -(-c.shape[0] // SUBLANES) * SUBLANES
    cp = jnp.pad(c, ((0, bp - c.shape[0]), (0, 0)))
    out = pl.pallas_call(
        _ada_kernel,
        grid=(depth, nblk),
        in_specs=[
            pl.BlockSpec((bp, d), lambda l, j: (0, 0)),
            pl.BlockSpec((None, d, d), lambda l, j: (l, 0, j)),
            pl.BlockSpec((None, None, 1, d), lambda l, j: (l, j, 0, 0)),
        ],
        out_specs=pl.BlockSpec((None, None, bp, d), lambda l, j: (l, j, 0, 0)),
        out_shape=jax.ShapeDtypeStruct((depth, nblk, bp, d), F32),
        compiler_params=_cparams("arbitrary", "arbitrary"),
        name="ada_mod",
    )(cp, ada_w, ada_b.reshape(depth, nblk, 1, d))
    return out[:, :, : c.shape[0], :]


PROJ_OUTS = (
    ("gq", GLA_W, BF16), ("gk", GLA_W, BF16), ("gv", GLA_W, BF16), ("gr", GLA_W, BF16),
    ("glr", LANES, BF16),
    ("dq", DSA_WIDTH, BF16), ("dk", DSA_WIDTH, BF16), ("dv", DSA_WIDTH, BF16),
    ("iq", DSA_IDX_HEADS * DSA_IDX_DIM, BF16), ("ik", DSA_IDX_HEADS * DSA_IDX_DIM, BF16),
    ("iw", LANES, F32), ("su0", LANES, F32), ("su1", LANES, F32), ("su2", LANES, F32),
)
PROJ_COLS = sum(w for _, w, _ in PROJ_OUTS)


def _pad_heads(w, heads, dim):
    d = w.shape[0]
    w = w.reshape(d, heads, dim)
    return jnp.pad(w, ((0, 0), (0, 0), (0, HEAD_PAD - dim))).reshape(d, heads * HEAD_PAD)


def _pad_cols(w, width):
    return jnp.pad(w, ((0, 0), (0, width - w.shape[1])))


def _layout_w_in(w_in):
    offs = np.cumsum((0,) + IN_SIZES)
    p = [w_in[:, offs[i]:offs[i + 1]] for i in range(len(IN_SIZES))]
    cols = [
        _pad_heads(p[0], GLA_HEADS, GLA_DK), _pad_heads(p[1], GLA_HEADS, GLA_DK),
        _pad_heads(p[2], GLA_HEADS, GLA_DV), _pad_heads(p[4], GLA_HEADS, GLA_DV),
        _pad_cols(p[3], LANES),
        p[5], p[6], p[7], p[8], jnp.tile(p[9], (1, DSA_IDX_HEADS)),
        _pad_cols(p[10], LANES), p[11],
    ]
    return jnp.concatenate(cols, axis=1).astype(BF16)


def _proj_kernel(x_ref, g_ref, sc_ref, sh_ref, w_ref, *o_refs):
    x = x_ref[...]
    h = _rms(x, g_ref[...]) * (1.0 + sc_ref[...]) + sh_ref[...]
    hb = h.astype(BF16)
    c0 = 0
    for o_ref, (_, width, _) in zip(o_refs, PROJ_OUTS):
        o_ref[...] = jnp.dot(hb, w_ref[:, c0:c0 + width], preferred_element_type=F32).astype(o_ref.dtype)
        c0 += width
    sx_ref = o_refs[len(PROJ_OUTS)]
    n_chunks = x.shape[0] // S5_CHUNK
    for q in range(S5_TILES):
        su_ref = o_refs[len(PROJ_OUTS) - S5_TILES + q]
        for s in range(S5_CHUNK):
            sx_ref[q, :, s * LANES:(s + 1) * LANES] = su_ref[pl.ds(s, n_chunks, stride=S5_CHUNK), :]


def _in_proj(x2, norm_g, sc, sh, w_lay, seq, tm=512):
    t, d = x2.shape
    per_b = seq // tm
    outs = pl.pallas_call(
        _proj_kernel,
        grid=(t // tm,),
        in_specs=[
            pl.BlockSpec((tm, d), lambda i: (i, 0)),
            pl.BlockSpec((1, d), lambda i: (0, 0)),
            pl.BlockSpec((None, 1, d), lambda i: (i // per_b, 0, 0)),
            pl.BlockSpec((None, 1, d), lambda i: (i // per_b, 0, 0)),
            pl.BlockSpec((d, PROJ_COLS), lambda i: (0, 0)),
        ],
        out_specs=[pl.BlockSpec((tm, w), lambda i: (i, 0)) for _, w, _ in PROJ_OUTS]
        + [pl.BlockSpec((S5_TILES, tm // S5_CHUNK, S5_ROW_W), lambda i: (0, i, 0))],
        out_shape=[jax.ShapeDtypeStruct((t, w), dt) for _, w, dt in PROJ_OUTS]
        + [jax.ShapeDtypeStruct((S5_TILES, t // S5_CHUNK, S5_ROW_W), F32)],
        compiler_params=_cparams("arbitrary"),
        name="in_proj",
    )(x2, norm_g.reshape(1, d), sc[:, None, :], sh[:, None, :], w_lay)
    p = {name: o for (name, _, _), o in zip(PROJ_OUTS, outs)}
    p["sx"] = outs[-1]
    return p


def _gla_kernel(q_ref, k_ref, v_ref, r_ref, lr_ref, gw_ref, gb_ref, ng_ref, o_ref, s_ref, *, rows):
    @pl.when(pl.program_id(1) == 0)
    def _():
        s_ref[...] = jnp.zeros_like(s_ref)

    c = GLA_CHUNK
    gl = jnp.dot(lr_ref[...], gw_ref[...], preferred_element_type=F32) + gb_ref[...]
    g = (jnp.minimum(gl, 0.0) - jnp.log1p(jnp.exp(-jnp.abs(gl)))) * (1.0 / GLA_TAU)
    g1 = g.astype(BF16)
    rem = g - g1.astype(F32)
    g2 = rem.astype(BF16)
    g3 = (rem - g2.astype(F32)).astype(BF16)
    ri = lax.broadcasted_iota(I32, (c, c), 0)
    ci = lax.broadcasted_iota(I32, (c, c), 1)
    causal = ri >= ci
    tri = causal.astype(BF16)
    ng = ng_ref[...]
    for ch in range(rows // c):
        rs = slice(ch * c, (ch + 1) * c)
        bcum = (jnp.dot(tri, g1[rs], preferred_element_type=F32)
                + jnp.dot(tri, g2[rs], preferred_element_type=F32)
                + jnp.dot(tri, g3[rs], preferred_element_type=F32))
        blast = bcum[c - 1:c, :]
        e_pos = jnp.exp(bcum)
        kf = k_ref[rs, :].astype(F32)
        qd = (q_ref[rs, :].astype(F32) * (GLA_DK ** -0.5) * e_pos).astype(BF16)
        kd = (kf * jnp.exp(-bcum)).astype(BF16)
        ku = (kf * jnp.exp(blast - bcum)).astype(BF16)
        dec = jnp.exp(blast)
        for h in range(GLA_HEADS):
            hs = slice(h * HEAD_PAD, (h + 1) * HEAD_PAD)
            vh = v_ref[rs, hs]
            att = lax.dot_general(qd[:, hs], kd[:, hs], (((1,), (1,)), ((), ())),
                                  preferred_element_type=F32)
            att = jnp.where(causal, att, 0.0).astype(BF16)
            st = s_ref[h]
            o = jnp.dot(att, vh, preferred_element_type=F32)
            o = o + lax.dot_general(qd[:, hs], st.astype(BF16), (((1,), (1,)), ((), ())),
                                    preferred_element_type=F32)
            upd = lax.dot_general(vh, ku[:, hs], (((0,), (0,)), ((), ())),
                                  preferred_element_type=F32)
            s_ref[h] = st * dec[:, hs] + upd
            y = _rms(o, ng[:, hs], GLA_DV)
            rr = r_ref[rs, hs].astype(F32)
            o_ref[rs, hs] = (y * (rr * _sigmoid(rr))).astype(o_ref.dtype)


def _gla(p, gate_w, gate_b, norm_g, batch, seq, rows=512):
    gw = _pad_cols(jnp.pad(_pad_heads(gate_w, GLA_HEADS, GLA_DK), ((0, LANES - GLA_RANK), (0, 0))), GLA_W)
    gb = _pad_heads(gate_b.reshape(1, -1), GLA_HEADS, GLA_DK)
    ng = _pad_heads(jnp.tile(norm_g.reshape(1, GLA_DV), (1, GLA_HEADS)), GLA_HEADS, GLA_DV)
    rows = min(rows, seq)
    blk = lambda w: pl.BlockSpec((None, rows, w), lambda b, j: (b, j, 0))
    full = lambda a: pl.BlockSpec(a.shape, lambda b, j: (0, 0))
    r3 = lambda a: a.reshape(batch, seq, a.shape[-1])
    out = pl.pallas_call(
        functools.partial(_gla_kernel, rows=rows),
        grid=(batch, seq // rows),
        in_specs=[blk(GLA_W), blk(GLA_W), blk(GLA_W), blk(GLA_W), blk(LANES),
                  full(gw), full(gb), full(ng)],
        out_specs=blk(GLA_W),
        out_shape=jax.ShapeDtypeStruct((batch, seq, GLA_W), BF16),
        scratch_shapes=[pltpu.VMEM((GLA_HEADS, HEAD_PAD, HEAD_PAD), F32)],
        compiler_params=_cparams("arbitrary", "arbitrary"),
        name="gla",
    )(r3(p["gq"]), r3(p["gk"]), r3(p["gv"]), r3(p["gr"]), r3(p["glr"]), gw.astype(BF16), gb, ng)
    return out.reshape(batch * seq, GLA_W)


def _dsa_kernel(iq_ref, ik_ref, iw_ref, q_ref, k_ref, v_ref, ng_ref, o_ref,
                qs_ref, q4_ref, keys_ref, acc_ref, run_ref, top_ref, sa_ref, sb_ref, kmax_ref, topf_ref,
                *, tq, kc, sub, n_sel, top_r):
    assert kc == 2 * sub
    qi = pl.program_id(1)
    nkc = ((qi + 1) * tq + kc - 1) // kc
    lane_w = lax.broadcasted_iota(I32, (tq, DSA_WIDTH), 1)
    iq = iq_ref[...]
    for h in range(DSA_IDX_HEADS):
        m = (lane_w >= h * DSA_IDX_DIM) & (lane_w < (h + 1) * DSA_IDX_DIM)
        qs_ref[h * tq:(h + 1) * tq, :] = jnp.where(m, iq, jnp.zeros_like(iq))
    iw = iw_ref[...] * (DSA_IDX_HEADS ** -0.5 * DSA_IDX_DIM ** -0.5)
    lane = lax.broadcasted_iota(I32, (tq, LANES), 1)
    w_cols = [iw[:, h:h + 1] for h in range(DSA_IDX_HEADS)]
    qpos = qi * tq + lax.broadcasted_iota(I32, (tq, sub), 0)
    kcol = lax.broadcasted_iota(I32, (tq, sub), 1)
    nt = (((1,), (1,)), ((), ()))

    ninf = jnp.float32(-jnp.inf)

    def score_chunk(c):
        for u in range(kc // sub):
            k0 = pl.multiple_of(c * kc + u * sub, sub)
            lg = lax.dot_general(qs_ref[...], ik_ref[pl.ds(k0, sub), :], nt,
                                 preferred_element_type=F32)
            sc = jnp.zeros((tq, sub), F32)
            for h in range(DSA_IDX_HEADS):
                sc = sc + w_cols[h] * jnp.maximum(lg[h * tq:(h + 1) * tq, :], 0.0)
            sc = sc + 0.0
            keys_ref[c, :, u * sub:(u + 1) * sub] = jnp.where(kcol + k0 <= qpos, sc, ninf)

    topf_ref[...] = jnp.full(topf_ref.shape, ninf, F32)

    def insert_chunk(c):
        for rt in range(tq // SUBLANES):
            rs = slice(rt * SUBLANES, (rt + 1) * SUBLANES)
            ts = [topf_ref[r, rs, :] for r in range(top_r)]
            for j in range(kc // LANES):
                x = keys_ref[c, rs, j * LANES:(j + 1) * LANES]
                for r in range(top_r):
                    hi = jnp.maximum(ts[r], x)
                    x = jnp.minimum(ts[r], x)
                    ts[r] = hi
            for r in range(top_r):
                topf_ref[r, rs, :] = ts[r]

    score_chunk(0)

    def fused_body(c, carry):
        insert_chunk(c - 1)
        score_chunk(c)
        return carry

    lax.fori_loop(1, nkc, fused_body, 0)
    insert_chunk(nkc - 1)

    def to_key(x):
        bits = lax.bitcast_convert_type(x, I32)
        key = jnp.where(bits < 0, bits ^ jnp.int32(0x7FFFFFFF), bits)
        return jnp.where(x == ninf, jnp.int32(INT_MIN), key)

    def count_ge(cand):
        candb = jnp.broadcast_to(cand, (tq, LANES))

        def body(c, acc):
            blk = keys_ref[c]
            for j in range(kc // LANES):
                acc = acc + jnp.where(to_key(blk[:, j * LANES:(j + 1) * LANES]) >= candb, 1.0, 0.0)
            return acc

        acc = lax.fori_loop(0, nkc, body, jnp.zeros((tq, LANES), F32))
        return jnp.sum(acc, axis=1, keepdims=True)

    need = jnp.float32(n_sel)

    def select(count):
        t0 = jnp.full((tq, 1), INT_MIN, I32)
        zero = jnp.zeros((tq, 1), I32)
        t0 = jnp.where(count(zero) >= need, zero, t0)

        def bit_body(i, t):
            cand = t | (jnp.int32(1) << (30 - i))
            return jnp.where(count(cand) >= need, cand, t)

        t = lax.fori_loop(0, 31, bit_body, t0)
        t = jnp.maximum(t, jnp.int32(INT_MIN + 1))
        return t, count(t + 1), count(t)

    for r in range(top_r):
        top_ref[r] = to_key(topf_ref[r])

    def count_cand(cand):
        candb = jnp.broadcast_to(cand, (tq, LANES))
        acc = jnp.zeros((tq, LANES), F32)
        for r in range(top_r):
            acc = acc + jnp.where(top_ref[r] >= candb, 1.0, 0.0)
        return jnp.sum(acc, axis=1, keepdims=True)

    thr_c, n_gt_c, n_ge_c = select(count_cand)
    hidden = jnp.max(jnp.where(top_ref[top_r - 1] >= thr_c, 1.0, 0.0)) > 0.0
    thr, n_gt, n_ge = lax.cond(hidden, lambda: select(count_ge), lambda: (thr_c, n_gt_c, n_ge_c))
    n_tie = need - n_gt
    any_tie = jnp.max(n_ge) > need
    thr_bits = jnp.where(thr < 0, thr ^ jnp.int32(0x7FFFFFFF), thr)
    thr_f = jnp.where(thr == jnp.int32(INT_MIN + 1), jnp.float32(jnp.finfo(jnp.float32).min),
                      lax.bitcast_convert_type(thr_bits, F32))

    ui = lax.broadcasted_iota(I32, (kc, kc), 0)
    uj = lax.broadcasted_iota(I32, (kc, kc), 1)
    before = (ui < uj).astype(BF16)
    q = q_ref[...] * jnp.asarray(DSA_HEAD_DIM ** -0.5, BF16)
    for h in range(DSA_HEADS):
        m = (lane_w >= h * DSA_HEAD_DIM) & (lane_w < (h + 1) * DSA_HEAD_DIM)
        q4_ref[h * tq:(h + 1) * tq, :] = jnp.where(m, q, jnp.zeros_like(q))
    neg = jnp.float32(-1e30)

    def lane_fold(op, acc, s):
        for j in range(s.shape[1] // LANES):
            acc = op(acc, s[:, j * LANES:(j + 1) * LANES])
        return acc

    @pl.when(qi == 0)
    def _():
        kmax_ref[...] = jnp.zeros_like(kmax_ref)

    hl = lax.broadcasted_iota(I32, (DSA_WIDTH, LANES), 0) // DSA_HEAD_DIM
    head_of = (hl == lax.broadcasted_iota(I32, (DSA_WIDTH, LANES), 1)).astype(BF16)
    kb = k_ref[pl.ds(pl.multiple_of(qi * tq, tq), tq), :].astype(F32)
    kn2 = jnp.dot((kb * kb).astype(BF16), head_of, preferred_element_type=F32)
    kmax = jnp.maximum(kmax_ref[0:1, :], jnp.max(kn2, axis=0, keepdims=True))
    kmax_ref[...] = jnp.broadcast_to(kmax, kmax_ref.shape)
    qf = q.astype(F32)
    qn2 = jnp.dot((qf * qf).astype(BF16), head_of, preferred_element_type=F32)
    bound = jnp.sqrt(qn2 * kmax) * 1.02
    ms_bound = [jnp.sum(jnp.where(lane == h, bound, 0.0), axis=1, keepdims=True) for h in range(DSA_HEADS)]

    def tie_bias_pass():
        def body(c, seen):
            sc = keys_ref[c]
            eq = sc == thr_f
            eqf = jnp.where(eq, 1.0, 0.0)
            rank = seen + jnp.dot(eqf.astype(BF16), before, preferred_element_type=F32)
            sel = (sc > thr_f) | (eq & (rank < n_tie))
            keys_ref[c] = jnp.where(sel, 0.0, neg)
            return seen + jnp.sum(eqf, axis=1, keepdims=True)

        lax.fori_loop(0, nkc, body, jnp.zeros((tq, 1), F32))

    def bias_of(c, stored):
        sc = keys_ref[c]
        return sc if stored else jnp.where(sc >= thr_f, 0.0, neg)

    def qk(k0):
        return lax.dot_general(q4_ref[...], k_ref[pl.ds(k0, sub), :], nt, preferred_element_type=F32)

    def max_pass(stored):
        run_ref[...] = jnp.full(run_ref.shape, neg, F32)

        def body(c, carry):
            bias = bias_of(c, stored)
            for u in range(kc // sub):
                s = qk(pl.multiple_of(c * kc + u * sub, sub))
                for h in range(DSA_HEADS):
                    run_ref[h] = lane_fold(jnp.maximum, run_ref[h],
                                           s[h * tq:(h + 1) * tq] + bias[:, u * sub:(u + 1) * sub])
            return carry

        lax.fori_loop(0, nkc, body, 0)
        return [jnp.max(run_ref[h], axis=1, keepdims=True) for h in range(DSA_HEADS)]

    def att_pass(stored, ms):
        run_ref[...] = jnp.zeros_like(run_ref)
        acc_ref[...] = jnp.zeros_like(acc_ref)

        def half(s_ref, bias, k0):
            ps = []
            for h in range(DSA_HEADS):
                p = jnp.exp(s_ref[h * tq:(h + 1) * tq, :] + bias - ms[h])
                run_ref[h] = lane_fold(jnp.add, run_ref[h], p)
                ps.append(p.astype(BF16))
            acc_ref[...] += jnp.dot(jnp.concatenate(ps, axis=0), v_ref[pl.ds(k0, sub), :],
                                    preferred_element_type=F32)

        sa_ref[...] = qk(0)

        def body(c, carry):
            k0 = pl.multiple_of(c * kc, kc)
            sb_ref[...] = qk(k0 + sub)
            bias = bias_of(c, stored)
            half(sa_ref, bias[:, :sub], k0)
            sa_ref[...] = qk(pl.multiple_of(jnp.minimum(c + 1, nkc - 1) * kc, kc))
            half(sb_ref, bias[:, sub:], k0 + sub)
            return carry

        lax.fori_loop(0, nkc, body, 0)

    def finish(stored):
        if stored:
            tie_bias_pass()
        att_pass(stored, ms_bound)
        lmin = jnp.min(jnp.sum(run_ref[0], axis=1, keepdims=True))
        for h in range(1, DSA_HEADS):
            lmin = jnp.minimum(lmin, jnp.min(jnp.sum(run_ref[h], axis=1, keepdims=True)))

        def redo():
            att_pass(stored, max_pass(stored))
            return 0

        lax.cond(lmin >= 1e-26, lambda: 0, redo)
        return 0

    lax.cond(any_tie, lambda: finish(True), lambda: finish(False))
    out = jnp.zeros((tq, DSA_WIDTH), F32)
    for h in range(DSA_HEADS):
        m = (lane_w >= h * DSA_HEAD_DIM) & (lane_w < (h + 1) * DSA_HEAD_DIM)
        out = out + jnp.where(m, acc_ref[h * tq:(h + 1) * tq, :] / jnp.sum(run_ref[h], axis=1, keepdims=True), 0.0)
    o_ref[...] = _rms(out, ng_ref[...]).astype(o_ref.dtype)


def _dsa(p, norm_g, batch, seq, tq=512, kc=512, top_r=12):
    n_sel = min(DSA_TOPK, seq // 4)
    kc = min(kc, seq)
    sub = kc // 2
    r3 = lambda a: a.reshape(batch, seq, a.shape[-1])
    qblk = lambda w: pl.BlockSpec((None, tq, w), lambda b, i: (b, i, 0))
    kfull = lambda w: pl.BlockSpec((None, seq, w), lambda b, i: (b, 0, 0), pipeline_mode=pl.Buffered(1))
    out = pl.pallas_call(
        functools.partial(_dsa_kernel, tq=tq, kc=kc, sub=sub, n_sel=n_sel, top_r=top_r),
        grid=(batch, seq // tq),
        in_specs=[qblk(DSA_WIDTH), kfull(DSA_WIDTH), qblk(LANES), qblk(DSA_WIDTH),
                  kfull(DSA_WIDTH), kfull(DSA_WIDTH), pl.BlockSpec((1, DSA_WIDTH), lambda b, i: (0, 0))],
        out_specs=qblk(DSA_WIDTH),
        out_shape=jax.ShapeDtypeStruct((batch, seq, DSA_WIDTH), BF16),
        scratch_shapes=[
            pltpu.VMEM((DSA_IDX_HEADS * tq, DSA_WIDTH), BF16),
            pltpu.VMEM((DSA_HEADS * tq, DSA_WIDTH), BF16),
            pltpu.VMEM((seq // kc, tq, kc), F32),
            pltpu.VMEM((DSA_HEADS * tq, DSA_WIDTH), F32),
            pltpu.VMEM((DSA_HEADS, tq, LANES), F32),
            pltpu.VMEM((top_r, tq, LANES), I32),
            pltpu.VMEM((DSA_HEADS * tq, sub), F32),
            pltpu.VMEM((DSA_HEADS * tq, sub), F32),
            pltpu.VMEM((SUBLANES, LANES), F32),
            pltpu.VMEM((top_r, tq, LANES), F32),
        ],
        compiler_params=_cparams("arbitrary", "arbitrary"),
        name="dsa",
    )(r3(p["iq"]), r3(p["ik"]), r3(p["iw"]), r3(p["dq"]), r3(p["dk"]), r3(p["dv"]),
      norm_g.reshape(1, DSA_WIDTH))
    return out.reshape(batch * seq, DSA_WIDTH)


def _s5_tables(a_re, a_im, log_step, b_re, b_im, c_re, c_im):
    hp = lax.Precision.HIGHEST
    s = S5_CHUNK
    dt = jnp.exp(log_step)[:, None]
    mag = jnp.exp(a_re * dt)
    lam_re, lam_im = mag * jnp.cos(a_im * dt), mag * jnp.sin(a_im * dt)
    den = a_re * a_re + a_im * a_im
    nr, ni = lam_re - 1.0, lam_im
    coef_re = (nr * a_re + ni * a_im) / den
    coef_im = (ni * a_re - nr * a_im) / den
    bb_re = coef_re[..., None] * b_re - coef_im[..., None] * b_im
    bb_im = coef_re[..., None] * b_im + coef_im[..., None] * b_re
    j = jnp.arange(s + 1, dtype=F32)[:, None, None]
    pmag = jnp.exp(a_re * dt * j)
    pw_re, pw_im = pmag * jnp.cos(a_im * dt * j), pmag * jnp.sin(a_im * dt * j)
    e_re = c_re[None] * pw_re[:, :, None, :] - c_im[None] * pw_im[:, :, None, :]
    e_im = c_re[None] * pw_im[:, :, None, :] + c_im[None] * pw_re[:, :, None, :]
    kk = (jnp.einsum("jgcp,gpd->jgcd", e_re, bb_re, precision=hp)
          - jnp.einsum("jgcp,gpd->jgcd", e_im, bb_im, precision=hp))
    lag = np.arange(s)[None, :] - np.arange(s)[:, None]
    toe = jnp.where((lag >= 0)[:, :, None, None, None], kk[np.clip(lag, 0, s)], 0.0)
    rp_re, rp_im = pw_re[s - 1 - np.arange(s)], pw_im[s - 1 - np.arange(s)]
    x_re = rp_re[..., None] * bb_re[None] - rp_im[..., None] * bb_im[None]
    x_im = rp_re[..., None] * bb_im[None] + rp_im[..., None] * bb_re[None]
    f_re, f_im = e_re[1:], e_im[1:]
    sc_w = s * S5_GROUP_CH
    g_ids = jnp.arange(S5_TILE_G)[:, None, None]

    def placement(inner, width):
        a = jnp.arange(width // S5_TILE_G)[None, :, None]
        dst = (a // inner) * (S5_TILE_G * inner) + g_ids * inner + a % inner
        return (dst == jnp.arange(width)[None, None, :]).astype(F32)

    p_row = placement(S5_GROUP_CH, S5_ROW_W)
    p_state = placement(S5_STATE, S5_STATE_W)
    per_tile = lambda a: a.astype(BF16).astype(F32).reshape((S5_TILES, S5_TILE_G) + a.shape[1:])

    def place(tab, p_in, p_out):
        t = jnp.einsum("qgab,gbc->qgac", per_tile(tab), p_out)
        return jnp.einsum("gar,qgac->qrc", p_in, t).astype(BF16)

    g_toe = toe.transpose(2, 0, 4, 1, 3).reshape(S5_GROUPS, sc_w, sc_w)
    g_w1 = lambda x: x.transpose(1, 0, 3, 2).reshape(S5_GROUPS, sc_w, S5_STATE)
    g_w2 = lambda f: f.transpose(1, 3, 0, 2).reshape(S5_GROUPS, S5_STATE, sc_w)
    tabs = dict(toe=place(g_toe, p_row, p_row),
                w1_re=place(g_w1(x_re), p_row, p_state), w1_im=place(g_w1(x_im), p_row, p_state),
                w2_re=place(g_w2(f_re), p_state, p_row), w2_im=place(g_w2(-f_im), p_state, p_row))
    tabs["a_re"], tabs["a_im"] = pw_re[s].reshape(-1), pw_im[s].reshape(-1)
    return tabs


def _s5_state_in_kernel(u_ref, wr_ref, wi_ref, xr_ref, xi_ref):
    u = u_ref[...].astype(BF16)
    xr_ref[...] = jnp.dot(u, wr_ref[...], preferred_element_type=F32)
    xi_ref[...] = jnp.dot(u, wi_ref[...], preferred_element_type=F32)


def _s5_scan_kernel(xr_ref, xi_ref, ar_ref, ai_ref, hr_ref, hi_ref, cr_ref, ci_ref, *, steps):
    @pl.when(pl.program_id(0) == 0)
    def _():
        cr_ref[...] = jnp.zeros_like(cr_ref)
        ci_ref[...] = jnp.zeros_like(ci_ref)

    ar, ai = ar_ref[...], ai_ref[...]

    def body(n, carry):
        hr, hi = carry
        hr_ref[n] = hr
        hi_ref[n] = hi
        return ar * hr - ai * hi + xr_ref[n], ar * hi + ai * hr + xi_ref[n]

    hr, hi = lax.fori_loop(0, steps, body, (cr_ref[...], ci_ref[...]))
    cr_ref[...] = hr
    ci_ref[...] = hi


def _s5_out_kernel(u_ref, toe_ref, hr_ref, hi_ref, w2r_ref, w2i_ref, y_ref):
    y = jnp.dot(u_ref[...].astype(BF16), toe_ref[...], preferred_element_type=F32)
    y = y + jnp.dot(hr_ref[...].astype(BF16), w2r_ref[...], preferred_element_type=F32)
    y = y + jnp.dot(hi_ref[...].astype(BF16), w2i_ref[...], preferred_element_type=F32)
    y_ref[...] = y


def _s5_post_kernel(y_ref, u0_ref, u1_ref, u2_ref, d_ref, gw_ref, gb_ref, ng_ref, o_ref, yt_ref):
    n_chunks = yt_ref.shape[1] // S5_CHUNK
    for q in range(S5_TILES):
        for s in range(S5_CHUNK):
            yt_ref[q, pl.ds(s, n_chunks, stride=S5_CHUNK), :] = y_ref[q, :, s * LANES:(s + 1) * LANES]
    y = jnp.concatenate([yt_ref[q] for q in range(S5_TILES)], axis=1)
    u = jnp.concatenate([u0_ref[...], u1_ref[...], u2_ref[...]], axis=1)
    z = y + d_ref[...] * u
    z = 0.5 * z * (1.0 + jnp.tanh(math.sqrt(2.0 / math.pi) * (z + 0.044715 * (z * z * z))))
    gate = _sigmoid(jnp.dot(z.astype(BF16), gw_ref[...], preferred_element_type=F32) + gb_ref[...])
    o_ref[...] = _rms(z * gate, ng_ref[...]).astype(o_ref.dtype)


def _s5(us, xq, tabs, d, glu_w, glu_b, norm_g, batch, seq, tm=512, rb=512):
    s = S5_CHUNK
    nch = seq // s
    rows = nch * batch
    rb = min(rb, rows)
    gp = S5_GROUPS * S5_STATE
    row_in = pl.BlockSpec((None, rb, S5_ROW_W), lambda q, i: (q, i, 0))
    wspec = lambda a: pl.BlockSpec((None,) + a.shape[1:], lambda q, i: (q, 0, 0))
    colblk = pl.BlockSpec((rb, S5_STATE_W), lambda q, i: (i, q))
    xr, xi = pl.pallas_call(
        _s5_state_in_kernel,
        grid=(S5_TILES, rows // rb),
        in_specs=[row_in, wspec(tabs["w1_re"]), wspec(tabs["w1_im"])],
        out_specs=[colblk, colblk],
        out_shape=[jax.ShapeDtypeStruct((rows, gp), F32)] * 2,
        compiler_params=_cparams("arbitrary", "arbitrary"),
        name="s5_state_in",
    )(xq, tabs["w1_re"], tabs["w1_im"])
    parts = SUBLANES // batch
    wl = gp // parts
    slab = lambda a: a.reshape(batch, nch, gp).transpose(1, 0, 2).reshape(nch, SUBLANES, wl)
    unslab = lambda a: a.reshape(nch, batch, gp).transpose(1, 0, 2).reshape(rows, gp)
    coef = lambda a: jnp.tile(a.reshape(parts, wl), (batch, 1))
    steps = min(64, nch)
    sblk = pl.BlockSpec((steps, SUBLANES, wl), lambda i: (i, 0, 0))
    cblk = pl.BlockSpec((SUBLANES, wl), lambda i: (0, 0))
    hr, hi = pl.pallas_call(
        functools.partial(_s5_scan_kernel, steps=steps),
        grid=(nch // steps,),
        in_specs=[sblk, sblk, cblk, cblk],
        out_specs=[sblk, sblk],
        out_shape=[jax.ShapeDtypeStruct((nch, SUBLANES, wl), F32)] * 2,
        scratch_shapes=[pltpu.VMEM((SUBLANES, wl), F32)] * 2,
        compiler_params=_cparams("arbitrary"),
        name="s5_scan",
    )(slab(xr), slab(xi), coef(tabs["a_re"]), coef(tabs["a_im"]))
    y = pl.pallas_call(
        _s5_out_kernel,
        grid=(S5_TILES, rows // rb),
        in_specs=[row_in, wspec(tabs["toe"]), colblk, colblk, wspec(tabs["w2_re"]), wspec(tabs["w2_im"])],
        out_specs=row_in,
        out_shape=jax.ShapeDtypeStruct((S5_TILES, rows, S5_ROW_W), F32),
        compiler_params=_cparams("arbitrary", "arbitrary"),
        name="s5_out",
    )(xq, tabs["toe"], unslab(hr), unslab(hi), tabs["w2_re"], tabs["w2_im"])
    t = batch * seq
    tm = min(tm, t)
    tok = lambda w: pl.BlockSpec((tm, w), lambda i: (i, 0))
    vec = pl.BlockSpec((1, S5_WIDTH), lambda i: (0, 0))
    return pl.pallas_call(
        _s5_post_kernel,
        grid=(t // tm,),
        in_specs=[pl.BlockSpec((S5_TILES, tm // s, S5_ROW_W), lambda i: (0, i, 0)), tok(LANES), tok(LANES),
                  tok(LANES), vec, pl.BlockSpec((S5_WIDTH, S5_WIDTH), lambda i: (0, 0)), vec, vec],
        out_specs=tok(S5_WIDTH),
        out_shape=jax.ShapeDtypeStruct((t, S5_WIDTH), BF16),
        scratch_shapes=[pltpu.VMEM((S5_TILES, tm, LANES), F32)],
        compiler_params=_cparams("arbitrary"),
        name="s5_post",
    )(y, us[0], us[1], us[2], d.reshape(1, -1), glu_w.astype(BF16), glu_b.reshape(1, -1),
      norm_g.reshape(1, -1))


def _mix_out_kernel(og_ref, od_ref, os_ref, wg_ref, wd_ref, ws_ref, x_ref, gt_ref, n2_ref, sc_ref,
                    sh_ref, rw1_ref, rw2_ref, rb_ref, x1_ref, h2_ref, route_ref, cnt_ref, carry_ref, *, tm):
    @pl.when(pl.program_id(0) == 0)
    def _():
        carry_ref[...] = jnp.zeros_like(carry_ref)

    mix = jnp.dot(og_ref[...], wg_ref[...], preferred_element_type=F32)
    mix = mix + jnp.dot(od_ref[...], wd_ref[...], preferred_element_type=F32)
    mix = mix + jnp.dot(os_ref[...], ws_ref[...], preferred_element_type=F32)
    x1 = x_ref[...] + gt_ref[...] * mix
    x1_ref[...] = x1
    h2 = _rms(x1, n2_ref[...]) * (1.0 + sc_ref[...]) + sh_ref[...]
    h2_ref[...] = h2
    ha = h2.astype(BF16)
    hb = (h2 - ha.astype(F32)).astype(BF16)
    logit = (jnp.dot(ha, rw1_ref[...], preferred_element_type=F32)
             + jnp.dot(ha, rw2_ref[...], preferred_element_type=F32)
             + jnp.dot(hb, rw1_ref[...], preferred_element_type=F32)) + rb_ref[...]
    lane = lax.broadcasted_iota(I32, (tm, LANES), 1)
    lanef = lane.astype(F32)
    ninf = jnp.float32(-jnp.inf)
    big = jnp.float32(LANES)
    gl = jnp.where(lane < MOE_GROUPS, logit, ninf)
    gmax = jnp.max(gl, axis=1, keepdims=True)
    grp_p = 1.0 / jnp.sum(jnp.exp(gl - gmax), axis=1, keepdims=True)
    grp_i = jnp.min(jnp.where(gl == gmax, lanef, big), axis=1, keepdims=True)
    lo = MOE_GROUPS + MOE_EXPERTS_PER_GROUP * grp_i
    el = jnp.where((lanef >= lo) & (lanef < lo + MOE_EXPERTS_PER_GROUP), logit, ninf)
    l1 = jnp.max(el, axis=1, keepdims=True)
    i1 = jnp.min(jnp.where(el == l1, lanef, big), axis=1, keepdims=True)
    el2 = jnp.where(lanef == i1, ninf, el)
    l2 = jnp.max(el2, axis=1, keepdims=True)
    i2 = jnp.min(jnp.where(el2 == l2, lanef, big), axis=1, keepdims=True)
    e21 = jnp.exp(l2 - l1)
    w1 = grp_p / (1.0 + e21)
    w2 = grp_p * e21 / (1.0 + e21)
    e1 = i1 - MOE_GROUPS
    e2 = i2 - MOE_GROUPS
    oh1 = lanef == e1
    oh2 = lanef == e2
    hot = jnp.where(oh1 | oh2, 1.0, 0.0)
    ri = lax.broadcasted_iota(I32, (tm, tm), 0)
    ci = lax.broadcasted_iota(I32, (tm, tm), 1)
    before = (ci < ri).astype(BF16)
    prior = carry_ref[...] + jnp.dot(before, hot.astype(BF16), preferred_element_type=F32)
    r1 = jnp.sum(jnp.where(oh1, prior, 0.0), axis=1, keepdims=True)
    r2 = jnp.sum(jnp.where(oh2, prior, 0.0), axis=1, keepdims=True)
    total = carry_ref[...] + jnp.sum(hot, axis=0, keepdims=True)
    carry_ref[...] = total
    cnt_ref[...] = jnp.broadcast_to(total, cnt_ref.shape)
    route = jnp.where(lane == 0, e1, jnp.where(lane == 1, e2, jnp.where(lane == 2, w1, jnp.where(
        lane == 3, w2, jnp.where(lane == 4, r1, jnp.where(lane == 5, r2, 0.0))))))
    route_ref[...] = route


def _mix_out(o_gla, o_dsa, o_s5, w_out, x2, gt1, norm2_g, sc2, sh2, rgw, rgb, rew, reb, seq, tm=512):
    t, d = x2.shape
    per_b = seq // tm
    wg = jnp.pad(w_out[:GLA_HEADS * GLA_DV].reshape(GLA_HEADS, GLA_DV, d),
                 ((0, 0), (0, HEAD_PAD - GLA_DV), (0, 0))).reshape(GLA_W, d).astype(BF16)
    wd = w_out[GLA_HEADS * GLA_DV:GLA_HEADS * GLA_DV + DSA_WIDTH].astype(BF16)
    ws = w_out[GLA_HEADS * GLA_DV + DSA_WIDTH:].astype(BF16)
    rw = _pad_cols(jnp.concatenate([rgw, rew], axis=1), LANES)
    rb = _pad_cols(jnp.concatenate([rgb, reb]).reshape(1, -1), LANES)
    rw1 = rw.astype(BF16)
    rw2 = (rw - rw1.astype(F32)).astype(BF16)
    tok = lambda w: pl.BlockSpec((tm, w), lambda i: (i, 0))
    full = lambda a: pl.BlockSpec(a.shape, lambda i: (0, 0))
    perb = pl.BlockSpec((None, 1, d), lambda i: (i // per_b, 0, 0))
    vec = pl.BlockSpec((1, d), lambda i: (0, 0))
    return pl.pallas_call(
        functools.partial(_mix_out_kernel, tm=tm),
        grid=(t // tm,),
        in_specs=[tok(GLA_W), tok(DSA_WIDTH), tok(S5_WIDTH), full(wg), full(wd), full(ws), tok(d),
                  perb, vec, perb, perb, full(rw1), full(rw2), full(rb)],
        out_specs=[tok(d), tok(d), tok(LANES), pl.BlockSpec((SUBLANES, LANES), lambda i: (0, 0))],
        out_shape=[jax.ShapeDtypeStruct((t, d), F32), jax.ShapeDtypeStruct((t, d), F32),
                   jax.ShapeDtypeStruct((t, LANES), F32), jax.ShapeDtypeStruct((SUBLANES, LANES), F32)],
        scratch_shapes=[pltpu.VMEM((1, LANES), F32)],
        compiler_params=_cparams("arbitrary"),
        name="mix_out",
    )(o_gla, o_dsa, o_s5, wg, wd, ws, x2, gt1[:, None, :], norm2_g.reshape(1, d), sc2[:, None, :],
      sh2[:, None, :], rw1, rw2, rb)


def _row(ref, r):
    return ref.at[pl.ds(r, 1), :]


def _dispatch_kernel(dest_ref, h_ref, zero_ref, out_ref, sem, *, td):
    del zero_ref

    def start(g, c):
        for j in range(SUBLANES):
            r = g * SUBLANES + j
            pltpu.make_async_copy(_row(h_ref, r), _row(out_ref, dest_ref[0, r]), sem).start()
            pltpu.make_async_copy(_row(h_ref, r), _row(out_ref, dest_ref[1, r]), sem).start()
        return c

    lax.fori_loop(0, td // SUBLANES, start, 0)
    for _ in range(2):
        pltpu.make_async_copy(h_ref, out_ref.at[pl.ds(0, td), :], sem).wait()


def _dispatch(h2, dest, n_rows, td=256):
    t, d = h2.shape
    dest3 = dest.reshape(2, t // td, td).transpose(1, 0, 2)
    zeros = jnp.zeros((n_rows, d), F32)
    return pl.pallas_call(
        functools.partial(_dispatch_kernel, td=td),
        grid=(t // td,),
        in_specs=[pl.BlockSpec((None, 2, td), lambda i: (i, 0, 0), memory_space=pltpu.SMEM),
                  pl.BlockSpec((td, d), lambda i: (i, 0)),
                  pl.BlockSpec(memory_space=pl.ANY)],
        out_specs=pl.BlockSpec(memory_space=pl.ANY),
        out_shape=jax.ShapeDtypeStruct((n_rows, d), F32),
        scratch_shapes=[pltpu.SemaphoreType.DMA(())],
        input_output_aliases={2: 0},
        compiler_params=_cparams("arbitrary"),
        name="moe_dispatch",
    )(dest3, h2, zeros)


def _expert_kernel(be_ref, nb_ref, x_ref, wg_ref, wu_ref, wd_ref, o_ref):
    del be_ref
    i = pl.program_id(0)

    @pl.when(i < nb_ref[0])
    def _():
        xb = x_ref[...].astype(BF16)
        g = jnp.dot(xb, wg_ref[...].astype(BF16), preferred_element_type=F32)
        u = jnp.dot(xb, wu_ref[...].astype(BF16), preferred_element_type=F32)
        hid = (g * _sigmoid(g) * u).astype(BF16)
        o_ref[...] = jnp.dot(hid, wd_ref[...].astype(BF16), preferred_element_type=F32)

    @pl.when(i >= nb_ref[0])
    def _():
        o_ref[...] = jnp.zeros_like(o_ref)


def _experts(xs, blk_e, n_used, wg, wu, wd, layer):
    n_rows = xs.shape[0]
    nb = n_rows // MOE_BM
    d, hdim = wg.shape[-2:]
    rows = pl.BlockSpec((MOE_BM, d), lambda i, be, nu: (i, 0))
    wspec = lambda a, b: pl.BlockSpec((None, None, a, b), lambda i, be, nu: (layer, be[i], 0, 0))
    return pl.pallas_call(
        _expert_kernel,
        grid_spec=pltpu.PrefetchScalarGridSpec(
            num_scalar_prefetch=2,
            grid=(nb,),
            in_specs=[rows, wspec(d, hdim), wspec(d, hdim), wspec(hdim, d)],
            out_specs=rows,
        ),
        out_shape=jax.ShapeDtypeStruct((n_rows, d), F32),
        compiler_params=_cparams("arbitrary"),
        name="moe_experts",
    )(blk_e, n_used, xs, wg, wu, wd)


def _combine_kernel(dest_ref, x_ref, route_ref, gt_ref, fg_ref, eo_ref, o_ref, rows_ref, sem, *, tc, final):
    def start(g, c):
        for j in range(SUBLANES):
            r = g * SUBLANES + j
            pltpu.make_async_copy(_row(eo_ref, dest_ref[0, r]), _row(rows_ref.at[0], r), sem).start()
            pltpu.make_async_copy(_row(eo_ref, dest_ref[1, r]), _row(rows_ref.at[1], r), sem).start()
        return c

    lax.fori_loop(0, tc // SUBLANES, start, 0)
    for which in range(2):
        pltpu.make_async_copy(eo_ref.at[pl.ds(0, tc), :], rows_ref.at[which], sem).wait()
    route = route_ref[...]
    lane = lax.broadcasted_iota(I32, route.shape, 1)
    w1 = jnp.sum(jnp.where(lane == 2, route, 0.0), axis=1, keepdims=True)
    w2 = jnp.sum(jnp.where(lane == 3, route, 0.0), axis=1, keepdims=True)
    y = rows_ref[0] * w1 + rows_ref[1] * w2
    x = x_ref[...] + gt_ref[...] * y
    if final:
        x = _rms(x, fg_ref[...])
    o_ref[...] = x


def _combine(x1, route, dest, eo, gt2, final_g, seq, final, tc=256):
    t, d = x1.shape
    per_b = seq // tc
    dest3 = dest.reshape(2, t // tc, tc).transpose(1, 0, 2)
    return pl.pallas_call(
        functools.partial(_combine_kernel, tc=tc, final=final),
        grid=(t // tc,),
        in_specs=[pl.BlockSpec((None, 2, tc), lambda i: (i, 0, 0), memory_space=pltpu.SMEM),
                  pl.BlockSpec((tc, d), lambda i: (i, 0)),
                  pl.BlockSpec((tc, LANES), lambda i: (i, 0)),
                  pl.BlockSpec((None, 1, d), lambda i: (i // per_b, 0, 0)),
                  pl.BlockSpec((1, d), lambda i: (0, 0)),
                  pl.BlockSpec(memory_space=pl.ANY)],
        out_specs=pl.BlockSpec((tc, d), lambda i: (i, 0)),
        out_shape=jax.ShapeDtypeStruct((t, d), F32),
        scratch_shapes=[pltpu.VMEM((2, tc, d), F32), pltpu.SemaphoreType.DMA(())],
        compiler_params=_cparams("arbitrary"),
        name="moe_combine",
    )(dest3, x1, route, gt2[:, None, :], final_g.reshape(1, d), eo)


def _moe_plan(route, counts, t):
    e = route[:, 0:2].astype(I32).T
    rank = route[:, 4:6].astype(I32).T
    cnt = counts[0, :MOE_EXPERTS].astype(I32)
    padded = (cnt + MOE_BM - 1) // MOE_BM * MOE_BM
    pend = jnp.cumsum(padded)
    pstart = pend - padded
    ids = jnp.arange(MOE_EXPERTS, dtype=I32)
    dest = rank + jnp.sum(jnp.where(e[..., None] == ids, pstart, 0), axis=-1)
    nb = (2 * t) // MOE_BM + MOE_EXPERTS
    n_used = (pend[-1] // MOE_BM).astype(I32)
    first_row = jnp.minimum(jnp.arange(nb, dtype=I32), n_used - 1) * MOE_BM
    blk_e = jnp.sum((pend[None, :] <= first_row[:, None]).astype(I32), axis=1)
    blk_e = jnp.minimum(blk_e, MOE_EXPERTS - 1)
    return dest, blk_e, n_used.reshape(1), nb * MOE_BM


def kernel(x, c, ada_w, ada_b, norm1_g, w_in, gla_gate_w, gla_gate_b, gla_norm_g, dsa_norm_g, s5_a_re, s5_a_im, s5_log_step, s5_b_re, s5_b_im, s5_c_re, s5_c_im, s5_d, s5_glu_w, s5_glu_b, s5_norm_g, w_out, norm2_g, router_grp_w, router_grp_b, router_exp_w, router_exp_b, exp_w_gate, exp_w_up, exp_w_down, final_norm_g):
    batch, seq, d = x.shape
    depth = ada_w.shape[0]
    t = batch * seq
    mod = _ada_mod(c, ada_w, ada_b)
    x2 = x.reshape(t, d)
    for l in range(depth):
        sh1, sc1, gt1, sh2, sc2, gt2 = (mod[l, j] for j in range(6))
        p = _in_proj(x2, norm1_g[l], sc1, sh1, _layout_w_in(w_in[l]), seq)
        o_gla = _gla(p, gla_gate_w[l], gla_gate_b[l], gla_norm_g[l], batch, seq)
        o_dsa = _dsa(p, dsa_norm_g[l], batch, seq)
        tabs = _s5_tables(s5_a_re[l], s5_a_im[l], s5_log_step[l], s5_b_re[l], s5_b_im[l],
                          s5_c_re[l], s5_c_im[l])
        o_s5 = _s5((p["su0"], p["su1"], p["su2"]), p["sx"], tabs, s5_d[l], s5_glu_w[l], s5_glu_b[l], s5_norm_g[l], batch, seq)
        x1, h2, route, counts = _mix_out(o_gla, o_dsa, o_s5, w_out[l], x2, gt1, norm2_g[l], sc2, sh2,
                                         router_grp_w[l], router_grp_b[l], router_exp_w[l],
                                         router_exp_b[l], seq)
        dest, blk_e, n_used, n_rows = _moe_plan(route, counts, t)
        xs = _dispatch(h2, dest, n_rows)
        eo = _experts(xs, blk_e, n_used, exp_w_gate, exp_w_up, exp_w_down, l)
        x2 = _combine(x1, route, dest, eo, gt2, final_norm_g, seq, final=(l == depth - 1))
    return x2.reshape(batch, seq, d)
```

```python
import functools
import math

import jax
import jax.numpy as jnp
import numpy as np
from jax import lax
from jax.experimental import pallas as pl
from jax.experimental.pallas import tpu as pltpu

F32 = jnp.float32
BF16 = jnp.bfloat16
I32 = jnp.int32

D_MODEL = 1024
GLA_HEADS = 4
GLA_DV = 96
GLA_DK = 48
GLA_RANK = 16
GLA_TAU = 16.0
GLA_CHUNK = 64
DSA_HEADS = 4
DSA_HEAD_DIM = 64
DSA_WIDTH = DSA_HEADS * DSA_HEAD_DIM
DSA_IDX_HEADS = 8
DSA_IDX_DIM = 32
DSA_TOPK = 256
S5_GROUPS = 24
S5_GROUP_CH = 16
S5_STATE = 64
S5_WIDTH = S5_GROUPS * S5_GROUP_CH
MOE_GROUPS = 4
MOE_EXPERTS_PER_GROUP = 8
MOE_EXPERTS = MOE_GROUPS * MOE_EXPERTS_PER_GROUP
MOE_HIDDEN = 512
RMS_EPS = 1e-6
IN_SIZES = (192, 192, 384, 16, 384, 256, 256, 256, 256, 32, 8, 384)

LANES = 128
SUBLANES = 8
VMEM_LIMIT = 56 * 1024 * 1024

HEAD_PAD = LANES
GLA_W = GLA_HEADS * HEAD_PAD
S5_CHUNK = 16
S5_TILE_G = LANES // S5_GROUP_CH
S5_TILES = S5_GROUPS // S5_TILE_G
S5_ROW_W = S5_CHUNK * LANES
S5_STATE_W = S5_TILE_G * S5_STATE
MOE_BM = 512
INT_MIN = -2 ** 31


def _cparams(*sem):
    return pltpu.CompilerParams(dimension_semantics=tuple(sem), vmem_limit_bytes=VMEM_LIMIT)


def _sigmoid(x):
    return 0.5 * (jnp.tanh(0.5 * x) + 1.0)


def _rms(x, g, n=None):
    n = x.shape[-1] if n is None else n
    ms = jnp.sum(x * x, axis=-1, keepdims=True) * (1.0 / n)
    return x * lax.rsqrt(ms + RMS_EPS) * g


def _ada_kernel(c_ref, w_ref, b_ref, o_ref):
    c = c_ref[...]
    cond = (c * _sigmoid(c)).astype(BF16)
    o_ref[...] = jnp.dot(cond, w_ref[...].astype(BF16), preferred_element_type=F32) + b_ref[...]


def _ada_mod(c, ada_w, ada_b):
    depth, d, d6 = ada_w.shape
    nblk = d6 // d
    bp = -(-c.shape[0] // SUBLANES) * SUBLANES
    cp = jnp.pad(c, ((0, bp - c.shape[0]), (0, 0)))
    out = pl.pallas_call(
        _ada_kernel,
        grid=(depth, nblk),
        in_specs=[
            pl.BlockSpec((bp, d), lambda l, j: (0, 0)),
            pl.BlockSpec((None, d, d), lambda l, j: (l, 0, j)),
            pl.BlockSpec((None, None, 1, d), lambda l, j: (l, j, 0, 0)),
        ],
        out_specs=pl.BlockSpec((None, None, bp, d), lambda l, j: (l, j, 0, 0)),
        out_shape=jax.ShapeDtypeStruct((depth, nblk, bp, d), F32),
        compiler_params=_cparams("arbitrary", "arbitrary"),
        name="ada_mod",
    )(cp, ada_w, ada_b.reshape(depth, nblk, 1, d))
    return out[:, :, : c.shape[0], :]


PROJ_OUTS = (
    ("gq", GLA_W, BF16), ("gk", GLA_W, BF16), ("gv", GLA_W, BF16), ("gr", GLA_W, BF16),
    ("glr", LANES, BF16),
    ("dq", DSA_WIDTH, BF16), ("dk", DSA_WIDTH, BF16), ("dv", DSA_WIDTH, BF16),
    ("iq", DSA_IDX_HEADS * DSA_IDX_DIM, BF16), ("ik", DSA_IDX_HEADS * DSA_IDX_DIM, BF16),
    ("iw", LANES, F32), ("su0", LANES, F32), ("su1", LANES, F32), ("su2", LANES, F32),
)
PROJ_COLS = sum(w for _, w, _ in PROJ_OUTS)


def _pad_heads(w, heads, dim):
    d = w.shape[0]
    w = w.reshape(d, heads, dim)
    return jnp.pad(w, ((0, 0), (0, 0), (0, HEAD_PAD - dim))).reshape(d, heads * HEAD_PAD)


def _pad_cols(w, width):
    return jnp.pad(w, ((0, 0), (0, width - w.shape[1])))


def _layout_w_in(w_in):
    offs = np.cumsum((0,) + IN_SIZES)
    p = [w_in[:, offs[i]:offs[i + 1]] for i in range(len(IN_SIZES))]
    cols = [
        _pad_heads(p[0], GLA_HEADS, GLA_DK), _pad_heads(p[1], GLA_HEADS, GLA_DK),
        _pad_heads(p[2], GLA_HEADS, GLA_DV), _pad_heads(p[4], GLA_HEADS, GLA_DV),
        _pad_cols(p[3], LANES),
        p[5], p[6], p[7], p[8], jnp.tile(p[9], (1, DSA_IDX_HEADS)),
        _pad_cols(p[10], LANES), p[11],
    ]
    return jnp.concatenate(cols, axis=1).astype(BF16)


def _proj_kernel(x_ref, g_ref, sc_ref, sh_ref, w_ref, *o_refs):
    x = x_ref[...]
    h = _rms(x, g_ref[...]) * (1.0 + sc_ref[...]) + sh_ref[...]
    hb = h.astype(BF16)
    c0 = 0
    for o_ref, (_, width, _) in zip(o_refs, PROJ_OUTS):
        o_ref[...] = jnp.dot(hb, w_ref[:, c0:c0 + width], preferred_element_type=F32).astype(o_ref.dtype)
        c0 += width
    sx_ref = o_refs[len(PROJ_OUTS)]
    n_chunks = x.shape[0] // S5_CHUNK
    for q in range(S5_TILES):
        su_ref = o_refs[len(PROJ_OUTS) - S5_TILES + q]
        for s in range(S5_CHUNK):
            sx_ref[q, :, s * LANES:(s + 1) * LANES] = su_ref[pl.ds(s, n_chunks, stride=S5_CHUNK), :]


def _in_proj(x2, norm_g, sc, sh, w_lay, seq, tm=512):
    t, d = x2.shape
    per_b = seq // tm
    outs = pl.pallas_call(
        _proj_kernel,
        grid=(t // tm,),
        in_specs=[
            pl.BlockSpec((tm, d), lambda i: (i, 0)),
            pl.BlockSpec((1, d), lambda i: (0, 0)),
            pl.BlockSpec((None, 1, d), lambda i: (i // per_b, 0, 0)),
            pl.BlockSpec((None, 1, d), lambda i: (i // per_b, 0, 0)),
            pl.BlockSpec((d, PROJ_COLS), lambda i: (0, 0)),
        ],
        out_specs=[pl.BlockSpec((tm, w), lambda i: (i, 0)) for _, w, _ in PROJ_OUTS]
        + [pl.BlockSpec((S5_TILES, tm // S5_CHUNK, S5_ROW_W), lambda i: (0, i, 0))],
        out_shape=[jax.ShapeDtypeStruct((t, w), dt) for _, w, dt in PROJ_OUTS]
        + [jax.ShapeDtypeStruct((S5_TILES, t // S5_CHUNK, S5_ROW_W), F32)],
        compiler_params=_cparams("arbitrary"),
        name="in_proj",
    )(x2, norm_g.reshape(1, d), sc[:, None, :], sh[:, None, :], w_lay)
    p = {name: o for (name, _, _), o in zip(PROJ_OUTS, outs)}
    p["sx"] = outs[-1]
    return p


def _gla_kernel(q_ref, k_ref, v_ref, r_ref, lr_ref, gw_ref, gb_ref, ng_ref, o_ref, s_ref, *, rows):
    @pl.when(pl.program_id(1) == 0)
    def _():
        s_ref[...] = jnp.zeros_like(s_ref)

    c = GLA_CHUNK
    gl = jnp.dot(lr_ref[...], gw_ref[...], preferred_element_type=F32) + gb_ref[...]
    g = (jnp.minimum(gl, 0.0) - jnp.log1p(jnp.exp(-jnp.abs(gl)))) * (1.0 / GLA_TAU)
    g1 = g.astype(BF16)
    rem = g - g1.astype(F32)
    g2 = rem.astype(BF16)
    g3 = (rem - g2.astype(F32)).astype(BF16)
    ri = lax.broadcasted_iota(I32, (c, c), 0)
    ci = lax.broadcasted_iota(I32, (c, c), 1)
    causal = ri >= ci
    tri = causal.astype(BF16)
    ng = ng_ref[...]
    for ch in range(rows // c):
        rs = slice(ch * c, (ch + 1) * c)
        bcum = (jnp.dot(tri, g1[rs], preferred_element_type=F32)
                + jnp.dot(tri, g2[rs], preferred_element_type=F32)
                + jnp.dot(tri, g3[rs], preferred_element_type=F32))
        blast = bcum[c - 1:c, :]
        e_pos = jnp.exp(bcum)
        kf = k_ref[rs, :].astype(F32)
        qd = (q_ref[rs, :].astype(F32) * (GLA_DK ** -0.5) * e_pos).astype(BF16)
        kd = (kf * jnp.exp(-bcum)).astype(BF16)
        ku = (kf * jnp.exp(blast - bcum)).astype(BF16)
        dec = jnp.exp(blast)
        for h in range(GLA_HEADS):
            hs = slice(h * HEAD_PAD, (h + 1) * HEAD_PAD)
            vh = v_ref[rs, hs]
            att = lax.dot_general(qd[:, hs], kd[:, hs], (((1,), (1,)), ((), ())),
                                  preferred_element_type=F32)
            att = jnp.where(causal, att, 0.0).astype(BF16)
            st = s_ref[h]
            o = jnp.dot(att, vh, preferred_element_type=F32)
            o = o + lax.dot_general(qd[:, hs], st.astype(BF16), (((1,), (1,)), ((), ())),
                                    preferred_element_type=F32)
            upd = lax.dot_general(vh, ku[:, hs], (((0,), (0,)), ((), ())),
                                  preferred_element_type=F32)
            s_ref[h] = st * dec[:, hs] + upd
            y = _rms(o, ng[:, hs], GLA_DV)
            rr = r_ref[rs, hs].astype(F32)
            o_ref[rs, hs] = (y * (rr * _sigmoid(rr))).astype(o_ref.dtype)


def _gla(p, gate_w, gate_b, norm_g, batch, seq, rows=512):
    gw = _pad_cols(jnp.pad(_pad_heads(gate_w, GLA_HEADS, GLA_DK), ((0, LANES - GLA_RANK), (0, 0))), GLA_W)
    gb = _pad_heads(gate_b.reshape(1, -1), GLA_HEADS, GLA_DK)
    ng = _pad_heads(jnp.tile(norm_g.reshape(1, GLA_DV), (1, GLA_HEADS)), GLA_HEADS, GLA_DV)
    rows = min(rows, seq)
    blk = lambda w: pl.BlockSpec((None, rows, w), lambda b, j: (b, j, 0))
    full = lambda a: pl.BlockSpec(a.shape, lambda b, j: (0, 0))
    r3 = lambda a: a.reshape(batch, seq, a.shape[-1])
    out = pl.pallas_call(
        functools.partial(_gla_kernel, rows=rows),
        grid=(batch, seq // rows),
        in_specs=[blk(GLA_W), blk(GLA_W), blk(GLA_W), blk(GLA_W), blk(LANES),
                  full(gw), full(gb), full(ng)],
        out_specs=blk(GLA_W),
        out_shape=jax.ShapeDtypeStruct((batch, seq, GLA_W), BF16),
        scratch_shapes=[pltpu.VMEM((GLA_HEADS, HEAD_PAD, HEAD_PAD), F32)],
        compiler_params=_cparams("arbitrary", "arbitrary"),
        name="gla",
    )(r3(p["gq"]), r3(p["gk"]), r3(p["gv"]), r3(p["gr"]), r3(p["glr"]), gw.astype(BF16), gb, ng)
    return out.reshape(batch * seq, GLA_W)


def _dsa_kernel(iq_ref, ik_ref, iw_ref, q_ref, k_ref, v_ref, ng_ref, o_ref,
                qs_ref, q4_ref, keys_ref, acc_ref, run_ref, top_ref, sa_ref, sb_ref, kmax_ref, topf_ref,
                *, tq, kc, sub, n_sel, top_r):
    assert kc == 2 * sub
    qi = pl.program_id(1)
    nkc = ((qi + 1) * tq + kc - 1) // kc
    lane_w = lax.broadcasted_iota(I32, (tq, DSA_WIDTH), 1)
    iq = iq_ref[...]
    for h in range(DSA_IDX_HEADS):
        m = (lane_w >= h * DSA_IDX_DIM) & (lane_w < (h + 1) * DSA_IDX_DIM)
        qs_ref[h * tq:(h + 1) * tq, :] = jnp.where(m, iq, jnp.zeros_like(iq))
    iw = iw_ref[...] * (DSA_IDX_HEADS ** -0.5 * DSA_IDX_DIM ** -0.5)
    lane = lax.broadcasted_iota(I32, (tq, LANES), 1)
    w_cols = [iw[:, h:h + 1] for h in range(DSA_IDX_HEADS)]
    qpos = qi * tq + lax.broadcasted_iota(I32, (tq, sub), 0)
    kcol = lax.broadcasted_iota(I32, (tq, sub), 1)
    nt = (((1,), (1,)), ((), ()))

    ninf = jnp.float32(-jnp.inf)

    def score_chunk(c, causal=True):
        for u in range(kc // sub):
            k0 = pl.multiple_of(c * kc + u * sub, sub)
            lg = lax.dot_general(qs_ref[...], ik_ref[pl.ds(k0, sub), :], nt,
                                 preferred_element_type=F32)
            sc = jnp.zeros((tq, sub), F32)
            for h in range(DSA_IDX_HEADS):
                sc = sc + w_cols[h] * jnp.maximum(lg[h * tq:(h + 1) * tq, :], 0.0)
            sc = sc + 0.0
            if causal:
                sc = jnp.where(kcol + k0 <= qpos, sc, ninf)
            keys_ref[c, :, u * sub:(u + 1) * sub] = sc

    topf_ref[...] = jnp.full(topf_ref.shape, ninf, F32)

    def insert_chunk(c):
        for rt in range(tq // SUBLANES):
            rs = slice(rt * SUBLANES, (rt + 1) * SUBLANES)
            ts = [topf_ref[r, rs, :] for r in range(top_r)]
            for j in range(kc // LANES):
                x = keys_ref[c, rs, j * LANES:(j + 1) * LANES]
                for r in range(top_r):
                    hi = jnp.maximum(ts[r], x)
                    x = jnp.minimum(ts[r], x)
                    ts[r] = hi
            for r in range(top_r):
                topf_ref[r, rs, :] = ts[r]

    score_chunk(0)

    def fused_body(causal):
        def body(c, carry):
            insert_chunk(c - 1)
            score_chunk(c, causal)
            return carry
        return body

    n_open = jnp.maximum((qi * tq + 1) // kc, 1)
    lax.fori_loop(1, n_open, fused_body(False), 0)
    lax.fori_loop(n_open, nkc, fused_body(True), 0)
    insert_chunk(nkc - 1)

    def to_key(x):
        bits = lax.bitcast_convert_type(x, I32)
        key = jnp.where(bits < 0, bits ^ jnp.int32(0x7FFFFFFF), bits)
        return jnp.where(x == ninf, jnp.int32(INT_MIN), key)

    def count_ge(cand):
        candb = jnp.broadcast_to(cand, (tq, LANES))

        def body(c, acc):
            blk = keys_ref[c]
            for j in range(kc // LANES):
                acc = acc + jnp.where(to_key(blk[:, j * LANES:(j + 1) * LANES]) >= candb, 1.0, 0.0)
            return acc

        acc = lax.fori_loop(0, nkc, body, jnp.zeros((tq, LANES), F32))
        return jnp.sum(acc, axis=1, keepdims=True)

    need = jnp.float32(n_sel)

    def select(count):
        t0 = jnp.full((tq, 1), INT_MIN, I32)
        zero = jnp.zeros((tq, 1), I32)
        t0 = jnp.where(count(zero) >= need, zero, t0)

        def bit_body(i, t):
            cand = t | (jnp.int32(1) << (30 - i))
            return jnp.where(count(cand) >= need, cand, t)

        t = lax.fori_loop(0, 31, bit_body, t0)
        t = jnp.maximum(t, jnp.int32(INT_MIN + 1))
        return t, count(t + 1), count(t)

    for r in range(top_r):
        top_ref[r] = to_key(topf_ref[r])

    def count_cand(cand):
        candb = jnp.broadcast_to(cand, (tq, LANES))
        acc = jnp.zeros((tq, LANES), F32)
        for r in range(top_r):
            acc = acc + jnp.where(top_ref[r] >= candb, 1.0, 0.0)
        return jnp.sum(acc, axis=1, keepdims=True)

    thr_c, n_gt_c, n_ge_c = select(count_cand)
    hidden = jnp.max(jnp.where(top_ref[top_r - 1] >= thr_c, 1.0, 0.0)) > 0.0
    thr, n_gt, n_ge = lax.cond(hidden, lambda: select(count_ge), lambda: (thr_c, n_gt_c, n_ge_c))
    n_tie = need - n_gt
    any_tie = jnp.max(n_ge) > need
    thr_bits = jnp.where(thr < 0, thr ^ jnp.int32(0x7FFFFFFF), thr)
    thr_f = jnp.where(thr == jnp.int32(INT_MIN + 1), jnp.float32(jnp.finfo(jnp.float32).min),
                      lax.bitcast_convert_type(thr_bits, F32))

    q = q_ref[...] * jnp.asarray(DSA_HEAD_DIM ** -0.5, BF16)
    for h in range(DSA_HEADS):
        m = (lane_w >= h * DSA_HEAD_DIM) & (lane_w < (h + 1) * DSA_HEAD_DIM)
        q4_ref[h * tq:(h + 1) * tq, :] = jnp.where(m, q, jnp.zeros_like(q))
    neg = jnp.float32(-1e30)

    def lane_fold(op, acc, s):
        for j in range(s.shape[1] // LANES):
            acc = op(acc, s[:, j * LANES:(j + 1) * LANES])
        return acc

    @pl.when(qi == 0)
    def _():
        kmax_ref[...] = jnp.zeros_like(kmax_ref)

    hl = lax.broadcasted_iota(I32, (DSA_WIDTH, LANES), 0) // DSA_HEAD_DIM
    head_of = (hl == lax.broadcasted_iota(I32, (DSA_WIDTH, LANES), 1)).astype(BF16)
    kb = k_ref[pl.ds(pl.multiple_of(qi * tq, tq), tq), :].astype(F32)
    kn2 = jnp.dot((kb * kb).astype(BF16), head_of, preferred_element_type=F32)
    kmax = jnp.maximum(kmax_ref[0:1, :], jnp.max(kn2, axis=0, keepdims=True))
    kmax_ref[...] = jnp.broadcast_to(kmax, kmax_ref.shape)
    qf = q.astype(F32)
    qn2 = jnp.dot((qf * qf).astype(BF16), head_of, preferred_element_type=F32)
    bound = jnp.sqrt(qn2 * kmax) * 1.02
    ms_bound = [jnp.sum(jnp.where(lane == h, bound, 0.0), axis=1, keepdims=True) for h in range(DSA_HEADS)]

    def tie_bias_pass():
        ui = lax.broadcasted_iota(I32, (kc, kc), 0)
        uj = lax.broadcasted_iota(I32, (kc, kc), 1)
        before = (ui < uj).astype(BF16)

        def body(c, seen):
            sc = keys_ref[c]
            eq = sc == thr_f
            eqf = jnp.where(eq, 1.0, 0.0)
            rank = seen + jnp.dot(eqf.astype(BF16), before, preferred_element_type=F32)
            sel = (sc > thr_f) | (eq & (rank < n_tie))
            keys_ref[c] = jnp.where(sel, 0.0, neg)
            return seen + jnp.sum(eqf, axis=1, keepdims=True)

        lax.fori_loop(0, nkc, body, jnp.zeros((tq, 1), F32))

    def bias_of(c, stored):
        sc = keys_ref[c]
        return sc if stored else jnp.where(sc >= thr_f, 0.0, neg)

    def qk(k0):
        return lax.dot_general(q4_ref[...], k_ref[pl.ds(k0, sub), :], nt, preferred_element_type=F32)

    def max_pass(stored):
        run_ref[...] = jnp.full(run_ref.shape, neg, F32)

        def body(c, carry):
            bias = bias_of(c, stored)
            for u in range(kc // sub):
                s = qk(pl.multiple_of(c * kc + u * sub, sub))
                for h in range(DSA_HEADS):
                    run_ref[h] = lane_fold(jnp.maximum, run_ref[h],
                                           s[h * tq:(h + 1) * tq] + bias[:, u * sub:(u + 1) * sub])
            return carry

        lax.fori_loop(0, nkc, body, 0)
        return [jnp.max(run_ref[h], axis=1, keepdims=True) for h in range(DSA_HEADS)]

    def att_pass(stored, ms):
        run_ref[...] = jnp.zeros_like(run_ref)
        acc_ref[...] = jnp.zeros_like(acc_ref)

        def half(s_ref, bias, k0):
            ps = []
            for h in range(DSA_HEADS):
                p = jnp.exp(s_ref[h * tq:(h + 1) * tq, :] + bias - ms[h])
                run_ref[h] = lane_fold(jnp.add, run_ref[h], p)
                ps.append(p.astype(BF16))
            acc_ref[...] += jnp.dot(jnp.concatenate(ps, axis=0), v_ref[pl.ds(k0, sub), :],
                                    preferred_element_type=F32)

        sa_ref[...] = qk(0)

        def body(c, carry):
            k0 = pl.multiple_of(c * kc, kc)
            sb_ref[...] = qk(k0 + sub)
            bias = bias_of(c, stored)
            half(sa_ref, bias[:, :sub], k0)
            sa_ref[...] = qk(pl.multiple_of(jnp.minimum(c + 1, nkc - 1) * kc, kc))
            half(sb_ref, bias[:, sub:], k0 + sub)
            return carry

        lax.fori_loop(0, nkc, body, 0)

    def finish(stored):
        if stored:
            tie_bias_pass()
        att_pass(stored, ms_bound)
        lmin = jnp.min(jnp.sum(run_ref[0], axis=1, keepdims=True))
        for h in range(1, DSA_HEADS):
            lmin = jnp.minimum(lmin, jnp.min(jnp.sum(run_ref[h], axis=1, keepdims=True)))

        def redo():
            att_pass(stored, max_pass(stored))
            return 0

        lax.cond(lmin >= 1e-26, lambda: 0, redo)
        return 0

    lax.cond(any_tie, lambda: finish(True), lambda: finish(False))
    out = jnp.zeros((tq, DSA_WIDTH), F32)
    for h in range(DSA_HEADS):
        m = (lane_w >= h * DSA_HEAD_DIM) & (lane_w < (h + 1) * DSA_HEAD_DIM)
        out = out + jnp.where(m, acc_ref[h * tq:(h + 1) * tq, :] / jnp.sum(run_ref[h], axis=1, keepdims=True), 0.0)
    o_ref[...] = _rms(out, ng_ref[...]).astype(o_ref.dtype)


def _dsa(p, norm_g, batch, seq, tq=512, kc=512, top_r=12):
    n_sel = min(DSA_TOPK, seq // 4)
    kc = min(kc, seq)
    sub = kc // 2
    r3 = lambda a: a.reshape(batch, seq, a.shape[-1])
    qblk = lambda w: pl.BlockSpec((None, tq, w), lambda b, i: (b, i, 0))
    kfull = lambda w: pl.BlockSpec((None, seq, w), lambda b, i: (b, 0, 0), pipeline_mode=pl.Buffered(1))
    out = pl.pallas_call(
        functools.partial(_dsa_kernel, tq=tq, kc=kc, sub=sub, n_sel=n_sel, top_r=top_r),
        grid=(batch, seq // tq),
        in_specs=[qblk(DSA_WIDTH), kfull(DSA_WIDTH), qblk(LANES), qblk(DSA_WIDTH),
                  kfull(DSA_WIDTH), kfull(DSA_WIDTH), pl.BlockSpec((1, DSA_WIDTH), lambda b, i: (0, 0))],
        out_specs=qblk(DSA_WIDTH),
        out_shape=jax.ShapeDtypeStruct((batch, seq, DSA_WIDTH), BF16),
        scratch_shapes=[
            pltpu.VMEM((DSA_IDX_HEADS * tq, DSA_WIDTH), BF16),
            pltpu.VMEM((DSA_HEADS * tq, DSA_WIDTH), BF16),
            pltpu.VMEM((seq // kc, tq, kc), F32),
            pltpu.VMEM((DSA_HEADS * tq, DSA_WIDTH), F32),
            pltpu.VMEM((DSA_HEADS, tq, LANES), F32),
            pltpu.VMEM((top_r, tq, LANES), I32),
            pltpu.VMEM((DSA_HEADS * tq, sub), F32),
            pltpu.VMEM((DSA_HEADS * tq, sub), F32),
            pltpu.VMEM((SUBLANES, LANES), F32),
            pltpu.VMEM((top_r, tq, LANES), F32),
        ],
        compiler_params=_cparams("arbitrary", "arbitrary"),
        name="dsa",
    )(r3(p["iq"]), r3(p["ik"]), r3(p["iw"]), r3(p["dq"]), r3(p["dk"]), r3(p["dv"]),
      norm_g.reshape(1, DSA_WIDTH))
    return out.reshape(batch * seq, DSA_WIDTH)


def _s5_tables(a_re, a_im, log_step, b_re, b_im, c_re, c_im):
    hp = lax.Precision.HIGHEST
    s = S5_CHUNK
    dt = jnp.exp(log_step)[:, None]
    mag = jnp.exp(a_re * dt)
    lam_re, lam_im = mag * jnp.cos(a_im * dt), mag * jnp.sin(a_im * dt)
    den = a_re * a_re + a_im * a_im
    nr, ni = lam_re - 1.0, lam_im
    coef_re = (nr * a_re + ni * a_im) / den
    coef_im = (ni * a_re - nr * a_im) / den
    bb_re = coef_re[..., None] * b_re - coef_im[..., None] * b_im
    bb_im = coef_re[..., None] * b_im + coef_im[..., None] * b_re
    j = jnp.arange(s + 1, dtype=F32)[:, None, None]
    pmag = jnp.exp(a_re * dt * j)
    pw_re, pw_im = pmag * jnp.cos(a_im * dt * j), pmag * jnp.sin(a_im * dt * j)
    e_re = c_re[None] * pw_re[:, :, None, :] - c_im[None] * pw_im[:, :, None, :]
    e_im = c_re[None] * pw_im[:, :, None, :] + c_im[None] * pw_re[:, :, None, :]
    kk = (jnp.einsum("jgcp,gpd->jgcd", e_re, bb_re, precision=hp)
          - jnp.einsum("jgcp,gpd->jgcd", e_im, bb_im, precision=hp))
    lag = np.arange(s)[None, :] - np.arange(s)[:, None]
    toe = jnp.where((lag >= 0)[:, :, None, None, None], kk[np.clip(lag, 0, s)], 0.0)
    rp_re, rp_im = pw_re[s - 1 - np.arange(s)], pw_im[s - 1 - np.arange(s)]
    x_re = rp_re[..., None] * bb_re[None] - rp_im[..., None] * bb_im[None]
    x_im = rp_re[..., None] * bb_im[None] + rp_im[..., None] * bb_re[None]
    f_re, f_im = e_re[1:], e_im[1:]
    sc_w = s * S5_GROUP_CH
    g_ids = jnp.arange(S5_TILE_G)[:, None, None]

    def placement(inner, width):
        a = jnp.arange(width // S5_TILE_G)[None, :, None]
        dst = (a // inner) * (S5_TILE_G * inner) + g_ids * inner + a % inner
        return (dst == jnp.arange(width)[None, None, :]).astype(F32)

    p_row = placement(S5_GROUP_CH, S5_ROW_W)
    p_state = placement(S5_STATE, S5_STATE_W)
    per_tile = lambda a: a.astype(BF16).astype(F32).reshape((S5_TILES, S5_TILE_G) + a.shape[1:])

    def place(tab, p_in, p_out):
        t = jnp.einsum("qgab,gbc->qgac", per_tile(tab), p_out)
        return jnp.einsum("gar,qgac->qrc", p_in, t).astype(BF16)

    g_toe = toe.transpose(2, 0, 4, 1, 3).reshape(S5_GROUPS, sc_w, sc_w)
    g_w1 = lambda x: x.transpose(1, 0, 3, 2).reshape(S5_GROUPS, sc_w, S5_STATE)
    g_w2 = lambda f: f.transpose(1, 3, 0, 2).reshape(S5_GROUPS, S5_STATE, sc_w)
    tabs = dict(toe=place(g_toe, p_row, p_row),
                w1_re=place(g_w1(x_re), p_row, p_state), w1_im=place(g_w1(x_im), p_row, p_state),
                w2_re=place(g_w2(f_re), p_state, p_row), w2_im=place(g_w2(-f_im), p_state, p_row))
    tabs["a_re"], tabs["a_im"] = pw_re[s].reshape(-1), pw_im[s].reshape(-1)
    return tabs


def _s5_state_in_kernel(u_ref, wr_ref, wi_ref, xr_ref, xi_ref):
    u = u_ref[...].astype(BF16)
    xr_ref[...] = jnp.dot(u, wr_ref[...], preferred_element_type=F32)
    xi_ref[...] = jnp.dot(u, wi_ref[...], preferred_element_type=F32)


def _s5_scan_kernel(xr_ref, xi_ref, ar_ref, ai_ref, hr_ref, hi_ref, cr_ref, ci_ref, *, steps):
    @pl.when(pl.program_id(0) == 0)
    def _():
        cr_ref[...] = jnp.zeros_like(cr_ref)
        ci_ref[...] = jnp.zeros_like(ci_ref)

    ar, ai = ar_ref[...], ai_ref[...]

    def body(n, carry):
        hr, hi = carry
        hr_ref[n] = hr
        hi_ref[n] = hi
        return ar * hr - ai * hi + xr_ref[n], ar * hi + ai * hr + xi_ref[n]

    hr, hi = lax.fori_loop(0, steps, body, (cr_ref[...], ci_ref[...]))
    cr_ref[...] = hr
    ci_ref[...] = hi


def _s5_out_kernel(u_ref, toe_ref, hr_ref, hi_ref, w2r_ref, w2i_ref, y_ref):
    y = jnp.dot(u_ref[...].astype(BF16), toe_ref[...], preferred_element_type=F32)
    y = y + jnp.dot(hr_ref[...].astype(BF16), w2r_ref[...], preferred_element_type=F32)
    y = y + jnp.dot(hi_ref[...].astype(BF16), w2i_ref[...], preferred_element_type=F32)
    y_ref[...] = y


def _s5_post_kernel(y_ref, u0_ref, u1_ref, u2_ref, d_ref, gw_ref, gb_ref, ng_ref, o_ref, yt_ref):
    n_chunks = yt_ref.shape[1] // S5_CHUNK
    for q in range(S5_TILES):
        for s in range(S5_CHUNK):
            yt_ref[q, pl.ds(s, n_chunks, stride=S5_CHUNK), :] = y_ref[q, :, s * LANES:(s + 1) * LANES]
    y = jnp.concatenate([yt_ref[q] for q in range(S5_TILES)], axis=1)
    u = jnp.concatenate([u0_ref[...], u1_ref[...], u2_ref[...]], axis=1)
    z = y + d_ref[...] * u
    z = 0.5 * z * (1.0 + jnp.tanh(math.sqrt(2.0 / math.pi) * (z + 0.044715 * (z * z * z))))
    gate = _sigmoid(jnp.dot(z.astype(BF16), gw_ref[...], preferred_element_type=F32) + gb_ref[...])
    o_ref[...] = _rms(z * gate, ng_ref[...]).astype(o_ref.dtype)


def _s5(us, xq, tabs, d, glu_w, glu_b, norm_g, batch, seq, tm=512, rb=512):
    s = S5_CHUNK
    nch = seq // s
    rows = nch * batch
    rb = min(rb, rows)
    gp = S5_GROUPS * S5_STATE
    row_in = pl.BlockSpec((None, rb, S5_ROW_W), lambda q, i: (q, i, 0))
    wspec = lambda a: pl.BlockSpec((None,) + a.shape[1:], lambda q, i: (q, 0, 0))
    colblk = pl.BlockSpec((rb, S5_STATE_W), lambda q, i: (i, q))
    xr, xi = pl.pallas_call(
        _s5_state_in_kernel,
        grid=(S5_TILES, rows // rb),
        in_specs=[row_in, wspec(tabs["w1_re"]), wspec(tabs["w1_im"])],
        out_specs=[colblk, colblk],
        out_shape=[jax.ShapeDtypeStruct((rows, gp), F32)] * 2,
        compiler_params=_cparams("arbitrary", "arbitrary"),
        name="s5_state_in",
    )(xq, tabs["w1_re"], tabs["w1_im"])
    parts = SUBLANES // batch
    wl = gp // parts
    slab = lambda a: a.reshape(batch, nch, gp).transpose(1, 0, 2).reshape(nch, SUBLANES, wl)
    unslab = lambda a: a.reshape(nch, batch, gp).transpose(1, 0, 2).reshape(rows, gp)
    coef = lambda a: jnp.tile(a.reshape(parts, wl), (batch, 1))
    steps = min(64, nch)
    sblk = pl.BlockSpec((steps, SUBLANES, wl), lambda i: (i, 0, 0))
    cblk = pl.BlockSpec((SUBLANES, wl), lambda i: (0, 0))
    hr, hi = pl.pallas_call(
        functools.partial(_s5_scan_kernel, steps=steps),
        grid=(nch // steps,),
        in_specs=[sblk, sblk, cblk, cblk],
        out_specs=[sblk, sblk],
        out_shape=[jax.ShapeDtypeStruct((nch, SUBLANES, wl), F32)] * 2,
        scratch_shapes=[pltpu.VMEM((SUBLANES, wl), F32)] * 2,
        compiler_params=_cparams("arbitrary"),
        name="s5_scan",
    )(slab(xr), slab(xi), coef(tabs["a_re"]), coef(tabs["a_im"]))
    y = pl.pallas_call(
        _s5_out_kernel,
        grid=(S5_TILES, rows // rb),
        in_specs=[row_in, wspec(tabs["toe"]), colblk, colblk, wspec(tabs["w2_re"]), wspec(tabs["w2_im"])],
        out_specs=row_in,
        out_shape=jax.ShapeDtypeStruct((S5_TILES, rows, S5_ROW_W), F32),
        compiler_params=_cparams("arbitrary", "arbitrary"),
        name="s5_out",
    )(xq, tabs["toe"], unslab(hr), unslab(hi), tabs["w2_re"], tabs["w2_im"])
    t = batch * seq
    tm = min(tm, t)
    tok = lambda w: pl.BlockSpec((tm, w), lambda i: (i, 0))
    vec = pl.BlockSpec((1, S5_WIDTH), lambda i: (0, 0))
    return pl.pallas_call(
        _s5_post_kernel,
        grid=(t // tm,),
        in_specs=[pl.BlockSpec((S5_TILES, tm // s, S5_ROW_W), lambda i: (0, i, 0)), tok(LANES), tok(LANES),
                  tok(LANES), vec, pl.BlockSpec((S5_WIDTH, S5_WIDTH), lambda i: (0, 0)), vec, vec],
        out_specs=tok(S5_WIDTH),
        out_shape=jax.ShapeDtypeStruct((t, S5_WIDTH), BF16),
        scratch_shapes=[pltpu.VMEM((S5_TILES, tm, LANES), F32)],
        compiler_params=_cparams("arbitrary"),
        name="s5_post",
    )(y, us[0], us[1], us[2], d.reshape(1, -1), glu_w.astype(BF16), glu_b.reshape(1, -1),
      norm_g.reshape(1, -1))


def _mix_out_kernel(og_ref, od_ref, os_ref, wg_ref, wd_ref, ws_ref, x_ref, gt_ref, n2_ref, sc_ref,
                    sh_ref, rw1_ref, rw2_ref, rb_ref, x1_ref, h2_ref, route_ref, cnt_ref, carry_ref, *, tm):
    @pl.when(pl.program_id(0) == 0)
    def _():
        carry_ref[...] = jnp.zeros_like(carry_ref)

    mix = jnp.dot(og_ref[...], wg_ref[...], preferred_element_type=F32)
    mix = mix + jnp.dot(od_ref[...], wd_ref[...], preferred_element_type=F32)
    mix = mix + jnp.dot(os_ref[...], ws_ref[...], preferred_element_type=F32)
    x1 = x_ref[...] + gt_ref[...] * mix
    x1_ref[...] = x1
    h2 = _rms(x1, n2_ref[...]) * (1.0 + sc_ref[...]) + sh_ref[...]
    h2_ref[...] = h2
    ha = h2.astype(BF16)
    hb = (h2 - ha.astype(F32)).astype(BF16)
    logit = (jnp.dot(ha, rw1_ref[...], preferred_element_type=F32)
             + jnp.dot(ha, rw2_ref[...], preferred_element_type=F32)
             + jnp.dot(hb, rw1_ref[...], preferred_element_type=F32)) + rb_ref[...]
    lane = lax.broadcasted_iota(I32, (tm, LANES), 1)
    lanef = lane.astype(F32)
    ninf = jnp.float32(-jnp.inf)
    big = jnp.float32(LANES)
    gl = jnp.where(lane < MOE_GROUPS, logit, ninf)
    gmax = jnp.max(gl, axis=1, keepdims=True)
    grp_p = 1.0 / jnp.sum(jnp.exp(gl - gmax), axis=1, keepdims=True)
    grp_i = jnp.min(jnp.where(gl == gmax, lanef, big), axis=1, keepdims=True)
    lo = MOE_GROUPS + MOE_EXPERTS_PER_GROUP * grp_i
    el = jnp.where((lanef >= lo) & (lanef < lo + MOE_EXPERTS_PER_GROUP), logit, ninf)
    l1 = jnp.max(el, axis=1, keepdims=True)
    i1 = jnp.min(jnp.where(el == l1, lanef, big), axis=1, keepdims=True)
    el2 = jnp.where(lanef == i1, ninf, el)
    l2 = jnp.max(el2, axis=1, keepdims=True)
    i2 = jnp.min(jnp.where(el2 == l2, lanef, big), axis=1, keepdims=True)
    e21 = jnp.exp(l2 - l1)
    w1 = grp_p / (1.0 + e21)
    w2 = grp_p * e21 / (1.0 + e21)
    e1 = i1 - MOE_GROUPS
    e2 = i2 - MOE_GROUPS
    oh1 = lanef == e1
    oh2 = lanef == e2
    hot = jnp.where(oh1 | oh2, 1.0, 0.0)
    ri = lax.broadcasted_iota(I32, (tm, tm), 0)
    ci = lax.broadcasted_iota(I32, (tm, tm), 1)
    before = (ci < ri).astype(BF16)
    prior = carry_ref[...] + jnp.dot(before, hot.astype(BF16), preferred_element_type=F32)
    r1 = jnp.sum(jnp.where(oh1, prior, 0.0), axis=1, keepdims=True)
    r2 = jnp.sum(jnp.where(oh2, prior, 0.0), axis=1, keepdims=True)
    total = carry_ref[...] + jnp.sum(hot, axis=0, keepdims=True)
    carry_ref[...] = total
    cnt_ref[...] = jnp.broadcast_to(total, cnt_ref.shape)
    route = jnp.where(lane == 0, e1, jnp.where(lane == 1, e2, jnp.where(lane == 2, w1, jnp.where(
        lane == 3, w2, jnp.where(lane == 4, r1, jnp.where(lane == 5, r2, 0.0))))))
    route_ref[...] = route


def _mix_out(o_gla, o_dsa, o_s5, w_out, x2, gt1, norm2_g, sc2, sh2, rgw, rgb, rew, reb, seq, tm=512):
    t, d = x2.shape
    per_b = seq // tm
    wg = jnp.pad(w_out[:GLA_HEADS * GLA_DV].reshape(GLA_HEADS, GLA_DV, d),
                 ((0, 0), (0, HEAD_PAD - GLA_DV), (0, 0))).reshape(GLA_W, d).astype(BF16)
    wd = w_out[GLA_HEADS * GLA_DV:GLA_HEADS * GLA_DV + DSA_WIDTH].astype(BF16)
    ws = w_out[GLA_HEADS * GLA_DV + DSA_WIDTH:].astype(BF16)
    rw = _pad_cols(jnp.concatenate([rgw, rew], axis=1), LANES)
    rb = _pad_cols(jnp.concatenate([rgb, reb]).reshape(1, -1), LANES)
    rw1 = rw.astype(BF16)
    rw2 = (rw - rw1.astype(F32)).astype(BF16)
    tok = lambda w: pl.BlockSpec((tm, w), lambda i: (i, 0))
    full = lambda a: pl.BlockSpec(a.shape, lambda i: (0, 0))
    perb = pl.BlockSpec((None, 1, d), lambda i: (i // per_b, 0, 0))
    vec = pl.BlockSpec((1, d), lambda i: (0, 0))
    return pl.pallas_call(
        functools.partial(_mix_out_kernel, tm=tm),
        grid=(t // tm,),
        in_specs=[tok(GLA_W), tok(DSA_WIDTH), tok(S5_WIDTH), full(wg), full(wd), full(ws), tok(d),
                  perb, vec, perb, perb, full(rw1), full(rw2), full(rb)],
        out_specs=[tok(d), tok(d), tok(LANES), pl.BlockSpec((SUBLANES, LANES), lambda i: (0, 0))],
        out_shape=[jax.ShapeDtypeStruct((t, d), F32), jax.ShapeDtypeStruct((t, d), F32),
                   jax.ShapeDtypeStruct((t, LANES), F32), jax.ShapeDtypeStruct((SUBLANES, LANES), F32)],
        scratch_shapes=[pltpu.VMEM((1, LANES), F32)],
        compiler_params=_cparams("arbitrary"),
        name="mix_out",
    )(o_gla, o_dsa, o_s5, wg, wd, ws, x2, gt1[:, None, :], norm2_g.reshape(1, d), sc2[:, None, :],
      sh2[:, None, :], rw1, rw2, rb)


def _row(ref, r):
    return ref.at[pl.ds(r, 1), :]


def _dispatch_kernel(dest_ref, h_ref, zero_ref, out_ref, sem, *, td):
    del zero_ref

    def start(g, c):
        for j in range(SUBLANES):
            r = g * SUBLANES + j
            pltpu.make_async_copy(_row(h_ref, r), _row(out_ref, dest_ref[0, r]), sem).start()
            pltpu.make_async_copy(_row(h_ref, r), _row(out_ref, dest_ref[1, r]), sem).start()
        return c

    lax.fori_loop(0, td // SUBLANES, start, 0)
    for _ in range(2):
        pltpu.make_async_copy(h_ref, out_ref.at[pl.ds(0, td), :], sem).wait()


def _dispatch(h2, dest, n_rows, td=256):
    t, d = h2.shape
    dest3 = dest.reshape(2, t // td, td).transpose(1, 0, 2)
    zeros = jnp.zeros((n_rows, d), F32)
    return pl.pallas_call(
        functools.partial(_dispatch_kernel, td=td),
        grid=(t // td,),
        in_specs=[pl.BlockSpec((None, 2, td), lambda i: (i, 0, 0), memory_space=pltpu.SMEM),
                  pl.BlockSpec((td, d), lambda i: (i, 0)),
                  pl.BlockSpec(memory_space=pl.ANY)],
        out_specs=pl.BlockSpec(memory_space=pl.ANY),
        out_shape=jax.ShapeDtypeStruct((n_rows, d), F32),
        scratch_shapes=[pltpu.SemaphoreType.DMA(())],
        input_output_aliases={2: 0},
        compiler_params=_cparams("arbitrary"),
        name="moe_dispatch",
    )(dest3, h2, zeros)


def _expert_kernel(be_ref, nb_ref, x_ref, wg_ref, wu_ref, wd_ref, o_ref):
    del be_ref
    i = pl.program_id(0)

    @pl.when(i < nb_ref[0])
    def _():
        xb = x_ref[...].astype(BF16)
        g = jnp.dot(xb, wg_ref[...].astype(BF16), preferred_element_type=F32)
        u = jnp.dot(xb, wu_ref[...].astype(BF16), preferred_element_type=F32)
        hid = (g * _sigmoid(g) * u).astype(BF16)
        o_ref[...] = jnp.dot(hid, wd_ref[...].astype(BF16), preferred_element_type=F32)

    @pl.when(i >= nb_ref[0])
    def _():
        o_ref[...] = jnp.zeros_like(o_ref)


def _experts(xs, blk_e, n_used, wg, wu, wd, layer):
    n_rows = xs.shape[0]
    nb = n_rows // MOE_BM
    d, hdim = wg.shape[-2:]
    rows = pl.BlockSpec((MOE_BM, d), lambda i, be, nu: (i, 0))
    wspec = lambda a, b: pl.BlockSpec((None, None, a, b), lambda i, be, nu: (layer, be[i], 0, 0))
    return pl.pallas_call(
        _expert_kernel,
        grid_spec=pltpu.PrefetchScalarGridSpec(
            num_scalar_prefetch=2,
            grid=(nb,),
            in_specs=[rows, wspec(d, hdim), wspec(d, hdim), wspec(hdim, d)],
            out_specs=rows,
        ),
        out_shape=jax.ShapeDtypeStruct((n_rows, d), F32),
        compiler_params=_cparams("arbitrary"),
        name="moe_experts",
    )(blk_e, n_used, xs, wg, wu, wd)


def _combine_kernel(dest_ref, next_ref, x_ref, route_ref, gt_ref, fg_ref, eo_ref, o_ref, rows_ref, sem,
                    *, tc, final):
    i = pl.program_id(0)
    slot = i % 2

    def issue(d_ref, s):
        def start(g, c):
            for j in range(SUBLANES):
                r = g * SUBLANES + j
                pltpu.make_async_copy(_row(eo_ref, d_ref[0, r]), _row(rows_ref.at[s, 0], r), sem.at[s]).start()
                pltpu.make_async_copy(_row(eo_ref, d_ref[1, r]), _row(rows_ref.at[s, 1], r), sem.at[s]).start()
            return c

        lax.fori_loop(0, tc // SUBLANES, start, 0)

    @pl.when(i == 0)
    def _():
        issue(dest_ref, 0)

    @pl.when(i + 1 < pl.num_programs(0))
    def _():
        issue(next_ref, 1 - slot)

    for which in range(2):
        pltpu.make_async_copy(eo_ref.at[pl.ds(0, tc), :], rows_ref.at[slot, which], sem.at[slot]).wait()
    route = route_ref[...]
    lane = lax.broadcasted_iota(I32, route.shape, 1)
    w1 = jnp.sum(jnp.where(lane == 2, route, 0.0), axis=1, keepdims=True)
    w2 = jnp.sum(jnp.where(lane == 3, route, 0.0), axis=1, keepdims=True)
    y = rows_ref[slot, 0] * w1 + rows_ref[slot, 1] * w2
    x = x_ref[...] + gt_ref[...] * y
    if final:
        x = _rms(x, fg_ref[...])
    o_ref[...] = x


def _combine(x1, route, dest, eo, gt2, final_g, seq, final, tc=256):
    t, d = x1.shape
    per_b = seq // tc
    steps = t // tc
    dest3 = dest.reshape(2, steps, tc).transpose(1, 0, 2)
    return pl.pallas_call(
        functools.partial(_combine_kernel, tc=tc, final=final),
        grid=(steps,),
        in_specs=[pl.BlockSpec((None, 2, tc), lambda i: (i, 0, 0), memory_space=pltpu.SMEM),
                  pl.BlockSpec((None, 2, tc), lambda i: (jnp.minimum(i + 1, steps - 1), 0, 0),
                               memory_space=pltpu.SMEM),
                  pl.BlockSpec((tc, d), lambda i: (i, 0)),
                  pl.BlockSpec((tc, LANES), lambda i: (i, 0)),
                  pl.BlockSpec((None, 1, d), lambda i: (i // per_b, 0, 0)),
                  pl.BlockSpec((1, d), lambda i: (0, 0)),
                  pl.BlockSpec(memory_space=pl.ANY)],
        out_specs=pl.BlockSpec((tc, d), lambda i: (i, 0)),
        out_shape=jax.ShapeDtypeStruct((t, d), F32),
        scratch_shapes=[pltpu.VMEM((2, 2, tc, d), F32), pltpu.SemaphoreType.DMA((2,))],
        compiler_params=_cparams("arbitrary"),
        name="moe_combine",
    )(dest3, dest3, x1, route, gt2[:, None, :], final_g.reshape(1, d), eo)


def _moe_plan(route, counts, t):
    e = route[:, 0:2].astype(I32).T
    rank = route[:, 4:6].astype(I32).T
    cnt = counts[0, :MOE_EXPERTS].astype(I32)
    padded = (cnt + MOE_BM - 1) // MOE_BM * MOE_BM
    pend = jnp.cumsum(padded)
    pstart = pend - padded
    ids = jnp.arange(MOE_EXPERTS, dtype=I32)
    dest = rank + jnp.sum(jnp.where(e[..., None] == ids, pstart, 0), axis=-1)
    nb = (2 * t) // MOE_BM + MOE_EXPERTS
    n_used = (pend[-1] // MOE_BM).astype(I32)
    first_row = jnp.minimum(jnp.arange(nb, dtype=I32), n_used - 1) * MOE_BM
    blk_e = jnp.sum((pend[None, :] <= first_row[:, None]).astype(I32), axis=1)
    blk_e = jnp.minimum(blk_e, MOE_EXPERTS - 1)
    return dest, blk_e, n_used.reshape(1), nb * MOE_BM


def kernel(x, c, ada_w, ada_b, norm1_g, w_in, gla_gate_w, gla_gate_b, gla_norm_g, dsa_norm_g, s5_a_re, s5_a_im, s5_log_step, s5_b_re, s5_b_im, s5_c_re, s5_c_im, s5_d, s5_glu_w, s5_glu_b, s5_norm_g, w_out, norm2_g, router_grp_w, router_grp_b, router_exp_w, router_exp_b, exp_w_gate, exp_w_up, exp_w_down, final_norm_g):
    batch, seq, d = x.shape
    depth = ada_w.shape[0]
    t = batch * seq
    mod = _ada_mod(c, ada_w, ada_b)
    x2 = x.reshape(t, d)
    for l in range(depth):
        sh1, sc1, gt1, sh2, sc2, gt2 = (mod[l, j] for j in range(6))
        p = _in_proj(x2, norm1_g[l], sc1, sh1, _layout_w_in(w_in[l]), seq)
        o_gla = _gla(p, gla_gate_w[l], gla_gate_b[l], gla_norm_g[l], batch, seq)
        o_dsa = _dsa(p, dsa_norm_g[l], batch, seq)
        tabs = _s5_tables(s5_a_re[l], s5_a_im[l], s5_log_step[l], s5_b_re[l], s5_b_im[l],
                          s5_c_re[l], s5_c_im[l])
        o_s5 = _s5((p["su0"], p["su1"], p["su2"]), p["sx"], tabs, s5_d[l], s5_glu_w[l], s5_glu_b[l], s5_norm_g[l], batch, seq)
        x1, h2, route, counts = _mix_out(o_gla, o_dsa, o_s5, w_out[l], x2, gt1, norm2_g[l], sc2, sh2,
                                         router_grp_w[l], router_grp_b[l], router_exp_w[l],
                                         router_exp_b[l], seq)
        dest, blk_e, n_used, n_rows = _moe_plan(route, counts, t)
        xs = _dispatch(h2, dest, n_rows)
        eo = _experts(xs, blk_e, n_used, exp_w_gate, exp_w_up, exp_w_down, l)
        x2 = _combine(x1, route, dest, eo, gt2, final_norm_g, seq, final=(l == depth - 1))
    return x2.reshape(batch, seq, d)
```

```python
import functools
import math

import jax
import jax.numpy as jnp
import numpy as np
from jax import lax
from jax.experimental import pallas as pl
from jax.experimental.pallas import tpu as pltpu

F32 = jnp.float32
BF16 = jnp.bfloat16
I32 = jnp.int32

D_MODEL = 1024
GLA_HEADS = 4
GLA_DV = 96
GLA_DK = 48
GLA_RANK = 16
GLA_TAU = 16.0
GLA_CHUNK = 64
DSA_HEADS = 4
DSA_HEAD_DIM = 64
DSA_WIDTH = DSA_HEADS * DSA_HEAD_DIM
DSA_IDX_HEADS = 8
DSA_IDX_DIM = 32
DSA_TOPK = 256
S5_GROUPS = 24
S5_GROUP_CH = 16
S5_STATE = 64
S5_WIDTH = S5_GROUPS * S5_GROUP_CH
MOE_GROUPS = 4
MOE_EXPERTS_PER_GROUP = 8
MOE_EXPERTS = MOE_GROUPS * MOE_EXPERTS_PER_GROUP
MOE_HIDDEN = 512
RMS_EPS = 1e-6
IN_SIZES = (192, 192, 384, 16, 384, 256, 256, 256, 256, 32, 8, 384)

LANES = 128
SUBLANES = 8
VMEM_LIMIT = 56 * 1024 * 1024

HEAD_PAD = LANES
GLA_W = GLA_HEADS * HEAD_PAD
S5_CHUNK = 16
S5_TILE_G = LANES // S5_GROUP_CH
S5_TILES = S5_GROUPS // S5_TILE_G
S5_ROW_W = S5_CHUNK * LANES
S5_STATE_W = S5_TILE_G * S5_STATE
MOE_BM = 512
INT_MIN = -2 ** 31


def _cparams(*sem):
    return pltpu.CompilerParams(dimension_semantics=tuple(sem), vmem_limit_bytes=VMEM_LIMIT)


def _sigmoid(x):
    return 0.5 * (jnp.tanh(0.5 * x) + 1.0)


def _rms(x, g, n=None):
    n = x.shape[-1] if n is None else n
    ms = jnp.sum(x * x, axis=-1, keepdims=True) * (1.0 / n)
    return x * lax.rsqrt(ms + RMS_EPS) * g


def _ada_kernel(c_ref, w_ref, b_ref, o_ref):
    c = c_ref[...]
    cond = (c * _sigmoid(c)).astype(BF16)
    o_ref[...] = jnp.dot(cond, w_ref[...].astype(BF16), preferred_element_type=F32) + b_ref[...]


def _ada_mod(c, ada_w, ada_b):
    depth, d, d6 = ada_w.shape
    nblk = d6 // d
    bp = -(-c.shape[0] // SUBLANES) * SUBLANES
    cp = jnp.pad(c, ((0, bp - c.shape[0]), (0, 0)))
    out = pl.pallas_call(
        _ada_kernel,
        grid=(depth, nblk),
        in_specs=[
            pl.BlockSpec((bp, d), lambda l, j: (0, 0)),
            pl.BlockSpec((None, d, d), lambda l, j: (l, 0, j)),
            pl.BlockSpec((None, None, 1, d), lambda l, j: (l, j, 0, 0)),
        ],
        out_specs=pl.BlockSpec((None, None, bp, d), lambda l, j: (l, j, 0, 0)),
        out_shape=jax.ShapeDtypeStruct((depth, nblk, bp, d), F32),
        compiler_params=_cparams("arbitrary", "arbitrary"),
        name="ada_mod",
    )(cp, ada_w, ada_b.reshape(depth, nblk, 1, d))
    return out[:, :, : c.shape[0], :]


PROJ_OUTS = (
    ("gq", GLA_W, BF16), ("gk", GLA_W, BF16), ("gv", GLA_W, BF16), ("gr", GLA_W, BF16),
    ("glr", LANES, BF16),
    ("dq", DSA_WIDTH, BF16), ("dk", DSA_WIDTH, BF16), ("dv", DSA_WIDTH, BF16),
    ("iq", DSA_IDX_HEADS * DSA_IDX_DIM, BF16), ("ik", DSA_IDX_HEADS * DSA_IDX_DIM, BF16),
    ("iw", LANES, F32), ("su0", LANES, F32), ("su1", LANES, F32), ("su2", LANES, F32),
)
PROJ_COLS = sum(w for _, w, _ in PROJ_OUTS)


def _pad_heads(w, heads, dim):
    d = w.shape[0]
    w = w.reshape(d, heads, dim)
    return jnp.pad(w, ((0, 0), (0, 0), (0, HEAD_PAD - dim))).reshape(d, heads * HEAD_PAD)


def _pad_cols(w, width):
    return jnp.pad(w, ((0, 0), (0, width - w.shape[1])))


def _layout_w_in(w_in):
    offs = np.cumsum((0,) + IN_SIZES)
    p = [w_in[:, offs[i]:offs[i + 1]] for i in range(len(IN_SIZES))]
    cols = [
        _pad_heads(p[0], GLA_HEADS, GLA_DK), _pad_heads(p[1], GLA_HEADS, GLA_DK),
        _pad_heads(p[2], GLA_HEADS, GLA_DV), _pad_heads(p[4], GLA_HEADS, GLA_DV),
        _pad_cols(p[3], LANES),
        p[5], p[6], p[7], p[8], jnp.tile(p[9], (1, DSA_IDX_HEADS)),
        _pad_cols(p[10], LANES), p[11],
    ]
    return jnp.concatenate(cols, axis=1).astype(BF16)


def _proj_kernel(x_ref, g_ref, sc_ref, sh_ref, w_ref, *o_refs):
    x = x_ref[...]
    h = _rms(x, g_ref[...]) * (1.0 + sc_ref[...]) + sh_ref[...]
    hb = h.astype(BF16)
    c0 = 0
    for o_ref, (_, width, _) in zip(o_refs, PROJ_OUTS):
        o_ref[...] = jnp.dot(hb, w_ref[:, c0:c0 + width], preferred_element_type=F32).astype(o_ref.dtype)
        c0 += width
    sx_ref = o_refs[len(PROJ_OUTS)]
    n_chunks = x.shape[0] // S5_CHUNK
    for q in range(S5_TILES):
        su_ref = o_refs[len(PROJ_OUTS) - S5_TILES + q]
        for s in range(S5_CHUNK):
            sx_ref[q, :, s * LANES:(s + 1) * LANES] = su_ref[pl.ds(s, n_chunks, stride=S5_CHUNK), :]


def _in_proj(x2, norm_g, sc, sh, w_lay, seq, tm=512):
    t, d = x2.shape
    per_b = seq // tm
    outs = pl.pallas_call(
        _proj_kernel,
        grid=(t // tm,),
        in_specs=[
            pl.BlockSpec((tm, d), lambda i: (i, 0)),
            pl.BlockSpec((1, d), lambda i: (0, 0)),
            pl.BlockSpec((None, 1, d), lambda i: (i // per_b, 0, 0)),
            pl.BlockSpec((None, 1, d), lambda i: (i // per_b, 0, 0)),
            pl.BlockSpec((d, PROJ_COLS), lambda i: (0, 0)),
        ],
        out_specs=[pl.BlockSpec((tm, w), lambda i: (i, 0)) for _, w, _ in PROJ_OUTS]
        + [pl.BlockSpec((S5_TILES, tm // S5_CHUNK, S5_ROW_W), lambda i: (0, i, 0))],
        out_shape=[jax.ShapeDtypeStruct((t, w), dt) for _, w, dt in PROJ_OUTS]
        + [jax.ShapeDtypeStruct((S5_TILES, t // S5_CHUNK, S5_ROW_W), F32)],
        compiler_params=_cparams("arbitrary"),
        name="in_proj",
    )(x2, norm_g.reshape(1, d), sc[:, None, :], sh[:, None, :], w_lay)
    p = {name: o for (name, _, _), o in zip(PROJ_OUTS, outs)}
    p["sx"] = outs[-1]
    return p


def _gla_kernel(q_ref, k_ref, v_ref, r_ref, lr_ref, gw_ref, gb_ref, ng_ref, o_ref, s_ref, *, rows):
    @pl.when(pl.program_id(1) == 0)
    def _():
        s_ref[...] = jnp.zeros_like(s_ref)

    c = GLA_CHUNK
    gl = jnp.dot(lr_ref[...], gw_ref[...], preferred_element_type=F32) + gb_ref[...]
    g = (jnp.minimum(gl, 0.0) - jnp.log1p(jnp.exp(-jnp.abs(gl)))) * (1.0 / GLA_TAU)
    g1 = g.astype(BF16)
    rem = g - g1.astype(F32)
    g2 = rem.astype(BF16)
    g3 = (rem - g2.astype(F32)).astype(BF16)
    ri = lax.broadcasted_iota(I32, (c, c), 0)
    ci = lax.broadcasted_iota(I32, (c, c), 1)
    causal = ri >= ci
    tri = causal.astype(BF16)
    ng = ng_ref[...]
    for ch in range(rows // c):
        rs = slice(ch * c, (ch + 1) * c)
        bcum = (jnp.dot(tri, g1[rs], preferred_element_type=F32)
                + jnp.dot(tri, g2[rs], preferred_element_type=F32)
                + jnp.dot(tri, g3[rs], preferred_element_type=F32))
        blast = bcum[c - 1:c, :]
        e_pos = jnp.exp(bcum)
        kf = k_ref[rs, :].astype(F32)
        qd = (q_ref[rs, :].astype(F32) * (GLA_DK ** -0.5) * e_pos).astype(BF16)
        kd = (kf * jnp.exp(-bcum)).astype(BF16)
        ku = (kf * jnp.exp(blast - bcum)).astype(BF16)
        dec = jnp.exp(blast)
        for h in range(GLA_HEADS):
            hs = slice(h * HEAD_PAD, (h + 1) * HEAD_PAD)
            vh = v_ref[rs, hs]
            att = lax.dot_general(qd[:, hs], kd[:, hs], (((1,), (1,)), ((), ())),
                                  preferred_element_type=F32)
            att = jnp.where(causal, att, 0.0).astype(BF16)
            st = s_ref[h]
            o = jnp.dot(att, vh, preferred_element_type=F32)
            o = o + lax.dot_general(qd[:, hs], st.astype(BF16), (((1,), (1,)), ((), ())),
                                    preferred_element_type=F32)
            upd = lax.dot_general(vh, ku[:, hs], (((0,), (0,)), ((), ())),
                                  preferred_element_type=F32)
            s_ref[h] = st * dec[:, hs] + upd
            y = _rms(o, ng[:, hs], GLA_DV)
            rr = r_ref[rs, hs].astype(F32)
            o_ref[rs, hs] = (y * (rr * _sigmoid(rr))).astype(o_ref.dtype)


def _gla(p, gate_w, gate_b, norm_g, batch, seq, rows=512):
    gw = _pad_cols(jnp.pad(_pad_heads(gate_w, GLA_HEADS, GLA_DK), ((0, LANES - GLA_RANK), (0, 0))), GLA_W)
    gb = _pad_heads(gate_b.reshape(1, -1), GLA_HEADS, GLA_DK)
    ng = _pad_heads(jnp.tile(norm_g.reshape(1, GLA_DV), (1, GLA_HEADS)), GLA_HEADS, GLA_DV)
    rows = min(rows, seq)
    blk = lambda w: pl.BlockSpec((None, rows, w), lambda b, j: (b, j, 0))
    full = lambda a: pl.BlockSpec(a.shape, lambda b, j: (0, 0))
    r3 = lambda a: a.reshape(batch, seq, a.shape[-1])
    out = pl.pallas_call(
        functools.partial(_gla_kernel, rows=rows),
        grid=(batch, seq // rows),
        in_specs=[blk(GLA_W), blk(GLA_W), blk(GLA_W), blk(GLA_W), blk(LANES),
                  full(gw), full(gb), full(ng)],
        out_specs=blk(GLA_W),
        out_shape=jax.ShapeDtypeStruct((batch, seq, GLA_W), BF16),
        scratch_shapes=[pltpu.VMEM((GLA_HEADS, HEAD_PAD, HEAD_PAD), F32)],
        compiler_params=_cparams("arbitrary", "arbitrary"),
        name="gla",
    )(r3(p["gq"]), r3(p["gk"]), r3(p["gv"]), r3(p["gr"]), r3(p["glr"]), gw.astype(BF16), gb, ng)
    return out.reshape(batch * seq, GLA_W)


def _dsa_kernel(iq_ref, ik_ref, iw_ref, q_ref, k_ref, v_ref, ng_ref, o_ref,
                qs_ref, q4_ref, keys_ref, acc_ref, run_ref, top_ref, sa_ref, sb_ref, kmax_ref, topf_ref,
                *, tq, kc, sub, n_sel, top_r):
    assert kc == 2 * sub
    qi = pl.program_id(1)
    nkc = ((qi + 1) * tq + kc - 1) // kc
    lane_w = lax.broadcasted_iota(I32, (tq, DSA_WIDTH), 1)
    iq = iq_ref[...]
    for h in range(DSA_IDX_HEADS):
        m = (lane_w >= h * DSA_IDX_DIM) & (lane_w < (h + 1) * DSA_IDX_DIM)
        qs_ref[h * tq:(h + 1) * tq, :] = jnp.where(m, iq, jnp.zeros_like(iq))
    iw = iw_ref[...] * (DSA_IDX_HEADS ** -0.5 * DSA_IDX_DIM ** -0.5)
    lane = lax.broadcasted_iota(I32, (tq, LANES), 1)
    w_cols = [iw[:, h:h + 1] for h in range(DSA_IDX_HEADS)]
    qpos = qi * tq + lax.broadcasted_iota(I32, (tq, sub), 0)
    kcol = lax.broadcasted_iota(I32, (tq, sub), 1)
    nt = (((1,), (1,)), ((), ()))

    ninf = jnp.float32(-jnp.inf)

    def score_chunk(c, causal=True):
        for u in range(kc // sub):
            k0 = pl.multiple_of(c * kc + u * sub, sub)
            lg = lax.dot_general(qs_ref[...], ik_ref[pl.ds(k0, sub), :], nt,
                                 preferred_element_type=F32)
            sc = jnp.zeros((tq, sub), F32)
            for h in range(DSA_IDX_HEADS):
                sc = sc + w_cols[h] * jnp.maximum(lg[h * tq:(h + 1) * tq, :], 0.0)
            sc = sc + 0.0
            if causal:
                sc = jnp.where(kcol + k0 <= qpos, sc, ninf)
            keys_ref[c, :, u * sub:(u + 1) * sub] = sc

    topf_ref[...] = jnp.full(topf_ref.shape, ninf, F32)

    def insert_chunk(c):
        for rt in range(tq // SUBLANES):
            rs = slice(rt * SUBLANES, (rt + 1) * SUBLANES)
            ts = [topf_ref[r, rs, :] for r in range(top_r)]
            for j in range(kc // LANES):
                x = keys_ref[c, rs, j * LANES:(j + 1) * LANES]
                for r in range(top_r):
                    hi = jnp.maximum(ts[r], x)
                    x = jnp.minimum(ts[r], x)
                    ts[r] = hi
            for r in range(top_r):
                topf_ref[r, rs, :] = ts[r]

    score_chunk(0)

    def fused_body(causal):
        def body(c, carry):
            insert_chunk(c - 1)
            score_chunk(c, causal)
            return carry
        return body

    n_open = jnp.maximum((qi * tq + 1) // kc, 1)
    lax.fori_loop(1, n_open, fused_body(False), 0)
    lax.fori_loop(n_open, nkc, fused_body(True), 0)
    insert_chunk(nkc - 1)

    def to_key(x):
        bits = lax.bitcast_convert_type(x, I32)
        key = jnp.where(bits < 0, bits ^ jnp.int32(0x7FFFFFFF), bits)
        return jnp.where(x == ninf, jnp.int32(INT_MIN), key)

    def count_ge(cand):
        candb = jnp.broadcast_to(cand, (tq, LANES))

        def body(c, acc):
            blk = keys_ref[c]
            for j in range(kc // LANES):
                acc = acc + jnp.where(to_key(blk[:, j * LANES:(j + 1) * LANES]) >= candb, 1.0, 0.0)
            return acc

        acc = lax.fori_loop(0, nkc, body, jnp.zeros((tq, LANES), F32))
        return jnp.sum(acc, axis=1, keepdims=True)

    need = jnp.float32(n_sel)

    def select(count):
        t0 = jnp.full((tq, 1), INT_MIN, I32)
        zero = jnp.zeros((tq, 1), I32)
        t0 = jnp.where(count(zero) >= need, zero, t0)

        def bit_body(i, t):
            cand = t | (jnp.int32(1) << (30 - i))
            return jnp.where(count(cand) >= need, cand, t)

        t = lax.fori_loop(0, 31, bit_body, t0)
        t = jnp.maximum(t, jnp.int32(INT_MIN + 1))
        return t, count(t + 1), count(t)

    for r in range(top_r):
        top_ref[r] = to_key(topf_ref[r])

    def count_cand(cand):
        candb = jnp.broadcast_to(cand, (tq, LANES))
        acc = jnp.zeros((tq, LANES), F32)
        for r in range(top_r):
            acc = acc + jnp.where(top_ref[r] >= candb, 1.0, 0.0)
        return jnp.sum(acc, axis=1, keepdims=True)

    thr_c, n_gt_c, n_ge_c = select(count_cand)
    hidden = jnp.max(jnp.where(top_ref[top_r - 1] >= thr_c, 1.0, 0.0)) > 0.0
    thr, n_gt, n_ge = lax.cond(hidden, lambda: select(count_ge), lambda: (thr_c, n_gt_c, n_ge_c))
    n_tie = need - n_gt
    any_tie = jnp.max(n_ge) > need
    thr_bits = jnp.where(thr < 0, thr ^ jnp.int32(0x7FFFFFFF), thr)
    thr_f = jnp.where(thr == jnp.int32(INT_MIN + 1), jnp.float32(jnp.finfo(jnp.float32).min),
                      lax.bitcast_convert_type(thr_bits, F32))

    q = q_ref[...] * jnp.asarray(DSA_HEAD_DIM ** -0.5, BF16)
    for h in range(DSA_HEADS):
        m = (lane_w >= h * DSA_HEAD_DIM) & (lane_w < (h + 1) * DSA_HEAD_DIM)
        q4_ref[h * tq:(h + 1) * tq, :] = jnp.where(m, q, jnp.zeros_like(q))
    neg = jnp.float32(-1e30)

    def lane_fold(op, acc, s):
        for j in range(s.shape[1] // LANES):
            acc = op(acc, s[:, j * LANES:(j + 1) * LANES])
        return acc

    @pl.when(qi == 0)
    def _():
        kmax_ref[...] = jnp.zeros_like(kmax_ref)

    hl = lax.broadcasted_iota(I32, (DSA_WIDTH, LANES), 0) // DSA_HEAD_DIM
    head_of = (hl == lax.broadcasted_iota(I32, (DSA_WIDTH, LANES), 1)).astype(BF16)
    kb = k_ref[pl.ds(pl.multiple_of(qi * tq, tq), tq), :].astype(F32)
    kn2 = jnp.dot((kb * kb).astype(BF16), head_of, preferred_element_type=F32)
    kmax = jnp.maximum(kmax_ref[0:1, :], jnp.max(kn2, axis=0, keepdims=True))
    kmax_ref[...] = jnp.broadcast_to(kmax, kmax_ref.shape)
    qf = q.astype(F32)
    qn2 = jnp.dot((qf * qf).astype(BF16), head_of, preferred_element_type=F32)
    bound = jnp.sqrt(qn2 * kmax) * 1.02
    ms_bound = [jnp.sum(jnp.where(lane == h, bound, 0.0), axis=1, keepdims=True) for h in range(DSA_HEADS)]

    def tie_bias_pass():
        ui = lax.broadcasted_iota(I32, (kc, kc), 0)
        uj = lax.broadcasted_iota(I32, (kc, kc), 1)
        before = (ui < uj).astype(BF16)

        def body(c, seen):
            sc = keys_ref[c]
            eq = sc == thr_f
            eqf = jnp.where(eq, 1.0, 0.0)
            rank = seen + jnp.dot(eqf.astype(BF16), before, preferred_element_type=F32)
            sel = (sc > thr_f) | (eq & (rank < n_tie))
            keys_ref[c] = jnp.where(sel, 0.0, neg)
            return seen + jnp.sum(eqf, axis=1, keepdims=True)

        lax.fori_loop(0, nkc, body, jnp.zeros((tq, 1), F32))

    def bias_of(c, stored):
        sc = keys_ref[c]
        return sc if stored else jnp.where(sc >= thr_f, 0.0, neg)

    def qk(k0):
        return lax.dot_general(q4_ref[...], k_ref[pl.ds(k0, sub), :], nt, preferred_element_type=F32)

    def max_pass(stored):
        run_ref[...] = jnp.full(run_ref.shape, neg, F32)

        def body(c, carry):
            bias = bias_of(c, stored)
            for u in range(kc // sub):
                s = qk(pl.multiple_of(c * kc + u * sub, sub))
                for h in range(DSA_HEADS):
                    run_ref[h] = lane_fold(jnp.maximum, run_ref[h],
                                           s[h * tq:(h + 1) * tq] + bias[:, u * sub:(u + 1) * sub])
            return carry

        lax.fori_loop(0, nkc, body, 0)
        return [jnp.max(run_ref[h], axis=1, keepdims=True) for h in range(DSA_HEADS)]

    def att_pass(stored, ms):
        run_ref[...] = jnp.zeros_like(run_ref)
        acc_ref[...] = jnp.zeros_like(acc_ref)

        def half(s_ref, bias, k0):
            ps = []
            for h in range(DSA_HEADS):
                p = jnp.exp(s_ref[h * tq:(h + 1) * tq, :] + bias - ms[h])
                run_ref[h] = lane_fold(jnp.add, run_ref[h], p)
                ps.append(p.astype(BF16))
            acc_ref[...] += jnp.dot(jnp.concatenate(ps, axis=0), v_ref[pl.ds(k0, sub), :],
                                    preferred_element_type=F32)

        sa_ref[...] = qk(0)

        def body(c, carry):
            k0 = pl.multiple_of(c * kc, kc)
            sb_ref[...] = qk(k0 + sub)
            bias = bias_of(c, stored)
            half(sa_ref, bias[:, :sub], k0)
            sa_ref[...] = qk(pl.multiple_of(jnp.minimum(c + 1, nkc - 1) * kc, kc))
            half(sb_ref, bias[:, sub:], k0 + sub)
            return carry

        lax.fori_loop(0, nkc, body, 0)

    def finish(stored):
        if stored:
            tie_bias_pass()
        att_pass(stored, ms_bound)
        lmin = jnp.min(jnp.sum(run_ref[0], axis=1, keepdims=True))
        for h in range(1, DSA_HEADS):
            lmin = jnp.minimum(lmin, jnp.min(jnp.sum(run_ref[h], axis=1, keepdims=True)))

        def redo():
            att_pass(stored, max_pass(stored))
            return 0

        lax.cond(lmin >= 1e-26, lambda: 0, redo)
        return 0

    lax.cond(any_tie, lambda: finish(True), lambda: finish(False))
    out = jnp.zeros((tq, DSA_WIDTH), F32)
    for h in range(DSA_HEADS):
        m = (lane_w >= h * DSA_HEAD_DIM) & (lane_w < (h + 1) * DSA_HEAD_DIM)
        out = out + jnp.where(m, acc_ref[h * tq:(h + 1) * tq, :] / jnp.sum(run_ref[h], axis=1, keepdims=True), 0.0)
    o_ref[...] = _rms(out, ng_ref[...]).astype(o_ref.dtype)


def _dsa(p, norm_g, batch, seq, tq=512, kc=512, top_r=12):
    n_sel = min(DSA_TOPK, seq // 4)
    kc = min(kc, seq)
    sub = kc // 2
    r3 = lambda a: a.reshape(batch, seq, a.shape[-1])
    qblk = lambda w: pl.BlockSpec((None, tq, w), lambda b, i: (b, i, 0))
    kfull = lambda w: pl.BlockSpec((None, seq, w), lambda b, i: (b, 0, 0), pipeline_mode=pl.Buffered(1))
    out = pl.pallas_call(
        functools.partial(_dsa_kernel, tq=tq, kc=kc, sub=sub, n_sel=n_sel, top_r=top_r),
        grid=(batch, seq // tq),
        in_specs=[qblk(DSA_WIDTH), kfull(DSA_WIDTH), qblk(LANES), qblk(DSA_WIDTH),
                  kfull(DSA_WIDTH), kfull(DSA_WIDTH), pl.BlockSpec((1, DSA_WIDTH), lambda b, i: (0, 0))],
        out_specs=qblk(DSA_WIDTH),
        out_shape=jax.ShapeDtypeStruct((batch, seq, DSA_WIDTH), BF16),
        scratch_shapes=[
            pltpu.VMEM((DSA_IDX_HEADS * tq, DSA_WIDTH), BF16),
            pltpu.VMEM((DSA_HEADS * tq, DSA_WIDTH), BF16),
            pltpu.VMEM((seq // kc, tq, kc), F32),
            pltpu.VMEM((DSA_HEADS * tq, DSA_WIDTH), F32),
            pltpu.VMEM((DSA_HEADS, tq, LANES), F32),
            pltpu.VMEM((top_r, tq, LANES), I32),
            pltpu.VMEM((DSA_HEADS * tq, sub), F32),
            pltpu.VMEM((DSA_HEADS * tq, sub), F32),
            pltpu.VMEM((SUBLANES, LANES), F32),
            pltpu.VMEM((top_r, tq, LANES), F32),
        ],
        compiler_params=_cparams("arbitrary", "arbitrary"),
        name="dsa",
    )(r3(p["iq"]), r3(p["ik"]), r3(p["iw"]), r3(p["dq"]), r3(p["dk"]), r3(p["dv"]),
      norm_g.reshape(1, DSA_WIDTH))
    return out.reshape(batch * seq, DSA_WIDTH)


def _s5_tables(a_re, a_im, log_step, b_re, b_im, c_re, c_im):
    hp = lax.Precision.HIGHEST
    s = S5_CHUNK
    dt = jnp.exp(log_step)[:, None]
    mag = jnp.exp(a_re * dt)
    lam_re, lam_im = mag * jnp.cos(a_im * dt), mag * jnp.sin(a_im * dt)
    den = a_re * a_re + a_im * a_im
    nr, ni = lam_re - 1.0, lam_im
    coef_re = (nr * a_re + ni * a_im) / den
    coef_im = (ni * a_re - nr * a_im) / den
    bb_re = coef_re[..., None] * b_re - coef_im[..., None] * b_im
    bb_im = coef_re[..., None] * b_im + coef_im[..., None] * b_re
    j = jnp.arange(s + 1, dtype=F32)[:, None, None]
    pmag = jnp.exp(a_re * dt * j)
    pw_re, pw_im = pmag * jnp.cos(a_im * dt * j), pmag * jnp.sin(a_im * dt * j)
    e_re = c_re[None] * pw_re[:, :, None, :] - c_im[None] * pw_im[:, :, None, :]
    e_im = c_re[None] * pw_im[:, :, None, :] + c_im[None] * pw_re[:, :, None, :]
    kk = (jnp.einsum("jgcp,gpd->jgcd", e_re, bb_re, precision=hp)
          - jnp.einsum("jgcp,gpd->jgcd", e_im, bb_im, precision=hp))
    lag = np.arange(s)[None, :] - np.arange(s)[:, None]
    toe = jnp.where((lag >= 0)[:, :, None, None, None], kk[np.clip(lag, 0, s)], 0.0)
    rp_re, rp_im = pw_re[s - 1 - np.arange(s)], pw_im[s - 1 - np.arange(s)]
    x_re = rp_re[..., None] * bb_re[None] - rp_im[..., None] * bb_im[None]
    x_im = rp_re[..., None] * bb_im[None] + rp_im[..., None] * bb_re[None]
    f_re, f_im = e_re[1:], e_im[1:]
    sc_w = s * S5_GROUP_CH
    g_ids = jnp.arange(S5_TILE_G)[:, None, None]

    def placement(inner, width):
        a = jnp.arange(width // S5_TILE_G)[None, :, None]
        dst = (a // inner) * (S5_TILE_G * inner) + g_ids * inner + a % inner
        return (dst == jnp.arange(width)[None, None, :]).astype(F32)

    p_row = placement(S5_GROUP_CH, S5_ROW_W)
    p_state = placement(S5_STATE, S5_STATE_W)
    per_tile = lambda a: a.astype(BF16).astype(F32).reshape((S5_TILES, S5_TILE_G) + a.shape[1:])

    def place(tab, p_in, p_out):
        t = jnp.einsum("qgab,gbc->qgac", per_tile(tab), p_out)
        return jnp.einsum("gar,qgac->qrc", p_in, t).astype(BF16)

    g_toe = toe.transpose(2, 0, 4, 1, 3).reshape(S5_GROUPS, sc_w, sc_w)
    g_w1 = lambda x: x.transpose(1, 0, 3, 2).reshape(S5_GROUPS, sc_w, S5_STATE)
    g_w2 = lambda f: f.transpose(1, 3, 0, 2).reshape(S5_GROUPS, S5_STATE, sc_w)
    tabs = dict(toe=place(g_toe, p_row, p_row),
                w1_re=place(g_w1(x_re), p_row, p_state), w1_im=place(g_w1(x_im), p_row, p_state),
                w2_re=place(g_w2(f_re), p_state, p_row), w2_im=place(g_w2(-f_im), p_state, p_row))
    tabs["a_re"], tabs["a_im"] = pw_re[s].reshape(-1), pw_im[s].reshape(-1)
    return tabs


def _s5_state_in_kernel(u_ref, wr_ref, wi_ref, xr_ref, xi_ref):
    u = u_ref[...].astype(BF16)
    xr_ref[...] = jnp.dot(u, wr_ref[...], preferred_element_type=F32)
    xi_ref[...] = jnp.dot(u, wi_ref[...], preferred_element_type=F32)


def _s5_scan_kernel(xr_ref, xi_ref, ar_ref, ai_ref, hr_ref, hi_ref, cr_ref, ci_ref, *, steps):
    @pl.when(pl.program_id(0) == 0)
    def _():
        cr_ref[...] = jnp.zeros_like(cr_ref)
        ci_ref[...] = jnp.zeros_like(ci_ref)

    ar, ai = ar_ref[...], ai_ref[...]

    def body(n, carry):
        hr, hi = carry
        hr_ref[n] = hr
        hi_ref[n] = hi
        return ar * hr - ai * hi + xr_ref[n], ar * hi + ai * hr + xi_ref[n]

    hr, hi = lax.fori_loop(0, steps, body, (cr_ref[...], ci_ref[...]))
    cr_ref[...] = hr
    ci_ref[...] = hi


def _s5_out_kernel(u_ref, toe_ref, hr_ref, hi_ref, w2r_ref, w2i_ref, y_ref):
    y = jnp.dot(u_ref[...].astype(BF16), toe_ref[...], preferred_element_type=F32)
    y = y + jnp.dot(hr_ref[...].astype(BF16), w2r_ref[...], preferred_element_type=F32)
    y = y + jnp.dot(hi_ref[...].astype(BF16), w2i_ref[...], preferred_element_type=F32)
    y_ref[...] = y


def _s5_post_kernel(y_ref, u0_ref, u1_ref, u2_ref, d_ref, gw_ref, gb_ref, ng_ref, o_ref, yt_ref):
    n_chunks = yt_ref.shape[1] // S5_CHUNK
    for q in range(S5_TILES):
        for s in range(S5_CHUNK):
            yt_ref[q, pl.ds(s, n_chunks, stride=S5_CHUNK), :] = y_ref[q, :, s * LANES:(s + 1) * LANES]
    y = jnp.concatenate([yt_ref[q] for q in range(S5_TILES)], axis=1)
    u = jnp.concatenate([u0_ref[...], u1_ref[...], u2_ref[...]], axis=1)
    z = y + d_ref[...] * u
    z = 0.5 * z * (1.0 + jnp.tanh(math.sqrt(2.0 / math.pi) * (z + 0.044715 * (z * z * z))))
    gate = _sigmoid(jnp.dot(z.astype(BF16), gw_ref[...], preferred_element_type=F32) + gb_ref[...])
    o_ref[...] = _rms(z * gate, ng_ref[...]).astype(o_ref.dtype)


def _s5(us, xq, tabs, d, glu_w, glu_b, norm_g, batch, seq, tm=512, rb=512):
    s = S5_CHUNK
    nch = seq // s
    rows = nch * batch
    rb = min(rb, rows)
    gp = S5_GROUPS * S5_STATE
    row_in = pl.BlockSpec((None, rb, S5_ROW_W), lambda q, i: (q, i, 0))
    wspec = lambda a: pl.BlockSpec((None,) + a.shape[1:], lambda q, i: (q, 0, 0))
    colblk = pl.BlockSpec((rb, S5_STATE_W), lambda q, i: (i, q))
    xr, xi = pl.pallas_call(
        _s5_state_in_kernel,
        grid=(S5_TILES, rows // rb),
        in_specs=[row_in, wspec(tabs["w1_re"]), wspec(tabs["w1_im"])],
        out_specs=[colblk, colblk],
        out_shape=[jax.ShapeDtypeStruct((rows, gp), F32)] * 2,
        compiler_params=_cparams("arbitrary", "arbitrary"),
        name="s5_state_in",
    )(xq, tabs["w1_re"], tabs["w1_im"])
    parts = SUBLANES // batch
    wl = gp // parts
    slab = lambda a: a.reshape(batch, nch, gp).transpose(1, 0, 2).reshape(nch, SUBLANES, wl)
    unslab = lambda a: a.reshape(nch, batch, gp).transpose(1, 0, 2).reshape(rows, gp)
    coef = lambda a: jnp.tile(a.reshape(parts, wl), (batch, 1))
    steps = min(64, nch)
    sblk = pl.BlockSpec((steps, SUBLANES, wl), lambda i: (i, 0, 0))
    cblk = pl.BlockSpec((SUBLANES, wl), lambda i: (0, 0))
    hr, hi = pl.pallas_call(
        functools.partial(_s5_scan_kernel, steps=steps),
        grid=(nch // steps,),
        in_specs=[sblk, sblk, cblk, cblk],
        out_specs=[sblk, sblk],
        out_shape=[jax.ShapeDtypeStruct((nch, SUBLANES, wl), F32)] * 2,
        scratch_shapes=[pltpu.VMEM((SUBLANES, wl), F32)] * 2,
        compiler_params=_cparams("arbitrary"),
        name="s5_scan",
    )(slab(xr), slab(xi), coef(tabs["a_re"]), coef(tabs["a_im"]))
    y = pl.pallas_call(
        _s5_out_kernel,
        grid=(S5_TILES, rows // rb),
        in_specs=[row_in, wspec(tabs["toe"]), colblk, colblk, wspec(tabs["w2_re"]), wspec(tabs["w2_im"])],
        out_specs=row_in,
        out_shape=jax.ShapeDtypeStruct((S5_TILES, rows, S5_ROW_W), F32),
        compiler_params=_cparams("arbitrary", "arbitrary"),
        name="s5_out",
    )(xq, tabs["toe"], unslab(hr), unslab(hi), tabs["w2_re"], tabs["w2_im"])
    t = batch * seq
    tm = min(tm, t)
    tok = lambda w: pl.BlockSpec((tm, w), lambda i: (i, 0))
    vec = pl.BlockSpec((1, S5_WIDTH), lambda i: (0, 0))
    return pl.pallas_call(
        _s5_post_kernel,
        grid=(t // tm,),
        in_specs=[pl.BlockSpec((S5_TILES, tm // s, S5_ROW_W), lambda i: (0, i, 0)), tok(LANES), tok(LANES),
                  tok(LANES), vec, pl.BlockSpec((S5_WIDTH, S5_WIDTH), lambda i: (0, 0)), vec, vec],
        out_specs=tok(S5_WIDTH),
        out_shape=jax.ShapeDtypeStruct((t, S5_WIDTH), BF16),
        scratch_shapes=[pltpu.VMEM((S5_TILES, tm, LANES), F32)],
        compiler_params=_cparams("arbitrary"),
        name="s5_post",
    )(y, us[0], us[1], us[2], d.reshape(1, -1), glu_w.astype(BF16), glu_b.reshape(1, -1),
      norm_g.reshape(1, -1))


def _mix_out_kernel(og_ref, od_ref, os_ref, wg_ref, wd_ref, ws_ref, x_ref, gt_ref, n2_ref, sc_ref,
                    sh_ref, rw1_ref, rw2_ref, rb_ref, x1_ref, h2_ref, route_ref, cnt_ref, carry_ref, *, tm):
    @pl.when(pl.program_id(0) == 0)
    def _():
        carry_ref[...] = jnp.zeros_like(carry_ref)

    mix = jnp.dot(og_ref[...], wg_ref[...], preferred_element_type=F32)
    mix = mix + jnp.dot(od_ref[...], wd_ref[...], preferred_element_type=F32)
    mix = mix + jnp.dot(os_ref[...], ws_ref[...], preferred_element_type=F32)
    x1 = x_ref[...] + gt_ref[...] * mix
    x1_ref[...] = x1
    h2 = _rms(x1, n2_ref[...]) * (1.0 + sc_ref[...]) + sh_ref[...]
    h2_ref[...] = h2
    ha = h2.astype(BF16)
    hb = (h2 - ha.astype(F32)).astype(BF16)
    logit = (jnp.dot(ha, rw1_ref[...], preferred_element_type=F32)
             + jnp.dot(ha, rw2_ref[...], preferred_element_type=F32)
             + jnp.dot(hb, rw1_ref[...], preferred_element_type=F32)) + rb_ref[...]
    lane = lax.broadcasted_iota(I32, (tm, LANES), 1)
    lanef = lane.astype(F32)
    ninf = jnp.float32(-jnp.inf)
    big = jnp.float32(LANES)
    gl = jnp.where(lane < MOE_GROUPS, logit, ninf)
    gmax = jnp.max(gl, axis=1, keepdims=True)
    grp_p = 1.0 / jnp.sum(jnp.exp(gl - gmax), axis=1, keepdims=True)
    grp_i = jnp.min(jnp.where(gl == gmax, lanef, big), axis=1, keepdims=True)
    lo = MOE_GROUPS + MOE_EXPERTS_PER_GROUP * grp_i
    el = jnp.where((lanef >= lo) & (lanef < lo + MOE_EXPERTS_PER_GROUP), logit, ninf)
    l1 = jnp.max(el, axis=1, keepdims=True)
    i1 = jnp.min(jnp.where(el == l1, lanef, big), axis=1, keepdims=True)
    el2 = jnp.where(lanef == i1, ninf, el)
    l2 = jnp.max(el2, axis=1, keepdims=True)
    i2 = jnp.min(jnp.where(el2 == l2, lanef, big), axis=1, keepdims=True)
    e21 = jnp.exp(l2 - l1)
    w1 = grp_p / (1.0 + e21)
    w2 = grp_p * e21 / (1.0 + e21)
    e1 = i1 - MOE_GROUPS
    e2 = i2 - MOE_GROUPS
    oh1 = lanef == e1
    oh2 = lanef == e2
    hot = jnp.where(oh1 | oh2, 1.0, 0.0)
    ri = lax.broadcasted_iota(I32, (tm, tm), 0)
    ci = lax.broadcasted_iota(I32, (tm, tm), 1)
    before = (ci < ri).astype(BF16)
    prior = carry_ref[...] + jnp.dot(before, hot.astype(BF16), preferred_element_type=F32)
    r1 = jnp.sum(jnp.where(oh1, prior, 0.0), axis=1, keepdims=True)
    r2 = jnp.sum(jnp.where(oh2, prior, 0.0), axis=1, keepdims=True)
    total = carry_ref[...] + jnp.sum(hot, axis=0, keepdims=True)
    carry_ref[...] = total
    cnt_ref[...] = jnp.broadcast_to(total, cnt_ref.shape)
    route = jnp.where(lane == 0, e1, jnp.where(lane == 1, e2, jnp.where(lane == 2, w1, jnp.where(
        lane == 3, w2, jnp.where(lane == 4, r1, jnp.where(lane == 5, r2, 0.0))))))
    route_ref[...] = route


def _mix_out(o_gla, o_dsa, o_s5, w_out, x2, gt1, norm2_g, sc2, sh2, rgw, rgb, rew, reb, seq, tm=512):
    t, d = x2.shape
    per_b = seq // tm
    wg = jnp.pad(w_out[:GLA_HEADS * GLA_DV].reshape(GLA_HEADS, GLA_DV, d),
                 ((0, 0), (0, HEAD_PAD - GLA_DV), (0, 0))).reshape(GLA_W, d).astype(BF16)
    wd = w_out[GLA_HEADS * GLA_DV:GLA_HEADS * GLA_DV + DSA_WIDTH].astype(BF16)
    ws = w_out[GLA_HEADS * GLA_DV + DSA_WIDTH:].astype(BF16)
    rw = _pad_cols(jnp.concatenate([rgw, rew], axis=1), LANES)
    rb = _pad_cols(jnp.concatenate([rgb, reb]).reshape(1, -1), LANES)
    rw1 = rw.astype(BF16)
    rw2 = (rw - rw1.astype(F32)).astype(BF16)
    tok = lambda w: pl.BlockSpec((tm, w), lambda i: (i, 0))
    full = lambda a: pl.BlockSpec(a.shape, lambda i: (0, 0))
    perb = pl.BlockSpec((None, 1, d), lambda i: (i // per_b, 0, 0))
    vec = pl.BlockSpec((1, d), lambda i: (0, 0))
    return pl.pallas_call(
        functools.partial(_mix_out_kernel, tm=tm),
        grid=(t // tm,),
        in_specs=[tok(GLA_W), tok(DSA_WIDTH), tok(S5_WIDTH), full(wg), full(wd), full(ws), tok(d),
                  perb, vec, perb, perb, full(rw1), full(rw2), full(rb)],
        out_specs=[tok(d), tok(d), tok(LANES), pl.BlockSpec((SUBLANES, LANES), lambda i: (0, 0))],
        out_shape=[jax.ShapeDtypeStruct((t, d), F32), jax.ShapeDtypeStruct((t, d), F32),
                   jax.ShapeDtypeStruct((t, LANES), F32), jax.ShapeDtypeStruct((SUBLANES, LANES), F32)],
        scratch_shapes=[pltpu.VMEM((1, LANES), F32)],
        compiler_params=_cparams("arbitrary"),
        name="mix_out",
    )(o_gla, o_dsa, o_s5, wg, wd, ws, x2, gt1[:, None, :], norm2_g.reshape(1, d), sc2[:, None, :],
      sh2[:, None, :], rw1, rw2, rb)


def _row(ref, r):
    return ref.at[pl.ds(r, 1), :]


def _dispatch_kernel(dest_ref, h_ref, zero_ref, out_ref, sem, *, td):
    del zero_ref

    def start(g, c):
        for j in range(SUBLANES):
            r = g * SUBLANES + j
            pltpu.make_async_copy(_row(h_ref, r), _row(out_ref, dest_ref[0, r]), sem).start()
            pltpu.make_async_copy(_row(h_ref, r), _row(out_ref, dest_ref[1, r]), sem).start()
        return c

    lax.fori_loop(0, td // SUBLANES, start, 0)
    for _ in range(2):
        pltpu.make_async_copy(h_ref, out_ref.at[pl.ds(0, td), :], sem).wait()


def _dispatch(h2, dest, n_rows, td=512):
    t, d = h2.shape
    dest3 = dest.reshape(2, t // td, td).transpose(1, 0, 2)
    zeros = jnp.zeros((n_rows, d), F32)
    return pl.pallas_call(
        functools.partial(_dispatch_kernel, td=td),
        grid=(t // td,),
        in_specs=[pl.BlockSpec((None, 2, td), lambda i: (i, 0, 0), memory_space=pltpu.SMEM),
                  pl.BlockSpec((td, d), lambda i: (i, 0)),
                  pl.BlockSpec(memory_space=pl.ANY)],
        out_specs=pl.BlockSpec(memory_space=pl.ANY),
        out_shape=jax.ShapeDtypeStruct((n_rows, d), F32),
        scratch_shapes=[pltpu.SemaphoreType.DMA(())],
        input_output_aliases={2: 0},
        compiler_params=_cparams("arbitrary"),
        name="moe_dispatch",
    )(dest3, h2, zeros)


def _expert_kernel(be_ref, nb_ref, x_ref, wg_ref, wu_ref, wd_ref, o_ref):
    del be_ref
    i = pl.program_id(0)

    @pl.when(i < nb_ref[0])
    def _():
        xb = x_ref[...].astype(BF16)
        g = jnp.dot(xb, wg_ref[...].astype(BF16), preferred_element_type=F32)
        u = jnp.dot(xb, wu_ref[...].astype(BF16), preferred_element_type=F32)
        hid = (g * _sigmoid(g) * u).astype(BF16)
        o_ref[...] = jnp.dot(hid, wd_ref[...].astype(BF16), preferred_element_type=F32)

    @pl.when(i >= nb_ref[0])
    def _():
        o_ref[...] = jnp.zeros_like(o_ref)


def _experts(xs, blk_e, n_used, wg, wu, wd, layer):
    n_rows = xs.shape[0]
    nb = n_rows // MOE_BM
    d, hdim = wg.shape[-2:]
    rows = pl.BlockSpec((MOE_BM, d), lambda i, be, nu: (i, 0))
    wspec = lambda a, b: pl.BlockSpec((None, None, a, b), lambda i, be, nu: (layer, be[i], 0, 0))
    return pl.pallas_call(
        _expert_kernel,
        grid_spec=pltpu.PrefetchScalarGridSpec(
            num_scalar_prefetch=2,
            grid=(nb,),
            in_specs=[rows, wspec(d, hdim), wspec(d, hdim), wspec(hdim, d)],
            out_specs=rows,
        ),
        out_shape=jax.ShapeDtypeStruct((n_rows, d), F32),
        compiler_params=_cparams("arbitrary"),
        name="moe_experts",
    )(blk_e, n_used, xs, wg, wu, wd)


def _combine_kernel(dest_ref, next_ref, x_ref, route_ref, gt_ref, fg_ref, eo_ref, o_ref, rows_ref, sem,
                    *, tc, final):
    i = pl.program_id(0)
    slot = i % 2

    def issue(d_ref, s):
        def start(g, c):
            for j in range(SUBLANES):
                r = g * SUBLANES + j
                pltpu.make_async_copy(_row(eo_ref, d_ref[0, r]), _row(rows_ref.at[s, 0], r), sem.at[s]).start()
                pltpu.make_async_copy(_row(eo_ref, d_ref[1, r]), _row(rows_ref.at[s, 1], r), sem.at[s]).start()
            return c

        lax.fori_loop(0, tc // SUBLANES, start, 0)

    @pl.when(i == 0)
    def _():
        issue(dest_ref, 0)

    @pl.when(i + 1 < pl.num_programs(0))
    def _():
        issue(next_ref, 1 - slot)

    for which in range(2):
        pltpu.make_async_copy(eo_ref.at[pl.ds(0, tc), :], rows_ref.at[slot, which], sem.at[slot]).wait()
    route = route_ref[...]
    lane = lax.broadcasted_iota(I32, route.shape, 1)
    w1 = jnp.sum(jnp.where(lane == 2, route, 0.0), axis=1, keepdims=True)
    w2 = jnp.sum(jnp.where(lane == 3, route, 0.0), axis=1, keepdims=True)
    y = rows_ref[slot, 0] * w1 + rows_ref[slot, 1] * w2
    x = x_ref[...] + gt_ref[...] * y
    if final:
        x = _rms(x, fg_ref[...])
    o_ref[...] = x


def _combine(x1, route, dest, eo, gt2, final_g, seq, final, tc=512):
    t, d = x1.shape
    per_b = seq // tc
    steps = t // tc
    dest3 = dest.reshape(2, steps, tc).transpose(1, 0, 2)
    return pl.pallas_call(
        functools.partial(_combine_kernel, tc=tc, final=final),
        grid=(steps,),
        in_specs=[pl.BlockSpec((None, 2, tc), lambda i: (i, 0, 0), memory_space=pltpu.SMEM),
                  pl.BlockSpec((None, 2, tc), lambda i: (jnp.minimum(i + 1, steps - 1), 0, 0),
                               memory_space=pltpu.SMEM),
                  pl.BlockSpec((tc, d), lambda i: (i, 0)),
                  pl.BlockSpec((tc, LANES), lambda i: (i, 0)),
                  pl.BlockSpec((None, 1, d), lambda i: (i // per_b, 0, 0)),
                  pl.BlockSpec((1, d), lambda i: (0, 0)),
                  pl.BlockSpec(memory_space=pl.ANY)],
        out_specs=pl.BlockSpec((tc, d), lambda i: (i, 0)),
        out_shape=jax.ShapeDtypeStruct((t, d), F32),
        scratch_shapes=[pltpu.VMEM((2, 2, tc, d), F32), pltpu.SemaphoreType.DMA((2,))],
        compiler_params=_cparams("arbitrary"),
        name="moe_combine",
    )(dest3, dest3, x1, route, gt2[:, None, :], final_g.reshape(1, d), eo)


def _moe_plan(route, counts, t):
    e = route[:, 0:2].astype(I32).T
    rank = route[:, 4:6].astype(I32).T
    cnt = counts[0, :MOE_EXPERTS].astype(I32)
    padded = (cnt + MOE_BM - 1) // MOE_BM * MOE_BM
    pend = jnp.cumsum(padded)
    pstart = pend - padded
    ids = jnp.arange(MOE_EXPERTS, dtype=I32)
    dest = rank + jnp.sum(jnp.where(e[..., None] == ids, pstart, 0), axis=-1)
    nb = (2 * t) // MOE_BM + MOE_EXPERTS
    n_used = (pend[-1] // MOE_BM).astype(I32)
    first_row = jnp.minimum(jnp.arange(nb, dtype=I32), n_used - 1) * MOE_BM
    blk_e = jnp.sum((pend[None, :] <= first_row[:, None]).astype(I32), axis=1)
    blk_e = jnp.minimum(blk_e, MOE_EXPERTS - 1)
    return dest, blk_e, n_used.reshape(1), nb * MOE_BM


def kernel(x, c, ada_w, ada_b, norm1_g, w_in, gla_gate_w, gla_gate_b, gla_norm_g, dsa_norm_g, s5_a_re, s5_a_im, s5_log_step, s5_b_re, s5_b_im, s5_c_re, s5_c_im, s5_d, s5_glu_w, s5_glu_b, s5_norm_g, w_out, norm2_g, router_grp_w, router_grp_b, router_exp_w, router_exp_b, exp_w_gate, exp_w_up, exp_w_down, final_norm_g):
    batch, seq, d = x.shape
    depth = ada_w.shape[0]
    t = batch * seq
    mod = _ada_mod(c, ada_w, ada_b)
    x2 = x.reshape(t, d)
    for l in range(depth):
        sh1, sc1, gt1, sh2, sc2, gt2 = (mod[l, j] for j in range(6))
        p = _in_proj(x2, norm1_g[l], sc1, sh1, _layout_w_in(w_in[l]), seq)
        o_gla = _gla(p, gla_gate_w[l], gla_gate_b[l], gla_norm_g[l], batch, seq)
        o_dsa = _dsa(p, dsa_norm_g[l], batch, seq)
        tabs = _s5_tables(s5_a_re[l], s5_a_im[l], s5_log_step[l], s5_b_re[l], s5_b_im[l],
                          s5_c_re[l], s5_c_im[l])
        o_s5 = _s5((p["su0"], p["su1"], p["su2"]), p["sx"], tabs, s5_d[l], s5_glu_w[l], s5_glu_b[l], s5_norm_g[l], batch, seq)
        x1, h2, route, counts = _mix_out(o_gla, o_dsa, o_s5, w_out[l], x2, gt1, norm2_g[l], sc2, sh2,
                                         router_grp_w[l], router_grp_b[l], router_exp_w[l],
                                         router_exp_b[l], seq)
        dest, blk_e, n_used, n_rows = _moe_plan(route, counts, t)
        xs = _dispatch(h2, dest, n_rows)
        eo = _experts(xs, blk_e, n_used, exp_w_gate, exp_w_up, exp_w_down, l)
        x2 = _combine(x1, route, dest, eo, gt2, final_norm_g, seq, final=(l == depth - 1))
    return x2.reshape(batch, seq, d)
```

```python
import functools
import math

import jax
import jax.numpy as jnp
import numpy as np
from jax import lax
from jax.experimental import pallas as pl
from jax.experimental.pallas import tpu as pltpu

F32 = jnp.float32
BF16 = jnp.bfloat16
I32 = jnp.int32

D_MODEL = 1024
GLA_HEADS = 4
GLA_DV = 96
GLA_DK = 48
GLA_RANK = 16
GLA_TAU = 16.0
GLA_CHUNK = 64
DSA_HEADS = 4
DSA_HEAD_DIM = 64
DSA_WIDTH = DSA_HEADS * DSA_HEAD_DIM
DSA_IDX_HEADS = 8
DSA_IDX_DIM = 32
DSA_TOPK = 256
S5_GROUPS = 24
S5_GROUP_CH = 16
S5_STATE = 64
S5_WIDTH = S5_GROUPS * S5_GROUP_CH
MOE_GROUPS = 4
MOE_EXPERTS_PER_GROUP = 8
MOE_EXPERTS = MOE_GROUPS * MOE_EXPERTS_PER_GROUP
MOE_HIDDEN = 512
RMS_EPS = 1e-6
IN_SIZES = (192, 192, 384, 16, 384, 256, 256, 256, 256, 32, 8, 384)

LANES = 128
SUBLANES = 8
VMEM_LIMIT = 56 * 1024 * 1024

HEAD_PAD = LANES
GLA_W = GLA_HEADS * HEAD_PAD
S5_CHUNK = 16
S5_TILE_G = LANES // S5_GROUP_CH
S5_TILES = S5_GROUPS // S5_TILE_G
S5_ROW_W = S5_CHUNK * LANES
S5_STATE_W = S5_TILE_G * S5_STATE
MOE_BM = 512
INT_MIN = -2 ** 31


def _cparams(*sem):
    return pltpu.CompilerParams(dimension_semantics=tuple(sem), vmem_limit_bytes=VMEM_LIMIT)


def _sigmoid(x):
    return 0.5 * (jnp.tanh(0.5 * x) + 1.0)


def _rms(x, g, n=None):
    n = x.shape[-1] if n is None else n
    ms = jnp.sum(x * x, axis=-1, keepdims=True) * (1.0 / n)
    return x * lax.rsqrt(ms + RMS_EPS) * g


def _ada_kernel(c_ref, w_ref, b_ref, o_ref):
    c = c_ref[...]
    cond = (c * _sigmoid(c)).astype(BF16)
    o_ref[...] = jnp.dot(cond, w_ref[...].astype(BF16), preferred_element_type=F32) + b_ref[...]


def _ada_mod(c, ada_w, ada_b):
    depth, d, d6 = ada_w.shape
    nblk = d6 // d
    bp = -(-c.shape[0] // SUBLANES) * SUBLANES
    cp = jnp.pad(c, ((0, bp - c.shape[0]), (0, 0)))
    out = pl.pallas_call(
        _ada_kernel,
        grid=(depth, nblk),
        in_specs=[
            pl.BlockSpec((bp, d), lambda l, j: (0, 0)),
            pl.BlockSpec((None, d, d), lambda l, j: (l, 0, j)),
            pl.BlockSpec((None, None, 1, d), lambda l, j: (l, j, 0, 0)),
        ],
        out_specs=pl.BlockSpec((None, None, bp, d), lambda l, j: (l, j, 0, 0)),
        out_shape=jax.ShapeDtypeStruct((depth, nblk, bp, d), F32),
        compiler_params=_cparams("arbitrary", "arbitrary"),
        name="ada_mod",
    )(cp, ada_w, ada_b.reshape(depth, nblk, 1, d))
    return out[:, :, : c.shape[0], :]


PROJ_OUTS = (
    ("gq", GLA_W, BF16), ("gk", GLA_W, BF16), ("gv", GLA_W, BF16), ("gr", GLA_W, BF16),
    ("glr", LANES, BF16),
    ("dq", DSA_WIDTH, BF16), ("dk", DSA_WIDTH, BF16), ("dv", DSA_WIDTH, BF16),
    ("iq", DSA_IDX_HEADS * DSA_IDX_DIM, BF16), ("ik", DSA_IDX_HEADS * DSA_IDX_DIM, BF16),
    ("iw", LANES, F32), ("su0", LANES, F32), ("su1", LANES, F32), ("su2", LANES, F32),
)
PROJ_COLS = sum(w for _, w, _ in PROJ_OUTS)


def _pad_heads(w, heads, dim):
    d = w.shape[0]
    w = w.reshape(d, heads, dim)
    return jnp.pad(w, ((0, 0), (0, 0), (0, HEAD_PAD - dim))).reshape(d, heads * HEAD_PAD)


def _pad_cols(w, width):
    return jnp.pad(w, ((0, 0), (0, width - w.shape[1])))


def _layout_w_in(w_in):
    offs = np.cumsum((0,) + IN_SIZES)
    p = [w_in[:, offs[i]:offs[i + 1]] for i in range(len(IN_SIZES))]
    cols = [
        _pad_heads(p[0], GLA_HEADS, GLA_DK), _pad_heads(p[1], GLA_HEADS, GLA_DK),
        _pad_heads(p[2], GLA_HEADS, GLA_DV), _pad_heads(p[4], GLA_HEADS, GLA_DV),
        _pad_cols(p[3], LANES),
        p[5], p[6], p[7], p[8], jnp.tile(p[9], (1, DSA_IDX_HEADS)),
        _pad_cols(p[10], LANES), p[11],
    ]
    return jnp.concatenate(cols, axis=1).astype(BF16)


def _proj_kernel(x_ref, g_ref, sc_ref, sh_ref, w_ref, *o_refs):
    x = x_ref[...]
    h = _rms(x, g_ref[...]) * (1.0 + sc_ref[...]) + sh_ref[...]
    hb = h.astype(BF16)
    c0 = 0
    for o_ref, (_, width, _) in zip(o_refs, PROJ_OUTS):
        o_ref[...] = jnp.dot(hb, w_ref[:, c0:c0 + width], preferred_element_type=F32).astype(o_ref.dtype)
        c0 += width
    sx_ref = o_refs[len(PROJ_OUTS)]
    n_chunks = x.shape[0] // S5_CHUNK
    for q in range(S5_TILES):
        su_ref = o_refs[len(PROJ_OUTS) - S5_TILES + q]
        for s in range(S5_CHUNK):
            sx_ref[q, :, s * LANES:(s + 1) * LANES] = su_ref[pl.ds(s, n_chunks, stride=S5_CHUNK), :]


def _in_proj(x2, norm_g, sc, sh, w_lay, seq, tm=512):
    t, d = x2.shape
    per_b = seq // tm
    outs = pl.pallas_call(
        _proj_kernel,
        grid=(t // tm,),
        in_specs=[
            pl.BlockSpec((tm, d), lambda i: (i, 0)),
            pl.BlockSpec((1, d), lambda i: (0, 0)),
            pl.BlockSpec((None, 1, d), lambda i: (i // per_b, 0, 0)),
            pl.BlockSpec((None, 1, d), lambda i: (i // per_b, 0, 0)),
            pl.BlockSpec((d, PROJ_COLS), lambda i: (0, 0)),
        ],
        out_specs=[pl.BlockSpec((tm, w), lambda i: (i, 0)) for _, w, _ in PROJ_OUTS]
        + [pl.BlockSpec((S5_TILES, tm // S5_CHUNK, S5_ROW_W), lambda i: (0, i, 0))],
        out_shape=[jax.ShapeDtypeStruct((t, w), dt) for _, w, dt in PROJ_OUTS]
        + [jax.ShapeDtypeStruct((S5_TILES, t // S5_CHUNK, S5_ROW_W), F32)],
        compiler_params=_cparams("arbitrary"),
        name="in_proj",
    )(x2, norm_g.reshape(1, d), sc[:, None, :], sh[:, None, :], w_lay)
    p = {name: o for (name, _, _), o in zip(PROJ_OUTS, outs)}
    p["sx"] = outs[-1]
    return p


def _gla_kernel(q_ref, k_ref, v_ref, r_ref, lr_ref, gw_ref, gb_ref, ng_ref, o_ref, s_ref, *, rows):
    @pl.when(pl.program_id(1) == 0)
    def _():
        s_ref[...] = jnp.zeros_like(s_ref)

    c = GLA_CHUNK
    gl = jnp.dot(lr_ref[...], gw_ref[...], preferred_element_type=F32) + gb_ref[...]
    g = (jnp.minimum(gl, 0.0) - jnp.log1p(jnp.exp(-jnp.abs(gl)))) * (1.0 / GLA_TAU)
    g1 = g.astype(BF16)
    rem = g - g1.astype(F32)
    g2 = rem.astype(BF16)
    g3 = (rem - g2.astype(F32)).astype(BF16)
    ri = lax.broadcasted_iota(I32, (c, c), 0)
    ci = lax.broadcasted_iota(I32, (c, c), 1)
    causal = ri >= ci
    tri = causal.astype(BF16)
    ng = ng_ref[...]
    for ch in range(rows // c):
        rs = slice(ch * c, (ch + 1) * c)
        bcum = (jnp.dot(tri, g1[rs], preferred_element_type=F32)
                + jnp.dot(tri, g2[rs], preferred_element_type=F32)
                + jnp.dot(tri, g3[rs], preferred_element_type=F32))
        blast = bcum[c - 1:c, :]
        e_pos = jnp.exp(bcum)
        kf = k_ref[rs, :].astype(F32)
        qd = (q_ref[rs, :].astype(F32) * (GLA_DK ** -0.5) * e_pos).astype(BF16)
        kd = (kf * jnp.exp(-bcum)).astype(BF16)
        ku = (kf * jnp.exp(blast - bcum)).astype(BF16)
        dec = jnp.exp(blast)
        for h in range(GLA_HEADS):
            hs = slice(h * HEAD_PAD, (h + 1) * HEAD_PAD)
            vh = v_ref[rs, hs]
            att = lax.dot_general(qd[:, hs], kd[:, hs], (((1,), (1,)), ((), ())),
                                  preferred_element_type=F32)
            att = jnp.where(causal, att, 0.0).astype(BF16)
            st = s_ref[h]
            o = jnp.dot(att, vh, preferred_element_type=F32)
            o = o + lax.dot_general(qd[:, hs], st.astype(BF16), (((1,), (1,)), ((), ())),
                                    preferred_element_type=F32)
            upd = lax.dot_general(vh, ku[:, hs], (((0,), (0,)), ((), ())),
                                  preferred_element_type=F32)
            s_ref[h] = st * dec[:, hs] + upd
            y = _rms(o, ng[:, hs], GLA_DV)
            rr = r_ref[rs, hs].astype(F32)
            o_ref[rs, hs] = (y * (rr * _sigmoid(rr))).astype(o_ref.dtype)


def _gla(p, gate_w, gate_b, norm_g, batch, seq, rows=512):
    gw = _pad_cols(jnp.pad(_pad_heads(gate_w, GLA_HEADS, GLA_DK), ((0, LANES - GLA_RANK), (0, 0))), GLA_W)
    gb = _pad_heads(gate_b.reshape(1, -1), GLA_HEADS, GLA_DK)
    ng = _pad_heads(jnp.tile(norm_g.reshape(1, GLA_DV), (1, GLA_HEADS)), GLA_HEADS, GLA_DV)
    rows = min(rows, seq)
    blk = lambda w: pl.BlockSpec((None, rows, w), lambda b, j: (b, j, 0))
    full = lambda a: pl.BlockSpec(a.shape, lambda b, j: (0, 0))
    r3 = lambda a: a.reshape(batch, seq, a.shape[-1])
    out = pl.pallas_call(
        functools.partial(_gla_kernel, rows=rows),
        grid=(batch, seq // rows),
        in_specs=[blk(GLA_W), blk(GLA_W), blk(GLA_W), blk(GLA_W), blk(LANES),
                  full(gw), full(gb), full(ng)],
        out_specs=blk(GLA_W),
        out_shape=jax.ShapeDtypeStruct((batch, seq, GLA_W), BF16),
        scratch_shapes=[pltpu.VMEM((GLA_HEADS, HEAD_PAD, HEAD_PAD), F32)],
        compiler_params=_cparams("arbitrary", "arbitrary"),
        name="gla",
    )(r3(p["gq"]), r3(p["gk"]), r3(p["gv"]), r3(p["gr"]), r3(p["glr"]), gw.astype(BF16), gb, ng)
    return out.reshape(batch * seq, GLA_W)


def _dsa_kernel(iq_ref, ik_ref, iw_ref, q_ref, k_ref, v_ref, ng_ref, o_ref,
                qs_ref, q4_ref, keys_ref, acc_ref, run_ref, top_ref, sa_ref, sb_ref, kmax_ref, topf_ref,
                *, tq, kc, sub, n_sel, top_r):
    assert kc == 2 * sub
    qi = pl.program_id(1)
    nkc = ((qi + 1) * tq + kc - 1) // kc
    lane_w = lax.broadcasted_iota(I32, (tq, DSA_WIDTH), 1)
    iq = iq_ref[...]
    for h in range(DSA_IDX_HEADS):
        m = (lane_w >= h * DSA_IDX_DIM) & (lane_w < (h + 1) * DSA_IDX_DIM)
        qs_ref[h * tq:(h + 1) * tq, :] = jnp.where(m, iq, jnp.zeros_like(iq))
    iw = iw_ref[...] * (DSA_IDX_HEADS ** -0.5 * DSA_IDX_DIM ** -0.5)
    lane = lax.broadcasted_iota(I32, (tq, LANES), 1)
    w_cols = [iw[:, h:h + 1] for h in range(DSA_IDX_HEADS)]
    qpos = qi * tq + lax.broadcasted_iota(I32, (tq, sub), 0)
    kcol = lax.broadcasted_iota(I32, (tq, sub), 1)
    nt = (((1,), (1,)), ((), ()))

    ninf = jnp.float32(-jnp.inf)

    def score_chunk(c, causal=True):
        for u in range(kc // sub):
            k0 = pl.multiple_of(c * kc + u * sub, sub)
            lg = lax.dot_general(qs_ref[...], ik_ref[pl.ds(k0, sub), :], nt,
                                 preferred_element_type=F32)
            sc = jnp.zeros((tq, sub), F32)
            for h in range(DSA_IDX_HEADS):
                sc = sc + w_cols[h] * jnp.maximum(lg[h * tq:(h + 1) * tq, :], 0.0)
            sc = sc + 0.0
            if causal:
                sc = jnp.where(kcol + k0 <= qpos, sc, ninf)
            keys_ref[c, :, u * sub:(u + 1) * sub] = sc

    topf_ref[...] = jnp.full(topf_ref.shape, ninf, F32)

    def insert_chunk(c):
        for rt in range(tq // SUBLANES):
            rs = slice(rt * SUBLANES, (rt + 1) * SUBLANES)
            ts = [topf_ref[r, rs, :] for r in range(top_r)]
            for j in range(kc // LANES):
                x = keys_ref[c, rs, j * LANES:(j + 1) * LANES]
                for r in range(top_r):
                    hi = jnp.maximum(ts[r], x)
                    x = jnp.minimum(ts[r], x)
                    ts[r] = hi
            for r in range(top_r):
                topf_ref[r, rs, :] = ts[r]

    score_chunk(0)

    def fused_body(causal):
        def body(c, carry):
            insert_chunk(c - 1)
            score_chunk(c, causal)
            return carry
        return body

    n_open = jnp.maximum((qi * tq + 1) // kc, 1)
    lax.fori_loop(1, n_open, fused_body(False), 0)
    lax.fori_loop(n_open, nkc, fused_body(True), 0)
    insert_chunk(nkc - 1)

    def to_key(x):
        bits = lax.bitcast_convert_type(x, I32)
        key = jnp.where(bits < 0, bits ^ jnp.int32(0x7FFFFFFF), bits)
        return jnp.where(x == ninf, jnp.int32(INT_MIN), key)

    def count_ge(cand):
        candb = jnp.broadcast_to(cand, (tq, LANES))

        def body(c, acc):
            blk = keys_ref[c]
            for j in range(kc // LANES):
                acc = acc + jnp.where(to_key(blk[:, j * LANES:(j + 1) * LANES]) >= candb, 1.0, 0.0)
            return acc

        acc = lax.fori_loop(0, nkc, body, jnp.zeros((tq, LANES), F32))
        return jnp.sum(acc, axis=1, keepdims=True)

    need = jnp.float32(n_sel)

    def select(count):
        t0 = jnp.full((tq, 1), INT_MIN, I32)
        zero = jnp.zeros((tq, 1), I32)
        t0 = jnp.where(count(zero) >= need, zero, t0)

        def bit_body(i, t):
            cand = t | (jnp.int32(1) << (30 - i))
            return jnp.where(count(cand) >= need, cand, t)

        t = lax.fori_loop(0, 31, bit_body, t0)
        t = jnp.maximum(t, jnp.int32(INT_MIN + 1))
        return t, count(t + 1), count(t)

    for r in range(top_r):
        top_ref[r] = to_key(topf_ref[r])

    def count_cand(cand):
        candb = jnp.broadcast_to(cand, (tq, LANES))
        acc = jnp.zeros((tq, LANES), F32)
        for r in range(top_r):
            acc = acc + jnp.where(top_ref[r] >= candb, 1.0, 0.0)
        return jnp.sum(acc, axis=1, keepdims=True)

    thr_c, n_gt_c, n_ge_c = select(count_cand)
    hidden = jnp.max(jnp.where(top_ref[top_r - 1] >= thr_c, 1.0, 0.0)) > 0.0
    thr, n_gt, n_ge = lax.cond(hidden, lambda: select(count_ge), lambda: (thr_c, n_gt_c, n_ge_c))
    n_tie = need - n_gt
    any_tie = jnp.max(n_ge) > need
    thr_bits = jnp.where(thr < 0, thr ^ jnp.int32(0x7FFFFFFF), thr)
    thr_f = jnp.where(thr == jnp.int32(INT_MIN + 1), jnp.float32(jnp.finfo(jnp.float32).min),
                      lax.bitcast_convert_type(thr_bits, F32))

    q = q_ref[...] * jnp.asarray(DSA_HEAD_DIM ** -0.5, BF16)
    for h in range(DSA_HEADS):
        m = (lane_w >= h * DSA_HEAD_DIM) & (lane_w < (h + 1) * DSA_HEAD_DIM)
        q4_ref[h * tq:(h + 1) * tq, :] = jnp.where(m, q, jnp.zeros_like(q))
    neg = jnp.float32(-1e30)

    def lane_fold(op, acc, s):
        for j in range(s.shape[1] // LANES):
            acc = op(acc, s[:, j * LANES:(j + 1) * LANES])
        return acc

    @pl.when(qi == 0)
    def _():
        kmax_ref[...] = jnp.zeros_like(kmax_ref)

    hl = lax.broadcasted_iota(I32, (DSA_WIDTH, LANES), 0) // DSA_HEAD_DIM
    head_of = (hl == lax.broadcasted_iota(I32, (DSA_WIDTH, LANES), 1)).astype(BF16)
    kb = k_ref[pl.ds(pl.multiple_of(qi * tq, tq), tq), :].astype(F32)
    kn2 = jnp.dot((kb * kb).astype(BF16), head_of, preferred_element_type=F32)
    kmax = jnp.maximum(kmax_ref[0:1, :], jnp.max(kn2, axis=0, keepdims=True))
    kmax_ref[...] = jnp.broadcast_to(kmax, kmax_ref.shape)
    qf = q.astype(F32)
    qn2 = jnp.dot((qf * qf).astype(BF16), head_of, preferred_element_type=F32)
    bound = jnp.sqrt(qn2 * kmax) * 1.02
    ms_bound = [jnp.sum(jnp.where(lane == h, bound, 0.0), axis=1, keepdims=True) for h in range(DSA_HEADS)]

    def tie_bias_pass():
        ui = lax.broadcasted_iota(I32, (kc, kc), 0)
        uj = lax.broadcasted_iota(I32, (kc, kc), 1)
        before = (ui < uj).astype(BF16)

        def body(c, seen):
            sc = keys_ref[c]
            eq = sc == thr_f
            eqf = jnp.where(eq, 1.0, 0.0)
            rank = seen + jnp.dot(eqf.astype(BF16), before, preferred_element_type=F32)
            sel = (sc > thr_f) | (eq & (rank < n_tie))
            keys_ref[c] = jnp.where(sel, 0.0, neg)
            return seen + jnp.sum(eqf, axis=1, keepdims=True)

        lax.fori_loop(0, nkc, body, jnp.zeros((tq, 1), F32))

    def bias_of(c, stored):
        sc = keys_ref[c]
        return sc if stored else jnp.where(sc >= thr_f, 0.0, neg)

    def qk(k0):
        return lax.dot_general(q4_ref[...], k_ref[pl.ds(k0, sub), :], nt, preferred_element_type=F32)

    def max_pass(stored):
        run_ref[...] = jnp.full(run_ref.shape, neg, F32)

        def body(c, carry):
            bias = bias_of(c, stored)
            for u in range(kc // sub):
                s = qk(pl.multiple_of(c * kc + u * sub, sub))
                for h in range(DSA_HEADS):
                    run_ref[h] = lane_fold(jnp.maximum, run_ref[h],
                                           s[h * tq:(h + 1) * tq] + bias[:, u * sub:(u + 1) * sub])
            return carry

        lax.fori_loop(0, nkc, body, 0)
        return [jnp.max(run_ref[h], axis=1, keepdims=True) for h in range(DSA_HEADS)]

    def att_pass(stored, ms):
        run_ref[...] = jnp.zeros_like(run_ref)
        acc_ref[...] = jnp.zeros_like(acc_ref)

        def half(s_ref, bias):
            ps = []
            for h in range(DSA_HEADS):
                p = jnp.exp(s_ref[h * tq:(h + 1) * tq, :] + bias - ms[h])
                run_ref[h] = lane_fold(jnp.add, run_ref[h], p)
                ps.append(p.astype(BF16))
            return jnp.concatenate(ps, axis=0)

        sa_ref[...] = qk(0)

        def body(c, carry):
            k0 = pl.multiple_of(c * kc, kc)
            sb_ref[...] = qk(k0 + sub)
            bias = bias_of(c, stored)
            pa = half(sa_ref, bias[:, :sub])
            sa_ref[...] = qk(pl.multiple_of(jnp.minimum(c + 1, nkc - 1) * kc, kc))
            pb = half(sb_ref, bias[:, sub:])
            acc_ref[...] += jnp.dot(jnp.concatenate([pa, pb], axis=1), v_ref[pl.ds(k0, kc), :],
                                    preferred_element_type=F32)
            return carry

        lax.fori_loop(0, nkc, body, 0)

    def finish(stored):
        if stored:
            tie_bias_pass()
        att_pass(stored, ms_bound)
        lmin = jnp.min(jnp.sum(run_ref[0], axis=1, keepdims=True))
        for h in range(1, DSA_HEADS):
            lmin = jnp.minimum(lmin, jnp.min(jnp.sum(run_ref[h], axis=1, keepdims=True)))

        def redo():
            att_pass(stored, max_pass(stored))
            return 0

        lax.cond(lmin >= 1e-26, lambda: 0, redo)
        return 0

    lax.cond(any_tie, lambda: finish(True), lambda: finish(False))
    out = jnp.zeros((tq, DSA_WIDTH), F32)
    for h in range(DSA_HEADS):
        m = (lane_w >= h * DSA_HEAD_DIM) & (lane_w < (h + 1) * DSA_HEAD_DIM)
        out = out + jnp.where(m, acc_ref[h * tq:(h + 1) * tq, :] / jnp.sum(run_ref[h], axis=1, keepdims=True), 0.0)
    o_ref[...] = _rms(out, ng_ref[...]).astype(o_ref.dtype)


def _dsa(p, norm_g, batch, seq, tq=512, kc=512, top_r=12):
    n_sel = min(DSA_TOPK, seq // 4)
    kc = min(kc, seq)
    sub = kc // 2
    r3 = lambda a: a.reshape(batch, seq, a.shape[-1])
    qblk = lambda w: pl.BlockSpec((None, tq, w), lambda b, i: (b, i, 0))
    kfull = lambda w: pl.BlockSpec((None, seq, w), lambda b, i: (b, 0, 0), pipeline_mode=pl.Buffered(1))
    out = pl.pallas_call(
        functools.partial(_dsa_kernel, tq=tq, kc=kc, sub=sub, n_sel=n_sel, top_r=top_r),
        grid=(batch, seq // tq),
        in_specs=[qblk(DSA_WIDTH), kfull(DSA_WIDTH), qblk(LANES), qblk(DSA_WIDTH),
                  kfull(DSA_WIDTH), kfull(DSA_WIDTH), pl.BlockSpec((1, DSA_WIDTH), lambda b, i: (0, 0))],
        out_specs=qblk(DSA_WIDTH),
        out_shape=jax.ShapeDtypeStruct((batch, seq, DSA_WIDTH), BF16),
        scratch_shapes=[
            pltpu.VMEM((DSA_IDX_HEADS * tq, DSA_WIDTH), BF16),
            pltpu.VMEM((DSA_HEADS * tq, DSA_WIDTH), BF16),
            pltpu.VMEM((seq // kc, tq, kc), F32),
            pltpu.VMEM((DSA_HEADS * tq, DSA_WIDTH), F32),
            pltpu.VMEM((DSA_HEADS, tq, LANES), F32),
            pltpu.VMEM((top_r, tq, LANES), I32),
            pltpu.VMEM((DSA_HEADS * tq, sub), F32),
            pltpu.VMEM((DSA_HEADS * tq, sub), F32),
            pltpu.VMEM((SUBLANES, LANES), F32),
            pltpu.VMEM((top_r, tq, LANES), F32),
        ],
        compiler_params=_cparams("arbitrary", "arbitrary"),
        name="dsa",
    )(r3(p["iq"]), r3(p["ik"]), r3(p["iw"]), r3(p["dq"]), r3(p["dk"]), r3(p["dv"]),
      norm_g.reshape(1, DSA_WIDTH))
    return out.reshape(batch * seq, DSA_WIDTH)


def _s5_tables(a_re, a_im, log_step, b_re, b_im, c_re, c_im):
    hp = lax.Precision.HIGHEST
    s = S5_CHUNK
    dt = jnp.exp(log_step)[:, None]
    mag = jnp.exp(a_re * dt)
    lam_re, lam_im = mag * jnp.cos(a_im * dt), mag * jnp.sin(a_im * dt)
    den = a_re * a_re + a_im * a_im
    nr, ni = lam_re - 1.0, lam_im
    coef_re = (nr * a_re + ni * a_im) / den
    coef_im = (ni * a_re - nr * a_im) / den
    bb_re = coef_re[..., None] * b_re - coef_im[..., None] * b_im
    bb_im = coef_re[..., None] * b_im + coef_im[..., None] * b_re
    j = jnp.arange(s + 1, dtype=F32)[:, None, None]
    pmag = jnp.exp(a_re * dt * j)
    pw_re, pw_im = pmag * jnp.cos(a_im * dt * j), pmag * jnp.sin(a_im * dt * j)
    e_re = c_re[None] * pw_re[:, :, None, :] - c_im[None] * pw_im[:, :, None, :]
    e_im = c_re[None] * pw_im[:, :, None, :] + c_im[None] * pw_re[:, :, None, :]
    kk = (jnp.einsum("jgcp,gpd->jgcd", e_re, bb_re, precision=hp)
          - jnp.einsum("jgcp,gpd->jgcd", e_im, bb_im, precision=hp))
    lag = np.arange(s)[None, :] - np.arange(s)[:, None]
    toe = jnp.where((lag >= 0)[:, :, None, None, None], kk[np.clip(lag, 0, s)], 0.0)
    rp_re, rp_im = pw_re[s - 1 - np.arange(s)], pw_im[s - 1 - np.arange(s)]
    x_re = rp_re[..., None] * bb_re[None] - rp_im[..., None] * bb_im[None]
    x_im = rp_re[..., None] * bb_im[None] + rp_im[..., None] * bb_re[None]
    f_re, f_im = e_re[1:], e_im[1:]
    sc_w = s * S5_GROUP_CH
    g_ids = jnp.arange(S5_TILE_G)[:, None, None]

    def placement(inner, width):
        a = jnp.arange(width // S5_TILE_G)[None, :, None]
        dst = (a // inner) * (S5_TILE_G * inner) + g_ids * inner + a % inner
        return (dst == jnp.arange(width)[None, None, :]).astype(F32)

    p_row = placement(S5_GROUP_CH, S5_ROW_W)
    p_state = placement(S5_STATE, S5_STATE_W)
    per_tile = lambda a: a.astype(BF16).astype(F32).reshape((S5_TILES, S5_TILE_G) + a.shape[1:])

    def place(tab, p_in, p_out):
        t = jnp.einsum("qgab,gbc->qgac", per_tile(tab), p_out)
        return jnp.einsum("gar,qgac->qrc", p_in, t).astype(BF16)

    g_toe = toe.transpose(2, 0, 4, 1, 3).reshape(S5_GROUPS, sc_w, sc_w)
    g_w1 = lambda x: x.transpose(1, 0, 3, 2).reshape(S5_GROUPS, sc_w, S5_STATE)
    g_w2 = lambda f: f.transpose(1, 3, 0, 2).reshape(S5_GROUPS, S5_STATE, sc_w)
    tabs = dict(toe=place(g_toe, p_row, p_row),
                w1_re=place(g_w1(x_re), p_row, p_state), w1_im=place(g_w1(x_im), p_row, p_state),
                w2_re=place(g_w2(f_re), p_state, p_row), w2_im=place(g_w2(-f_im), p_state, p_row))
    tabs["a_re"], tabs["a_im"] = pw_re[s].reshape(-1), pw_im[s].reshape(-1)
    return tabs


def _s5_state_in_kernel(u_ref, wr_ref, wi_ref, xr_ref, xi_ref):
    u = u_ref[...].astype(BF16)
    xr_ref[...] = jnp.dot(u, wr_ref[...], preferred_element_type=F32)
    xi_ref[...] = jnp.dot(u, wi_ref[...], preferred_element_type=F32)


def _s5_scan_kernel(xr_ref, xi_ref, ar_ref, ai_ref, hr_ref, hi_ref, cr_ref, ci_ref, *, steps):
    @pl.when(pl.program_id(0) == 0)
    def _():
        cr_ref[...] = jnp.zeros_like(cr_ref)
        ci_ref[...] = jnp.zeros_like(ci_ref)

    ar, ai = ar_ref[...], ai_ref[...]

    def body(n, carry):
        hr, hi = carry
        hr_ref[n] = hr
        hi_ref[n] = hi
        return ar * hr - ai * hi + xr_ref[n], ar * hi + ai * hr + xi_ref[n]

    hr, hi = lax.fori_loop(0, steps, body, (cr_ref[...], ci_ref[...]))
    cr_ref[...] = hr
    ci_ref[...] = hi


def _s5_out_kernel(u_ref, toe_ref, hr_ref, hi_ref, w2r_ref, w2i_ref, y_ref):
    y = jnp.dot(u_ref[...].astype(BF16), toe_ref[...], preferred_element_type=F32)
    y = y + jnp.dot(hr_ref[...].astype(BF16), w2r_ref[...], preferred_element_type=F32)
    y = y + jnp.dot(hi_ref[...].astype(BF16), w2i_ref[...], preferred_element_type=F32)
    y_ref[...] = y


def _s5_post_kernel(y_ref, u0_ref, u1_ref, u2_ref, d_ref, gw_ref, gb_ref, ng_ref, o_ref, yt_ref):
    n_chunks = yt_ref.shape[1] // S5_CHUNK
    for q in range(S5_TILES):
        for s in range(S5_CHUNK):
            yt_ref[q, pl.ds(s, n_chunks, stride=S5_CHUNK), :] = y_ref[q, :, s * LANES:(s + 1) * LANES]
    y = jnp.concatenate([yt_ref[q] for q in range(S5_TILES)], axis=1)
    u = jnp.concatenate([u0_ref[...], u1_ref[...], u2_ref[...]], axis=1)
    z = y + d_ref[...] * u
    z = 0.5 * z * (1.0 + jnp.tanh(math.sqrt(2.0 / math.pi) * (z + 0.044715 * (z * z * z))))
    gate = _sigmoid(jnp.dot(z.astype(BF16), gw_ref[...], preferred_element_type=F32) + gb_ref[...])
    o_ref[...] = _rms(z * gate, ng_ref[...]).astype(o_ref.dtype)


def _s5(us, xq, tabs, d, glu_w, glu_b, norm_g, batch, seq, tm=512, rb=512):
    s = S5_CHUNK
    nch = seq // s
    rows = nch * batch
    rb = min(rb, rows)
    gp = S5_GROUPS * S5_STATE
    row_in = pl.BlockSpec((None, rb, S5_ROW_W), lambda q, i: (q, i, 0))
    wspec = lambda a: pl.BlockSpec((None,) + a.shape[1:], lambda q, i: (q, 0, 0))
    colblk = pl.BlockSpec((rb, S5_STATE_W), lambda q, i: (i, q))
    xr, xi = pl.pallas_call(
        _s5_state_in_kernel,
        grid=(S5_TILES, rows // rb),
        in_specs=[row_in, wspec(tabs["w1_re"]), wspec(tabs["w1_im"])],
        out_specs=[colblk, colblk],
        out_shape=[jax.ShapeDtypeStruct((rows, gp), F32)] * 2,
        compiler_params=_cparams("arbitrary", "arbitrary"),
        name="s5_state_in",
    )(xq, tabs["w1_re"], tabs["w1_im"])
    parts = SUBLANES // batch
    wl = gp // parts
    slab = lambda a: a.reshape(batch, nch, gp).transpose(1, 0, 2).reshape(nch, SUBLANES, wl)
    unslab = lambda a: a.reshape(nch, batch, gp).transpose(1, 0, 2).reshape(rows, gp)
    coef = lambda a: jnp.tile(a.reshape(parts, wl), (batch, 1))
    steps = min(64, nch)
    sblk = pl.BlockSpec((steps, SUBLANES, wl), lambda i: (i, 0, 0))
    cblk = pl.BlockSpec((SUBLANES, wl), lambda i: (0, 0))
    hr, hi = pl.pallas_call(
        functools.partial(_s5_scan_kernel, steps=steps),
        grid=(nch // steps,),
        in_specs=[sblk, sblk, cblk, cblk],
        out_specs=[sblk, sblk],
        out_shape=[jax.ShapeDtypeStruct((nch, SUBLANES, wl), F32)] * 2,
        scratch_shapes=[pltpu.VMEM((SUBLANES, wl), F32)] * 2,
        compiler_params=_cparams("arbitrary"),
        name="s5_scan",
    )(slab(xr), slab(xi), coef(tabs["a_re"]), coef(tabs["a_im"]))
    y = pl.pallas_call(
        _s5_out_kernel,
        grid=(S5_TILES, rows // rb),
        in_specs=[row_in, wspec(tabs["toe"]), colblk, colblk, wspec(tabs["w2_re"]), wspec(tabs["w2_im"])],
        out_specs=row_in,
        out_shape=jax.ShapeDtypeStruct((S5_TILES, rows, S5_ROW_W), F32),
        compiler_params=_cparams("arbitrary", "arbitrary"),
        name="s5_out",
    )(xq, tabs["toe"], unslab(hr), unslab(hi), tabs["w2_re"], tabs["w2_im"])
    t = batch * seq
    tm = min(tm, t)
    tok = lambda w: pl.BlockSpec((tm, w), lambda i: (i, 0))
    vec = pl.BlockSpec((1, S5_WIDTH), lambda i: (0, 0))
    return pl.pallas_call(
        _s5_post_kernel,
        grid=(t // tm,),
        in_specs=[pl.BlockSpec((S5_TILES, tm // s, S5_ROW_W), lambda i: (0, i, 0)), tok(LANES), tok(LANES),
                  tok(LANES), vec, pl.BlockSpec((S5_WIDTH, S5_WIDTH), lambda i: (0, 0)), vec, vec],
        out_specs=tok(S5_WIDTH),
        out_shape=jax.ShapeDtypeStruct((t, S5_WIDTH), BF16),
        scratch_shapes=[pltpu.VMEM((S5_TILES, tm, LANES), F32)],
        compiler_params=_cparams("arbitrary"),
        name="s5_post",
    )(y, us[0], us[1], us[2], d.reshape(1, -1), glu_w.astype(BF16), glu_b.reshape(1, -1),
      norm_g.reshape(1, -1))


def _mix_out_kernel(og_ref, od_ref, os_ref, wg_ref, wd_ref, ws_ref, x_ref, gt_ref, n2_ref, sc_ref,
                    sh_ref, rw1_ref, rw2_ref, rb_ref, x1_ref, h2_ref, route_ref, cnt_ref, carry_ref, *, tm):
    @pl.when(pl.program_id(0) == 0)
    def _():
        carry_ref[...] = jnp.zeros_like(carry_ref)

    mix = jnp.dot(og_ref[...], wg_ref[...], preferred_element_type=F32)
    mix = mix + jnp.dot(od_ref[...], wd_ref[...], preferred_element_type=F32)
    mix = mix + jnp.dot(os_ref[...], ws_ref[...], preferred_element_type=F32)
    x1 = x_ref[...] + gt_ref[...] * mix
    x1_ref[...] = x1
    h2 = _rms(x1, n2_ref[...]) * (1.0 + sc_ref[...]) + sh_ref[...]
    h2_ref[...] = h2
    ha = h2.astype(BF16)
    hb = (h2 - ha.astype(F32)).astype(BF16)
    logit = (jnp.dot(ha, rw1_ref[...], preferred_element_type=F32)
             + jnp.dot(ha, rw2_ref[...], preferred_element_type=F32)
             + jnp.dot(hb, rw1_ref[...], preferred_element_type=F32)) + rb_ref[...]
    lane = lax.broadcasted_iota(I32, (tm, LANES), 1)
    lanef = lane.astype(F32)
    ninf = jnp.float32(-jnp.inf)
    big = jnp.float32(LANES)
    gl = jnp.where(lane < MOE_GROUPS, logit, ninf)
    gmax = jnp.max(gl, axis=1, keepdims=True)
    grp_p = 1.0 / jnp.sum(jnp.exp(gl - gmax), axis=1, keepdims=True)
    grp_i = jnp.min(jnp.where(gl == gmax, lanef, big), axis=1, keepdims=True)
    lo = MOE_GROUPS + MOE_EXPERTS_PER_GROUP * grp_i
    el = jnp.where((lanef >= lo) & (lanef < lo + MOE_EXPERTS_PER_GROUP), logit, ninf)
    l1 = jnp.max(el, axis=1, keepdims=True)
    i1 = jnp.min(jnp.where(el == l1, lanef, big), axis=1, keepdims=True)
    el2 = jnp.where(lanef == i1, ninf, el)
    l2 = jnp.max(el2, axis=1, keepdims=True)
    i2 = jnp.min(jnp.where(el2 == l2, lanef, big), axis=1, keepdims=True)
    e21 = jnp.exp(l2 - l1)
    w1 = grp_p / (1.0 + e21)
    w2 = grp_p * e21 / (1.0 + e21)
    e1 = i1 - MOE_GROUPS
    e2 = i2 - MOE_GROUPS
    oh1 = lanef == e1
    oh2 = lanef == e2
    hot = jnp.where(oh1 | oh2, 1.0, 0.0)
    ri = lax.broadcasted_iota(I32, (tm, tm), 0)
    ci = lax.broadcasted_iota(I32, (tm, tm), 1)
    before = (ci < ri).astype(BF16)
    prior = carry_ref[...] + jnp.dot(before, hot.astype(BF16), preferred_element_type=F32)
    r1 = jnp.sum(jnp.where(oh1, prior, 0.0), axis=1, keepdims=True)
    r2 = jnp.sum(jnp.where(oh2, prior, 0.0), axis=1, keepdims=True)
    total = carry_ref[...] + jnp.sum(hot, axis=0, keepdims=True)
    carry_ref[...] = total
    cnt_ref[...] = jnp.broadcast_to(total, cnt_ref.shape)
    route = jnp.where(lane == 0, e1, jnp.where(lane == 1, e2, jnp.where(lane == 2, w1, jnp.where(
        lane == 3, w2, jnp.where(lane == 4, r1, jnp.where(lane == 5, r2, 0.0))))))
    route_ref[...] = route


def _mix_out(o_gla, o_dsa, o_s5, w_out, x2, gt1, norm2_g, sc2, sh2, rgw, rgb, rew, reb, seq, tm=512):
    t, d = x2.shape
    per_b = seq // tm
    wg = jnp.pad(w_out[:GLA_HEADS * GLA_DV].reshape(GLA_HEADS, GLA_DV, d),
                 ((0, 0), (0, HEAD_PAD - GLA_DV), (0, 0))).reshape(GLA_W, d).astype(BF16)
    wd = w_out[GLA_HEADS * GLA_DV:GLA_HEADS * GLA_DV + DSA_WIDTH].astype(BF16)
    ws = w_out[GLA_HEADS * GLA_DV + DSA_WIDTH:].astype(BF16)
    rw = _pad_cols(jnp.concatenate([rgw, rew], axis=1), LANES)
    rb = _pad_cols(jnp.concatenate([rgb, reb]).reshape(1, -1), LANES)
    rw1 = rw.astype(BF16)
    rw2 = (rw - rw1.astype(F32)).astype(BF16)
    tok = lambda w: pl.BlockSpec((tm, w), lambda i: (i, 0))
    full = lambda a: pl.BlockSpec(a.shape, lambda i: (0, 0))
    perb = pl.BlockSpec((None, 1, d), lambda i: (i // per_b, 0, 0))
    vec = pl.BlockSpec((1, d), lambda i: (0, 0))
    return pl.pallas_call(
        functools.partial(_mix_out_kernel, tm=tm),
        grid=(t // tm,),
        in_specs=[tok(GLA_W), tok(DSA_WIDTH), tok(S5_WIDTH), full(wg), full(wd), full(ws), tok(d),
                  perb, vec, perb, perb, full(rw1), full(rw2), full(rb)],
        out_specs=[tok(d), tok(d), tok(LANES), pl.BlockSpec((SUBLANES, LANES), lambda i: (0, 0))],
        out_shape=[jax.ShapeDtypeStruct((t, d), F32), jax.ShapeDtypeStruct((t, d), F32),
                   jax.ShapeDtypeStruct((t, LANES), F32), jax.ShapeDtypeStruct((SUBLANES, LANES), F32)],
        scratch_shapes=[pltpu.VMEM((1, LANES), F32)],
        compiler_params=_cparams("arbitrary"),
        name="mix_out",
    )(o_gla, o_dsa, o_s5, wg, wd, ws, x2, gt1[:, None, :], norm2_g.reshape(1, d), sc2[:, None, :],
      sh2[:, None, :], rw1, rw2, rb)


def _row(ref, r):
    return ref.at[pl.ds(r, 1), :]


def _dispatch_kernel(dest_ref, h_ref, zero_ref, out_ref, sem, *, td):
    del zero_ref

    def start(g, c):
        for j in range(SUBLANES):
            r = g * SUBLANES + j
            pltpu.make_async_copy(_row(h_ref, r), _row(out_ref, dest_ref[0, r]), sem).start()
            pltpu.make_async_copy(_row(h_ref, r), _row(out_ref, dest_ref[1, r]), sem).start()
        return c

    lax.fori_loop(0, td // SUBLANES, start, 0)
    for _ in range(2):
        pltpu.make_async_copy(h_ref, out_ref.at[pl.ds(0, td), :], sem).wait()


def _dispatch(h2, dest, n_rows, td=512):
    t, d = h2.shape
    dest3 = dest.reshape(2, t // td, td).transpose(1, 0, 2)
    zeros = jnp.zeros((n_rows, d), F32)
    return pl.pallas_call(
        functools.partial(_dispatch_kernel, td=td),
        grid=(t // td,),
        in_specs=[pl.BlockSpec((None, 2, td), lambda i: (i, 0, 0), memory_space=pltpu.SMEM),
                  pl.BlockSpec((td, d), lambda i: (i, 0)),
                  pl.BlockSpec(memory_space=pl.ANY)],
        out_specs=pl.BlockSpec(memory_space=pl.ANY),
        out_shape=jax.ShapeDtypeStruct((n_rows, d), F32),
        scratch_shapes=[pltpu.SemaphoreType.DMA(())],
        input_output_aliases={2: 0},
        compiler_params=_cparams("arbitrary"),
        name="moe_dispatch",
    )(dest3, h2, zeros)


def _expert_kernel(be_ref, nb_ref, x_ref, wg_ref, wu_ref, wd_ref, o_ref):
    del be_ref
    i = pl.program_id(0)

    @pl.when(i < nb_ref[0])
    def _():
        xb = x_ref[...].astype(BF16)
        g = jnp.dot(xb, wg_ref[...].astype(BF16), preferred_element_type=F32)
        u = jnp.dot(xb, wu_ref[...].astype(BF16), preferred_element_type=F32)
        hid = (g * _sigmoid(g) * u).astype(BF16)
        o_ref[...] = jnp.dot(hid, wd_ref[...].astype(BF16), preferred_element_type=F32)

    @pl.when(i >= nb_ref[0])
    def _():
        o_ref[...] = jnp.zeros_like(o_ref)


def _experts(xs, blk_e, n_used, wg, wu, wd, layer):
    n_rows = xs.shape[0]
    nb = n_rows // MOE_BM
    d, hdim = wg.shape[-2:]
    rows = pl.BlockSpec((MOE_BM, d), lambda i, be, nu: (i, 0))
    wspec = lambda a, b: pl.BlockSpec((None, None, a, b), lambda i, be, nu: (layer, be[i], 0, 0))
    return pl.pallas_call(
        _expert_kernel,
        grid_spec=pltpu.PrefetchScalarGridSpec(
            num_scalar_prefetch=2,
            grid=(nb,),
            in_specs=[rows, wspec(d, hdim), wspec(d, hdim), wspec(hdim, d)],
            out_specs=rows,
        ),
        out_shape=jax.ShapeDtypeStruct((n_rows, d), F32),
        compiler_params=_cparams("arbitrary"),
        name="moe_experts",
    )(blk_e, n_used, xs, wg, wu, wd)


def _combine_kernel(dest_ref, next_ref, x_ref, route_ref, gt_ref, fg_ref, eo_ref, o_ref, rows_ref, sem,
                    *, tc, final):
    i = pl.program_id(0)
    slot = i % 2

    def issue(d_ref, s):
        def start(g, c):
            for j in range(SUBLANES):
                r = g * SUBLANES + j
                pltpu.make_async_copy(_row(eo_ref, d_ref[0, r]), _row(rows_ref.at[s, 0], r), sem.at[s]).start()
                pltpu.make_async_copy(_row(eo_ref, d_ref[1, r]), _row(rows_ref.at[s, 1], r), sem.at[s]).start()
            return c

        lax.fori_loop(0, tc // SUBLANES, start, 0)

    @pl.when(i == 0)
    def _():
        issue(dest_ref, 0)

    @pl.when(i + 1 < pl.num_programs(0))
    def _():
        issue(next_ref, 1 - slot)

    for which in range(2):
        pltpu.make_async_copy(eo_ref.at[pl.ds(0, tc), :], rows_ref.at[slot, which], sem.at[slot]).wait()
    route = route_ref[...]
    lane = lax.broadcasted_iota(I32, route.shape, 1)
    w1 = jnp.sum(jnp.where(lane == 2, route, 0.0), axis=1, keepdims=True)
    w2 = jnp.sum(jnp.where(lane == 3, route, 0.0), axis=1, keepdims=True)
    y = rows_ref[slot, 0] * w1 + rows_ref[slot, 1] * w2
    x = x_ref[...] + gt_ref[...] * y
    if final:
        x = _rms(x, fg_ref[...])
    o_ref[...] = x


def _combine(x1, route, dest, eo, gt2, final_g, seq, final, tc=512):
    t, d = x1.shape
    per_b = seq // tc
    steps = t // tc
    dest3 = dest.reshape(2, steps, tc).transpose(1, 0, 2)
    return pl.pallas_call(
        functools.partial(_combine_kernel, tc=tc, final=final),
        grid=(steps,),
        in_specs=[pl.BlockSpec((None, 2, tc), lambda i: (i, 0, 0), memory_space=pltpu.SMEM),
                  pl.BlockSpec((None, 2, tc), lambda i: (jnp.minimum(i + 1, steps - 1), 0, 0),
                               memory_space=pltpu.SMEM),
                  pl.BlockSpec((tc, d), lambda i: (i, 0)),
                  pl.BlockSpec((tc, LANES), lambda i: (i, 0)),
                  pl.BlockSpec((None, 1, d), lambda i: (i // per_b, 0, 0)),
                  pl.BlockSpec((1, d), lambda i: (0, 0)),
                  pl.BlockSpec(memory_space=pl.ANY)],
        out_specs=pl.BlockSpec((tc, d), lambda i: (i, 0)),
        out_shape=jax.ShapeDtypeStruct((t, d), F32),
        scratch_shapes=[pltpu.VMEM((2, 2, tc, d), F32), pltpu.SemaphoreType.DMA((2,))],
        compiler_params=_cparams("arbitrary"),
        name="moe_combine",
    )(dest3, dest3, x1, route, gt2[:, None, :], final_g.reshape(1, d), eo)


def _moe_plan(route, counts, t):
    e = route[:, 0:2].astype(I32).T
    rank = route[:, 4:6].astype(I32).T
    cnt = counts[0, :MOE_EXPERTS].astype(I32)
    padded = (cnt + MOE_BM - 1) // MOE_BM * MOE_BM
    pend = jnp.cumsum(padded)
    pstart = pend - padded
    ids = jnp.arange(MOE_EXPERTS, dtype=I32)
    dest = rank + jnp.sum(jnp.where(e[..., None] == ids, pstart, 0), axis=-1)
    nb = (2 * t) // MOE_BM + MOE_EXPERTS
    n_used = (pend[-1] // MOE_BM).astype(I32)
    first_row = jnp.minimum(jnp.arange(nb, dtype=I32), n_used - 1) * MOE_BM
    blk_e = jnp.sum((pend[None, :] <= first_row[:, None]).astype(I32), axis=1)
    blk_e = jnp.minimum(blk_e, MOE_EXPERTS - 1)
    return dest, blk_e, n_used.reshape(1), nb * MOE_BM


def kernel(x, c, ada_w, ada_b, norm1_g, w_in, gla_gate_w, gla_gate_b, gla_norm_g, dsa_norm_g, s5_a_re, s5_a_im, s5_log_step, s5_b_re, s5_b_im, s5_c_re, s5_c_im, s5_d, s5_glu_w, s5_glu_b, s5_norm_g, w_out, norm2_g, router_grp_w, router_grp_b, router_exp_w, router_exp_b, exp_w_gate, exp_w_up, exp_w_down, final_norm_g):
    batch, seq, d = x.shape
    depth = ada_w.shape[0]
    t = batch * seq
    mod = _ada_mod(c, ada_w, ada_b)
    x2 = x.reshape(t, d)
    for l in range(depth):
        sh1, sc1, gt1, sh2, sc2, gt2 = (mod[l, j] for j in range(6))
        p = _in_proj(x2, norm1_g[l], sc1, sh1, _layout_w_in(w_in[l]), seq)
        o_gla = _gla(p, gla_gate_w[l], gla_gate_b[l], gla_norm_g[l], batch, seq)
        o_dsa = _dsa(p, dsa_norm_g[l], batch, seq)
        tabs = _s5_tables(s5_a_re[l], s5_a_im[l], s5_log_step[l], s5_b_re[l], s5_b_im[l],
                          s5_c_re[l], s5_c_im[l])
        o_s5 = _s5((p["su0"], p["su1"], p["su2"]), p["sx"], tabs, s5_d[l], s5_glu_w[l], s5_glu_b[l], s5_norm_g[l], batch, seq)
        x1, h2, route, counts = _mix_out(o_gla, o_dsa, o_s5, w_out[l], x2, gt1, norm2_g[l], sc2, sh2,
                                         router_grp_w[l], router_grp_b[l], router_exp_w[l],
                                         router_exp_b[l], seq)
        dest, blk_e, n_used, n_rows = _moe_plan(route, counts, t)
        xs = _dispatch(h2, dest, n_rows)
        eo = _experts(xs, blk_e, n_used, exp_w_gate, exp_w_up, exp_w_down, l)
        x2 = _combine(x1, route, dest, eo, gt2, final_norm_g, seq, final=(l == depth - 1))
    return x2.reshape(batch, seq, d)
```

```python
import functools
import math

import jax
import jax.numpy as jnp
import numpy as np
from jax import lax
from jax.experimental import pallas as pl
from jax.experimental.pallas import tpu as pltpu

F32 = jnp.float32
BF16 = jnp.bfloat16
I32 = jnp.int32

D_MODEL = 1024
GLA_HEADS = 4
GLA_DV = 96
GLA_DK = 48
GLA_RANK = 16
GLA_TAU = 16.0
GLA_CHUNK = 64
DSA_HEADS = 4
DSA_HEAD_DIM = 64
DSA_WIDTH = DSA_HEADS * DSA_HEAD_DIM
DSA_IDX_HEADS = 8
DSA_IDX_DIM = 32
DSA_TOPK = 256
S5_GROUPS = 24
S5_GROUP_CH = 16
S5_STATE = 64
S5_WIDTH = S5_GROUPS * S5_GROUP_CH
MOE_GROUPS = 4
MOE_EXPERTS_PER_GROUP = 8
MOE_EXPERTS = MOE_GROUPS * MOE_EXPERTS_PER_GROUP
MOE_HIDDEN = 512
RMS_EPS = 1e-6
IN_SIZES = (192, 192, 384, 16, 384, 256, 256, 256, 256, 32, 8, 384)

LANES = 128
SUBLANES = 8
VMEM_LIMIT = 56 * 1024 * 1024

HEAD_PAD = LANES
GLA_W = GLA_HEADS * HEAD_PAD
S5_CHUNK = 16
S5_TILE_G = LANES // S5_GROUP_CH
S5_TILES = S5_GROUPS // S5_TILE_G
S5_ROW_W = S5_CHUNK * LANES
S5_STATE_W = S5_TILE_G * S5_STATE
MOE_BM = 512
INT_MIN = -2 ** 31


def _cparams(*sem):
    return pltpu.CompilerParams(dimension_semantics=tuple(sem), vmem_limit_bytes=VMEM_LIMIT)


def _sigmoid(x):
    return 0.5 * (jnp.tanh(0.5 * x) + 1.0)


def _rms(x, g, n=None):
    n = x.shape[-1] if n is None else n
    ms = jnp.sum(x * x, axis=-1, keepdims=True) * (1.0 / n)
    return x * lax.rsqrt(ms + RMS_EPS) * g


def _ada_kernel(c_ref, w_ref, b_ref, o_ref):
    c = c_ref[...]
    cond = (c * _sigmoid(c)).astype(BF16)
    o_ref[...] = jnp.dot(cond, w_ref[...].astype(BF16), preferred_element_type=F32) + b_ref[...]


def _ada_mod(c, ada_w, ada_b):
    depth, d, d6 = ada_w.shape
    nblk = d6 // d
    bp = -(-c.shape[0] // SUBLANES) * SUBLANES
    cp = jnp.pad(c, ((0, bp - c.shape[0]), (0, 0)))
    out = pl.pallas_call(
        _ada_kernel,
        grid=(depth, nblk),
        in_specs=[
            pl.BlockSpec((bp, d), lambda l, j: (0, 0)),
            pl.BlockSpec((None, d, d), lambda l, j: (l, 0, j)),
            pl.BlockSpec((None, None, 1, d), lambda l, j: (l, j, 0, 0)),
        ],
        out_specs=pl.BlockSpec((None, None, bp, d), lambda l, j: (l, j, 0, 0)),
        out_shape=jax.ShapeDtypeStruct((depth, nblk, bp, d), F32),
        compiler_params=_cparams("arbitrary", "arbitrary"),
        name="ada_mod",
    )(cp, ada_w, ada_b.reshape(depth, nblk, 1, d))
    return out[:, :, : c.shape[0], :]


PROJ_OUTS = (
    ("gq", GLA_W, BF16), ("gk", GLA_W, BF16), ("gv", GLA_W, BF16), ("gr", GLA_W, BF16),
    ("glr", LANES, BF16),
    ("dq", DSA_WIDTH, BF16), ("dk", DSA_WIDTH, BF16), ("dv", DSA_WIDTH, BF16),
    ("iq", DSA_IDX_HEADS * DSA_IDX_DIM, BF16), ("ik", DSA_IDX_HEADS * DSA_IDX_DIM, BF16),
    ("iw", LANES, F32), ("su0", LANES, F32), ("su1", LANES, F32), ("su2", LANES, F32),
)
PROJ_COLS = sum(w for _, w, _ in PROJ_OUTS)


def _pad_heads(w, heads, dim):
    d = w.shape[0]
    w = w.reshape(d, heads, dim)
    return jnp.pad(w, ((0, 0), (0, 0), (0, HEAD_PAD - dim))).reshape(d, heads * HEAD_PAD)


def _pad_cols(w, width):
    return jnp.pad(w, ((0, 0), (0, width - w.shape[1])))


def _layout_w_in(w_in):
    offs = np.cumsum((0,) + IN_SIZES)
    p = [w_in[:, offs[i]:offs[i + 1]] for i in range(len(IN_SIZES))]
    cols = [
        _pad_heads(p[0], GLA_HEADS, GLA_DK), _pad_heads(p[1], GLA_HEADS, GLA_DK),
        _pad_heads(p[2], GLA_HEADS, GLA_DV), _pad_heads(p[4], GLA_HEADS, GLA_DV),
        _pad_cols(p[3], LANES),
        p[5], p[6], p[7], p[8], jnp.tile(p[9], (1, DSA_IDX_HEADS)),
        _pad_cols(p[10], LANES), p[11],
    ]
    return jnp.concatenate(cols, axis=1).astype(BF16)


def _proj_kernel(x_ref, g_ref, sc_ref, sh_ref, w_ref, *o_refs):
    x = x_ref[...]
    h = _rms(x, g_ref[...]) * (1.0 + sc_ref[...]) + sh_ref[...]
    hb = h.astype(BF16)
    c0 = 0
    for o_ref, (_, width, _) in zip(o_refs, PROJ_OUTS):
        o_ref[...] = jnp.dot(hb, w_ref[:, c0:c0 + width], preferred_element_type=F32).astype(o_ref.dtype)
        c0 += width
    sx_ref = o_refs[len(PROJ_OUTS)]
    n_chunks = x.shape[0] // S5_CHUNK
    for q in range(S5_TILES):
        su_ref = o_refs[len(PROJ_OUTS) - S5_TILES + q]
        for s in range(S5_CHUNK):
            sx_ref[q, :, s * LANES:(s + 1) * LANES] = su_ref[pl.ds(s, n_chunks, stride=S5_CHUNK), :]


def _in_proj(x2, norm_g, sc, sh, w_lay, seq, tm=512):
    t, d = x2.shape
    per_b = seq // tm
    outs = pl.pallas_call(
        _proj_kernel,
        grid=(t // tm,),
        in_specs=[
            pl.BlockSpec((tm, d), lambda i: (i, 0)),
            pl.BlockSpec((1, d), lambda i: (0, 0)),
            pl.BlockSpec((None, 1, d), lambda i: (i // per_b, 0, 0)),
            pl.BlockSpec((None, 1, d), lambda i: (i // per_b, 0, 0)),
            pl.BlockSpec((d, PROJ_COLS), lambda i: (0, 0)),
        ],
        out_specs=[pl.BlockSpec((tm, w), lambda i: (i, 0)) for _, w, _ in PROJ_OUTS]
        + [pl.BlockSpec((S5_TILES, tm // S5_CHUNK, S5_ROW_W), lambda i: (0, i, 0))],
        out_shape=[jax.ShapeDtypeStruct((t, w), dt) for _, w, dt in PROJ_OUTS]
        + [jax.ShapeDtypeStruct((S5_TILES, t // S5_CHUNK, S5_ROW_W), F32)],
        compiler_params=_cparams("arbitrary"),
        name="in_proj",
    )(x2, norm_g.reshape(1, d), sc[:, None, :], sh[:, None, :], w_lay)
    p = {name: o for (name, _, _), o in zip(PROJ_OUTS, outs)}
    p["sx"] = outs[-1]
    return p


def _gla_kernel(q_ref, k_ref, v_ref, r_ref, lr_ref, gw_ref, gb_ref, ng_ref, o_ref, s_ref, *, rows):
    @pl.when(pl.program_id(1) == 0)
    def _():
        s_ref[...] = jnp.zeros_like(s_ref)

    c = GLA_CHUNK
    gl = jnp.dot(lr_ref[...], gw_ref[...], preferred_element_type=F32) + gb_ref[...]
    g = (jnp.minimum(gl, 0.0) - jnp.log1p(jnp.exp(-jnp.abs(gl)))) * (1.0 / GLA_TAU)
    g1 = g.astype(BF16)
    rem = g - g1.astype(F32)
    g2 = rem.astype(BF16)
    g3 = (rem - g2.astype(F32)).astype(BF16)
    ri = lax.broadcasted_iota(I32, (c, c), 0)
    ci = lax.broadcasted_iota(I32, (c, c), 1)
    causal = ri >= ci
    tri = causal.astype(BF16)
    ng = ng_ref[...]
    for ch in range(rows // c):
        rs = slice(ch * c, (ch + 1) * c)
        bcum = (jnp.dot(tri, g1[rs], preferred_element_type=F32)
                + jnp.dot(tri, g2[rs], preferred_element_type=F32)
                + jnp.dot(tri, g3[rs], preferred_element_type=F32))
        blast = bcum[c - 1:c, :]
        e_pos = jnp.exp(bcum)
        kf = k_ref[rs, :].astype(F32)
        qd = (q_ref[rs, :].astype(F32) * (GLA_DK ** -0.5) * e_pos).astype(BF16)
        kd = (kf * jnp.exp(-bcum)).astype(BF16)
        ku = (kf * jnp.exp(blast - bcum)).astype(BF16)
        dec = jnp.exp(blast)
        for h in range(GLA_HEADS):
            hs = slice(h * HEAD_PAD, (h + 1) * HEAD_PAD)
            vh = v_ref[rs, hs]
            att = lax.dot_general(qd[:, hs], kd[:, hs], (((1,), (1,)), ((), ())),
                                  preferred_element_type=F32)
            att = jnp.where(causal, att, 0.0).astype(BF16)
            st = s_ref[h]
            o = jnp.dot(att, vh, preferred_element_type=F32)
            o = o + lax.dot_general(qd[:, hs], st.astype(BF16), (((1,), (1,)), ((), ())),
                                    preferred_element_type=F32)
            upd = lax.dot_general(vh, ku[:, hs], (((0,), (0,)), ((), ())),
                                  preferred_element_type=F32)
            s_ref[h] = st * dec[:, hs] + upd
            y = _rms(o, ng[:, hs], GLA_DV)
            rr = r_ref[rs, hs].astype(F32)
            o_ref[rs, hs] = (y * (rr * _sigmoid(rr))).astype(o_ref.dtype)


def _gla(p, gate_w, gate_b, norm_g, batch, seq, rows=512):
    gw = _pad_cols(jnp.pad(_pad_heads(gate_w, GLA_HEADS, GLA_DK), ((0, LANES - GLA_RANK), (0, 0))), GLA_W)
    gb = _pad_heads(gate_b.reshape(1, -1), GLA_HEADS, GLA_DK)
    ng = _pad_heads(jnp.tile(norm_g.reshape(1, GLA_DV), (1, GLA_HEADS)), GLA_HEADS, GLA_DV)
    rows = min(rows, seq)
    blk = lambda w: pl.BlockSpec((None, rows, w), lambda b, j: (b, j, 0))
    full = lambda a: pl.BlockSpec(a.shape, lambda b, j: (0, 0))
    r3 = lambda a: a.reshape(batch, seq, a.shape[-1])
    out = pl.pallas_call(
        functools.partial(_gla_kernel, rows=rows),
        grid=(batch, seq // rows),
        in_specs=[blk(GLA_W), blk(GLA_W), blk(GLA_W), blk(GLA_W), blk(LANES),
                  full(gw), full(gb), full(ng)],
        out_specs=blk(GLA_W),
        out_shape=jax.ShapeDtypeStruct((batch, seq, GLA_W), BF16),
        scratch_shapes=[pltpu.VMEM((GLA_HEADS, HEAD_PAD, HEAD_PAD), F32)],
        compiler_params=_cparams("arbitrary", "arbitrary"),
        name="gla",
    )(r3(p["gq"]), r3(p["gk"]), r3(p["gv"]), r3(p["gr"]), r3(p["glr"]), gw.astype(BF16), gb, ng)
    return out.reshape(batch * seq, GLA_W)


def _dsa_kernel(iq_ref, ik_ref, iw_ref, q_ref, k_ref, v_ref, ng_ref, o_ref,
                qs_ref, q4_ref, keys_ref, acc_ref, run_ref, top_ref, sa_ref, sb_ref, kmax_ref, topf_ref,
                *, tq, kc, sub, n_sel, top_r):
    assert kc == 2 * sub
    qi = pl.program_id(1)
    nkc = ((qi + 1) * tq + kc - 1) // kc
    lane_w = lax.broadcasted_iota(I32, (tq, DSA_WIDTH), 1)
    iq = iq_ref[...]
    for h in range(DSA_IDX_HEADS):
        m = (lane_w >= h * DSA_IDX_DIM) & (lane_w < (h + 1) * DSA_IDX_DIM)
        qs_ref[h * tq:(h + 1) * tq, :] = jnp.where(m, iq, jnp.zeros_like(iq))
    iw = iw_ref[...] * (DSA_IDX_HEADS ** -0.5 * DSA_IDX_DIM ** -0.5)
    lane = lax.broadcasted_iota(I32, (tq, LANES), 1)
    w_cols = [iw[:, h:h + 1] for h in range(DSA_IDX_HEADS)]
    qpos = qi * tq + lax.broadcasted_iota(I32, (tq, sub), 0)
    kcol = lax.broadcasted_iota(I32, (tq, sub), 1)
    nt = (((1,), (1,)), ((), ()))

    ninf = jnp.float32(-jnp.inf)

    def score_chunk(c, causal=True):
        for u in range(kc // sub):
            k0 = pl.multiple_of(c * kc + u * sub, sub)
            lg = lax.dot_general(qs_ref[...], ik_ref[pl.ds(k0, sub), :], nt,
                                 preferred_element_type=F32)
            sc = jnp.zeros((tq, sub), F32)
            for h in range(DSA_IDX_HEADS):
                sc = sc + w_cols[h] * jnp.maximum(lg[h * tq:(h + 1) * tq, :], 0.0)
            sc = sc + 0.0
            if causal:
                sc = jnp.where(kcol + k0 <= qpos, sc, ninf)
            keys_ref[c, :, u * sub:(u + 1) * sub] = sc

    topf_ref[...] = jnp.full(topf_ref.shape, ninf, F32)

    def insert_chunk(c):
        for rt in range(tq // SUBLANES):
            rs = slice(rt * SUBLANES, (rt + 1) * SUBLANES)
            ts = [topf_ref[r, rs, :] for r in range(top_r)]
            for j in range(kc // LANES):
                x = keys_ref[c, rs, j * LANES:(j + 1) * LANES]
                for r in range(top_r):
                    hi = jnp.maximum(ts[r], x)
                    x = jnp.minimum(ts[r], x)
                    ts[r] = hi
            for r in range(top_r):
                topf_ref[r, rs, :] = ts[r]

    score_chunk(0)

    def fused_body(causal):
        def body(c, carry):
            insert_chunk(c - 1)
            score_chunk(c, causal)
            return carry
        return body

    n_open = jnp.maximum((qi * tq + 1) // kc, 1)
    lax.fori_loop(1, n_open, fused_body(False), 0)
    lax.fori_loop(n_open, nkc, fused_body(True), 0)
    insert_chunk(nkc - 1)

    def to_key(x):
        bits = lax.bitcast_convert_type(x, I32)
        key = jnp.where(bits < 0, bits ^ jnp.int32(0x7FFFFFFF), bits)
        return jnp.where(x == ninf, jnp.int32(INT_MIN), key)

    def count_ge(cand):
        candb = jnp.broadcast_to(cand, (tq, LANES))

        def body(c, acc):
            blk = keys_ref[c]
            for j in range(kc // LANES):
                acc = acc + jnp.where(to_key(blk[:, j * LANES:(j + 1) * LANES]) >= candb, 1.0, 0.0)
            return acc

        acc = lax.fori_loop(0, nkc, body, jnp.zeros((tq, LANES), F32))
        return jnp.sum(acc, axis=1, keepdims=True)

    need = jnp.float32(n_sel)

    def select(count):
        t0 = jnp.full((tq, 1), INT_MIN, I32)
        zero = jnp.zeros((tq, 1), I32)
        t0 = jnp.where(count(zero) >= need, zero, t0)

        def bit_body(i, t):
            cand = t | (jnp.int32(1) << (30 - i))
            return jnp.where(count(cand) >= need, cand, t)

        t = lax.fori_loop(0, 31, bit_body, t0)
        t = jnp.maximum(t, jnp.int32(INT_MIN + 1))
        return t, count(t + 1), count(t)

    for r in range(top_r):
        top_ref[r] = to_key(topf_ref[r])

    def count_cand(cand):
        candb = jnp.broadcast_to(cand, (tq, LANES))
        acc = jnp.zeros((tq, LANES), F32)
        for r in range(top_r):
            acc = acc + jnp.where(top_ref[r] >= candb, 1.0, 0.0)
        return jnp.sum(acc, axis=1, keepdims=True)

    thr_c, n_gt_c, n_ge_c = select(count_cand)
    hidden = jnp.max(jnp.where(top_ref[top_r - 1] >= thr_c, 1.0, 0.0)) > 0.0
    thr, n_gt, n_ge = lax.cond(hidden, lambda: select(count_ge), lambda: (thr_c, n_gt_c, n_ge_c))
    n_tie = need - n_gt
    any_tie = jnp.max(n_ge) > need
    thr_bits = jnp.where(thr < 0, thr ^ jnp.int32(0x7FFFFFFF), thr)
    thr_f = jnp.where(thr == jnp.int32(INT_MIN + 1), jnp.float32(jnp.finfo(jnp.float32).min),
                      lax.bitcast_convert_type(thr_bits, F32))

    q = q_ref[...] * jnp.asarray(DSA_HEAD_DIM ** -0.5, BF16)
    for h in range(DSA_HEADS):
        m = (lane_w >= h * DSA_HEAD_DIM) & (lane_w < (h + 1) * DSA_HEAD_DIM)
        q4_ref[h * tq:(h + 1) * tq, :] = jnp.where(m, q, jnp.zeros_like(q))
    neg = jnp.float32(-1e30)

    def lane_fold(op, acc, s):
        for j in range(s.shape[1] // LANES):
            acc = op(acc, s[:, j * LANES:(j + 1) * LANES])
        return acc

    @pl.when(qi == 0)
    def _():
        kmax_ref[...] = jnp.zeros_like(kmax_ref)

    hl = lax.broadcasted_iota(I32, (DSA_WIDTH, LANES), 0) // DSA_HEAD_DIM
    head_of = (hl == lax.broadcasted_iota(I32, (DSA_WIDTH, LANES), 1)).astype(BF16)
    kb = k_ref[pl.ds(pl.multiple_of(qi * tq, tq), tq), :].astype(F32)
    kn2 = jnp.dot((kb * kb).astype(BF16), head_of, preferred_element_type=F32)
    kmax = jnp.maximum(kmax_ref[0:1, :], jnp.max(kn2, axis=0, keepdims=True))
    kmax_ref[...] = jnp.broadcast_to(kmax, kmax_ref.shape)
    qf = q.astype(F32)
    qn2 = jnp.dot((qf * qf).astype(BF16), head_of, preferred_element_type=F32)
    bound = jnp.sqrt(qn2 * kmax) * 1.02
    ms_bound = [jnp.sum(jnp.where(lane == h, bound, 0.0), axis=1, keepdims=True) for h in range(DSA_HEADS)]

    def tie_bias_pass():
        ui = lax.broadcasted_iota(I32, (kc, kc), 0)
        uj = lax.broadcasted_iota(I32, (kc, kc), 1)
        before = (ui < uj).astype(BF16)

        def body(c, seen):
            sc = keys_ref[c]
            eq = sc == thr_f
            eqf = jnp.where(eq, 1.0, 0.0)
            rank = seen + jnp.dot(eqf.astype(BF16), before, preferred_element_type=F32)
            sel = (sc > thr_f) | (eq & (rank < n_tie))
            keys_ref[c] = jnp.where(sel, 0.0, neg)
            return seen + jnp.sum(eqf, axis=1, keepdims=True)

        lax.fori_loop(0, nkc, body, jnp.zeros((tq, 1), F32))

    def bias_of(c, stored):
        sc = keys_ref[c]
        return sc if stored else jnp.where(sc >= thr_f, 0.0, neg)

    def qk(k0):
        return lax.dot_general(q4_ref[...], k_ref[pl.ds(k0, sub), :], nt, preferred_element_type=F32)

    def max_pass(stored):
        run_ref[...] = jnp.full(run_ref.shape, neg, F32)

        def body(c, carry):
            bias = bias_of(c, stored)
            for u in range(kc // sub):
                s = qk(pl.multiple_of(c * kc + u * sub, sub))
                for h in range(DSA_HEADS):
                    run_ref[h] = lane_fold(jnp.maximum, run_ref[h],
                                           s[h * tq:(h + 1) * tq] + bias[:, u * sub:(u + 1) * sub])
            return carry

        lax.fori_loop(0, nkc, body, 0)
        return [jnp.max(run_ref[h], axis=1, keepdims=True) for h in range(DSA_HEADS)]

    def att_pass(stored, ms):
        run_ref[...] = jnp.zeros_like(run_ref)
        acc_ref[...] = jnp.zeros_like(acc_ref)

        def half(s_ref, bias):
            ps = []
            for h in range(DSA_HEADS):
                p = jnp.exp(s_ref[h * tq:(h + 1) * tq, :] + bias - ms[h])
                run_ref[h] = lane_fold(jnp.add, run_ref[h], p)
                ps.append(p.astype(BF16))
            return jnp.concatenate(ps, axis=0)

        sa_ref[...] = qk(0)

        def body(c, carry):
            k0 = pl.multiple_of(c * kc, kc)
            sb_ref[...] = qk(k0 + sub)
            bias = bias_of(c, stored)
            pa = half(sa_ref, bias[:, :sub])
            sa_ref[...] = qk(pl.multiple_of(jnp.minimum(c + 1, nkc - 1) * kc, kc))
            pb = half(sb_ref, bias[:, sub:])
            acc_ref[...] += jnp.dot(jnp.concatenate([pa, pb], axis=1), v_ref[pl.ds(k0, kc), :],
                                    preferred_element_type=F32)
            return carry

        lax.fori_loop(0, nkc, body, 0)

    def finish(stored):
        if stored:
            tie_bias_pass()
        att_pass(stored, ms_bound)
        lmin = jnp.min(jnp.sum(run_ref[0], axis=1, keepdims=True))
        for h in range(1, DSA_HEADS):
            lmin = jnp.minimum(lmin, jnp.min(jnp.sum(run_ref[h], axis=1, keepdims=True)))

        def redo():
            att_pass(stored, max_pass(stored))
            return 0

        lax.cond(lmin >= 1e-26, lambda: 0, redo)
        return 0

    lax.cond(any_tie, lambda: finish(True), lambda: finish(False))
    out = jnp.zeros((tq, DSA_WIDTH), F32)
    for h in range(DSA_HEADS):
        m = (lane_w >= h * DSA_HEAD_DIM) & (lane_w < (h + 1) * DSA_HEAD_DIM)
        out = out + jnp.where(m, acc_ref[h * tq:(h + 1) * tq, :] / jnp.sum(run_ref[h], axis=1, keepdims=True), 0.0)
    o_ref[...] = _rms(out, ng_ref[...]).astype(o_ref.dtype)


def _dsa(p, norm_g, batch, seq, tq=512, kc=512, top_r=12):
    n_sel = min(DSA_TOPK, seq // 4)
    kc = min(kc, seq)
    sub = kc // 2
    r3 = lambda a: a.reshape(batch, seq, a.shape[-1])
    qblk = lambda w: pl.BlockSpec((None, tq, w), lambda b, i: (b, i, 0))
    kfull = lambda w: pl.BlockSpec((None, seq, w), lambda b, i: (b, 0, 0), pipeline_mode=pl.Buffered(1))
    out = pl.pallas_call(
        functools.partial(_dsa_kernel, tq=tq, kc=kc, sub=sub, n_sel=n_sel, top_r=top_r),
        grid=(batch, seq // tq),
        in_specs=[qblk(DSA_WIDTH), kfull(DSA_WIDTH), qblk(LANES), qblk(DSA_WIDTH),
                  kfull(DSA_WIDTH), kfull(DSA_WIDTH), pl.BlockSpec((1, DSA_WIDTH), lambda b, i: (0, 0))],
        out_specs=qblk(DSA_WIDTH),
        out_shape=jax.ShapeDtypeStruct((batch, seq, DSA_WIDTH), BF16),
        scratch_shapes=[
            pltpu.VMEM((DSA_IDX_HEADS * tq, DSA_WIDTH), BF16),
            pltpu.VMEM((DSA_HEADS * tq, DSA_WIDTH), BF16),
            pltpu.VMEM((seq // kc, tq, kc), F32),
            pltpu.VMEM((DSA_HEADS * tq, DSA_WIDTH), F32),
            pltpu.VMEM((DSA_HEADS, tq, LANES), F32),
            pltpu.VMEM((top_r, tq, LANES), I32),
            pltpu.VMEM((DSA_HEADS * tq, sub), F32),
            pltpu.VMEM((DSA_HEADS * tq, sub), F32),
            pltpu.VMEM((SUBLANES, LANES), F32),
            pltpu.VMEM((top_r, tq, LANES), F32),
        ],
        compiler_params=_cparams("arbitrary", "arbitrary"),
        name="dsa",
    )(r3(p["iq"]), r3(p["ik"]), r3(p["iw"]), r3(p["dq"]), r3(p["dk"]), r3(p["dv"]),
      norm_g.reshape(1, DSA_WIDTH))
    return out.reshape(batch * seq, DSA_WIDTH)


def _s5_tables(a_re, a_im, log_step, b_re, b_im, c_re, c_im):
    hp = lax.Precision.HIGHEST
    s = S5_CHUNK
    dt = jnp.exp(log_step)[:, None]
    mag = jnp.exp(a_re * dt)
    lam_re, lam_im = mag * jnp.cos(a_im * dt), mag * jnp.sin(a_im * dt)
    den = a_re * a_re + a_im * a_im
    nr, ni = lam_re - 1.0, lam_im
    coef_re = (nr * a_re + ni * a_im) / den
    coef_im = (ni * a_re - nr * a_im) / den
    bb_re = coef_re[..., None] * b_re - coef_im[..., None] * b_im
    bb_im = coef_re[..., None] * b_im + coef_im[..., None] * b_re
    j = jnp.arange(s + 1, dtype=F32)[:, None, None]
    pmag = jnp.exp(a_re * dt * j)
    pw_re, pw_im = pmag * jnp.cos(a_im * dt * j), pmag * jnp.sin(a_im * dt * j)
    e_re = c_re[None] * pw_re[:, :, None, :] - c_im[None] * pw_im[:, :, None, :]
    e_im = c_re[None] * pw_im[:, :, None, :] + c_im[None] * pw_re[:, :, None, :]
    kk = (jnp.einsum("jgcp,gpd->jgcd", e_re, bb_re, precision=hp)
          - jnp.einsum("jgcp,gpd->jgcd", e_im, bb_im, precision=hp))
    lag = np.arange(s)[None, :] - np.arange(s)[:, None]
    toe = jnp.where((lag >= 0)[:, :, None, None, None], kk[np.clip(lag, 0, s)], 0.0)
    rp_re, rp_im = pw_re[s - 1 - np.arange(s)], pw_im[s - 1 - np.arange(s)]
    x_re = rp_re[..., None] * bb_re[None] - rp_im[..., None] * bb_im[None]
    x_im = rp_re[..., None] * bb_im[None] + rp_im[..., None] * bb_re[None]
    f_re, f_im = e_re[1:], e_im[1:]
    sc_w = s * S5_GROUP_CH
    g_ids = jnp.arange(S5_TILE_G)[:, None, None]

    def placement(inner, width):
        a = jnp.arange(width // S5_TILE_G)[None, :, None]
        dst = (a // inner) * (S5_TILE_G * inner) + g_ids * inner + a % inner
        return (dst == jnp.arange(width)[None, None, :]).astype(F32)

    p_row = placement(S5_GROUP_CH, S5_ROW_W)
    p_state = placement(S5_STATE, S5_STATE_W)
    per_tile = lambda a: a.astype(BF16).astype(F32).reshape((S5_TILES, S5_TILE_G) + a.shape[1:])

    def place(tab, p_in, p_out):
        t = jnp.einsum("qgab,gbc->qgac", per_tile(tab), p_out)
        return jnp.einsum("gar,qgac->qrc", p_in, t).astype(BF16)

    g_toe = toe.transpose(2, 0, 4, 1, 3).reshape(S5_GROUPS, sc_w, sc_w)
    g_w1 = lambda x: x.transpose(1, 0, 3, 2).reshape(S5_GROUPS, sc_w, S5_STATE)
    g_w2 = lambda f: f.transpose(1, 3, 0, 2).reshape(S5_GROUPS, S5_STATE, sc_w)
    tabs = dict(toe=place(g_toe, p_row, p_row),
                w1_re=place(g_w1(x_re), p_row, p_state), w1_im=place(g_w1(x_im), p_row, p_state),
                w2_re=place(g_w2(f_re), p_state, p_row), w2_im=place(g_w2(-f_im), p_state, p_row))
    tabs["a_re"], tabs["a_im"] = pw_re[s].reshape(-1), pw_im[s].reshape(-1)
    return tabs


def _s5_state_in_kernel(u_ref, wr_ref, wi_ref, xr_ref, xi_ref):
    u = u_ref[...].astype(BF16)
    xr_ref[...] = jnp.dot(u, wr_ref[...], preferred_element_type=F32)
    xi_ref[...] = jnp.dot(u, wi_ref[...], preferred_element_type=F32)


def _s5_scan_kernel(xr_ref, xi_ref, ar_ref, ai_ref, hr_ref, hi_ref, cr_ref, ci_ref, *, steps):
    @pl.when(pl.program_id(1) == 0)
    def _():
        cr_ref[...] = jnp.zeros_like(cr_ref)
        ci_ref[...] = jnp.zeros_like(ci_ref)

    ar, ai = ar_ref[...], ai_ref[...]

    def body(n, carry):
        hr, hi = carry
        row = pl.ds(n, 1)
        hr_ref[row, :] = hr
        hi_ref[row, :] = hi
        return ar * hr - ai * hi + xr_ref[row, :], ar * hi + ai * hr + xi_ref[row, :]

    hr, hi = lax.fori_loop(0, steps, body, (cr_ref[...], ci_ref[...]))
    cr_ref[...] = hr
    ci_ref[...] = hi


def _s5_out_kernel(u_ref, toe_ref, hr_ref, hi_ref, w2r_ref, w2i_ref, y_ref):
    y = jnp.dot(u_ref[...].astype(BF16), toe_ref[...], preferred_element_type=F32)
    y = y + jnp.dot(hr_ref[...].astype(BF16), w2r_ref[...], preferred_element_type=F32)
    y = y + jnp.dot(hi_ref[...].astype(BF16), w2i_ref[...], preferred_element_type=F32)
    y_ref[...] = y


def _s5_post_kernel(y_ref, u0_ref, u1_ref, u2_ref, d_ref, gw_ref, gb_ref, ng_ref, o_ref, yt_ref):
    n_chunks = yt_ref.shape[1] // S5_CHUNK
    for q in range(S5_TILES):
        for s in range(S5_CHUNK):
            yt_ref[q, pl.ds(s, n_chunks, stride=S5_CHUNK), :] = y_ref[q, :, s * LANES:(s + 1) * LANES]
    y = jnp.concatenate([yt_ref[q] for q in range(S5_TILES)], axis=1)
    u = jnp.concatenate([u0_ref[...], u1_ref[...], u2_ref[...]], axis=1)
    z = y + d_ref[...] * u
    z = 0.5 * z * (1.0 + jnp.tanh(math.sqrt(2.0 / math.pi) * (z + 0.044715 * (z * z * z))))
    gate = _sigmoid(jnp.dot(z.astype(BF16), gw_ref[...], preferred_element_type=F32) + gb_ref[...])
    o_ref[...] = _rms(z * gate, ng_ref[...]).astype(o_ref.dtype)


def _s5(us, xq, tabs, d, glu_w, glu_b, norm_g, batch, seq, tm=512, rb=512):
    s = S5_CHUNK
    nch = seq // s
    rows = nch * batch
    rb = min(rb, rows)
    gp = S5_GROUPS * S5_STATE
    row_in = pl.BlockSpec((None, rb, S5_ROW_W), lambda q, i: (q, i, 0))
    wspec = lambda a: pl.BlockSpec((None,) + a.shape[1:], lambda q, i: (q, 0, 0))
    colblk = pl.BlockSpec((rb, S5_STATE_W), lambda q, i: (i, q))
    xr, xi = pl.pallas_call(
        _s5_state_in_kernel,
        grid=(S5_TILES, rows // rb),
        in_specs=[row_in, wspec(tabs["w1_re"]), wspec(tabs["w1_im"])],
        out_specs=[colblk, colblk],
        out_shape=[jax.ShapeDtypeStruct((rows, gp), F32)] * 2,
        compiler_params=_cparams("arbitrary", "arbitrary"),
        name="s5_state_in",
    )(xq, tabs["w1_re"], tabs["w1_im"])
    steps = min(128, nch)
    sblk = pl.BlockSpec((steps, gp), lambda b, i: (b * (nch // steps) + i, 0))
    cblk = pl.BlockSpec((1, gp), lambda b, i: (0, 0))
    hr, hi = pl.pallas_call(
        functools.partial(_s5_scan_kernel, steps=steps),
        grid=(batch, nch // steps),
        in_specs=[sblk, sblk, cblk, cblk],
        out_specs=[sblk, sblk],
        out_shape=[jax.ShapeDtypeStruct((rows, gp), F32)] * 2,
        scratch_shapes=[pltpu.VMEM((1, gp), F32)] * 2,
        compiler_params=_cparams("arbitrary", "arbitrary"),
        name="s5_scan",
    )(xr, xi, tabs["a_re"].reshape(1, gp), tabs["a_im"].reshape(1, gp))
    y = pl.pallas_call(
        _s5_out_kernel,
        grid=(S5_TILES, rows // rb),
        in_specs=[row_in, wspec(tabs["toe"]), colblk, colblk, wspec(tabs["w2_re"]), wspec(tabs["w2_im"])],
        out_specs=row_in,
        out_shape=jax.ShapeDtypeStruct((S5_TILES, rows, S5_ROW_W), F32),
        compiler_params=_cparams("arbitrary", "arbitrary"),
        name="s5_out",
    )(xq, tabs["toe"], hr, hi, tabs["w2_re"], tabs["w2_im"])
    t = batch * seq
    tm = min(tm, t)
    tok = lambda w: pl.BlockSpec((tm, w), lambda i: (i, 0))
    vec = pl.BlockSpec((1, S5_WIDTH), lambda i: (0, 0))
    return pl.pallas_call(
        _s5_post_kernel,
        grid=(t // tm,),
        in_specs=[pl.BlockSpec((S5_TILES, tm // s, S5_ROW_W), lambda i: (0, i, 0)), tok(LANES), tok(LANES),
                  tok(LANES), vec, pl.BlockSpec((S5_WIDTH, S5_WIDTH), lambda i: (0, 0)), vec, vec],
        out_specs=tok(S5_WIDTH),
        out_shape=jax.ShapeDtypeStruct((t, S5_WIDTH), BF16),
        scratch_shapes=[pltpu.VMEM((S5_TILES, tm, LANES), F32)],
        compiler_params=_cparams("arbitrary"),
        name="s5_post",
    )(y, us[0], us[1], us[2], d.reshape(1, -1), glu_w.astype(BF16), glu_b.reshape(1, -1),
      norm_g.reshape(1, -1))


def _mix_out_kernel(og_ref, od_ref, os_ref, wg_ref, wd_ref, ws_ref, x_ref, gt_ref, n2_ref, sc_ref,
                    sh_ref, rw1_ref, rw2_ref, rb_ref, x1_ref, h2_ref, route_ref, cnt_ref, carry_ref, *, tm):
    @pl.when(pl.program_id(0) == 0)
    def _():
        carry_ref[...] = jnp.zeros_like(carry_ref)

    mix = jnp.dot(og_ref[...], wg_ref[...], preferred_element_type=F32)
    mix = mix + jnp.dot(od_ref[...], wd_ref[...], preferred_element_type=F32)
    mix = mix + jnp.dot(os_ref[...], ws_ref[...], preferred_element_type=F32)
    x1 = x_ref[...] + gt_ref[...] * mix
    x1_ref[...] = x1
    h2 = _rms(x1, n2_ref[...]) * (1.0 + sc_ref[...]) + sh_ref[...]
    h2_ref[...] = h2
    ha = h2.astype(BF16)
    hb = (h2 - ha.astype(F32)).astype(BF16)
    logit = (jnp.dot(ha, rw1_ref[...], preferred_element_type=F32)
             + jnp.dot(ha, rw2_ref[...], preferred_element_type=F32)
             + jnp.dot(hb, rw1_ref[...], preferred_element_type=F32)) + rb_ref[...]
    lane = lax.broadcasted_iota(I32, (tm, LANES), 1)
    lanef = lane.astype(F32)
    ninf = jnp.float32(-jnp.inf)
    big = jnp.float32(LANES)
    gl = jnp.where(lane < MOE_GROUPS, logit, ninf)
    gmax = jnp.max(gl, axis=1, keepdims=True)
    grp_p = 1.0 / jnp.sum(jnp.exp(gl - gmax), axis=1, keepdims=True)
    grp_i = jnp.min(jnp.where(gl == gmax, lanef, big), axis=1, keepdims=True)
    lo = MOE_GROUPS + MOE_EXPERTS_PER_GROUP * grp_i
    el = jnp.where((lanef >= lo) & (lanef < lo + MOE_EXPERTS_PER_GROUP), logit, ninf)
    l1 = jnp.max(el, axis=1, keepdims=True)
    i1 = jnp.min(jnp.where(el == l1, lanef, big), axis=1, keepdims=True)
    el2 = jnp.where(lanef == i1, ninf, el)
    l2 = jnp.max(el2, axis=1, keepdims=True)
    i2 = jnp.min(jnp.where(el2 == l2, lanef, big), axis=1, keepdims=True)
    e21 = jnp.exp(l2 - l1)
    w1 = grp_p / (1.0 + e21)
    w2 = grp_p * e21 / (1.0 + e21)
    e1 = i1 - MOE_GROUPS
    e2 = i2 - MOE_GROUPS
    oh1 = lanef == e1
    oh2 = lanef == e2
    hot = jnp.where(oh1 | oh2, 1.0, 0.0)
    ri = lax.broadcasted_iota(I32, (tm, tm), 0)
    ci = lax.broadcasted_iota(I32, (tm, tm), 1)
    before = (ci < ri).astype(BF16)
    prior = carry_ref[...] + jnp.dot(before, hot.astype(BF16), preferred_element_type=F32)
    r1 = jnp.sum(jnp.where(oh1, prior, 0.0), axis=1, keepdims=True)
    r2 = jnp.sum(jnp.where(oh2, prior, 0.0), axis=1, keepdims=True)
    total = carry_ref[...] + jnp.sum(hot, axis=0, keepdims=True)
    carry_ref[...] = total
    cnt_ref[...] = jnp.broadcast_to(total, cnt_ref.shape)
    route = jnp.where(lane == 0, e1, jnp.where(lane == 1, e2, jnp.where(lane == 2, w1, jnp.where(
        lane == 3, w2, jnp.where(lane == 4, r1, jnp.where(lane == 5, r2, 0.0))))))
    route_ref[...] = route


def _mix_out(o_gla, o_dsa, o_s5, w_out, x2, gt1, norm2_g, sc2, sh2, rgw, rgb, rew, reb, seq, tm=512):
    t, d = x2.shape
    per_b = seq // tm
    wg = jnp.pad(w_out[:GLA_HEADS * GLA_DV].reshape(GLA_HEADS, GLA_DV, d),
                 ((0, 0), (0, HEAD_PAD - GLA_DV), (0, 0))).reshape(GLA_W, d).astype(BF16)
    wd = w_out[GLA_HEADS * GLA_DV:GLA_HEADS * GLA_DV + DSA_WIDTH].astype(BF16)
    ws = w_out[GLA_HEADS * GLA_DV + DSA_WIDTH:].astype(BF16)
    rw = _pad_cols(jnp.concatenate([rgw, rew], axis=1), LANES)
    rb = _pad_cols(jnp.concatenate([rgb, reb]).reshape(1, -1), LANES)
    rw1 = rw.astype(BF16)
    rw2 = (rw - rw1.astype(F32)).astype(BF16)
    tok = lambda w: pl.BlockSpec((tm, w), lambda i: (i, 0))
    full = lambda a: pl.BlockSpec(a.shape, lambda i: (0, 0))
    perb = pl.BlockSpec((None, 1, d), lambda i: (i // per_b, 0, 0))
    vec = pl.BlockSpec((1, d), lambda i: (0, 0))
    return pl.pallas_call(
        functools.partial(_mix_out_kernel, tm=tm),
        grid=(t // tm,),
        in_specs=[tok(GLA_W), tok(DSA_WIDTH), tok(S5_WIDTH), full(wg), full(wd), full(ws), tok(d),
                  perb, vec, perb, perb, full(rw1), full(rw2), full(rb)],
        out_specs=[tok(d), tok(d), tok(LANES), pl.BlockSpec((SUBLANES, LANES), lambda i: (0, 0))],
        out_shape=[jax.ShapeDtypeStruct((t, d), F32), jax.ShapeDtypeStruct((t, d), F32),
                   jax.ShapeDtypeStruct((t, LANES), F32), jax.ShapeDtypeStruct((SUBLANES, LANES), F32)],
        scratch_shapes=[pltpu.VMEM((1, LANES), F32)],
        compiler_params=_cparams("arbitrary"),
        name="mix_out",
    )(o_gla, o_dsa, o_s5, wg, wd, ws, x2, gt1[:, None, :], norm2_g.reshape(1, d), sc2[:, None, :],
      sh2[:, None, :], rw1, rw2, rb)


def _row(ref, r):
    return ref.at[pl.ds(r, 1), :]


def _dispatch_kernel(dest_ref, h_ref, zero_ref, out_ref, sem, *, td):
    del zero_ref

    def start(g, c):
        for j in range(SUBLANES):
            r = g * SUBLANES + j
            pltpu.make_async_copy(_row(h_ref, r), _row(out_ref, dest_ref[0, r]), sem).start()
            pltpu.make_async_copy(_row(h_ref, r), _row(out_ref, dest_ref[1, r]), sem).start()
        return c

    lax.fori_loop(0, td // SUBLANES, start, 0)
    for _ in range(2):
        pltpu.make_async_copy(h_ref, out_ref.at[pl.ds(0, td), :], sem).wait()


def _dispatch(h2, dest, n_rows, td=512):
    t, d = h2.shape
    dest3 = dest.reshape(2, t // td, td).transpose(1, 0, 2)
    zeros = jnp.zeros((n_rows, d), F32)
    return pl.pallas_call(
        functools.partial(_dispatch_kernel, td=td),
        grid=(t // td,),
        in_specs=[pl.BlockSpec((None, 2, td), lambda i: (i, 0, 0), memory_space=pltpu.SMEM),
                  pl.BlockSpec((td, d), lambda i: (i, 0)),
                  pl.BlockSpec(memory_space=pl.ANY)],
        out_specs=pl.BlockSpec(memory_space=pl.ANY),
        out_shape=jax.ShapeDtypeStruct((n_rows, d), F32),
        scratch_shapes=[pltpu.SemaphoreType.DMA(())],
        input_output_aliases={2: 0},
        compiler_params=_cparams("arbitrary"),
        name="moe_dispatch",
    )(dest3, h2, zeros)


def _expert_kernel(be_ref, nb_ref, x_ref, wg_ref, wu_ref, wd_ref, o_ref):
    del be_ref
    i = pl.program_id(0)

    @pl.when(i < nb_ref[0])
    def _():
        xb = x_ref[...].astype(BF16)
        g = jnp.dot(xb, wg_ref[...].astype(BF16), preferred_element_type=F32)
        u = jnp.dot(xb, wu_ref[...].astype(BF16), preferred_element_type=F32)
        hid = (g * _sigmoid(g) * u).astype(BF16)
        o_ref[...] = jnp.dot(hid, wd_ref[...].astype(BF16), preferred_element_type=F32)

    @pl.when(i >= nb_ref[0])
    def _():
        o_ref[...] = jnp.zeros_like(o_ref)


def _experts(xs, blk_e, n_used, wg, wu, wd, layer):
    n_rows = xs.shape[0]
    nb = n_rows // MOE_BM
    d, hdim = wg.shape[-2:]
    rows = pl.BlockSpec((MOE_BM, d), lambda i, be, nu: (i, 0))
    wspec = lambda a, b: pl.BlockSpec((None, None, a, b), lambda i, be, nu: (layer, be[i], 0, 0))
    return pl.pallas_call(
        _expert_kernel,
        grid_spec=pltpu.PrefetchScalarGridSpec(
            num_scalar_prefetch=2,
            grid=(nb,),
            in_specs=[rows, wspec(d, hdim), wspec(d, hdim), wspec(hdim, d)],
            out_specs=rows,
        ),
        out_shape=jax.ShapeDtypeStruct((n_rows, d), F32),
        compiler_params=_cparams("arbitrary"),
        name="moe_experts",
    )(blk_e, n_used, xs, wg, wu, wd)


def _combine_kernel(dest_ref, next_ref, x_ref, route_ref, gt_ref, fg_ref, eo_ref, o_ref, rows_ref, sem,
                    *, tc, final):
    i = pl.program_id(0)
    slot = i % 2

    def issue(d_ref, s):
        def start(g, c):
            for j in range(SUBLANES):
                r = g * SUBLANES + j
                pltpu.make_async_copy(_row(eo_ref, d_ref[0, r]), _row(rows_ref.at[s, 0], r), sem.at[s]).start()
                pltpu.make_async_copy(_row(eo_ref, d_ref[1, r]), _row(rows_ref.at[s, 1], r), sem.at[s]).start()
            return c

        lax.fori_loop(0, tc // SUBLANES, start, 0)

    @pl.when(i == 0)
    def _():
        issue(dest_ref, 0)

    @pl.when(i + 1 < pl.num_programs(0))
    def _():
        issue(next_ref, 1 - slot)

    for which in range(2):
        pltpu.make_async_copy(eo_ref.at[pl.ds(0, tc), :], rows_ref.at[slot, which], sem.at[slot]).wait()
    route = route_ref[...]
    lane = lax.broadcasted_iota(I32, route.shape, 1)
    w1 = jnp.sum(jnp.where(lane == 2, route, 0.0), axis=1, keepdims=True)
    w2 = jnp.sum(jnp.where(lane == 3, route, 0.0), axis=1, keepdims=True)
    y = rows_ref[slot, 0] * w1 + rows_ref[slot, 1] * w2
    x = x_ref[...] + gt_ref[...] * y
    if final:
        x = _rms(x, fg_ref[...])
    o_ref[...] = x


def _combine(x1, route, dest, eo, gt2, final_g, seq, final, tc=512):
    t, d = x1.shape
    per_b = seq // tc
    steps = t // tc
    dest3 = dest.reshape(2, steps, tc).transpose(1, 0, 2)
    return pl.pallas_call(
        functools.partial(_combine_kernel, tc=tc, final=final),
        grid=(steps,),
        in_specs=[pl.BlockSpec((None, 2, tc), lambda i: (i, 0, 0), memory_space=pltpu.SMEM),
                  pl.BlockSpec((None, 2, tc), lambda i: (jnp.minimum(i + 1, steps - 1), 0, 0),
                               memory_space=pltpu.SMEM),
                  pl.BlockSpec((tc, d), lambda i: (i, 0)),
                  pl.BlockSpec((tc, LANES), lambda i: (i, 0)),
                  pl.BlockSpec((None, 1, d), lambda i: (i // per_b, 0, 0)),
                  pl.BlockSpec((1, d), lambda i: (0, 0)),
                  pl.BlockSpec(memory_space=pl.ANY)],
        out_specs=pl.BlockSpec((tc, d), lambda i: (i, 0)),
        out_shape=jax.ShapeDtypeStruct((t, d), F32),
        scratch_shapes=[pltpu.VMEM((2, 2, tc, d), F32), pltpu.SemaphoreType.DMA((2,))],
        compiler_params=_cparams("arbitrary"),
        name="moe_combine",
    )(dest3, dest3, x1, route, gt2[:, None, :], final_g.reshape(1, d), eo)


def _moe_plan(route, counts, t):
    e = route[:, 0:2].astype(I32).T
    rank = route[:, 4:6].astype(I32).T
    cnt = counts[0, :MOE_EXPERTS].astype(I32)
    padded = (cnt + MOE_BM - 1) // MOE_BM * MOE_BM
    pend = jnp.cumsum(padded)
    pstart = pend - padded
    ids = jnp.arange(MOE_EXPERTS, dtype=I32)
    dest = rank + jnp.sum(jnp.where(e[..., None] == ids, pstart, 0), axis=-1)
    nb = (2 * t) // MOE_BM + MOE_EXPERTS
    n_used = (pend[-1] // MOE_BM).astype(I32)
    first_row = jnp.minimum(jnp.arange(nb, dtype=I32), n_used - 1) * MOE_BM
    blk_e = jnp.sum((pend[None, :] <= first_row[:, None]).astype(I32), axis=1)
    blk_e = jnp.minimum(blk_e, MOE_EXPERTS - 1)
    return dest, blk_e, n_used.reshape(1), nb * MOE_BM


def kernel(x, c, ada_w, ada_b, norm1_g, w_in, gla_gate_w, gla_gate_b, gla_norm_g, dsa_norm_g, s5_a_re, s5_a_im, s5_log_step, s5_b_re, s5_b_im, s5_c_re, s5_c_im, s5_d, s5_glu_w, s5_glu_b, s5_norm_g, w_out, norm2_g, router_grp_w, router_grp_b, router_exp_w, router_exp_b, exp_w_gate, exp_w_up, exp_w_down, final_norm_g):
    batch, seq, d = x.shape
    depth = ada_w.shape[0]
    t = batch * seq
    mod = _ada_mod(c, ada_w, ada_b)
    x2 = x.reshape(t, d)
    for l in range(depth):
        sh1, sc1, gt1, sh2, sc2, gt2 = (mod[l, j] for j in range(6))
        p = _in_proj(x2, norm1_g[l], sc1, sh1, _layout_w_in(w_in[l]), seq)
        o_gla = _gla(p, gla_gate_w[l], gla_gate_b[l], gla_norm_g[l], batch, seq)
        o_dsa = _dsa(p, dsa_norm_g[l], batch, seq)
        tabs = _s5_tables(s5_a_re[l], s5_a_im[l], s5_log_step[l], s5_b_re[l], s5_b_im[l],
                          s5_c_re[l], s5_c_im[l])
        o_s5 = _s5((p["su0"], p["su1"], p["su2"]), p["sx"], tabs, s5_d[l], s5_glu_w[l], s5_glu_b[l], s5_norm_g[l], batch, seq)
        x1, h2, route, counts = _mix_out(o_gla, o_dsa, o_s5, w_out[l], x2, gt1, norm2_g[l], sc2, sh2,
                                         router_grp_w[l], router_grp_b[l], router_exp_w[l],
                                         router_exp_b[l], seq)
        dest, blk_e, n_used, n_rows = _moe_plan(route, counts, t)
        xs = _dispatch(h2, dest, n_rows)
        eo = _experts(xs, blk_e, n_used, exp_w_gate, exp_w_up, exp_w_down, l)
        x2 = _combine(x1, route, dest, eo, gt2, final_norm_g, seq, final=(l == depth - 1))
    return x2.reshape(batch, seq, d)
```
